```python
import numpy as np
import jax
import jax.numpy as jnp
from jax import lax

D_MODEL = 1024
BATCH = 16
SEQ = 2048
DEPTH = 2
DEC_BATCH = 8
DEC_SEQ = 32
PAST_LEN = 4096

CHUNK = 64
Q_BLOCK = 128
N_MEM = 256
EPS = 1e-6
NEG_BIG = -1e30
EXP_CLIP = 80.0
FOX_HEADS = 8
FOX_DIM = 64
FOX_BIAS_INIT = 2.0
MLA_HEADS = 4
MLA_Q_RANK = 384
MLA_KV_RANK = 256
MLA_NOPE = 128
MLA_ROPE = 64
MLA_V = 128
ROPE_THETA = 10000.0
HG_HEADS = 4
HG_DK = 128
HG_DV = 128
X_HEADS = 4
X_DIM = 128
D_FF = 4 * D_MODEL
N_BRANCH = 3
BRANCH_W = 512
IN_SIZES = (FOX_HEADS * FOX_DIM, FOX_HEADS * FOX_DIM, FOX_HEADS * FOX_DIM, FOX_HEADS,
            MLA_Q_RANK, MLA_KV_RANK, MLA_ROPE,
            HG_HEADS * HG_DK, HG_HEADS * HG_DK, HG_HEADS * HG_DV, HG_HEADS * HG_DV,
            D_MODEL, D_MODEL, D_MODEL)
D_IN = sum(IN_SIZES)

kernel_name = 'hybrid_streaming_encoder_step'


def rmsnorm(x, g):
    xf = x.astype(jnp.float32)
    y = xf * lax.rsqrt(jnp.mean(xf * xf, axis=-1, keepdims=True) + EPS)
    return (y * g.astype(jnp.float32)).astype(x.dtype)


def rope(x, pos):
    half = MLA_ROPE // 2
    freq = ROPE_THETA ** (-jnp.arange(half, dtype=jnp.float32) / half)
    ang = pos.astype(jnp.float32)[:, None] * freq[None, :]
    shape = (pos.shape[0],) + (1,) * (x.ndim - 3) + (half,)
    cos, sin = jnp.cos(ang).reshape(shape), jnp.sin(ang).reshape(shape)
    x1 = x[..., :half].astype(jnp.float32)
    x2 = x[..., half:].astype(jnp.float32)
    return jnp.concatenate([x1 * cos - x2 * sin, x2 * cos + x1 * sin], axis=-1).astype(x.dtype)


def attend(q, k, v, bias, mask, scale):
    s = jnp.einsum('bthd,bshd->bhts', q, k).astype(jnp.float32) * scale
    if bias is not None:
        s = s + bias
    if mask is not None:
        s = jnp.where(mask, s, NEG_BIG)
    p = jax.nn.softmax(s, axis=-1).astype(v.dtype)
    return jnp.einsum('bhts,bshd->bthd', p, v)


def fox_block(q, k, v, cum_q, cum_k, qpos, kpos):
    bias = jnp.swapaxes(cum_q, 1, 2)[..., :, None] - jnp.swapaxes(cum_k, 1, 2)[..., None, :]
    mask = qpos[:, None] >= kpos[None, :]
    return attend(q, k, v, bias, mask, FOX_DIM ** -0.5)


def fox_prompt(q, k, v, logf):
    S = q.shape[1]
    cum = jnp.cumsum(logf.astype(jnp.float32), axis=1)
    pos = jnp.arange(S)
    outs = []
    for lo in range(0, S, Q_BLOCK):
        hi = lo + Q_BLOCK
        outs.append(fox_block(q[:, lo:hi], k[:, :hi], v[:, :hi], cum[:, lo:hi], cum[:, :hi],
                              pos[lo:hi], pos[:hi]))
    return jnp.concatenate(outs, axis=1)


def fox_sample(q, k, v, logf, c_k, c_v, c_logf):
    P, T = c_k.shape[1], q.shape[1]
    k_all = jnp.concatenate([c_k, k], axis=1)
    v_all = jnp.concatenate([c_v, v], axis=1)
    cum = jnp.cumsum(jnp.concatenate([c_logf.astype(jnp.float32), logf], axis=1), axis=1)
    return fox_block(q, k_all, v_all, cum[:, P:], cum, P + jnp.arange(T), jnp.arange(P + T))


def mla_query(c_q, pos, g_q, w_uq):
    B, T, _ = c_q.shape
    q = (rmsnorm(c_q, g_q) @ w_uq).reshape(B, T, MLA_HEADS, MLA_NOPE + MLA_ROPE)
    return jnp.concatenate([q[..., :MLA_NOPE], rope(q[..., MLA_NOPE:], pos).astype(q.dtype)], axis=-1)


def mla_keys_values(ckv_n, kpe_r, w_ukv):
    B, S, _ = ckv_n.shape
    kv = (ckv_n @ w_ukv).reshape(B, S, MLA_HEADS, MLA_NOPE + MLA_V)
    k_pe = jnp.broadcast_to(kpe_r[:, :, None, :], (B, S, MLA_HEADS, MLA_ROPE)).astype(kv.dtype)
    return jnp.concatenate([kv[..., :MLA_NOPE], k_pe], axis=-1), kv[..., MLA_NOPE:]


def mla_block(q, k, v, qpos, kpos):
    mask = (qpos[:, None] // CHUNK) >= (kpos[None, :] // CHUNK)
    return attend(q, k, v, None, mask, (MLA_NOPE + MLA_ROPE) ** -0.5)


def mla_prompt(q, k, v, pos):
    outs = []
    for lo in range(0, q.shape[1], Q_BLOCK):
        hi = lo + Q_BLOCK
        outs.append(mla_block(q[:, lo:hi], k[:, :hi], v[:, :hi], pos[lo:hi], pos[:hi]))
    return jnp.concatenate(outs, axis=1)


def hgrn_chunk(S, q, k, v, logf):
    q = q.astype(jnp.float32)
    k = k.astype(jnp.float32)
    v = v.astype(jnp.float32)
    L = q.shape[1]
    lc = jnp.cumsum(logf, axis=1)
    diff = lc[:, :, None] - lc[:, None, :]
    causal = jnp.tril(jnp.ones((L, L), dtype=bool))[None, :, :, None, None]
    decay = jnp.where(causal, jnp.exp(jnp.minimum(diff, 0.0)), 0.0)
    a = jnp.einsum('bthc,bshc,btshc->bhts', q, k, decay)
    o = jnp.einsum('bhts,bshv->bthv', a, v) + jnp.einsum('bthc,bhcv->bthv', q * jnp.exp(lc), S)
    last = lc[:, -1]
    S_new = jnp.exp(last)[..., None] * S + jnp.einsum('bshc,bshv->bhcv', k * jnp.exp(last[:, None] - lc), v)
    return S_new, o


def hgrn_prompt(q, k, v, logf):
    B, S = q.shape[:2]
    n = S // CHUNK

    def to_chunks(a):
        return a.reshape((B, n, CHUNK) + a.shape[2:]).swapaxes(0, 1)

    S0 = jnp.zeros((B, HG_HEADS, HG_DK, HG_DV), jnp.float32)
    state, o = lax.scan(lambda st, xs: hgrn_chunk(st, *xs), S0,
                        (to_chunks(q), to_chunks(k), to_chunks(v), to_chunks(logf)))
    return o.swapaxes(0, 1).reshape(B, S, HG_HEADS, HG_DV), state


def trunk_layer(x, pos, past, mem_k, mem_v, lb, w_in, b_fox, g_mla_q, w_mla_uq, g_mla_kv, w_mla_ukv,
                g_hgrn_out, w_branch, w_out, w_xq, w_xo, w_up, w_down, g_norm):
    B, T, _ = x.shape
    h = rmsnorm(x, g_norm[0])
    idx = np.cumsum(IN_SIZES)[:-1].tolist()
    (fq, fk, fv, ff, cq, ckv, kpe, hq, hf, hi, hg, ga, gb, gc) = jnp.split(h @ w_in, idx, axis=-1)
    fq = fq.reshape(B, T, FOX_HEADS, FOX_DIM)
    fk = fk.reshape(B, T, FOX_HEADS, FOX_DIM)
    fv = fv.reshape(B, T, FOX_HEADS, FOX_DIM)
    f_logf = jax.nn.log_sigmoid(ff.astype(jnp.float32) + b_fox.astype(jnp.float32))
    q_m = mla_query(cq, pos, g_mla_q, w_mla_uq)
    ckv_n = rmsnorm(ckv, g_mla_kv)
    kpe_r = rope(kpe, pos)
    zf = hf.astype(jnp.float32).reshape(B, T, HG_HEADS, HG_DK)
    lbh = lb.reshape(HG_HEADS, HG_DK)
    h_logf = jax.nn.log_sigmoid(zf) + jnp.log1p(lbh * jnp.exp(jnp.minimum(-zf, EXP_CLIP)))
    h_k = (1.0 - lbh) * jax.nn.sigmoid(-zf)
    hq = hq.reshape(B, T, HG_HEADS, HG_DK)
    hi = hi.reshape(B, T, HG_HEADS, HG_DV)
    if past is None:
        o_fox = fox_prompt(fq, fk, fv, f_logf)
        k_m, v_m = mla_keys_values(ckv_n, kpe_r, w_mla_ukv)
        o_mla = mla_prompt(q_m, k_m, v_m, pos)
        o_hg, hg_state = hgrn_prompt(hq, h_k, hi, h_logf)
    else:
        c_fk, c_fv, c_flogf, c_ckv, c_kpe, c_hg = past
        P = c_fk.shape[1]
        o_fox = fox_sample(fq, fk, fv, f_logf, c_fk, c_fv, c_flogf)
        k_m, v_m = mla_keys_values(jnp.concatenate([c_ckv, ckv_n], axis=1),
                                   jnp.concatenate([c_kpe, kpe_r], axis=1), w_mla_ukv)
        o_mla = mla_block(q_m, k_m, v_m, pos, jnp.arange(P + T))
        hg_state, o_hg = hgrn_chunk(c_hg.astype(jnp.float32), hq, h_k, hi, h_logf)
    o_hg = (rmsnorm(o_hg, g_hgrn_out) * jax.nn.sigmoid(hg.astype(jnp.float32)).reshape(B, T, HG_HEADS, HG_DV)).astype(x.dtype)
    merged = (jax.nn.sigmoid(ga) * (o_fox.reshape(B, T, BRANCH_W) @ w_branch[0])
              + jax.nn.sigmoid(gb) * (o_mla.reshape(B, T, BRANCH_W) @ w_branch[1])
              + jax.nn.sigmoid(gc) * (o_hg.reshape(B, T, BRANCH_W) @ w_branch[2]))
    x = x + rmsnorm(merged @ w_out, g_norm[1])
    h = rmsnorm(x, g_norm[2])
    qx = (h @ w_xq).reshape(B, T, X_HEADS, X_DIM)
    ox = attend(qx, mem_k, mem_v, None, None, X_DIM ** -0.5).reshape(B, T, X_HEADS * X_DIM)
    x = x + rmsnorm(ox @ w_xo, g_norm[3])
    h = rmsnorm(x, g_norm[4])
    u = jnp.square(jax.nn.relu(h @ w_up)) @ w_down
    x = x + rmsnorm(u, g_norm[5])
    return x, (fk, fv, f_logf, ckv_n, kpe_r, hg_state)


def setup_inputs(seed: int = 0) -> dict:
    key = jax.random.key(seed)
    ks = jax.random.split(key, 28)

    def nrm(i, shape, scale=1.0):
        return scale * jax.random.normal(ks[i], shape, jnp.float32)

    return {
        'x_prompt': nrm(0, (BATCH, SEQ, D_MODEL)),
        'x_sample': nrm(1, (DEC_BATCH, DEC_SEQ, D_MODEL)),
        'cache_fox_k': nrm(2, (DEPTH, DEC_BATCH, PAST_LEN, FOX_HEADS, FOX_DIM)),
        'cache_fox_v': nrm(3, (DEPTH, DEC_BATCH, PAST_LEN, FOX_HEADS, FOX_DIM)),
        'cache_fox_logf': jax.nn.log_sigmoid(nrm(4, (DEPTH, DEC_BATCH, PAST_LEN, FOX_HEADS)) + FOX_BIAS_INIT),
        'cache_mla_ckv': nrm(5, (DEPTH, DEC_BATCH, PAST_LEN, MLA_KV_RANK)),
        'cache_mla_kpe': nrm(6, (DEPTH, DEC_BATCH, PAST_LEN, MLA_ROPE)),
        'state_hgrn': nrm(7, (DEPTH, DEC_BATCH, HG_HEADS, HG_DK, HG_DV), 0.5),
        'cache_mem_k': nrm(8, (DEPTH, DEC_BATCH, N_MEM, X_HEADS, X_DIM)),
        'cache_mem_v': nrm(9, (DEPTH, DEC_BATCH, N_MEM, X_HEADS, X_DIM)),
        'mem_prompt': nrm(10, (BATCH, N_MEM, D_MODEL)),
        'w_in': nrm(11, (DEPTH, D_MODEL, D_IN), D_MODEL ** -0.5),
        'b_fox': FOX_BIAS_INIT + nrm(12, (DEPTH, FOX_HEADS), 0.1),
        'g_mla_q': 1.0 + nrm(13, (DEPTH, MLA_Q_RANK), 0.05),
        'w_mla_uq': nrm(14, (DEPTH, MLA_Q_RANK, MLA_HEADS * (MLA_NOPE + MLA_ROPE)), MLA_Q_RANK ** -0.5),
        'g_mla_kv': 1.0 + nrm(15, (DEPTH, MLA_KV_RANK), 0.05),
        'w_mla_ukv': nrm(16, (DEPTH, MLA_KV_RANK, MLA_HEADS * (MLA_NOPE + MLA_V)), MLA_KV_RANK ** -0.5),
        'g_hgrn_out': 1.0 + nrm(17, (DEPTH, HG_DV), 0.05),
        'lb_hgrn': nrm(18, (DEPTH, HG_HEADS * HG_DK), 0.1),
        'w_branch': nrm(19, (DEPTH, N_BRANCH, BRANCH_W, D_MODEL), BRANCH_W ** -0.5),
        'w_out': nrm(20, (DEPTH, D_MODEL, D_MODEL), D_MODEL ** -0.5),
        'w_xq': nrm(21, (DEPTH, D_MODEL, X_HEADS * X_DIM), D_MODEL ** -0.5),
        'w_mem_k': nrm(22, (DEPTH, D_MODEL, X_HEADS * X_DIM), D_MODEL ** -0.5),
        'w_mem_v': nrm(23, (DEPTH, D_MODEL, X_HEADS * X_DIM), D_MODEL ** -0.5),
        'w_xo': nrm(24, (DEPTH, X_HEADS * X_DIM, D_MODEL), (X_HEADS * X_DIM) ** -0.5),
        'w_up': nrm(25, (DEPTH, D_MODEL, D_FF), D_MODEL ** -0.5),
        'w_down': nrm(26, (DEPTH, D_FF, D_MODEL), D_FF ** -0.5),
        'g_norm': 1.0 + nrm(27, (DEPTH, 6, D_MODEL), 0.05),
    }


def reference(x_prompt, x_sample, cache_fox_k, cache_fox_v, cache_fox_logf, cache_mla_ckv, cache_mla_kpe,
              state_hgrn, cache_mem_k, cache_mem_v, mem_prompt, w_in, b_fox, g_mla_q, w_mla_uq, g_mla_kv,
              w_mla_ukv, g_hgrn_out, lb_hgrn, w_branch, w_out, w_xq, w_mem_k, w_mem_v, w_xo, w_up, w_down,
              g_norm):
    lb_p = jax.nn.softmax(lb_hgrn.astype(jnp.float32), axis=0)
    lb_all = jnp.cumsum(lb_p, axis=0) - lb_p[0]

    def layer(x, pos, past, mk, mv, l):
        return trunk_layer(x, pos, past, mk, mv, lb_all[l], w_in[l], b_fox[l], g_mla_q[l], w_mla_uq[l],
                           g_mla_kv[l], w_mla_ukv[l], g_hgrn_out[l], w_branch[l], w_out[l], w_xq[l],
                           w_xo[l], w_up[l], w_down[l], g_norm[l])

    Bp, Sp, _ = x_prompt.shape
    pos_p = jnp.arange(Sp)
    x = x_prompt
    p_states = []
    for l in range(DEPTH):
        mk = (mem_prompt @ w_mem_k[l]).reshape(Bp, N_MEM, X_HEADS, X_DIM)
        mv = (mem_prompt @ w_mem_v[l]).reshape(Bp, N_MEM, X_HEADS, X_DIM)
        x, st = layer(x, pos_p, None, mk, mv, l)
        p_states.append(st + (mk, mv))
    y_prompt = x
    (p_fox_k, p_fox_v, p_fox_logf, p_mla_ckv, p_mla_kpe, p_hgrn_state, p_mem_k,
     p_mem_v) = [jnp.stack(a) for a in zip(*p_states)]

    past_len = cache_fox_k.shape[2]
    pos_s = past_len + jnp.arange(x_sample.shape[1])
    x = x_sample
    s_states = []
    for l in range(DEPTH):
        past = (cache_fox_k[l], cache_fox_v[l], cache_fox_logf[l], cache_mla_ckv[l], cache_mla_kpe[l],
                state_hgrn[l])
        x, st = layer(x, pos_s, past, cache_mem_k[l], cache_mem_v[l], l)
        s_states.append(st)
    y_sample = x
    (s_fox_k, s_fox_v, s_fox_logf, s_mla_ckv, s_mla_kpe,
     s_hgrn_state) = [jnp.stack(a) for a in zip(*s_states)]

    return (y_prompt, y_sample, p_fox_k, p_fox_v, p_fox_logf, p_mla_ckv, p_mla_kpe, p_hgrn_state,
            p_mem_k, p_mem_v, s_fox_k, s_fox_v, s_fox_logf, s_mla_ckv, s_mla_kpe, s_hgrn_state)
```

```python
import functools

import numpy as np
import jax
import jax.numpy as jnp
from jax import lax
from jax.experimental import pallas as pl
from jax.experimental.pallas import tpu as pltpu

F32 = jnp.float32
BF16 = jnp.bfloat16

D_MODEL = 1024
CHUNK = 64
N_MEM = 256
EPS = 1e-6
NEG_BIG = -1e30
EXP_CLIP = 80.0
FOX_HEADS = 8
FOX_DIM = 64
MLA_HEADS = 4
MLA_Q_RANK = 384
MLA_KV_RANK = 256
MLA_NOPE = 128
MLA_ROPE = 64
MLA_V = 128
ROPE_THETA = 10000.0
HG_HEADS = 4
HG_DK = 128
HG_DV = 128
X_HEADS = 4
X_DIM = 128
D_FF = 4 * D_MODEL
BRANCH_W = 512
IN_SIZES = (512, 512, 512, FOX_HEADS, MLA_Q_RANK, MLA_KV_RANK, MLA_ROPE, 512, 512, 512, 512,
            D_MODEL, D_MODEL, D_MODEL)

LANE = 128
SUB_BLOCK = 16
VMEM_LIMIT = 56 * 1024 * 1024

NP_IN = 7680
OFF_MLA, OFF_GA, OFF_HQ, OFF_HF, OFF_HI, OFF_HG, OFF_FQ, OFF_FK, OFF_FV = (
    0, 1024, 4096, 4608, 5120, 5632, 6144, 6656, 7168)
MLA_CQ, MLA_CKV, MLA_KPE, MLA_KPE_SW, MLA_FF = 0, 384, 640, 768, 896
IN_TN = 768


def _params(sem, vmem=VMEM_LIMIT):
    return pltpu.CompilerParams(dimension_semantics=sem, vmem_limit_bytes=vmem)


def _dot(a, b):
    return jnp.dot(a, b, preferred_element_type=F32)


def _dot_nt(a, b):
    return lax.dot_general(a, b, (((1,), (1,)), ((), ())), preferred_element_type=F32)


def _dot_tn(a, b):
    return lax.dot_general(a, b, (((0,), (0,)), ((), ())), preferred_element_type=F32)


def _rms(x, g):
    y = x * lax.rsqrt(jnp.mean(x * x, axis=-1, keepdims=True) + EPS)
    return y * g


def _log_sigmoid(z):
    return jnp.minimum(z, 0.0) - jnp.log1p(jnp.exp(-jnp.abs(z)))


def _sigmoid(z):
    return 1.0 / (1.0 + jnp.exp(-z))


def _resident(shape):
    nd = len(shape)
    return pl.BlockSpec(shape, lambda *_: (0,) * nd, pipeline_mode=pl.Buffered(1))


def _norm_matmul_kernel(x_ref, g_ref, w_ref, o_ref, h_ref):
    j = pl.program_id(1)

    @pl.when(j == 0)
    def _():
        h_ref[...] = _rms(x_ref[...], g_ref[...]).astype(BF16)

    o_ref[...] = _dot(h_ref[...], w_ref[j])


def norm_matmul(x, g, w3, tm):
    m, k = x.shape
    nj, _, tn = w3.shape
    return pl.pallas_call(
        _norm_matmul_kernel,
        out_shape=jax.ShapeDtypeStruct((m, nj * tn), F32),
        grid=(m // tm, nj),
        in_specs=[pl.BlockSpec((tm, k), lambda i, j: (i, 0)),
                  pl.BlockSpec((1, k), lambda i, j: (0, 0)),
                  _resident((nj, k, tn))],
        out_specs=pl.BlockSpec((tm, tn), lambda i, j: (i, j)),
        scratch_shapes=[pltpu.VMEM((tm, k), BF16)],
        compiler_params=_params(("parallel", "arbitrary")),
        name="in_proj",
    )(x, g, w3)


def _mla_prep_kernel(blk_ref, cs_ref, sn_ref, gq_ref, gkv_ref, wuq_ref, wukv_ref,
                     qx_ref, kn_ref, kp_ref, v_ref, ckvn_ref, kper_ref):
    cos_t = cs_ref[...]
    sin_t = sn_ref[...]
    cq = blk_ref[:, MLA_CQ:MLA_CQ + MLA_Q_RANK]
    qn = _rms(cq, gq_ref[...]).astype(BF16)
    qall = _dot(qn, wuq_ref[...])
    for h in range(MLA_HEADS):
        lo = h * LANE
        qr = (qall[:, 512 + lo:512 + lo + LANE] * cos_t
              + qall[:, 1024 + lo:1024 + lo + LANE] * sin_t)
        qx_ref[:, 2 * lo:2 * lo + LANE] = qall[:, lo:lo + LANE].astype(BF16)
        qx_ref[:, 2 * lo + LANE:2 * lo + 2 * LANE] = qr.astype(BF16)
    ckvn = _rms(blk_ref[:, MLA_CKV:MLA_CKV + MLA_KV_RANK], gkv_ref[...])
    ckvn_ref[...] = ckvn
    kv = _dot(ckvn.astype(BF16), wukv_ref[...])
    kn_ref[...] = kv[:, :512].astype(BF16)
    v_ref[...] = kv[:, 512:].astype(BF16)
    lane = lax.broadcasted_iota(jnp.int32, (1, LANE), 1)
    kp = (blk_ref[:, MLA_KPE:MLA_KPE + LANE] * cos_t
          + blk_ref[:, MLA_KPE_SW:MLA_KPE_SW + LANE] * sin_t)
    kp = jnp.where(lane < MLA_ROPE, kp, 0.0)
    kper_ref[...] = kp[:, :MLA_ROPE]
    kp_ref[...] = kp.astype(BF16)


def mla_prep(big, cos_t, sin_t, gq, gkv, wuq, wukv, tm):
    m = big.shape[0]
    nt = cos_t.shape[0] // tm
    outs = (jax.ShapeDtypeStruct((m, 1024), BF16),
            jax.ShapeDtypeStruct((m, 512), BF16),
            jax.ShapeDtypeStruct((m, LANE), BF16),
            jax.ShapeDtypeStruct((m, 512), BF16),
            jax.ShapeDtypeStruct((m, MLA_KV_RANK), F32),
            jax.ShapeDtypeStruct((m, MLA_ROPE), F32))
    row = lambda w: pl.BlockSpec((tm, w), lambda i: (i, 0))
    return pl.pallas_call(
        _mla_prep_kernel,
        out_shape=outs,
        grid=(m // tm,),
        in_specs=[pl.BlockSpec((tm, 1024), lambda i: (i, 0)),
                  pl.BlockSpec((tm, LANE), lambda i: (i % nt, 0)),
                  pl.BlockSpec((tm, LANE), lambda i: (i % nt, 0)),
                  pl.BlockSpec((1, MLA_Q_RANK), lambda i: (0, 0)),
                  pl.BlockSpec((1, MLA_KV_RANK), lambda i: (0, 0)),
                  pl.BlockSpec(wuq.shape, lambda i: (0, 0)),
                  pl.BlockSpec(wukv.shape, lambda i: (0, 0))],
        out_specs=(row(1024), row(512), row(LANE), row(512), row(MLA_KV_RANK), row(MLA_ROPE)),
        compiler_params=_params(("parallel",)),
        name="mla_prep",
    )(big, cos_t, sin_t, gq, gkv, wuq, wukv)


def _cumsum_kernel(*refs, apply_logsig, has_init, want_t):
    refs = list(refs)
    x_ref = refs.pop(0)
    b_ref = refs.pop(0) if apply_logsig else None
    init_ref = refs.pop(0) if has_init else None
    carry_ref = refs.pop()
    lf_ref, cum_ref = refs[:2]
    cumt_ref = refs[2] if want_t else None
    s = pl.program_id(1)
    tm = x_ref.shape[1]

    @pl.when(s == 0)
    def _():
        carry_ref[...] = init_ref[0] if has_init else jnp.zeros_like(carry_ref)

    x = x_ref[0]
    lane = lax.broadcasted_iota(jnp.int32, (1, LANE), 1)
    if apply_logsig:
        x = _log_sigmoid(x + b_ref[...])
    x = jnp.where(lane < FOX_HEADS, x, 0.0)
    r = lax.broadcasted_iota(jnp.int32, (tm, tm), 0)
    c = lax.broadcasted_iota(jnp.int32, (tm, tm), 1)
    tri = (r >= c).astype(F32)
    cum = jnp.dot(tri, x, preferred_element_type=F32, precision=lax.Precision.HIGHEST) + carry_ref[...]
    lf_ref[0] = x[:, :FOX_HEADS]
    cum_ref[0] = cum[:, :FOX_HEADS]
    if want_t:
        cumt_ref[0] = cum.T[:FOX_HEADS, :]
    carry_ref[...] = cum[tm - 1:tm, :]


def fox_cumsum(x, tm, b=None, init=None, want_t=True):
    bsz, s, _ = x.shape
    apply_logsig = b is not None
    has_init = init is not None
    out_shape = [jax.ShapeDtypeStruct((bsz, s, FOX_HEADS), F32), jax.ShapeDtypeStruct((bsz, s, FOX_HEADS), F32)]
    out_specs = [pl.BlockSpec((1, tm, FOX_HEADS), lambda i, j: (i, j, 0)),
                 pl.BlockSpec((1, tm, FOX_HEADS), lambda i, j: (i, j, 0))]
    if want_t:
        out_shape.append(jax.ShapeDtypeStruct((bsz, FOX_HEADS, s), F32))
        out_specs.append(pl.BlockSpec((1, FOX_HEADS, tm), lambda i, j: (i, 0, j)))
    ins = [x]
    specs = [pl.BlockSpec((1, tm, LANE), lambda i, j: (i, j, 0))]
    if apply_logsig:
        ins.append(b)
        specs.append(pl.BlockSpec((1, LANE), lambda i, j: (0, 0)))
    if has_init:
        ins.append(init)
        specs.append(pl.BlockSpec((1, 1, LANE), lambda i, j: (i, 0, 0)))
    return pl.pallas_call(
        functools.partial(_cumsum_kernel, apply_logsig=apply_logsig, has_init=has_init, want_t=want_t),
        out_shape=tuple(out_shape),
        grid=(bsz, s // tm),
        in_specs=specs,
        out_specs=tuple(out_specs),
        scratch_shapes=[pltpu.VMEM((1, LANE), F32)],
        compiler_params=_params(("parallel", "arbitrary")),
        name="fox_cumsum",
    )(*ins)


def _online_update(state, s, v_b):
    m, l, acc = state
    m_new = jnp.maximum(m, jnp.max(s, axis=1, keepdims=True))
    alpha = jnp.exp(m - m_new)
    p = jnp.exp(s - m_new)
    l = alpha * l + jnp.sum(p, axis=1, keepdims=True)
    acc = alpha * acc + _dot(p.astype(BF16), v_b)
    return m_new, l, acc


def _head_col(x, head, width):
    lane = lax.broadcasted_iota(jnp.int32, (1, width), 1)
    return jnp.sum(jnp.where(lane == head, x, 0.0), axis=1, keepdims=True)


def _fox_prompt_kernel(q_ref, k_ref, v_ref, cq_ref, ckt_ref, o_ref, *, t):
    hp = pl.program_id(1)
    qi = pl.program_id(2)
    lane = lax.broadcasted_iota(jnp.int32, (1, LANE), 1)
    lo = lane < FOX_DIM
    q = q_ref[...] * (FOX_DIM ** -0.5)
    qs = (jnp.where(lo, q, 0.0).astype(BF16), jnp.where(lo, 0.0, q).astype(BF16))
    cqb = cq_ref[0]
    cqs = tuple(_head_col(cqb, 2 * hp + hh, FOX_HEADS) for hh in range(2))

    def scores(hh, kc, j):
        ck = ckt_ref[0, 2 * hp + hh, pl.ds(j, 1), :]
        return _dot_nt(qs[hh], kc) + (cqs[hh] - ck)

    def body(j, carry):
        r0 = pl.multiple_of(j * t, t)
        kc = k_ref[pl.ds(r0, t), :].astype(BF16)
        vc = v_ref[pl.ds(r0, t), :].astype(BF16)
        return tuple(_online_update(carry[hh], scores(hh, kc, j), vc) for hh in range(2))

    init = tuple((jnp.full((t, 1), NEG_BIG, F32), jnp.zeros((t, 1), F32), jnp.zeros((t, LANE), F32))
                 for _ in range(2))
    st = lax.fori_loop(0, qi, body, init)
    r0 = pl.multiple_of(qi * t, t)
    kc = k_ref[pl.ds(r0, t), :].astype(BF16)
    vc = v_ref[pl.ds(r0, t), :].astype(BF16)
    causal = (lax.broadcasted_iota(jnp.int32, (t, t), 0) >= lax.broadcasted_iota(jnp.int32, (t, t), 1))
    outs = []
    for hh in range(2):
        s = jnp.where(causal, scores(hh, kc, qi), NEG_BIG)
        _, l, acc = _online_update(st[hh], s, vc)
        outs.append(acc / l)
    o_ref[...] = jnp.where(lo, outs[0], outs[1]).astype(BF16)


def fox_prompt(big, cum, cumt, bsz, s, t):
    m = big.shape[0]
    nq = s // t
    cb = lambda off: off // LANE
    return pl.pallas_call(
        functools.partial(_fox_prompt_kernel, t=t),
        out_shape=jax.ShapeDtypeStruct((m, BRANCH_W), BF16),
        grid=(bsz, FOX_HEADS // 2, nq),
        in_specs=[pl.BlockSpec((t, LANE), lambda b, h, i: (b * nq + i, cb(OFF_FQ) + h)),
                  pl.BlockSpec((s, LANE), lambda b, h, i: (b, cb(OFF_FK) + h)),
                  pl.BlockSpec((s, LANE), lambda b, h, i: (b, cb(OFF_FV) + h)),
                  pl.BlockSpec((1, t, FOX_HEADS), lambda b, h, i: (b, i, 0)),
                  pl.BlockSpec((1, FOX_HEADS, nq, t), lambda b, h, i: (b, 0, 0, 0))],
        out_specs=pl.BlockSpec((t, LANE), lambda b, h, i: (b * nq + i, h)),
        compiler_params=_params(("parallel", "parallel", "arbitrary")),
        name="fox_prompt",
    )(big, big, big, cum, cumt.reshape(bsz, FOX_HEADS, nq, t))


def _mla_prompt_kernel(q_ref, kn_ref, kp_ref, v_ref, o_ref, *, t):
    qi = pl.program_id(2)
    scale = (MLA_NOPE + MLA_ROPE) ** -0.5
    q = q_ref[...]

    def scores(j):
        r0 = pl.multiple_of(j * t, t)
        kc = jnp.concatenate([kn_ref[pl.ds(r0, t), :], kp_ref[pl.ds(r0, t), :]], axis=1)
        return _dot_nt(q, kc) * scale, v_ref[pl.ds(r0, t), :]

    def body(j, carry):
        s, vc = scores(j)
        return _online_update(carry, s, vc)

    init = (jnp.full((t, 1), NEG_BIG, F32), jnp.zeros((t, 1), F32), jnp.zeros((t, LANE), F32))
    st = lax.fori_loop(0, qi, body, init)
    s, vc = scores(qi)
    rc = lax.broadcasted_iota(jnp.int32, (t, t), 0) // CHUNK
    cc = lax.broadcasted_iota(jnp.int32, (t, t), 1) // CHUNK
    s = jnp.where(rc >= cc, s, NEG_BIG)
    _, l, acc = _online_update(st, s, vc)
    o_ref[...] = (acc / l).astype(BF16)


def mla_prompt(qx, kn, kp, v, bsz, s, t):
    m = qx.shape[0]
    nq = s // t
    return pl.pallas_call(
        functools.partial(_mla_prompt_kernel, t=t),
        out_shape=jax.ShapeDtypeStruct((m, BRANCH_W), BF16),
        grid=(bsz, MLA_HEADS, nq),
        in_specs=[pl.BlockSpec((t, 2 * LANE), lambda b, h, i: (b * nq + i, h)),
                  pl.BlockSpec((s, LANE), lambda b, h, i: (b, h)),
                  pl.BlockSpec((s, LANE), lambda b, h, i: (b, 0)),
                  pl.BlockSpec((s, LANE), lambda b, h, i: (b, h))],
        out_specs=pl.BlockSpec((t, LANE), lambda b, h, i: (b * nq + i, h)),
        compiler_params=_params(("parallel", "parallel", "arbitrary")),
        name="mla_prompt",
    )(qx, kn, kp, v)


def _two_part_softmax(s_c, s_n, v_c, v_n):
    m = jnp.maximum(jnp.max(s_c, axis=1, keepdims=True), jnp.max(s_n, axis=1, keepdims=True))
    p_c = jnp.exp(s_c - m)
    p_n = jnp.exp(s_n - m)
    l = jnp.sum(p_c, axis=1, keepdims=True) + jnp.sum(p_n, axis=1, keepdims=True)
    return (_dot(p_c.astype(BF16), v_c) + _dot(p_n.astype(BF16), v_n)) / l


def _fox_sample_kernel(q_ref, kn_ref, vn_ref, kc_ref, vc_ref, cq_ref, cktc_ref, o_ref, *, t):
    hp = pl.program_id(1)
    lane = lax.broadcasted_iota(jnp.int32, (1, LANE), 1)
    lo = lane < FOX_DIM
    q = q_ref[...] * (FOX_DIM ** -0.5)
    k_c = kc_ref[...].astype(BF16)
    v_c = vc_ref[...].astype(BF16)
    k_n = kn_ref[...].astype(BF16)
    v_n = vn_ref[...].astype(BF16)
    ri = lax.broadcasted_iota(jnp.int32, (t, t), 0)
    ci = lax.broadcasted_iota(jnp.int32, (t, t), 1)
    causal = ri >= ci
    cqb = cq_ref[0]
    outs = []
    for hh in range(2):
        head = 2 * hp + hh
        qh = (jnp.where(lo, q, 0.0) if hh == 0 else jnp.where(lo, 0.0, q)).astype(BF16)
        cq = _head_col(cqb, head, FOX_HEADS)
        ck_n = jnp.sum(jnp.where(ri == ci, cq, 0.0), axis=0, keepdims=True)
        s_c = _dot_nt(qh, k_c) + (cq - cktc_ref[0, pl.ds(head, 1), :])
        s_n = _dot_nt(qh, k_n) + (cq - ck_n)
        s_n = jnp.where(causal, s_n, NEG_BIG)
        outs.append(_two_part_softmax(s_c, s_n, v_c, v_n))
    o_ref[...] = jnp.where(lo, outs[0], outs[1]).astype(BF16)


def fox_sample(big, ck, cv, cum_n, cumt_c, bsz, t, p):
    cb = lambda off: off // LANE
    return pl.pallas_call(
        functools.partial(_fox_sample_kernel, t=t),
        out_shape=jax.ShapeDtypeStruct((bsz * t, BRANCH_W), BF16),
        grid=(bsz, FOX_HEADS // 2),
        in_specs=[pl.BlockSpec((t, LANE), lambda b, h: (b, cb(OFF_FQ) + h)),
                  pl.BlockSpec((t, LANE), lambda b, h: (b, cb(OFF_FK) + h)),
                  pl.BlockSpec((t, LANE), lambda b, h: (b, cb(OFF_FV) + h)),
                  pl.BlockSpec((p, LANE), lambda b, h: (b, h)),
                  pl.BlockSpec((p, LANE), lambda b, h: (b, h)),
                  pl.BlockSpec((1, t, FOX_HEADS), lambda b, h: (b, 0, 0)),
                  pl.BlockSpec((1, FOX_HEADS, p), lambda b, h: (b, 0, 0))],
        out_specs=pl.BlockSpec((t, LANE), lambda b, h: (b, h)),
        compiler_params=_params(("parallel", "parallel")),
        name="fox_sample",
    )(big, big, big, ck, cv, cum_n, cumt_c)


def _mla_sample_kernel(q_ref, knn_ref, kpn_ref, vn_ref, knc_ref, kpc_ref, vc_ref, o_ref, *, t, p):
    scale = (MLA_NOPE + MLA_ROPE) ** -0.5
    q = q_ref[...]
    k_c = jnp.concatenate([knc_ref[...], kpc_ref[...]], axis=1)
    k_n = jnp.concatenate([knn_ref[...], kpn_ref[...]], axis=1)
    s_c = _dot_nt(q, k_c) * scale
    s_n = _dot_nt(q, k_n) * scale
    qc = (p + lax.broadcasted_iota(jnp.int32, (t, t), 0)) // CHUNK
    kc = (p + lax.broadcasted_iota(jnp.int32, (t, t), 1)) // CHUNK
    s_n = jnp.where(qc >= kc, s_n, NEG_BIG)
    o_ref[...] = _two_part_softmax(s_c, s_n, vc_ref[...], vn_ref[...]).astype(BF16)


def mla_sample(qx, kn_n, kp_n, v_n, kn_c, kp_c, v_c, bsz, t, p):
    assert (p - 1) // CHUNK <= p // CHUNK
    return pl.pallas_call(
        functools.partial(_mla_sample_kernel, t=t, p=p),
        out_shape=jax.ShapeDtypeStruct((bsz * t, BRANCH_W), BF16),
        grid=(bsz, MLA_HEADS),
        in_specs=[pl.BlockSpec((t, 2 * LANE), lambda b, h: (b, h)),
                  pl.BlockSpec((t, LANE), lambda b, h: (b, h)),
                  pl.BlockSpec((t, LANE), lambda b, h: (b, 0)),
                  pl.BlockSpec((t, LANE), lambda b, h: (b, h)),
                  pl.BlockSpec((p, LANE), lambda b, h: (b, h)),
                  pl.BlockSpec((p, LANE), lambda b, h: (b, 0)),
                  pl.BlockSpec((p, LANE), lambda b, h: (b, h))],
        out_specs=pl.BlockSpec((t, LANE), lambda b, h: (b, h)),
        compiler_params=_params(("parallel", "parallel")),
        name="mla_sample",
    )(qx, kn_n, kp_n, v_n, kn_c, kp_c, v_c)


def _latent_expand_kernel(c_ref, w_ref, kn_ref, v_ref):
    kv = _dot(c_ref[...].astype(BF16), w_ref[...])
    kn_ref[...] = kv[:, :512].astype(BF16)
    v_ref[...] = kv[:, 512:].astype(BF16)


def latent_expand(ckv, wukv, tm):
    m = ckv.shape[0]
    row = lambda w: pl.BlockSpec((tm, w), lambda i: (i, 0))
    return pl.pallas_call(
        _latent_expand_kernel,
        out_shape=(jax.ShapeDtypeStruct((m, 512), BF16), jax.ShapeDtypeStruct((m, 512), BF16)),
        grid=(m // tm,),
        in_specs=[row(MLA_KV_RANK), pl.BlockSpec(wukv.shape, lambda i: (0, 0))],
        out_specs=(row(512), row(512)),
        compiler_params=_params(("parallel",)),
        name="latent_expand",
    )(ckv, wukv)


def _hgrn_chunk_head(q, z, v, gz, lb, g_out, st, tri, ones_b):
    ln = q.shape[0]
    nb = ln // SUB_BLOCK
    logf = _log_sigmoid(z) + jnp.log1p(lb * jnp.exp(jnp.minimum(-z, EXP_CLIP)))
    k = (1.0 - lb) * (1.0 / (1.0 + jnp.exp(z)))
    lc = jnp.dot(tri, logf, preferred_element_type=F32, precision=lax.Precision.HIGHEST)
    v_b = v.astype(BF16)
    rows = lambda a, i: a[i * SUB_BLOCK:(i + 1) * SUB_BLOCK, :]
    lcb = [jnp.zeros((1, LANE), F32)] + [lc[i * SUB_BLOCK - 1:i * SUB_BLOCK, :] for i in range(1, nb)]
    lcb_rows = jnp.concatenate([jnp.broadcast_to(b, (SUB_BLOCK, LANE)) for b in lcb], axis=0)
    qh = q * jnp.exp(lc - lcb_rows)
    qe = q * jnp.exp(lc)
    o = _dot_nt(qe.astype(BF16), st.astype(BF16))
    off = [jnp.zeros((SUB_BLOCK, LANE), F32)]
    for i in range(1, nb):
        n = i * SUB_BLOCK
        kt = k[:n, :] * jnp.exp(lcb[i] - lc[:n, :])
        a = _dot_nt(rows(qh, i).astype(BF16), kt.astype(BF16))
        off.append(_dot(a.astype(BF16), v_b[:n, :]))
    o = o + jnp.concatenate(off, axis=0)
    w_all = []
    for i in range(nb):
        q_i, k_i, lc_i = rows(q, i), rows(k, i), rows(lc, i)
        for s in range(SUB_BLOCK):
            d = lc_i - lc_i[s:s + 1, :]
            w_all.append((q_i * k_i[s:s + 1, :]) * jnp.exp(jnp.minimum(d, 0.0)))
    r_all = _dot(jnp.concatenate(w_all, axis=0).astype(BF16), ones_b)
    tpos = lax.broadcasted_iota(jnp.int32, (SUB_BLOCK, 1), 0)
    v_f = v_b.astype(F32)
    diag = []
    for i in range(nb):
        acc = jnp.zeros((SUB_BLOCK, LANE), F32)
        for s in range(SUB_BLOCK):
            base = (i * SUB_BLOCK + s) * SUB_BLOCK
            a_s = jnp.where(tpos >= s, r_all[base:base + SUB_BLOCK, :], 0.0)
            acc = acc + a_s * v_f[i * SUB_BLOCK + s:i * SUB_BLOCK + s + 1, :]
        diag.append(acc)
    o = o + jnp.concatenate(diag, axis=0)
    last = lc[ln - 1:ln, :]
    kdec = k * jnp.exp(last - lc)
    st_new = st * jnp.exp(last) + _dot_tn(v_b, kdec.astype(BF16))
    out = _rms(o, g_out) * _sigmoid(gz)
    return out, st_new


def _hgrn_kernel(*refs, ln, nchunk, has_init):
    refs = list(refs)
    hq_ref, hf_ref, hi_ref, hg_ref, lb_ref, go_ref = refs[:6]
    s0_ref = refs[6] if has_init else None
    o_ref, sout_ref, st_ref = refs[-3:]
    step = pl.program_id(1)

    @pl.when(step == 0)
    def _():
        for h in range(HG_HEADS):
            st_ref[h] = s0_ref[0, h].T if has_init else jnp.zeros((HG_DV, HG_DK), F32)

    tri = (lax.broadcasted_iota(jnp.int32, (ln, ln), 0)
           >= lax.broadcasted_iota(jnp.int32, (ln, ln), 1)).astype(F32)
    ones_b = jnp.ones((LANE, LANE), BF16)
    g_out = go_ref[...]

    def chunk(c, carry):
        r0 = pl.multiple_of(c * ln, ln)
        for h in range(HG_HEADS):
            cs = slice(h * LANE, (h + 1) * LANE)
            out, st_new = _hgrn_chunk_head(
                hq_ref[pl.ds(r0, ln), cs], hf_ref[pl.ds(r0, ln), cs], hi_ref[pl.ds(r0, ln), cs],
                hg_ref[pl.ds(r0, ln), cs], lb_ref[:, cs], g_out, st_ref[h], tri, ones_b)
            o_ref[pl.ds(r0, ln), cs] = out.astype(BF16)
            st_ref[h] = st_new
        return carry

    lax.fori_loop(0, nchunk, chunk, 0)

    @pl.when(step == pl.num_programs(1) - 1)
    def _():
        for h in range(HG_HEADS):
            sout_ref[0, h] = st_ref[h].T


def hgrn(big, lb, g_out, bsz, s, ln, rows, s0=None):
    m = big.shape[0]
    ns = s // rows
    has_init = s0 is not None
    cb = lambda off: off // BRANCH_W
    blk = lambda off: pl.BlockSpec((rows, BRANCH_W), lambda b, i: (b * ns + i, cb(off)))
    ins = [big, big, big, big, lb, g_out]
    specs = [blk(OFF_HQ), blk(OFF_HF), blk(OFF_HI), blk(OFF_HG),
             pl.BlockSpec((1, BRANCH_W), lambda b, i: (0, 0)),
             pl.BlockSpec((1, HG_DV), lambda b, i: (0, 0))]
    if has_init:
        ins.append(s0)
        specs.append(pl.BlockSpec((1, HG_HEADS, HG_DK, HG_DV), lambda b, i: (b, 0, 0, 0)))
    return pl.pallas_call(
        functools.partial(_hgrn_kernel, ln=ln, nchunk=rows // ln, has_init=has_init),
        out_shape=(jax.ShapeDtypeStruct((m, BRANCH_W), BF16),
                   jax.ShapeDtypeStruct((bsz, HG_HEADS, HG_DK, HG_DV), F32)),
        grid=(bsz, ns),
        in_specs=specs,
        out_specs=(pl.BlockSpec((rows, BRANCH_W), lambda b, i: (b * ns + i, 0)),
                   pl.BlockSpec((1, HG_HEADS, HG_DK, HG_DV), lambda b, i: (b, 0, 0, 0))),
        scratch_shapes=[pltpu.VMEM((HG_HEADS, HG_DV, HG_DK), F32)],
        compiler_params=_params(("parallel", "arbitrary")),
        name="hgrn",
    )(*ins)


def _merge_kernel(of_ref, om_ref, oh_ref, ga_ref, gb_ref, gc_ref, x_ref, wb_ref, wo_ref, g_ref, o_ref):
    merged = (_sigmoid(ga_ref[...]) * _dot(of_ref[...], wb_ref[0])
              + _sigmoid(gb_ref[...]) * _dot(om_ref[...], wb_ref[1])
              + _sigmoid(gc_ref[...]) * _dot(oh_ref[...], wb_ref[2]))
    y = _dot(merged.astype(BF16), wo_ref[...])
    o_ref[...] = x_ref[...] + _rms(y, g_ref[...])


def merge_out(o_fox, o_mla, o_hg, big, x, wb, wo, g, tm):
    m = x.shape[0]
    row = lambda w: pl.BlockSpec((tm, w), lambda i: (i, 0))
    gate = lambda off: pl.BlockSpec((tm, D_MODEL), lambda i: (i, off // D_MODEL))
    return pl.pallas_call(
        _merge_kernel,
        out_shape=jax.ShapeDtypeStruct((m, D_MODEL), F32),
        grid=(m // tm,),
        in_specs=[row(BRANCH_W), row(BRANCH_W), row(BRANCH_W),
                  gate(OFF_GA), gate(OFF_GA + D_MODEL), gate(OFF_GA + 2 * D_MODEL),
                  row(D_MODEL), _resident(wb.shape), _resident(wo.shape),
                  pl.BlockSpec((1, D_MODEL), lambda i: (0, 0))],
        out_specs=row(D_MODEL),
        compiler_params=_params(("parallel",)),
        name="merge_out",
    )(o_fox, o_mla, o_hg, big, big, big, x, wb, wo, g)


def _matmul2_kernel(x_ref, w_ref, a_ref, b_ref):
    y = _dot(x_ref[...].astype(BF16), w_ref[...])
    n = a_ref.shape[1]
    a_ref[...] = y[:, :n]
    b_ref[...] = y[:, n:]


def mem_kv(mem, w, tm):
    m, k = mem.shape
    n = w.shape[1] // 2
    row = lambda w_: pl.BlockSpec((tm, w_), lambda i: (i, 0))
    return pl.pallas_call(
        _matmul2_kernel,
        out_shape=(jax.ShapeDtypeStruct((m, n), F32), jax.ShapeDtypeStruct((m, n), F32)),
        grid=(m // tm,),
        in_specs=[row(k), _resident(w.shape)],
        out_specs=(row(n), row(n)),
        compiler_params=_params(("parallel",)),
        name="mem_kv",
    )(mem, w)


def _cross_kernel(x_ref, mk_ref, mv_ref, wq_ref, wo_ref, g2_ref, g3_ref, o_ref):
    x = x_ref[...]
    h = _rms(x, g2_ref[...]).astype(BF16)
    q = _dot(h, wq_ref[...])
    scale = X_DIM ** -0.5
    outs = []
    for hd in range(X_HEADS):
        cs = slice(hd * X_DIM, (hd + 1) * X_DIM)
        s = _dot_nt(q[:, cs].astype(BF16), mk_ref[:, cs].astype(BF16)) * scale
        p = jnp.exp(s - jnp.max(s, axis=1, keepdims=True))
        l = jnp.sum(p, axis=1, keepdims=True)
        outs.append(_dot(p.astype(BF16), mv_ref[:, cs].astype(BF16)) / l)
    ox = jnp.concatenate(outs, axis=1).astype(BF16)
    o_ref[...] = x + _rms(_dot(ox, wo_ref[...]), g3_ref[...])


def cross_block(x, mk, mv, wq, wo, g2, g3, bsz, s, tm):
    m = x.shape[0]
    nt = s // tm
    vec = pl.BlockSpec((1, D_MODEL), lambda b, i: (0, 0))
    return pl.pallas_call(
        _cross_kernel,
        out_shape=jax.ShapeDtypeStruct((m, D_MODEL), F32),
        grid=(bsz, nt),
        in_specs=[pl.BlockSpec((tm, D_MODEL), lambda b, i: (b * nt + i, 0)),
                  pl.BlockSpec((N_MEM, X_HEADS * X_DIM), lambda b, i: (b, 0)),
                  pl.BlockSpec((N_MEM, X_HEADS * X_DIM), lambda b, i: (b, 0)),
                  _resident(wq.shape), _resident(wo.shape), vec, vec],
        out_specs=pl.BlockSpec((tm, D_MODEL), lambda b, i: (b * nt + i, 0)),
        compiler_params=_params(("parallel", "parallel")),
        name="cross_attn",
    )(x, mk, mv, wq, wo, g2, g3)


def _mlp_kernel(x_ref, wu_ref, wd_ref, g4_ref, g5_ref, o_ref, h_ref, acc_ref):
    j = pl.program_id(1)

    @pl.when(j == 0)
    def _():
        h_ref[...] = _rms(x_ref[...], g4_ref[...]).astype(BF16)
        acc_ref[...] = jnp.zeros_like(acc_ref)

    u = jnp.square(jnp.maximum(_dot(h_ref[...], wu_ref[j]), 0.0)).astype(BF16)
    acc_ref[...] += _dot(u, wd_ref[j])

    @pl.when(j == pl.num_programs(1) - 1)
    def _():
        o_ref[...] = x_ref[...] + _rms(acc_ref[...], g5_ref[...])


def mlp_block(x, wu3, wd3, g4, g5, tm):
    m = x.shape[0]
    nj = wu3.shape[0]
    vec = pl.BlockSpec((1, D_MODEL), lambda i, j: (0, 0))
    return pl.pallas_call(
        _mlp_kernel,
        out_shape=jax.ShapeDtypeStruct((m, D_MODEL), F32),
        grid=(m // tm, nj),
        in_specs=[pl.BlockSpec((tm, D_MODEL), lambda i, j: (i, 0)),
                  _resident(wu3.shape), _resident(wd3.shape), vec, vec],
        out_specs=pl.BlockSpec((tm, D_MODEL), lambda i, j: (i, 0)),
        scratch_shapes=[pltpu.VMEM((tm, D_MODEL), BF16), pltpu.VMEM((tm, D_MODEL), F32)],
        compiler_params=_params(("parallel", "arbitrary")),
        name="mlp",
    )(x, wu3, wd3, g4, g5)


def _prep_layer_weights(w_in, w_mla_uq, w_mla_ukv, w_branch, w_out, w_xq, w_mem_k, w_mem_v, w_xo,
                        w_up, w_down):
    idx = np.cumsum((0,) + IN_SIZES)
    seg = lambda i: w_in[:, idx[i]:idx[i + 1]]
    fq, fk, fv, ff, cq, ckv, kpe, hq, hf, hi, hg, ga, gb, gc = (seg(i) for i in range(14))
    z = lambda n: jnp.zeros((D_MODEL, n), w_in.dtype)
    half = MLA_ROPE // 2
    kpe_sw = jnp.concatenate([kpe[:, half:], kpe[:, :half]], axis=1)
    mla = jnp.concatenate([cq, ckv, kpe, z(64), kpe_sw, z(64), ff, z(1024 - MLA_FF - FOX_HEADS)], axis=1)
    w_p = jnp.concatenate([mla, ga, gb, gc, hq, hf, hi, hg, fq, fk, fv], axis=1).astype(BF16)
    w_in3 = w_p.reshape(D_MODEL, NP_IN // IN_TN, IN_TN).transpose(1, 0, 2)
    hd = MLA_NOPE + MLA_ROPE
    zq = jnp.zeros((MLA_Q_RANK, LANE - MLA_ROPE), w_mla_uq.dtype)
    nope, rope_n, rope_s = [], [], []
    for h in range(MLA_HEADS):
        base = h * hd
        nope.append(w_mla_uq[:, base:base + MLA_NOPE])
        x1 = w_mla_uq[:, base + MLA_NOPE:base + MLA_NOPE + half]
        x2 = w_mla_uq[:, base + MLA_NOPE + half:base + hd]
        rope_n += [x1, x2, zq]
        rope_s += [x2, x1, zq]
    wuq = jnp.concatenate(nope + rope_n + rope_s, axis=1).astype(BF16)
    kvd = MLA_NOPE + MLA_V
    wukv = jnp.concatenate([w_mla_ukv[:, h * kvd:h * kvd + MLA_NOPE] for h in range(MLA_HEADS)]
                           + [w_mla_ukv[:, h * kvd + MLA_NOPE:(h + 1) * kvd] for h in range(MLA_HEADS)],
                           axis=1).astype(BF16)
    nff = D_FF // D_MODEL
    return dict(
        w_in3=w_in3, wuq=wuq, wukv=wukv,
        wb=w_branch.astype(BF16), wo=w_out.astype(BF16), wxq=w_xq.astype(BF16), wxo=w_xo.astype(BF16),
        wmem=jnp.concatenate([w_mem_k, w_mem_v], axis=1).astype(BF16),
        wu3=w_up.astype(BF16).reshape(D_MODEL, nff, D_MODEL).transpose(1, 0, 2),
        wd3=w_down.astype(BF16).reshape(nff, D_MODEL, D_MODEL))


def _rope_tables(pos, reps):
    half = MLA_ROPE // 2
    freq = ROPE_THETA ** (-jnp.arange(half, dtype=F32) / half)
    ang = pos.astype(F32)[:, None] * freq[None, :]
    cos, sin = jnp.cos(ang), jnp.sin(ang)
    z = jnp.zeros((pos.shape[0], LANE - MLA_ROPE), F32)
    cos_t = jnp.concatenate([cos, cos, z], axis=1)
    sin_t = jnp.concatenate([-sin, sin, z], axis=1)
    return jnp.tile(cos_t, (reps, 1)), jnp.tile(sin_t, (reps, 1))


def _tile(n, pref):
    t = min(n, pref)
    assert n % t == 0
    return t


def _layer(x, bsz, s, pos0, w, lb, b_fox, g_q, g_kv, g_hout, g_norm, mem_k, mem_v, past, cfg):
    m = bsz * s
    g = lambda i: g_norm[i][None, :]
    big = norm_matmul(x, g(0), w["w_in3"], _tile(m, cfg["tm_in"]))
    tm_p = _tile(m, cfg["tm_prep"])
    if tm_p >= s:
        cos_t, sin_t = _rope_tables(pos0 + jnp.arange(s), tm_p // s)
    else:
        cos_t, sin_t = _rope_tables(pos0 + jnp.arange(s), 1)
    qx, kn, kp, v, ckv_n, kpe_r = mla_prep(big, cos_t, sin_t, g_q[None, :], g_kv[None, :],
                                           w["wuq"], w["wukv"], tm_p)
    bpad = jnp.pad(b_fox, (0, LANE - FOX_HEADS))[None, :]
    ff = big[:, OFF_MLA + MLA_FF:OFF_MLA + MLA_FF + LANE].reshape(bsz, s, LANE)
    if past is None:
        t = _tile(s, cfg["t_attn"])
        f_logf, cum, cumt = fox_cumsum(ff, t, b=bpad)
        o_fox = fox_prompt(big, cum, cumt, bsz, s, t)
        o_mla = mla_prompt(qx, kn, kp, v, bsz, s, t)
        o_hg, hg_state = hgrn(big, lb[None, :], g_hout[None, :], bsz, s, CHUNK,
                              _tile(s, cfg["hg_rows"]))
    else:
        c_fk, c_fv, c_cum, c_cumt, c_kn, c_kp, c_v, c_hg = past
        p = c_fk.shape[0] // bsz
        init = jnp.pad(c_cum[:, p - 1:p, :], ((0, 0), (0, 0), (0, LANE - FOX_HEADS)))
        f_logf, cum = fox_cumsum(ff, s, b=bpad, init=init, want_t=False)
        o_fox = fox_sample(big, c_fk, c_fv, cum, c_cumt, bsz, s, p)
        o_mla = mla_sample(qx, kn, kp, v, c_kn, c_kp, c_v, bsz, s, p)
        o_hg, hg_state = hgrn(big, lb[None, :], g_hout[None, :], bsz, s, s, s, s0=c_hg)
    x = merge_out(o_fox, o_mla, o_hg, big, x, w["wb"], w["wo"], g(1), _tile(m, cfg["tm_merge"]))
    x = cross_block(x, mem_k, mem_v, w["wxq"], w["wxo"], g(2), g(3), bsz, s, _tile(s, cfg["tm_cross"]))
    x = mlp_block(x, w["wu3"], w["wd3"], g(4), g(5), _tile(m, cfg["tm_mlp"]))
    fk = big[:, OFF_FK:OFF_FK + BRANCH_W].reshape(bsz, s, FOX_HEADS, FOX_DIM)
    fv = big[:, OFF_FV:OFF_FV + BRANCH_W].reshape(bsz, s, FOX_HEADS, FOX_DIM)
    state = (fk, fv, f_logf, ckv_n.reshape(bsz, s, MLA_KV_RANK), kpe_r.reshape(bsz, s, MLA_ROPE), hg_state)
    return x, state


_CFG = dict(tm_in=1024, tm_prep=512, t_attn=256, hg_rows=256, tm_merge=512, tm_cross=512, tm_mlp=512,
            tm_mem=512, tm_expand=1024, t_cache=512)


def kernel(x_prompt, x_sample, cache_fox_k, cache_fox_v, cache_fox_logf, cache_mla_ckv, cache_mla_kpe,
           state_hgrn, cache_mem_k, cache_mem_v, mem_prompt, w_in, b_fox, g_mla_q, w_mla_uq, g_mla_kv,
           w_mla_ukv, g_hgrn_out, lb_hgrn, w_branch, w_out, w_xq, w_mem_k, w_mem_v, w_xo, w_up, w_down,
           g_norm):
    cfg = _CFG
    depth = w_in.shape[0]
    lb_p = jax.nn.softmax(lb_hgrn.astype(F32), axis=0)
    lb_all = jnp.cumsum(lb_p, axis=0) - lb_p[0]
    ws = [_prep_layer_weights(w_in[l], w_mla_uq[l], w_mla_ukv[l], w_branch[l], w_out[l], w_xq[l],
                              w_mem_k[l], w_mem_v[l], w_xo[l], w_up[l], w_down[l]) for l in range(depth)]

    def run_layer(x, bsz, s, pos0, l, mk, mv, past):
        return _layer(x, bsz, s, pos0, ws[l], lb_all[l], b_fox[l], g_mla_q[l], g_mla_kv[l],
                      g_hgrn_out[l], g_norm[l], mk, mv, past, cfg)

    bp, sp, _ = x_prompt.shape
    x = x_prompt.reshape(bp * sp, D_MODEL)
    mem = mem_prompt.reshape(bp * N_MEM, D_MODEL)
    p_states = []
    for l in range(depth):
        mk, mv = mem_kv(mem, ws[l]["wmem"], _tile(bp * N_MEM, cfg["tm_mem"]))
        x, st = run_layer(x, bp, sp, 0, l, mk, mv, None)
        p_states.append(st + (mk.reshape(bp, N_MEM, X_HEADS, X_DIM), mv.reshape(bp, N_MEM, X_HEADS, X_DIM)))
    y_prompt = x.reshape(bp, sp, D_MODEL)
    p_out = [jnp.stack(a) for a in zip(*p_states)]

    bs, ts, _ = x_sample.shape
    p = cache_fox_k.shape[2]
    clf = jnp.pad(cache_fox_logf.reshape(depth * bs, p, FOX_HEADS), ((0, 0), (0, 0), (0, LANE - FOX_HEADS)))
    _, c_cum, c_cumt = fox_cumsum(clf, _tile(p, cfg["t_cache"]))
    c_cum = c_cum.reshape(depth, bs, p, FOX_HEADS)
    c_cumt = c_cumt.reshape(depth, bs, FOX_HEADS, p)
    x = x_sample.reshape(bs * ts, D_MODEL)
    s_states = []
    for l in range(depth):
        c_kn, c_v = latent_expand(cache_mla_ckv[l].reshape(bs * p, MLA_KV_RANK), ws[l]["wukv"],
                                  _tile(bs * p, cfg["tm_expand"]))
        c_kp = jnp.pad(cache_mla_kpe[l].reshape(bs * p, MLA_ROPE), ((0, 0), (0, LANE - MLA_ROPE))).astype(BF16)
        past = (cache_fox_k[l].reshape(bs * p, BRANCH_W), cache_fox_v[l].reshape(bs * p, BRANCH_W),
                c_cum[l], c_cumt[l], c_kn, c_kp, c_v, state_hgrn[l])
        x, st = run_layer(x, bs, ts, p, l, cache_mem_k[l].reshape(bs * N_MEM, X_HEADS * X_DIM),
                          cache_mem_v[l].reshape(bs * N_MEM, X_HEADS * X_DIM), past)
        s_states.append(st)
    y_sample = x.reshape(bs, ts, D_MODEL)
    s_out = [jnp.stack(a) for a in zip(*s_states)]
    return (y_prompt, y_sample, *p_out, *s_out)
```

```python
import functools

import numpy as np
import jax
import jax.numpy as jnp
from jax import lax
from jax.experimental import pallas as pl
from jax.experimental.pallas import tpu as pltpu

F32 = jnp.float32
BF16 = jnp.bfloat16

D_MODEL = 1024
CHUNK = 64
N_MEM = 256
EPS = 1e-6
NEG_BIG = -1e30
EXP_CLIP = 80.0
FOX_HEADS = 8
FOX_DIM = 64
MLA_HEADS = 4
MLA_Q_RANK = 384
MLA_KV_RANK = 256
MLA_NOPE = 128
MLA_ROPE = 64
MLA_V = 128
ROPE_THETA = 10000.0
HG_HEADS = 4
HG_DK = 128
HG_DV = 128
X_HEADS = 4
X_DIM = 128
D_FF = 4 * D_MODEL
BRANCH_W = 512
IN_SIZES = (512, 512, 512, FOX_HEADS, MLA_Q_RANK, MLA_KV_RANK, MLA_ROPE, 512, 512, 512, 512,
            D_MODEL, D_MODEL, D_MODEL)

LANE = 128
SUB_BLOCK = 16
VMEM_LIMIT = 56 * 1024 * 1024
LOG2E = 1.4426950408889634

OFF_HQ, OFF_HF, OFF_HI, OFF_HG, OFF_GA, OFF_FQ, OFF_CQ, OFF_CKV, NP_IN = (
    0, 512, 1024, 1536, 2048, 5120, 5760, 6144, 6400)
IN_TN = 640
T_FK, T_FV, T_KPE, T_KPE_SW, T_FF, NT_IN = 0, 512, 1024, 1088, 1152, 1160


def _params(sem, vmem=VMEM_LIMIT):
    return pltpu.CompilerParams(dimension_semantics=sem, vmem_limit_bytes=vmem)


def _dot(a, b):
    return jnp.dot(a, b, preferred_element_type=F32)


def _dot_nt(a, b):
    return lax.dot_general(a, b, (((1,), (1,)), ((), ())), preferred_element_type=F32)


def _dot_tn(a, b):
    return lax.dot_general(a, b, (((0,), (0,)), ((), ())), preferred_element_type=F32)


def _rms(x, g):
    y = x * lax.rsqrt(jnp.mean(x * x, axis=-1, keepdims=True) + EPS)
    return y * g


def _log_sigmoid(z):
    return jnp.minimum(z, 0.0) - jnp.log(1.0 + jnp.exp(-jnp.abs(z)))


def _sigmoid(z):
    return 1.0 / (1.0 + jnp.exp(-z))


def _resident(shape):
    nd = len(shape)
    return pl.BlockSpec(shape, lambda *_: (0,) * nd, pipeline_mode=pl.Buffered(1))


def _in_proj_kernel(x_ref, g_ref, w_ref, wt_ref, cos_ref, sin_ref, bf_ref,
                    big_ref, kt_ref, vt_ref, kpe_ref, lf_ref, h_ref):
    j = pl.program_id(1)

    @pl.when(j == 0)
    def _():
        h = _rms(x_ref[...], g_ref[...]).astype(BF16)
        h_ref[...] = h
        yt = _dot_nt(wt_ref[...], h)
        kt_ref[0] = yt[T_FK:T_FK + BRANCH_W]
        vt_ref[0] = yt[T_FV:T_FV + BRANCH_W]
        kpe_ref[0] = (yt[T_KPE:T_KPE + MLA_ROPE] * cos_ref[...]
                      + yt[T_KPE_SW:T_KPE_SW + MLA_ROPE] * sin_ref[...])
        lf_ref[0] = _log_sigmoid(yt[T_FF:T_FF + FOX_HEADS] + bf_ref[...])

    big_ref[...] = _dot(h_ref[...], w_ref[j])


def in_proj(x, g, w3, wt, cos_t, sin_t, b_col, bsz, s, tm):
    m, k = x.shape
    nj, _, tn = w3.shape
    nt = s // tm
    ntab = cos_t.shape[1] // tm
    tspec = lambda rows: pl.BlockSpec((1, rows, tm), lambda i, j: (i // nt, 0, i % nt))
    return pl.pallas_call(
        _in_proj_kernel,
        out_shape=(jax.ShapeDtypeStruct((m, nj * tn), F32),
                   jax.ShapeDtypeStruct((bsz, BRANCH_W, s), F32),
                   jax.ShapeDtypeStruct((bsz, BRANCH_W, s), F32),
                   jax.ShapeDtypeStruct((bsz, MLA_ROPE, s), F32),
                   jax.ShapeDtypeStruct((bsz, FOX_HEADS, s), F32)),
        grid=(m // tm, nj),
        in_specs=[pl.BlockSpec((tm, k), lambda i, j: (i, 0)),
                  pl.BlockSpec((1, k), lambda i, j: (0, 0)),
                  _resident((nj, k, tn)),
                  _resident(wt.shape),
                  pl.BlockSpec((MLA_ROPE, tm), lambda i, j: (0, i % ntab)),
                  pl.BlockSpec((MLA_ROPE, tm), lambda i, j: (0, i % ntab)),
                  pl.BlockSpec((FOX_HEADS, 1), lambda i, j: (0, 0))],
        out_specs=(pl.BlockSpec((tm, tn), lambda i, j: (i, j)),
                   tspec(BRANCH_W), tspec(BRANCH_W), tspec(MLA_ROPE), tspec(FOX_HEADS)),
        scratch_shapes=[pltpu.VMEM((tm, k), BF16)],
        compiler_params=_params(("parallel", "arbitrary")),
        name="in_proj",
    )(x, g, w3, wt, cos_t, sin_t, b_col)


def _mla_prep_kernel(cq_ref, ckv_ref, cs_ref, sn_ref, gq_ref, gkv_ref, wuq_ref, wkt_ref, wv_ref,
                     qx_ref, knt_ref, v_ref, ckvn_ref):
    cos_t = cs_ref[...]
    sin_t = sn_ref[...]
    qn = _rms(cq_ref[...], gq_ref[...]).astype(BF16)
    qall = _dot(qn, wuq_ref[...]) * ((MLA_NOPE + MLA_ROPE) ** -0.5 * LOG2E)
    for h in range(MLA_HEADS):
        lo = h * LANE
        qr = (qall[:, 512 + lo:512 + lo + LANE] * cos_t
              + qall[:, 1024 + lo:1024 + lo + LANE] * sin_t)
        qx_ref[:, 2 * lo:2 * lo + LANE] = qall[:, lo:lo + LANE].astype(BF16)
        qx_ref[:, 2 * lo + LANE:2 * lo + 2 * LANE] = qr.astype(BF16)
    ckvn = _rms(ckv_ref[...], gkv_ref[...])
    ckvn_ref[...] = ckvn
    cb = ckvn.astype(BF16)
    knt_ref[0] = _dot_nt(wkt_ref[...], cb).astype(BF16)
    v_ref[...] = _dot(cb, wv_ref[...]).astype(BF16)


def mla_prep(big, cos_t, sin_t, gq, gkv, wuq, wkt, wv, bsz, s, tm):
    m = big.shape[0]
    nt = s // tm
    ntab = cos_t.shape[0] // tm
    row = lambda w: pl.BlockSpec((tm, w), lambda i: (i, 0))
    full = lambda a: pl.BlockSpec(a.shape, lambda i: (0,) * a.ndim)
    return pl.pallas_call(
        _mla_prep_kernel,
        out_shape=(jax.ShapeDtypeStruct((m, 1024), BF16),
                   jax.ShapeDtypeStruct((bsz, 512, s), BF16),
                   jax.ShapeDtypeStruct((m, 512), BF16),
                   jax.ShapeDtypeStruct((m, MLA_KV_RANK), F32)),
        grid=(m // tm,),
        in_specs=[pl.BlockSpec((tm, MLA_Q_RANK), lambda i: (i, OFF_CQ // MLA_Q_RANK)),
                  pl.BlockSpec((tm, MLA_KV_RANK), lambda i: (i, OFF_CKV // MLA_KV_RANK)),
                  pl.BlockSpec((tm, LANE), lambda i: (i % ntab, 0)),
                  pl.BlockSpec((tm, LANE), lambda i: (i % ntab, 0)),
                  full(gq), full(gkv), full(wuq), full(wkt), full(wv)],
        out_specs=(row(1024), pl.BlockSpec((1, 512, tm), lambda i: (i // nt, 0, i % nt)),
                   row(512), row(MLA_KV_RANK)),
        compiler_params=_params(("parallel",)),
        name="mla_prep",
    )(big, big, cos_t, sin_t, gq, gkv, wuq, wkt, wv)


def _cumsum_kernel(x_ref, c_ref, carry_ref):
    tm = x_ref.shape[2]

    @pl.when(pl.program_id(1) == 0)
    def _():
        carry_ref[...] = jnp.zeros_like(carry_ref)

    r = lax.broadcasted_iota(jnp.int32, (tm, tm), 0)
    c = lax.broadcasted_iota(jnp.int32, (tm, tm), 1)
    upper = (r <= c).astype(F32)
    cum = (jnp.dot(x_ref[0], upper, preferred_element_type=F32, precision=lax.Precision.HIGHEST)
           + carry_ref[:, 0:1])
    c_ref[0] = cum
    carry_ref[...] = jnp.broadcast_to(cum[:, tm - 1:tm], carry_ref.shape)


def fox_cumsum(x, tm):
    bsz, h, s = x.shape
    return pl.pallas_call(
        _cumsum_kernel,
        out_shape=jax.ShapeDtypeStruct((bsz, h, s), F32),
        grid=(bsz, s // tm),
        in_specs=[pl.BlockSpec((1, h, tm), lambda i, j: (i, 0, j))],
        out_specs=pl.BlockSpec((1, h, tm), lambda i, j: (i, 0, j)),
        scratch_shapes=[pltpu.VMEM((h, LANE), F32)],
        compiler_params=_params(("parallel", "arbitrary")),
        name="fox_cumsum",
    )(x)


def _flash_step(state, s, pv):
    m, acc = state
    m_new = jnp.maximum(m, jnp.max(s, axis=1, keepdims=True))
    alpha = jnp.exp2(m - m_new)
    p = jnp.exp2(s - m_new).astype(BF16)
    return m_new, alpha * acc + pv(p)


def _pair_rows_mask(hh):
    sub = lax.broadcasted_iota(jnp.int32, (LANE, 1), 0)
    return (sub < FOX_DIM) if hh == 0 else (sub >= FOX_DIM)


def _fox_finish(acc0, acc1):
    lane = lax.broadcasted_iota(jnp.int32, (1, LANE), 1)
    o0 = acc0 / pltpu.roll(acc0, FOX_DIM, axis=1)
    o1 = acc1 / pltpu.roll(acc1, FOX_DIM, axis=1)
    return jnp.where(lane < FOX_DIM, o0, o1)


def _fox_prompt_kernel(q_ref, kt_ref, vt_ref, ct_ref, o_ref, *, t):
    hp = pl.program_id(1)
    qi = pl.program_id(2)
    lane = lax.broadcasted_iota(jnp.int32, (1, LANE), 1)
    lo = lane < FOX_DIM
    q = q_ref[...] * (FOX_DIM ** -0.5 * LOG2E)
    qs = (jnp.where(lo, q, 0.0).astype(BF16), jnp.where(lo, 0.0, q).astype(BF16))
    d0 = pl.multiple_of(qi * t, t)
    cref = tuple(ct_ref[0, 2 * hp + hh, :, pl.ds(d0, LANE)][:, 0:1] for hh in range(2))

    def chunk(j, carry, causal):
        c0 = pl.multiple_of(j * t, t)
        kt = kt_ref[0, :, pl.ds(c0, t)].astype(BF16)
        vt = vt_ref[0, :, pl.ds(c0, t)]
        out = []
        for hh in range(2):
            bias = (cref[hh] - ct_ref[0, 2 * hp + hh, :, pl.ds(c0, t)]) * LOG2E
            s = _dot(qs[hh], kt) + bias
            if causal is not None:
                s = jnp.where(causal, s, NEG_BIG)
            vt_h = jnp.where(_pair_rows_mask(hh), vt, 1.0).astype(BF16)
            out.append(_flash_step(carry[hh], s, lambda p: _dot_nt(p, vt_h)))
        return tuple(out)

    init = tuple((jnp.full((t, 1), NEG_BIG, F32), jnp.zeros((t, LANE), F32)) for _ in range(2))
    st = lax.fori_loop(0, qi, lambda j, c: chunk(j, c, None), init)
    causal = (lax.broadcasted_iota(jnp.int32, (t, t), 0) >= lax.broadcasted_iota(jnp.int32, (t, t), 1))
    st = chunk(qi, st, causal)
    o_ref[...] = _fox_finish(st[0][1], st[1][1]).astype(BF16)


def fox_prompt(big, kt, vt, cumt, bsz, s, t):
    m = big.shape[0]
    nq = s // t
    return pl.pallas_call(
        functools.partial(_fox_prompt_kernel, t=t),
        out_shape=jax.ShapeDtypeStruct((m, BRANCH_W), BF16),
        grid=(bsz, FOX_HEADS // 2, nq),
        in_specs=[pl.BlockSpec((t, LANE), lambda b, h, i: (b * nq + i, OFF_FQ // LANE + h)),
                  pl.BlockSpec((1, LANE, s), lambda b, h, i: (b, h, 0)),
                  pl.BlockSpec((1, LANE, s), lambda b, h, i: (b, h, 0)),
                  pl.BlockSpec((1, FOX_HEADS, 1, s), lambda b, h, i: (b, 0, 0, 0))],
        out_specs=pl.BlockSpec((t, LANE), lambda b, h, i: (b * nq + i, h)),
        compiler_params=_params(("parallel", "parallel", "arbitrary")),
        name="fox_prompt",
    )(big, kt, vt, cumt.reshape(bsz, FOX_HEADS, 1, s))


def _mla_keys(knt, kpt):
    n = knt.shape[1]
    return jnp.concatenate([knt, kpt.astype(BF16), jnp.zeros((LANE - MLA_ROPE, n), BF16)], axis=0)


def _mla_prompt_kernel(q_ref, knt_ref, kpt_ref, v_ref, o_ref, *, t):
    qi = pl.program_id(2)
    q = q_ref[...]

    def chunk(j, carry, mask):
        c0 = pl.multiple_of(j * t, t)
        s = _dot(q, _mla_keys(knt_ref[0, :, pl.ds(c0, t)], kpt_ref[0, :, pl.ds(c0, t)]))
        if mask is not None:
            s = jnp.where(mask, s, NEG_BIG)
        v = v_ref[pl.ds(c0, t), :]
        m, l, acc = carry
        m_new = jnp.maximum(m, jnp.max(s, axis=1, keepdims=True))
        alpha = jnp.exp2(m - m_new)
        p = jnp.exp2(s - m_new)
        return m_new, alpha * l + jnp.sum(p, axis=1, keepdims=True), alpha * acc + _dot(p.astype(BF16), v)

    init = (jnp.full((t, 1), NEG_BIG, F32), jnp.zeros((t, 1), F32), jnp.zeros((t, LANE), F32))
    st = lax.fori_loop(0, qi, lambda j, c: chunk(j, c, None), init)
    rc = lax.broadcasted_iota(jnp.int32, (t, t), 0) // CHUNK
    cc = lax.broadcasted_iota(jnp.int32, (t, t), 1) // CHUNK
    _, l, acc = chunk(qi, st, rc >= cc)
    o_ref[...] = (acc / l).astype(BF16)


def mla_prompt(qx, knt, kpt, v, bsz, s, t):
    m = qx.shape[0]
    nq = s // t
    return pl.pallas_call(
        functools.partial(_mla_prompt_kernel, t=t),
        out_shape=jax.ShapeDtypeStruct((m, BRANCH_W), BF16),
        grid=(bsz, MLA_HEADS, nq),
        in_specs=[pl.BlockSpec((t, 2 * LANE), lambda b, h, i: (b * nq + i, h)),
                  pl.BlockSpec((1, LANE, s), lambda b, h, i: (b, h, 0)),
                  pl.BlockSpec((1, MLA_ROPE, s), lambda b, h, i: (b, 0, 0)),
                  pl.BlockSpec((s, LANE), lambda b, h, i: (b, h))],
        out_specs=pl.BlockSpec((t, LANE), lambda b, h, i: (b * nq + i, h)),
        compiler_params=_params(("parallel", "parallel", "arbitrary")),
        name="mla_prompt",
    )(qx, knt, kpt, v)


def _fox_sample_kernel(q_ref, ktn_ref, vtn_ref, ktc_ref, vtc_ref, ctn_ref, ctc_ref, o_ref, *, t, p):
    hp = pl.program_id(1)
    lane = lax.broadcasted_iota(jnp.int32, (1, LANE), 1)
    lo = lane < FOX_DIM
    q = q_ref[...] * (FOX_DIM ** -0.5 * LOG2E)
    kt_c = ktc_ref[0].astype(BF16)
    kt_n = ktn_ref[0].astype(BF16)
    vt_c = vtc_ref[0]
    vt_n = vtn_ref[0]
    causal = (lax.broadcasted_iota(jnp.int32, (t, t), 0) >= lax.broadcasted_iota(jnp.int32, (t, t), 1))
    accs = []
    for hh in range(2):
        head = 2 * hp + hh
        qh = (jnp.where(lo, q, 0.0) if hh == 0 else jnp.where(lo, 0.0, q)).astype(BF16)
        cc = ctc_ref[0, pl.ds(head, 1), :]
        ctot = cc[:, p - 1:p]
        s_c = _dot(qh, kt_c) + (ctot - cc) * LOG2E
        s_n = _dot(qh, kt_n) - ctn_ref[0, pl.ds(head, 1), :] * LOG2E
        s_n = jnp.where(causal, s_n, NEG_BIG)
        m = jnp.maximum(jnp.max(s_c, axis=1, keepdims=True), jnp.max(s_n, axis=1, keepdims=True))
        rows = _pair_rows_mask(hh)
        accs.append(_dot_nt(jnp.exp2(s_c - m).astype(BF16), jnp.where(rows, vt_c, 1.0).astype(BF16))
                    + _dot_nt(jnp.exp2(s_n - m).astype(BF16), jnp.where(rows, vt_n, 1.0).astype(BF16)))
    o_ref[...] = _fox_finish(accs[0], accs[1]).astype(BF16)


def fox_sample(big, kt_n, vt_n, kt_c, vt_c, cumt_n, cumt_c, bsz, t, p, layer):
    cidx = lambda b, h: (layer * bsz + b, h, 0)
    return pl.pallas_call(
        functools.partial(_fox_sample_kernel, t=t, p=p),
        out_shape=jax.ShapeDtypeStruct((bsz * t, BRANCH_W), BF16),
        grid=(bsz, FOX_HEADS // 2),
        in_specs=[pl.BlockSpec((t, LANE), lambda b, h: (b, OFF_FQ // LANE + h)),
                  pl.BlockSpec((1, LANE, t), lambda b, h: (b, h, 0)),
                  pl.BlockSpec((1, LANE, t), lambda b, h: (b, h, 0)),
                  pl.BlockSpec((1, LANE, p), cidx),
                  pl.BlockSpec((1, LANE, p), cidx),
                  pl.BlockSpec((1, FOX_HEADS, t), lambda b, h: (b, 0, 0)),
                  pl.BlockSpec((1, FOX_HEADS, p), lambda b, h: (layer * bsz + b, 0, 0))],
        out_specs=pl.BlockSpec((t, LANE), lambda b, h: (b, h)),
        compiler_params=_params(("parallel", "parallel")),
        name="fox_sample",
    )(big, kt_n, vt_n, kt_c, vt_c, cumt_n, cumt_c)


def _mla_sample_kernel(q_ref, kntn_ref, kptn_ref, vn_ref, kntc_ref, kptc_ref, vc_ref, o_ref, *, t, p):
    q = q_ref[...]
    s_c = _dot(q, _mla_keys(kntc_ref[0], kptc_ref[0]))
    s_n = _dot(q, _mla_keys(kntn_ref[0], kptn_ref[0]))
    qc = (p + lax.broadcasted_iota(jnp.int32, (t, t), 0)) // CHUNK
    kc = (p + lax.broadcasted_iota(jnp.int32, (t, t), 1)) // CHUNK
    s_n = jnp.where(qc >= kc, s_n, NEG_BIG)
    m = jnp.maximum(jnp.max(s_c, axis=1, keepdims=True), jnp.max(s_n, axis=1, keepdims=True))
    p_c = jnp.exp2(s_c - m)
    p_n = jnp.exp2(s_n - m)
    l = jnp.sum(p_c, axis=1, keepdims=True) + jnp.sum(p_n, axis=1, keepdims=True)
    o = _dot(p_c.astype(BF16), vc_ref[...]) + _dot(p_n.astype(BF16), vn_ref[...])
    o_ref[...] = (o / l).astype(BF16)


def mla_sample(qx, knt_n, kpt_n, v_n, knt_c, kpt_c, v_c, bsz, t, p, layer):
    assert (p - 1) // CHUNK <= p // CHUNK
    return pl.pallas_call(
        functools.partial(_mla_sample_kernel, t=t, p=p),
        out_shape=jax.ShapeDtypeStruct((bsz * t, BRANCH_W), BF16),
        grid=(bsz, MLA_HEADS),
        in_specs=[pl.BlockSpec((t, 2 * LANE), lambda b, h: (b, h)),
                  pl.BlockSpec((1, LANE, t), lambda b, h: (b, h, 0)),
                  pl.BlockSpec((1, MLA_ROPE, t), lambda b, h: (b, 0, 0)),
                  pl.BlockSpec((t, LANE), lambda b, h: (b, h)),
                  pl.BlockSpec((1, LANE, p), lambda b, h: (b, h, 0)),
                  pl.BlockSpec((1, MLA_ROPE, p), lambda b, h: (layer * bsz + b, 0, 0)),
                  pl.BlockSpec((p, LANE), lambda b, h: (b, h))],
        out_specs=pl.BlockSpec((t, LANE), lambda b, h: (b, h)),
        compiler_params=_params(("parallel", "parallel")),
        name="mla_sample",
    )(qx, knt_n, kpt_n, v_n, knt_c, kpt_c, v_c)


def _latent_expand_kernel(c_ref, wkt_ref, wv_ref, knt_ref, v_ref):
    cb = c_ref[...].astype(BF16)
    knt_ref[0] = _dot_nt(wkt_ref[...], cb).astype(BF16)
    v_ref[...] = _dot(cb, wv_ref[...]).astype(BF16)


def latent_expand(ckv, wkt, wv, bsz, p, tm):
    m = ckv.shape[0]
    nt = p // tm
    return pl.pallas_call(
        _latent_expand_kernel,
        out_shape=(jax.ShapeDtypeStruct((bsz, 512, p), BF16), jax.ShapeDtypeStruct((m, 512), BF16)),
        grid=(m // tm,),
        in_specs=[pl.BlockSpec((tm, MLA_KV_RANK), lambda i: (i, 0)),
                  pl.BlockSpec(wkt.shape, lambda i: (0, 0)), pl.BlockSpec(wv.shape, lambda i: (0, 0))],
        out_specs=(pl.BlockSpec((1, 512, tm), lambda i: (i // nt, 0, i % nt)),
                   pl.BlockSpec((tm, 512), lambda i: (i, 0))),
        compiler_params=_params(("parallel",)),
        name="latent_expand",
    )(ckv, wkt, wv)


def _hgrn_gates(z, lb, tri):
    logf = _log_sigmoid(z) + jnp.log(1.0 + lb * jnp.exp(jnp.minimum(-z, EXP_CLIP)))
    k = (1.0 - lb) * (1.0 / (1.0 + jnp.exp(z)))
    lc = jnp.dot(tri, logf, preferred_element_type=F32, precision=lax.Precision.HIGHEST) * LOG2E
    return k, lc


def _hgrn_head(q, k, lc, v, st):
    ln = q.shape[0]
    nb = ln // SUB_BLOCK
    v_b = v.astype(BF16)
    rows = lambda a, i: a[i * SUB_BLOCK:(i + 1) * SUB_BLOCK, :]
    lcb = [jnp.zeros((1, LANE), F32)] + [lc[i * SUB_BLOCK - 1:i * SUB_BLOCK, :] for i in range(1, nb)]
    lcb_rows = jnp.concatenate([jnp.broadcast_to(b, (SUB_BLOCK, LANE)) for b in lcb], axis=0)
    qh = (q * jnp.exp2(lc - lcb_rows)).astype(BF16)
    qe = (q * jnp.exp2(lc)).astype(BF16)
    o = _dot_nt(qe, st.astype(BF16))
    off = [jnp.zeros((SUB_BLOCK, LANE), F32)]
    for i in range(1, nb):
        n = i * SUB_BLOCK
        kt = (k[:n, :] * jnp.exp2(lcb[i] - lc[:n, :])).astype(BF16)
        a = _dot_nt(rows(qh, i), kt)
        off.append(_dot(a.astype(BF16), v_b[:n, :]))
    o = o + jnp.concatenate(off, axis=0)
    pieces = []
    for i in range(nb):
        q_i, k_i, lc_i = rows(q, i), rows(k, i), rows(lc, i)
        cols = []
        for s in range(SUB_BLOCK):
            d = jnp.minimum(lc_i - lc_i[s:s + 1, :], 0.0)
            cols.append(((q_i * k_i[s:s + 1, :]) * jnp.exp2(d)).astype(BF16))
        pieces.append(jnp.concatenate(cols, axis=1))
    last = lc[ln - 1:ln, :]
    kdec = (k * jnp.exp2(last - lc)).astype(BF16)
    st_new = st * jnp.exp2(last) + _dot_tn(v_b, kdec)
    return o, st_new, pieces


def _hgrn_kernel(*refs, ln, nchunk, has_init):
    refs = list(refs)
    hq_ref, hf_ref, hi_ref, hg_ref, lb_ref, go_ref, sel_ref = refs[:7]
    s0_ref = refs[7] if has_init else None
    o_ref, sout_ref, st_ref, k_ref, lc_ref, acc_ref = refs[-6:]
    step = pl.program_id(1)
    nb = ln // SUB_BLOCK
    nrows = ln * nchunk

    @pl.when(step == 0)
    def _():
        for h in range(HG_HEADS):
            st_ref[h] = s0_ref[0, h].T if has_init else jnp.zeros((HG_DV, HG_DK), F32)

    ri = lax.broadcasted_iota(jnp.int32, (nrows, nrows), 0)
    ci = lax.broadcasted_iota(jnp.int32, (nrows, nrows), 1)
    tri = ((ri >= ci) & (ri // ln == ci // ln)).astype(F32)
    for h in range(HG_HEADS):
        cs = slice(h * LANE, (h + 1) * LANE)
        k_ref[:, cs], lc_ref[:, cs] = _hgrn_gates(hf_ref[:, cs], lb_ref[:, cs], tri)

    tpos = lax.broadcasted_iota(jnp.int32, (SUB_BLOCK, LANE), 0)
    spos = lax.broadcasted_iota(jnp.int32, (SUB_BLOCK, LANE), 1)
    pair_ok = tpos >= spos

    def chunk(c, carry):
        r0 = pl.multiple_of(c * ln, ln)
        partial, vals, pieces = [], [], []
        for h in range(HG_HEADS):
            cs = slice(h * LANE, (h + 1) * LANE)
            v = hi_ref[pl.ds(r0, ln), cs]
            o, st_new, pcs = _hgrn_head(hq_ref[pl.ds(r0, ln), cs], k_ref[pl.ds(r0, ln), cs],
                                        lc_ref[pl.ds(r0, ln), cs], v, st_ref[h])
            st_ref[h] = st_new
            partial.append(o)
            vals.append(v.astype(BF16))
            pieces += pcs
        a_all = _dot(jnp.concatenate(pieces, axis=0), sel_ref[...])
        for h in range(HG_HEADS):
            cs = slice(h * LANE, (h + 1) * LANE)
            diag = []
            for i in range(nb):
                base = (h * nb + i) * SUB_BLOCK
                a = jnp.where(pair_ok, a_all[base:base + SUB_BLOCK, :], 0.0)
                diag.append(_dot(a[:, :SUB_BLOCK].astype(BF16),
                                 vals[h][i * SUB_BLOCK:(i + 1) * SUB_BLOCK, :]))
            acc_ref[pl.ds(r0, ln), cs] = partial[h] + jnp.concatenate(diag, axis=0)
        return carry

    lax.fori_loop(0, nchunk, chunk, 0)

    for h in range(HG_HEADS):
        cs = slice(h * LANE, (h + 1) * LANE)
        o_ref[:, cs] = (_rms(acc_ref[:, cs], go_ref[...]) * _sigmoid(hg_ref[:, cs])).astype(BF16)

    @pl.when(step == pl.num_programs(1) - 1)
    def _():
        for h in range(HG_HEADS):
            sout_ref[0, h] = st_ref[h].T


def hgrn(big, lb, g_out, bsz, s, ln, rows, s0=None):
    m = big.shape[0]
    ns = s // rows
    has_init = s0 is not None
    sel = (np.arange(SUB_BLOCK * LANE)[:, None] // LANE == np.arange(LANE)[None, :])
    sel = jnp.asarray(sel, BF16)
    blk = lambda off: pl.BlockSpec((rows, BRANCH_W), lambda b, i: (b * ns + i, off // BRANCH_W))
    ins = [big, big, big, big, lb, g_out, sel]
    specs = [blk(OFF_HQ), blk(OFF_HF), blk(OFF_HI), blk(OFF_HG),
             pl.BlockSpec((1, BRANCH_W), lambda b, i: (0, 0)),
             pl.BlockSpec((1, HG_DV), lambda b, i: (0, 0)),
             pl.BlockSpec(sel.shape, lambda b, i: (0, 0))]
    if has_init:
        ins.append(s0)
        specs.append(pl.BlockSpec((1, HG_HEADS, HG_DK, HG_DV), lambda b, i: (b, 0, 0, 0)))
    return pl.pallas_call(
        functools.partial(_hgrn_kernel, ln=ln, nchunk=rows // ln, has_init=has_init),
        out_shape=(jax.ShapeDtypeStruct((m, BRANCH_W), BF16),
                   jax.ShapeDtypeStruct((bsz, HG_HEADS, HG_DK, HG_DV), F32)),
        grid=(bsz, ns),
        in_specs=specs,
        out_specs=(pl.BlockSpec((rows, BRANCH_W), lambda b, i: (b * ns + i, 0)),
                   pl.BlockSpec((1, HG_HEADS, HG_DK, HG_DV), lambda b, i: (b, 0, 0, 0))),
        scratch_shapes=[pltpu.VMEM((HG_HEADS, HG_DV, HG_DK), F32),
                        pltpu.VMEM((rows, BRANCH_W), F32),
                        pltpu.VMEM((rows, BRANCH_W), F32),
                        pltpu.VMEM((rows, BRANCH_W), F32)],
        compiler_params=_params(("parallel", "arbitrary")),
        name="hgrn",
    )(*ins)


def _merge_kernel(of_ref, om_ref, oh_ref, ga_ref, gb_ref, gc_ref, x_ref, wb_ref, wo_ref, g_ref, o_ref):
    merged = (_sigmoid(ga_ref[...]) * _dot(of_ref[...], wb_ref[0])
              + _sigmoid(gb_ref[...]) * _dot(om_ref[...], wb_ref[1])
              + _sigmoid(gc_ref[...]) * _dot(oh_ref[...], wb_ref[2]))
    y = _dot(merged.astype(BF16), wo_ref[...])
    o_ref[...] = x_ref[...] + _rms(y, g_ref[...])


def merge_out(o_fox, o_mla, o_hg, big, x, wb, wo, g, tm):
    m = x.shape[0]
    row = lambda w: pl.BlockSpec((tm, w), lambda i: (i, 0))
    gate = lambda off: pl.BlockSpec((tm, D_MODEL), lambda i: (i, off // D_MODEL))
    return pl.pallas_call(
        _merge_kernel,
        out_shape=jax.ShapeDtypeStruct((m, D_MODEL), F32),
        grid=(m // tm,),
        in_specs=[row(BRANCH_W), row(BRANCH_W), row(BRANCH_W),
                  gate(OFF_GA), gate(OFF_GA + D_MODEL), gate(OFF_GA + 2 * D_MODEL),
                  row(D_MODEL), _resident(wb.shape), _resident(wo.shape),
                  pl.BlockSpec((1, D_MODEL), lambda i: (0, 0))],
        out_specs=row(D_MODEL),
        compiler_params=_params(("parallel",)),
        name="merge_out",
    )(o_fox, o_mla, o_hg, big, big, big, x, wb, wo, g)


def _matmul2_kernel(x_ref, w_ref, a_ref, b_ref):
    y = _dot(x_ref[...].astype(BF16), w_ref[...])
    n = a_ref.shape[1]
    a_ref[...] = y[:, :n]
    b_ref[...] = y[:, n:]


def mem_kv(mem, w, tm):
    m, k = mem.shape
    n = w.shape[1] // 2
    row = lambda w_: pl.BlockSpec((tm, w_), lambda i: (i, 0))
    return pl.pallas_call(
        _matmul2_kernel,
        out_shape=(jax.ShapeDtypeStruct((m, n), F32), jax.ShapeDtypeStruct((m, n), F32)),
        grid=(m // tm,),
        in_specs=[row(k), _resident(w.shape)],
        out_specs=(row(n), row(n)),
        compiler_params=_params(("parallel",)),
        name="mem_kv",
    )(mem, w)


def _cross_kernel(x_ref, mk_ref, mv_ref, wq_ref, wo_ref, g2_ref, g3_ref, o_ref):
    x = x_ref[...]
    h = _rms(x, g2_ref[...]).astype(BF16)
    q = _dot(h, wq_ref[...])
    scale = X_DIM ** -0.5
    outs = []
    for hd in range(X_HEADS):
        cs = slice(hd * X_DIM, (hd + 1) * X_DIM)
        s = _dot_nt(q[:, cs].astype(BF16), mk_ref[:, cs].astype(BF16)) * scale
        p = jnp.exp(s - jnp.max(s, axis=1, keepdims=True))
        l = jnp.sum(p, axis=1, keepdims=True)
        outs.append(_dot(p.astype(BF16), mv_ref[:, cs].astype(BF16)) / l)
    ox = jnp.concatenate(outs, axis=1).astype(BF16)
    o_ref[...] = x + _rms(_dot(ox, wo_ref[...]), g3_ref[...])


def cross_block(x, mk, mv, wq, wo, g2, g3, bsz, s, tm):
    m = x.shape[0]
    nt = s // tm
    vec = pl.BlockSpec((1, D_MODEL), lambda b, i: (0, 0))
    return pl.pallas_call(
        _cross_kernel,
        out_shape=jax.ShapeDtypeStruct((m, D_MODEL), F32),
        grid=(bsz, nt),
        in_specs=[pl.BlockSpec((tm, D_MODEL), lambda b, i: (b * nt + i, 0)),
                  pl.BlockSpec((N_MEM, X_HEADS * X_DIM), lambda b, i: (b, 0)),
                  pl.BlockSpec((N_MEM, X_HEADS * X_DIM), lambda b, i: (b, 0)),
                  _resident(wq.shape), _resident(wo.shape), vec, vec],
        out_specs=pl.BlockSpec((tm, D_MODEL), lambda b, i: (b * nt + i, 0)),
        compiler_params=_params(("parallel", "parallel")),
        name="cross_attn",
    )(x, mk, mv, wq, wo, g2, g3)


def _mlp_kernel(x_ref, wu_ref, wd_ref, g4_ref, g5_ref, o_ref, h_ref, acc_ref):
    j = pl.program_id(1)

    @pl.when(j == 0)
    def _():
        h_ref[...] = _rms(x_ref[...], g4_ref[...]).astype(BF16)
        acc_ref[...] = jnp.zeros_like(acc_ref)

    u = jnp.square(jnp.maximum(_dot(h_ref[...], wu_ref[j]), 0.0)).astype(BF16)
    acc_ref[...] += _dot(u, wd_ref[j])

    @pl.when(j == pl.num_programs(1) - 1)
    def _():
        o_ref[...] = x_ref[...] + _rms(acc_ref[...], g5_ref[...])


def mlp_block(x, wu3, wd3, g4, g5, tm):
    m = x.shape[0]
    nj = wu3.shape[0]
    vec = pl.BlockSpec((1, D_MODEL), lambda i, j: (0, 0))
    return pl.pallas_call(
        _mlp_kernel,
        out_shape=jax.ShapeDtypeStruct((m, D_MODEL), F32),
        grid=(m // tm, nj),
        in_specs=[pl.BlockSpec((tm, D_MODEL), lambda i, j: (i, 0)),
                  _resident(wu3.shape), _resident(wd3.shape), vec, vec],
        out_specs=pl.BlockSpec((tm, D_MODEL), lambda i, j: (i, 0)),
        scratch_shapes=[pltpu.VMEM((tm, D_MODEL), BF16), pltpu.VMEM((tm, D_MODEL), F32)],
        compiler_params=_params(("parallel", "arbitrary")),
        name="mlp",
    )(x, wu3, wd3, g4, g5)


def _prep_layer_weights(w_in, w_mla_uq, w_mla_ukv, w_branch, w_out, w_xq, w_mem_k, w_mem_v, w_xo,
                        w_up, w_down):
    idx = np.cumsum((0,) + IN_SIZES)
    seg = lambda i: w_in[:, idx[i]:idx[i + 1]]
    fq, fk, fv, ff, cq, ckv, kpe, hq, hf, hi, hg, ga, gb, gc = (seg(i) for i in range(14))
    half = MLA_ROPE // 2
    kpe_sw = jnp.concatenate([kpe[:, half:], kpe[:, :half]], axis=1)
    pad = jnp.zeros((D_MODEL, OFF_CQ - OFF_FQ - BRANCH_W), w_in.dtype)
    w_p = jnp.concatenate([hq, hf, hi, hg, ga, gb, gc, fq, pad, cq, ckv], axis=1).astype(BF16)
    w_in3 = w_p.reshape(D_MODEL, NP_IN // IN_TN, IN_TN).transpose(1, 0, 2)
    w_t = jnp.concatenate([fk, fv, kpe, kpe_sw, ff], axis=1).T.astype(BF16)
    hd = MLA_NOPE + MLA_ROPE
    zq = jnp.zeros((MLA_Q_RANK, LANE - MLA_ROPE), w_mla_uq.dtype)
    nope, rope_n, rope_s = [], [], []
    for h in range(MLA_HEADS):
        base = h * hd
        nope.append(w_mla_uq[:, base:base + MLA_NOPE])
        x1 = w_mla_uq[:, base + MLA_NOPE:base + MLA_NOPE + half]
        x2 = w_mla_uq[:, base + MLA_NOPE + half:base + hd]
        rope_n += [x1, x2, zq]
        rope_s += [x2, x1, zq]
    wuq = jnp.concatenate(nope + rope_n + rope_s, axis=1).astype(BF16)
    kvd = MLA_NOPE + MLA_V
    wkt = jnp.concatenate([w_mla_ukv[:, h * kvd:h * kvd + MLA_NOPE] for h in range(MLA_HEADS)],
                          axis=1).T.astype(BF16)
    wv = jnp.concatenate([w_mla_ukv[:, h * kvd + MLA_NOPE:(h + 1) * kvd] for h in range(MLA_HEADS)],
                         axis=1).astype(BF16)
    nff = D_FF // D_MODEL
    return dict(
        w_in3=w_in3, w_t=w_t, wuq=wuq, wkt=wkt, wv=wv,
        wb=w_branch.astype(BF16), wo=w_out.astype(BF16), wxq=w_xq.astype(BF16), wxo=w_xo.astype(BF16),
        wmem=jnp.concatenate([w_mem_k, w_mem_v], axis=1).astype(BF16),
        wu3=w_up.astype(BF16).reshape(D_MODEL, nff, D_MODEL).transpose(1, 0, 2),
        wd3=w_down.astype(BF16).reshape(nff, D_MODEL, D_MODEL))


def _rope_tables(pos, reps):
    half = MLA_ROPE // 2
    freq = ROPE_THETA ** (-jnp.arange(half, dtype=F32) / half)
    ang = pos.astype(F32)[:, None] * freq[None, :]
    cos, sin = jnp.cos(ang), jnp.sin(ang)
    z = jnp.zeros((pos.shape[0], LANE - MLA_ROPE), F32)
    cos_r = jnp.tile(jnp.concatenate([cos, cos, z], axis=1), (reps, 1))
    sin_r = jnp.tile(jnp.concatenate([-sin, sin, z], axis=1), (reps, 1))
    return cos_r, sin_r, cos_r[:, :MLA_ROPE].T, sin_r[:, :MLA_ROPE].T


def _tile(n, pref):
    t = min(n, pref)
    assert n % t == 0
    return t


def _layer(x, bsz, s, pos0, w, lb, b_fox, g_q, g_kv, g_hout, g_norm, mem_k, mem_v, past, cfg, layer):
    m = bsz * s
    g = lambda i: g_norm[i][None, :]
    tm_in = _tile(s, cfg["tm_in"])
    tm_p = _tile(s, cfg["tm_prep"])
    cos_r, sin_r, cos_c, sin_c = _rope_tables(pos0 + jnp.arange(s), 1)
    big, kt, vt, kpet, logft = in_proj(x, g(0), w["w_in3"], w["w_t"], cos_c, sin_c, b_fox[:, None],
                                       bsz, s, tm_in)
    qx, knt, v, ckv_n = mla_prep(big, cos_r, sin_r, g_q[None, :], g_kv[None, :],
                                 w["wuq"], w["wkt"], w["wv"], bsz, s, tm_p)
    if past is None:
        t = _tile(s, cfg["t_attn"])
        cumt = fox_cumsum(logft, _tile(s, cfg["t_cum"]))
        o_fox = fox_prompt(big, kt, vt, cumt, bsz, s, t)
        o_mla = mla_prompt(qx, knt, kpet, v, bsz, s, t)
        o_hg, hg_state = hgrn(big, lb[None, :], g_hout[None, :], bsz, s, CHUNK,
                              _tile(s, cfg["hg_rows"]))
    else:
        c_kt, c_vt, c_cumt, c_knt, c_kpt, c_v, c_hg = past
        p = c_kt.shape[2]
        cumt = fox_cumsum(logft, s)
        o_fox = fox_sample(big, kt, vt, c_kt, c_vt, cumt, c_cumt, bsz, s, p, layer)
        o_mla = mla_sample(qx, knt, kpet, v, c_knt, c_kpt, c_v, bsz, s, p, layer)
        o_hg, hg_state = hgrn(big, lb[None, :], g_hout[None, :], bsz, s, s, s, s0=c_hg)
    x = merge_out(o_fox, o_mla, o_hg, big, x, w["wb"], w["wo"], g(1), _tile(m, cfg["tm_merge"]))
    x = cross_block(x, mem_k, mem_v, w["wxq"], w["wxo"], g(2), g(3), bsz, s, _tile(s, cfg["tm_cross"]))
    x = mlp_block(x, w["wu3"], w["wd3"], g(4), g(5), _tile(m, cfg["tm_mlp"]))
    state = (kt, vt, logft, ckv_n.reshape(bsz, s, MLA_KV_RANK), kpet, hg_state)
    return x, state


def _from_feature_major(stacked, heads):
    a = jnp.swapaxes(stacked, 2, 3)
    if heads:
        a = a.reshape(a.shape[:3] + (heads, a.shape[3] // heads))
    return a


def _assemble_states(states):
    kt, vt, logft, ckv, kpet, hg = [jnp.stack(a) for a in zip(*states)]
    return (_from_feature_major(kt, FOX_HEADS), _from_feature_major(vt, FOX_HEADS),
            _from_feature_major(logft, 0), ckv, _from_feature_major(kpet, 0), hg)


_CFG = dict(tm_in=1024, tm_prep=512, t_attn=512, t_cum=512, hg_rows=256, tm_merge=512, tm_cross=512,
            tm_mlp=512, tm_mem=512, tm_expand=1024)


def kernel(x_prompt, x_sample, cache_fox_k, cache_fox_v, cache_fox_logf, cache_mla_ckv, cache_mla_kpe,
           state_hgrn, cache_mem_k, cache_mem_v, mem_prompt, w_in, b_fox, g_mla_q, w_mla_uq, g_mla_kv,
           w_mla_ukv, g_hgrn_out, lb_hgrn, w_branch, w_out, w_xq, w_mem_k, w_mem_v, w_xo, w_up, w_down,
           g_norm):
    cfg = _CFG
    depth = w_in.shape[0]
    lb_p = jax.nn.softmax(lb_hgrn.astype(F32), axis=0)
    lb_all = jnp.cumsum(lb_p, axis=0) - lb_p[0]
    ws = [_prep_layer_weights(w_in[l], w_mla_uq[l], w_mla_ukv[l], w_branch[l], w_out[l], w_xq[l],
                              w_mem_k[l], w_mem_v[l], w_xo[l], w_up[l], w_down[l]) for l in range(depth)]

    def run_layer(x, bsz, s, pos0, l, mk, mv, past):
        return _layer(x, bsz, s, pos0, ws[l], lb_all[l], b_fox[l], g_mla_q[l], g_mla_kv[l],
                      g_hgrn_out[l], g_norm[l], mk, mv, past, cfg, l)

    bp, sp, _ = x_prompt.shape
    x = x_prompt.reshape(bp * sp, D_MODEL)
    mem = mem_prompt.reshape(bp * N_MEM, D_MODEL)
    p_states, p_mem = [], []
    for l in range(depth):
        mk, mv = mem_kv(mem, ws[l]["wmem"], _tile(bp * N_MEM, cfg["tm_mem"]))
        x, st = run_layer(x, bp, sp, 0, l, mk, mv, None)
        p_states.append(st)
        p_mem.append((mk.reshape(bp, N_MEM, X_HEADS, X_DIM), mv.reshape(bp, N_MEM, X_HEADS, X_DIM)))
    y_prompt = x.reshape(bp, sp, D_MODEL)
    p_out = _assemble_states(p_states) + tuple(jnp.stack(a) for a in zip(*p_mem))

    bs, ts, _ = x_sample.shape
    p = cache_fox_k.shape[2]
    fm = lambda c: jnp.moveaxis(c, 2, -1)
    c_kt = fm(cache_fox_k).reshape(depth * bs, BRANCH_W, p)
    c_vt = fm(cache_fox_v).reshape(depth * bs, BRANCH_W, p)
    c_kpt = fm(cache_mla_kpe).reshape(depth * bs, MLA_ROPE, p)
    c_cumt = fox_cumsum(fm(cache_fox_logf).reshape(depth * bs, FOX_HEADS, p), _tile(p, cfg["t_cum"]))
    x = x_sample.reshape(bs * ts, D_MODEL)
    s_states = []
    for l in range(depth):
        c_knt, c_v = latent_expand(cache_mla_ckv[l].reshape(bs * p, MLA_KV_RANK), ws[l]["wkt"],
                                   ws[l]["wv"], bs, p, _tile(p, cfg["tm_expand"]))
        past = (c_kt, c_vt, c_cumt, c_knt, c_kpt, c_v, state_hgrn[l])
        x, st = run_layer(x, bs, ts, p, l, cache_mem_k[l].reshape(bs * N_MEM, X_HEADS * X_DIM),
                          cache_mem_v[l].reshape(bs * N_MEM, X_HEADS * X_DIM), past)
        s_states.append(st)
    y_sample = x.reshape(bs, ts, D_MODEL)
    return (y_prompt, y_sample, *p_out, *_assemble_states(s_states))
```

```python
import functools

import numpy as np
import jax
import jax.numpy as jnp
from jax import lax
from jax.experimental import pallas as pl
from jax.experimental.pallas import tpu as pltpu

F32 = jnp.float32
BF16 = jnp.bfloat16

D_MODEL = 1024
CHUNK = 64
N_MEM = 256
EPS = 1e-6
NEG_BIG = -1e30
EXP_CLIP = 80.0
FOX_HEADS = 8
FOX_DIM = 64
MLA_HEADS = 4
MLA_Q_RANK = 384
MLA_KV_RANK = 256
MLA_NOPE = 128
MLA_ROPE = 64
MLA_V = 128
ROPE_THETA = 10000.0
HG_HEADS = 4
HG_DK = 128
HG_DV = 128
X_HEADS = 4
X_DIM = 128
D_FF = 4 * D_MODEL
BRANCH_W = 512
IN_SIZES = (512, 512, 512, FOX_HEADS, MLA_Q_RANK, MLA_KV_RANK, MLA_ROPE, 512, 512, 512, 512,
            D_MODEL, D_MODEL, D_MODEL)

LANE = 128
SUB_BLOCK = 16
VMEM_LIMIT = 56 * 1024 * 1024
LOG2E = 1.4426950408889634

OFF_HQ, OFF_HF, OFF_HI, OFF_HG, OFF_GA, OFF_FQ, OFF_CQ, OFF_CKV, NP_IN = (
    0, 512, 1024, 1536, 2048, 5120, 5760, 6144, 6400)
IN_TN = 1280
T_FK, T_FV, T_KPE, T_KPE_SW, T_FF, NT_IN = 0, 512, 1024, 1088, 1152, 1160


def _params(sem, vmem=VMEM_LIMIT):
    return pltpu.CompilerParams(dimension_semantics=sem, vmem_limit_bytes=vmem)


def _dot(a, b):
    return jnp.dot(a, b, preferred_element_type=F32)


def _dot_nt(a, b):
    return lax.dot_general(a, b, (((1,), (1,)), ((), ())), preferred_element_type=F32)


def _dot_tn(a, b):
    return lax.dot_general(a, b, (((0,), (0,)), ((), ())), preferred_element_type=F32)


def _rms(x, g):
    y = x * lax.rsqrt(jnp.mean(x * x, axis=-1, keepdims=True) + EPS)
    return y * g


def _log_sigmoid(z):
    return jnp.minimum(z, 0.0) - jnp.log(1.0 + jnp.exp(-jnp.abs(z)))


def _sigmoid(z):
    return 1.0 / (1.0 + jnp.exp(-z))


def _resident(shape):
    nd = len(shape)
    return pl.BlockSpec(shape, lambda *_: (0,) * nd, pipeline_mode=pl.Buffered(1))


def _in_proj_kernel(x_ref, g_ref, w_ref, wt_ref, cos_ref, sin_ref, bf_ref,
                    big_ref, kt_ref, vt_ref, kpe_ref, lf_ref, h_ref):
    j = pl.program_id(1)

    @pl.when(j == 0)
    def _():
        h = _rms(x_ref[...], g_ref[...]).astype(BF16)
        h_ref[...] = h
        yt = _dot_nt(wt_ref[...], h)
        kt_ref[0] = yt[T_FK:T_FK + BRANCH_W]
        vt_ref[0] = yt[T_FV:T_FV + BRANCH_W]
        kpe_ref[0] = (yt[T_KPE:T_KPE + MLA_ROPE] * cos_ref[...]
                      + yt[T_KPE_SW:T_KPE_SW + MLA_ROPE] * sin_ref[...])
        lf_ref[0] = _log_sigmoid(yt[T_FF:T_FF + FOX_HEADS] + bf_ref[...])

    big_ref[...] = _dot(h_ref[...], w_ref[j])


def in_proj(x, g, w3, wt, cos_t, sin_t, b_col, bsz, s, tm):
    m, k = x.shape
    nj, _, tn = w3.shape
    nt = s // tm
    ntab = cos_t.shape[1] // tm
    tspec = lambda rows: pl.BlockSpec((1, rows, tm), lambda i, j: (i // nt, 0, i % nt))
    return pl.pallas_call(
        _in_proj_kernel,
        out_shape=(jax.ShapeDtypeStruct((m, nj * tn), F32),
                   jax.ShapeDtypeStruct((bsz, BRANCH_W, s), F32),
                   jax.ShapeDtypeStruct((bsz, BRANCH_W, s), F32),
                   jax.ShapeDtypeStruct((bsz, MLA_ROPE, s), F32),
                   jax.ShapeDtypeStruct((bsz, FOX_HEADS, s), F32)),
        grid=(m // tm, nj),
        in_specs=[pl.BlockSpec((tm, k), lambda i, j: (i, 0)),
                  pl.BlockSpec((1, k), lambda i, j: (0, 0)),
                  _resident((nj, k, tn)),
                  _resident(wt.shape),
                  pl.BlockSpec((MLA_ROPE, tm), lambda i, j: (0, i % ntab)),
                  pl.BlockSpec((MLA_ROPE, tm), lambda i, j: (0, i % ntab)),
                  pl.BlockSpec((FOX_HEADS, 1), lambda i, j: (0, 0))],
        out_specs=(pl.BlockSpec((tm, tn), lambda i, j: (i, j)),
                   tspec(BRANCH_W), tspec(BRANCH_W), tspec(MLA_ROPE), tspec(FOX_HEADS)),
        scratch_shapes=[pltpu.VMEM((tm, k), BF16)],
        compiler_params=_params(("parallel", "arbitrary")),
        name="in_proj",
    )(x, g, w3, wt, cos_t, sin_t, b_col)


def _mla_prep_kernel(cq_ref, ckv_ref, cs_ref, sn_ref, gq_ref, gkv_ref, wuq_ref, wkt_ref, wv_ref,
                     qx_ref, knt_ref, v_ref, ckvn_ref):
    cos_t = cs_ref[...]
    sin_t = sn_ref[...]
    qn = _rms(cq_ref[...], gq_ref[...]).astype(BF16)
    qall = _dot(qn, wuq_ref[...]) * ((MLA_NOPE + MLA_ROPE) ** -0.5 * LOG2E)
    for h in range(MLA_HEADS):
        lo = h * LANE
        qr = (qall[:, 512 + lo:512 + lo + LANE] * cos_t
              + qall[:, 1024 + lo:1024 + lo + LANE] * sin_t)
        qx_ref[:, 2 * lo:2 * lo + LANE] = qall[:, lo:lo + LANE].astype(BF16)
        qx_ref[:, 2 * lo + LANE:2 * lo + 2 * LANE] = qr.astype(BF16)
    ckvn = _rms(ckv_ref[...], gkv_ref[...])
    ckvn_ref[...] = ckvn
    cb = ckvn.astype(BF16)
    knt_ref[0] = _dot_nt(wkt_ref[...], cb).astype(BF16)
    v_ref[...] = _dot(cb, wv_ref[...]).astype(BF16)


def mla_prep(big, cos_t, sin_t, gq, gkv, wuq, wkt, wv, bsz, s, tm):
    m = big.shape[0]
    nt = s // tm
    ntab = cos_t.shape[0] // tm
    row = lambda w: pl.BlockSpec((tm, w), lambda i: (i, 0))
    full = lambda a: pl.BlockSpec(a.shape, lambda i: (0,) * a.ndim)
    return pl.pallas_call(
        _mla_prep_kernel,
        out_shape=(jax.ShapeDtypeStruct((m, 1024), BF16),
                   jax.ShapeDtypeStruct((bsz, 512, s), BF16),
                   jax.ShapeDtypeStruct((m, 512), BF16),
                   jax.ShapeDtypeStruct((m, MLA_KV_RANK), F32)),
        grid=(m // tm,),
        in_specs=[pl.BlockSpec((tm, MLA_Q_RANK), lambda i: (i, OFF_CQ // MLA_Q_RANK)),
                  pl.BlockSpec((tm, MLA_KV_RANK), lambda i: (i, OFF_CKV // MLA_KV_RANK)),
                  pl.BlockSpec((tm, LANE), lambda i: (i % ntab, 0)),
                  pl.BlockSpec((tm, LANE), lambda i: (i % ntab, 0)),
                  full(gq), full(gkv), full(wuq), full(wkt), full(wv)],
        out_specs=(row(1024), pl.BlockSpec((1, 512, tm), lambda i: (i // nt, 0, i % nt)),
                   row(512), row(MLA_KV_RANK)),
        compiler_params=_params(("parallel",)),
        name="mla_prep",
    )(big, big, cos_t, sin_t, gq, gkv, wuq, wkt, wv)


def _cumsum_kernel(x_ref, c_ref, *, w):
    s = x_ref.shape[2]
    r = lax.broadcasted_iota(jnp.int32, (w, w), 0)
    c = lax.broadcasted_iota(jnp.int32, (w, w), 1)
    upper = (r <= c).astype(F32)
    local = [jnp.dot(x_ref[0, :, g * w:(g + 1) * w], upper, preferred_element_type=F32,
                     precision=lax.Precision.HIGHEST) for g in range(s // w)]
    carry = jnp.zeros((x_ref.shape[1], 1), F32)
    for g, cum in enumerate(local):
        cum = cum + carry
        c_ref[0, :, g * w:(g + 1) * w] = cum
        carry = cum[:, w - 1:w]


def fox_cumsum(x):
    bsz, h, s = x.shape
    return pl.pallas_call(
        functools.partial(_cumsum_kernel, w=min(LANE, s)),
        out_shape=jax.ShapeDtypeStruct((bsz, h, s), F32),
        grid=(bsz,),
        in_specs=[pl.BlockSpec((1, h, s), lambda i: (i, 0, 0))],
        out_specs=pl.BlockSpec((1, h, s), lambda i: (i, 0, 0)),
        compiler_params=_params(("parallel",)),
        name="fox_cumsum",
    )(x)


def _pair_rows_mask(hh):
    sub = lax.broadcasted_iota(jnp.int32, (LANE, 1), 0)
    return (sub < FOX_DIM) if hh == 0 else (sub >= FOX_DIM)


def _fox_finish(acc0, acc1):
    lane = lax.broadcasted_iota(jnp.int32, (1, LANE), 1)
    o0 = acc0 / pltpu.roll(acc0, FOX_DIM, axis=1)
    o1 = acc1 / pltpu.roll(acc1, FOX_DIM, axis=1)
    return jnp.where(lane < FOX_DIM, o0, o1)


def _fox_prompt_kernel(q_ref, kt_ref, vt_ref, ct_ref, o_ref, *, t):
    hp = pl.program_id(1)
    qi = pl.program_id(2)
    lane = lax.broadcasted_iota(jnp.int32, (1, LANE), 1)
    lo = lane < FOX_DIM
    q = q_ref[...] * (FOX_DIM ** -0.5 * LOG2E)
    qs = (jnp.where(lo, q, 0.0).astype(BF16), jnp.where(lo, 0.0, q).astype(BF16))
    d0 = pl.multiple_of(qi * t, t)
    cref = tuple(ct_ref[0, 2 * hp + hh, :, pl.ds(d0, LANE)][:, 0:1] for hh in range(2))

    def chunk(j, carry, causal):
        c0 = pl.multiple_of(j * t, t)
        kt = kt_ref[0, :, pl.ds(c0, t)].astype(BF16)
        vt = vt_ref[0, :, pl.ds(c0, t)]
        ss = []
        for hh in range(2):
            bias = (cref[hh] - ct_ref[0, 2 * hp + hh, :, pl.ds(c0, t)]) * LOG2E
            s = _dot(qs[hh], kt) + bias
            ss.append(s if causal is None else jnp.where(causal, s, NEG_BIG))
        vts = [jnp.where(_pair_rows_mask(hh), vt, 1.0).astype(BF16) for hh in range(2)]
        m_new = [jnp.maximum(carry[hh][0], jnp.max(ss[hh], axis=1, keepdims=True)) for hh in range(2)]
        ps = [jnp.exp2(ss[hh] - m_new[hh]).astype(BF16) for hh in range(2)]
        pvs = [_dot_nt(ps[hh], vts[hh]) for hh in range(2)]
        return tuple((m_new[hh], jnp.exp2(carry[hh][0] - m_new[hh]) * carry[hh][1] + pvs[hh])
                     for hh in range(2))

    init = tuple((jnp.full((t, 1), NEG_BIG, F32), jnp.zeros((t, LANE), F32)) for _ in range(2))
    st = lax.fori_loop(0, qi, lambda j, c: chunk(j, c, None), init)
    causal = (lax.broadcasted_iota(jnp.int32, (t, t), 0) >= lax.broadcasted_iota(jnp.int32, (t, t), 1))
    st = chunk(qi, st, causal)
    o_ref[...] = _fox_finish(st[0][1], st[1][1]).astype(BF16)


def fox_prompt(big, kt, vt, cumt, bsz, s, t):
    m = big.shape[0]
    nq = s // t
    return pl.pallas_call(
        functools.partial(_fox_prompt_kernel, t=t),
        out_shape=jax.ShapeDtypeStruct((m, BRANCH_W), BF16),
        grid=(bsz, FOX_HEADS // 2, nq),
        in_specs=[pl.BlockSpec((t, LANE), lambda b, h, i: (b * nq + i, OFF_FQ // LANE + h)),
                  pl.BlockSpec((1, LANE, s), lambda b, h, i: (b, h, 0)),
                  pl.BlockSpec((1, LANE, s), lambda b, h, i: (b, h, 0)),
                  pl.BlockSpec((1, FOX_HEADS, 1, s), lambda b, h, i: (b, 0, 0, 0))],
        out_specs=pl.BlockSpec((t, LANE), lambda b, h, i: (b * nq + i, h)),
        compiler_params=_params(("parallel", "parallel", "arbitrary")),
        name="fox_prompt",
    )(big, kt, vt, cumt.reshape(bsz, FOX_HEADS, 1, s))


def _mla_keys(knt, kpt):
    n = knt.shape[1]
    return jnp.concatenate([knt, kpt.astype(BF16), jnp.zeros((LANE - MLA_ROPE, n), BF16)], axis=0)


def _mla_prompt_kernel(q_ref, knt_ref, kpt_ref, v_ref, o_ref, *, t):
    qi = pl.program_id(2)
    hs = range(2)
    qs = [q_ref[:, hh * 2 * LANE:(hh + 1) * 2 * LANE] for hh in hs]

    def chunk(j, carry, mask):
        c0 = pl.multiple_of(j * t, t)
        kpt = kpt_ref[0, :, pl.ds(c0, t)]
        ss = [_dot(qs[hh], _mla_keys(knt_ref[0, hh * LANE:(hh + 1) * LANE, pl.ds(c0, t)], kpt)) for hh in hs]
        if mask is not None:
            ss = [jnp.where(mask, s, NEG_BIG) for s in ss]
        m_new = [jnp.maximum(carry[hh][0], jnp.max(ss[hh], axis=1, keepdims=True)) for hh in hs]
        ps = [jnp.exp2(ss[hh] - m_new[hh]) for hh in hs]
        pvs = [_dot(ps[hh].astype(BF16), v_ref[pl.ds(c0, t), hh * LANE:(hh + 1) * LANE]) for hh in hs]
        out = []
        for hh in hs:
            alpha = jnp.exp2(carry[hh][0] - m_new[hh])
            out.append((m_new[hh], alpha * carry[hh][1] + jnp.sum(ps[hh], axis=1, keepdims=True),
                        alpha * carry[hh][2] + pvs[hh]))
        return tuple(out)

    init = tuple((jnp.full((t, 1), NEG_BIG, F32), jnp.zeros((t, 1), F32), jnp.zeros((t, LANE), F32))
                 for _ in hs)
    st = lax.fori_loop(0, qi, lambda j, c: chunk(j, c, None), init)
    rc = lax.broadcasted_iota(jnp.int32, (t, t), 0) // CHUNK
    cc = lax.broadcasted_iota(jnp.int32, (t, t), 1) // CHUNK
    st = chunk(qi, st, rc >= cc)
    o_ref[...] = jnp.concatenate([st[hh][2] / st[hh][1] for hh in hs], axis=1).astype(BF16)


def mla_prompt(qx, knt, kpt, v, bsz, s, t):
    m = qx.shape[0]
    nq = s // t
    return pl.pallas_call(
        functools.partial(_mla_prompt_kernel, t=t),
        out_shape=jax.ShapeDtypeStruct((m, BRANCH_W), BF16),
        grid=(bsz, MLA_HEADS // 2, nq),
        in_specs=[pl.BlockSpec((t, 4 * LANE), lambda b, h, i: (b * nq + i, h)),
                  pl.BlockSpec((1, 2 * LANE, s), lambda b, h, i: (b, h, 0)),
                  pl.BlockSpec((1, MLA_ROPE, s), lambda b, h, i: (b, 0, 0)),
                  pl.BlockSpec((s, 2 * LANE), lambda b, h, i: (b, h))],
        out_specs=pl.BlockSpec((t, 2 * LANE), lambda b, h, i: (b * nq + i, h)),
        compiler_params=_params(("parallel", "parallel", "arbitrary")),
        name="mla_prompt",
    )(qx, knt, kpt, v)


def _fox_sample_kernel(q_ref, ktn_ref, vtn_ref, ktc_ref, vtc_ref, ctn_ref, ctc_ref, o_ref, *, t, p):
    hp = pl.program_id(1)
    lane = lax.broadcasted_iota(jnp.int32, (1, LANE), 1)
    lo = lane < FOX_DIM
    q = q_ref[...] * (FOX_DIM ** -0.5 * LOG2E)
    kt_c = ktc_ref[0].astype(BF16)
    kt_n = ktn_ref[0].astype(BF16)
    vt_c = vtc_ref[0]
    vt_n = vtn_ref[0]
    causal = (lax.broadcasted_iota(jnp.int32, (t, t), 0) >= lax.broadcasted_iota(jnp.int32, (t, t), 1))
    accs = []
    for hh in range(2):
        head = 2 * hp + hh
        qh = (jnp.where(lo, q, 0.0) if hh == 0 else jnp.where(lo, 0.0, q)).astype(BF16)
        cc = ctc_ref[0, pl.ds(head, 1), :]
        ctot = cc[:, p - 1:p]
        s_c = _dot(qh, kt_c) + (ctot - cc) * LOG2E
        s_n = _dot(qh, kt_n) - ctn_ref[0, pl.ds(head, 1), :] * LOG2E
        s_n = jnp.where(causal, s_n, NEG_BIG)
        m = jnp.maximum(jnp.max(s_c, axis=1, keepdims=True), jnp.max(s_n, axis=1, keepdims=True))
        rows = _pair_rows_mask(hh)
        accs.append(_dot_nt(jnp.exp2(s_c - m).astype(BF16), jnp.where(rows, vt_c, 1.0).astype(BF16))
                    + _dot_nt(jnp.exp2(s_n - m).astype(BF16), jnp.where(rows, vt_n, 1.0).astype(BF16)))
    o_ref[...] = _fox_finish(accs[0], accs[1]).astype(BF16)


def fox_sample(big, kt_n, vt_n, kt_c, vt_c, cumt_n, cumt_c, bsz, t, p, layer):
    cidx = lambda b, h: (layer * bsz + b, h, 0)
    return pl.pallas_call(
        functools.partial(_fox_sample_kernel, t=t, p=p),
        out_shape=jax.ShapeDtypeStruct((bsz * t, BRANCH_W), BF16),
        grid=(bsz, FOX_HEADS // 2),
        in_specs=[pl.BlockSpec((t, LANE), lambda b, h: (b, OFF_FQ // LANE + h)),
                  pl.BlockSpec((1, LANE, t), lambda b, h: (b, h, 0)),
                  pl.BlockSpec((1, LANE, t), lambda b, h: (b, h, 0)),
                  pl.BlockSpec((1, LANE, p), cidx),
                  pl.BlockSpec((1, LANE, p), cidx),
                  pl.BlockSpec((1, FOX_HEADS, t), lambda b, h: (b, 0, 0)),
                  pl.BlockSpec((1, FOX_HEADS, p), lambda b, h: (layer * bsz + b, 0, 0))],
        out_specs=pl.BlockSpec((t, LANE), lambda b, h: (b, h)),
        compiler_params=_params(("parallel", "parallel")),
        name="fox_sample",
    )(big, kt_n, vt_n, kt_c, vt_c, cumt_n, cumt_c)


def _mla_sample_kernel(q_ref, kntn_ref, kptn_ref, vn_ref, kntc_ref, kptc_ref, vc_ref, o_ref, *, t, p):
    q = q_ref[...]
    s_c = _dot(q, _mla_keys(kntc_ref[0], kptc_ref[0]))
    s_n = _dot(q, _mla_keys(kntn_ref[0], kptn_ref[0]))
    qc = (p + lax.broadcasted_iota(jnp.int32, (t, t), 0)) // CHUNK
    kc = (p + lax.broadcasted_iota(jnp.int32, (t, t), 1)) // CHUNK
    s_n = jnp.where(qc >= kc, s_n, NEG_BIG)
    m = jnp.maximum(jnp.max(s_c, axis=1, keepdims=True), jnp.max(s_n, axis=1, keepdims=True))
    p_c = jnp.exp2(s_c - m)
    p_n = jnp.exp2(s_n - m)
    l = jnp.sum(p_c, axis=1, keepdims=True) + jnp.sum(p_n, axis=1, keepdims=True)
    o = _dot(p_c.astype(BF16), vc_ref[...]) + _dot(p_n.astype(BF16), vn_ref[...])
    o_ref[...] = (o / l).astype(BF16)


def mla_sample(qx, knt_n, kpt_n, v_n, knt_c, kpt_c, v_c, bsz, t, p, layer):
    assert (p - 1) // CHUNK <= p // CHUNK
    return pl.pallas_call(
        functools.partial(_mla_sample_kernel, t=t, p=p),
        out_shape=jax.ShapeDtypeStruct((bsz * t, BRANCH_W), BF16),
        grid=(bsz, MLA_HEADS),
        in_specs=[pl.BlockSpec((t, 2 * LANE), lambda b, h: (b, h)),
                  pl.BlockSpec((1, LANE, t), lambda b, h: (b, h, 0)),
                  pl.BlockSpec((1, MLA_ROPE, t), lambda b, h: (b, 0, 0)),
                  pl.BlockSpec((t, LANE), lambda b, h: (b, h)),
                  pl.BlockSpec((1, LANE, p), lambda b, h: (b, h, 0)),
                  pl.BlockSpec((1, MLA_ROPE, p), lambda b, h: (layer * bsz + b, 0, 0)),
                  pl.BlockSpec((p, LANE), lambda b, h: (b, h))],
        out_specs=pl.BlockSpec((t, LANE), lambda b, h: (b, h)),
        compiler_params=_params(("parallel", "parallel")),
        name="mla_sample",
    )(qx, knt_n, kpt_n, v_n, knt_c, kpt_c, v_c)


def _latent_expand_kernel(c_ref, wkt_ref, wv_ref, knt_ref, v_ref):
    cb = c_ref[...].astype(BF16)
    knt_ref[0] = _dot_nt(wkt_ref[...], cb).astype(BF16)
    v_ref[...] = _dot(cb, wv_ref[...]).astype(BF16)


def latent_expand(ckv, wkt, wv, bsz, p, tm):
    m = ckv.shape[0]
    nt = p // tm
    return pl.pallas_call(
        _latent_expand_kernel,
        out_shape=(jax.ShapeDtypeStruct((bsz, 512, p), BF16), jax.ShapeDtypeStruct((m, 512), BF16)),
        grid=(m // tm,),
        in_specs=[pl.BlockSpec((tm, MLA_KV_RANK), lambda i: (i, 0)),
                  pl.BlockSpec(wkt.shape, lambda i: (0, 0)), pl.BlockSpec(wv.shape, lambda i: (0, 0))],
        out_specs=(pl.BlockSpec((1, 512, tm), lambda i: (i // nt, 0, i % nt)),
                   pl.BlockSpec((tm, 512), lambda i: (i, 0))),
        compiler_params=_params(("parallel",)),
        name="latent_expand",
    )(ckv, wkt, wv)


def _hgrn_gates(z, lb, tri):
    logf = _log_sigmoid(z) + jnp.log(1.0 + lb * jnp.exp(jnp.minimum(-z, EXP_CLIP)))
    k = (1.0 - lb) * (1.0 / (1.0 + jnp.exp(z)))
    lc = jnp.dot(tri, logf, preferred_element_type=F32, precision=lax.Precision.HIGHEST) * LOG2E
    return k, lc


def _hgrn_local(q, z, lb, tri, v_b, ln, sel):
    k, lc = _hgrn_gates(z, lb, tri)
    nchunk = q.shape[0] // ln
    nsb = q.shape[0] // SUB_BLOCK
    per = ln // SUB_BLOCK
    half = SUB_BLOCK // 2
    rows = lambda a, i: a[i * SUB_BLOCK:(i + 1) * SUB_BLOCK, :]
    lcb = [jnp.zeros((1, LANE), F32) if i % per == 0 else lc[i * SUB_BLOCK - 1:i * SUB_BLOCK, :]
           for i in range(nsb)]
    lcb_rows = jnp.concatenate([jnp.broadcast_to(b, (SUB_BLOCK, LANE)) for b in lcb], axis=0)
    last = [lc[(c + 1) * ln - 1:(c + 1) * ln, :] for c in range(nchunk)]
    last_rows = jnp.concatenate([jnp.broadcast_to(b, (ln, LANE)) for b in last], axis=0)
    qh = (q * jnp.exp2(lc - lcb_rows)).astype(BF16)
    qe = (q * jnp.exp2(lc)).astype(BF16)
    kdec = (k * jnp.exp2(last_rows - lc)).astype(BF16)
    a_off = {}
    for i in range(nsb):
        n = (i % per) * SUB_BLOCK
        if n:
            c0 = i * SUB_BLOCK - n
            kt = (k[c0:c0 + n, :] * jnp.exp2(lcb[i] - lc[c0:c0 + n, :])).astype(BF16)
            a_off[i] = _dot_nt(rows(qh, i), kt)
    yield None
    pieces = []
    for i in range(nsb):
        q_i, k_i, lc_i = rows(q, i), rows(k, i), rows(lc, i)
        cols = []
        for s in range(SUB_BLOCK):
            lo = 0 if s < half else half
            d = jnp.minimum(lc_i[lo:, :] - lc_i[s:s + 1, :], 0.0)
            w = (q_i[lo:, :] * k_i[s:s + 1, :]) * jnp.exp2(d)
            if lo:
                w = jnp.concatenate([jnp.zeros((lo, LANE), F32), w], axis=0)
            cols.append(w.astype(BF16))
        pieces.append(jnp.concatenate(cols, axis=1))
    a_all = _dot(jnp.concatenate(pieces, axis=0), sel)
    inc = [_dot_tn(v_b[c * ln:(c + 1) * ln, :], kdec[c * ln:(c + 1) * ln, :]) for c in range(nchunk)]
    yield None
    pair_ok = (lax.broadcasted_iota(jnp.int32, (SUB_BLOCK, LANE), 0)
               >= lax.broadcasted_iota(jnp.int32, (SUB_BLOCK, LANE), 1))
    off = []
    for i in range(nsb):
        n = (i % per) * SUB_BLOCK
        off.append(_dot(a_off[i].astype(BF16), v_b[i * SUB_BLOCK - n:i * SUB_BLOCK, :]) if n
                   else jnp.zeros((SUB_BLOCK, LANE), F32))
    diag = [_dot(jnp.where(pair_ok, rows(a_all, i), 0.0)[:, :SUB_BLOCK].astype(BF16), rows(v_b, i))
            for i in range(nsb)]
    local = jnp.concatenate(off, axis=0) + jnp.concatenate(diag, axis=0)
    dec = [jnp.exp2(b) for b in last]
    yield local, qe, inc, dec


def _hgrn_kernel(*refs, ln, nchunk, has_init):
    refs = list(refs)
    hq_ref, hf_ref, hi_ref, hg_ref, lb_ref, go_ref, sel_ref = refs[:7]
    s0_ref = refs[7] if has_init else None
    o_ref, sout_ref, st_ref = refs[-3:]
    step = pl.program_id(1)
    nrows = ln * nchunk

    @pl.when(step == 0)
    def _():
        for h in range(HG_HEADS):
            st_ref[h] = s0_ref[0, h].T if has_init else jnp.zeros((HG_DV, HG_DK), F32)

    ri = lax.broadcasted_iota(jnp.int32, (nrows, nrows), 0)
    ci = lax.broadcasted_iota(jnp.int32, (nrows, nrows), 1)
    tri = ((ri >= ci) & (ri // ln == ci // ln)).astype(F32)
    def finish(h, local, qe, inc, dec):
        cs = slice(h * LANE, (h + 1) * LANE)
        st = st_ref[h]
        parts = []
        for c in range(nchunk):
            parts.append(_dot_nt(qe[c * ln:(c + 1) * ln, :], st.astype(BF16)))
            st = st * dec[c] + inc[c]
        st_ref[h] = st
        o = local + jnp.concatenate(parts, axis=0)
        o_ref[:, cs] = (_rms(o, go_ref[...]) * _sigmoid(hg_ref[:, cs])).astype(BF16)

    def start(h):
        cs = slice(h * LANE, (h + 1) * LANE)
        gen = _hgrn_local(hq_ref[:, cs], hf_ref[:, cs], lb_ref[:, cs], tri, hi_ref[:, cs].astype(BF16),
                          ln, sel_ref[...])
        next(gen)
        return gen

    gens = {0: start(0)}
    for h in range(HG_HEADS):
        if h + 1 < HG_HEADS:
            gens[h + 1] = start(h + 1)
        next(gens[h])
        if h > 0:
            finish(h - 1, *next(gens.pop(h - 1)))
    finish(HG_HEADS - 1, *next(gens.pop(HG_HEADS - 1)))

    @pl.when(step == pl.num_programs(1) - 1)
    def _():
        for h in range(HG_HEADS):
            sout_ref[0, h] = st_ref[h].T


def hgrn(big, lb, g_out, bsz, s, ln, rows, s0=None):
    m = big.shape[0]
    ns = s // rows
    has_init = s0 is not None
    sel = (np.arange(SUB_BLOCK * LANE)[:, None] // LANE == np.arange(LANE)[None, :])
    sel = jnp.asarray(sel, BF16)
    blk = lambda off: pl.BlockSpec((rows, BRANCH_W), lambda b, i: (b * ns + i, off // BRANCH_W))
    ins = [big, big, big, big, lb, g_out, sel]
    specs = [blk(OFF_HQ), blk(OFF_HF), blk(OFF_HI), blk(OFF_HG),
             pl.BlockSpec((1, BRANCH_W), lambda b, i: (0, 0)),
             pl.BlockSpec((1, HG_DV), lambda b, i: (0, 0)),
             pl.BlockSpec(sel.shape, lambda b, i: (0, 0))]
    if has_init:
        ins.append(s0)
        specs.append(pl.BlockSpec((1, HG_HEADS, HG_DK, HG_DV), lambda b, i: (b, 0, 0, 0)))
    return pl.pallas_call(
        functools.partial(_hgrn_kernel, ln=ln, nchunk=rows // ln, has_init=has_init),
        out_shape=(jax.ShapeDtypeStruct((m, BRANCH_W), BF16),
                   jax.ShapeDtypeStruct((bsz, HG_HEADS, HG_DK, HG_DV), F32)),
        grid=(bsz, ns),
        in_specs=specs,
        out_specs=(pl.BlockSpec((rows, BRANCH_W), lambda b, i: (b * ns + i, 0)),
                   pl.BlockSpec((1, HG_HEADS, HG_DK, HG_DV), lambda b, i: (b, 0, 0, 0))),
        scratch_shapes=[pltpu.VMEM((HG_HEADS, HG_DV, HG_DK), F32)],
        compiler_params=_params(("parallel", "arbitrary")),
        name="hgrn",
    )(*ins)


def _merge_kernel(of_ref, om_ref, oh_ref, ga_ref, gb_ref, gc_ref, x_ref, wb_ref, wo_ref, g_ref, o_ref):
    merged = (_sigmoid(ga_ref[...]) * _dot(of_ref[...], wb_ref[0])
              + _sigmoid(gb_ref[...]) * _dot(om_ref[...], wb_ref[1])
              + _sigmoid(gc_ref[...]) * _dot(oh_ref[...], wb_ref[2]))
    y = _dot(merged.astype(BF16), wo_ref[...])
    o_ref[...] = x_ref[...] + _rms(y, g_ref[...])


def merge_out(o_fox, o_mla, o_hg, big, x, wb, wo, g, tm):
    m = x.shape[0]
    row = lambda w: pl.BlockSpec((tm, w), lambda i: (i, 0))
    gate = lambda off: pl.BlockSpec((tm, D_MODEL), lambda i: (i, off // D_MODEL))
    return pl.pallas_call(
        _merge_kernel,
        out_shape=jax.ShapeDtypeStruct((m, D_MODEL), F32),
        grid=(m // tm,),
        in_specs=[row(BRANCH_W), row(BRANCH_W), row(BRANCH_W),
                  gate(OFF_GA), gate(OFF_GA + D_MODEL), gate(OFF_GA + 2 * D_MODEL),
                  row(D_MODEL), _resident(wb.shape), _resident(wo.shape),
                  pl.BlockSpec((1, D_MODEL), lambda i: (0, 0))],
        out_specs=row(D_MODEL),
        compiler_params=_params(("parallel",)),
        name="merge_out",
    )(o_fox, o_mla, o_hg, big, big, big, x, wb, wo, g)


def _matmul2_kernel(x_ref, w_ref, a_ref, b_ref):
    y = _dot(x_ref[...].astype(BF16), w_ref[...])
    n = a_ref.shape[1]
    a_ref[...] = y[:, :n]
    b_ref[...] = y[:, n:]


def mem_kv(mem, w, tm):
    m, k = mem.shape
    n = w.shape[1] // 2
    row = lambda w_: pl.BlockSpec((tm, w_), lambda i: (i, 0))
    return pl.pallas_call(
        _matmul2_kernel,
        out_shape=(jax.ShapeDtypeStruct((m, n), F32), jax.ShapeDtypeStruct((m, n), F32)),
        grid=(m // tm,),
        in_specs=[row(k), _resident(w.shape)],
        out_specs=(row(n), row(n)),
        compiler_params=_params(("parallel",)),
        name="mem_kv",
    )(mem, w)


def _cross_kernel(x_ref, mk_ref, mv_ref, wq_ref, wo_ref, g2_ref, g3_ref, o_ref):
    x = x_ref[...]
    h = _rms(x, g2_ref[...]).astype(BF16)
    q = _dot(h, wq_ref[...])
    scale = X_DIM ** -0.5
    outs = []
    for hd in range(X_HEADS):
        cs = slice(hd * X_DIM, (hd + 1) * X_DIM)
        s = _dot_nt(q[:, cs].astype(BF16), mk_ref[:, cs].astype(BF16)) * scale
        p = jnp.exp(s - jnp.max(s, axis=1, keepdims=True))
        l = jnp.sum(p, axis=1, keepdims=True)
        outs.append(_dot(p.astype(BF16), mv_ref[:, cs].astype(BF16)) / l)
    ox = jnp.concatenate(outs, axis=1).astype(BF16)
    o_ref[...] = x + _rms(_dot(ox, wo_ref[...]), g3_ref[...])


def cross_block(x, mk, mv, wq, wo, g2, g3, bsz, s, tm):
    m = x.shape[0]
    nt = s // tm
    vec = pl.BlockSpec((1, D_MODEL), lambda b, i: (0, 0))
    return pl.pallas_call(
        _cross_kernel,
        out_shape=jax.ShapeDtypeStruct((m, D_MODEL), F32),
        grid=(bsz, nt),
        in_specs=[pl.BlockSpec((tm, D_MODEL), lambda b, i: (b * nt + i, 0)),
                  pl.BlockSpec((N_MEM, X_HEADS * X_DIM), lambda b, i: (b, 0)),
                  pl.BlockSpec((N_MEM, X_HEADS * X_DIM), lambda b, i: (b, 0)),
                  _resident(wq.shape), _resident(wo.shape), vec, vec],
        out_specs=pl.BlockSpec((tm, D_MODEL), lambda b, i: (b * nt + i, 0)),
        compiler_params=_params(("parallel", "parallel")),
        name="cross_attn",
    )(x, mk, mv, wq, wo, g2, g3)


def _mlp_kernel(x_ref, wu_ref, wd_ref, g4_ref, g5_ref, o_ref, h_ref, acc_ref):
    j = pl.program_id(1)

    @pl.when(j == 0)
    def _():
        h_ref[...] = _rms(x_ref[...], g4_ref[...]).astype(BF16)
        acc_ref[...] = jnp.zeros_like(acc_ref)

    u = jnp.square(jnp.maximum(_dot(h_ref[...], wu_ref[j]), 0.0)).astype(BF16)
    acc_ref[...] += _dot(u, wd_ref[j])

    @pl.when(j == pl.num_programs(1) - 1)
    def _():
        o_ref[...] = x_ref[...] + _rms(acc_ref[...], g5_ref[...])


def mlp_block(x, wu3, wd3, g4, g5, tm):
    m = x.shape[0]
    nj = wu3.shape[0]
    vec = pl.BlockSpec((1, D_MODEL), lambda i, j: (0, 0))
    return pl.pallas_call(
        _mlp_kernel,
        out_shape=jax.ShapeDtypeStruct((m, D_MODEL), F32),
        grid=(m // tm, nj),
        in_specs=[pl.BlockSpec((tm, D_MODEL), lambda i, j: (i, 0)),
                  _resident(wu3.shape), _resident(wd3.shape), vec, vec],
        out_specs=pl.BlockSpec((tm, D_MODEL), lambda i, j: (i, 0)),
        scratch_shapes=[pltpu.VMEM((tm, D_MODEL), BF16), pltpu.VMEM((tm, D_MODEL), F32)],
        compiler_params=_params(("parallel", "arbitrary")),
        name="mlp",
    )(x, wu3, wd3, g4, g5)


def _prep_layer_weights(w_in, w_mla_uq, w_mla_ukv, w_branch, w_out, w_xq, w_mem_k, w_mem_v, w_xo,
                        w_up, w_down):
    idx = np.cumsum((0,) + IN_SIZES)
    seg = lambda i: w_in[:, idx[i]:idx[i + 1]]
    fq, fk, fv, ff, cq, ckv, kpe, hq, hf, hi, hg, ga, gb, gc = (seg(i) for i in range(14))
    half = MLA_ROPE // 2
    kpe_sw = jnp.concatenate([kpe[:, half:], kpe[:, :half]], axis=1)
    pad = jnp.zeros((D_MODEL, OFF_CQ - OFF_FQ - BRANCH_W), w_in.dtype)
    w_p = jnp.concatenate([hq, hf, hi, hg, ga, gb, gc, fq, pad, cq, ckv], axis=1).astype(BF16)
    w_in3 = w_p.reshape(D_MODEL, NP_IN // IN_TN, IN_TN).transpose(1, 0, 2)
    w_t = jnp.concatenate([fk, fv, kpe, kpe_sw, ff], axis=1).T.astype(BF16)
    hd = MLA_NOPE + MLA_ROPE
    zq = jnp.zeros((MLA_Q_RANK, LANE - MLA_ROPE), w_mla_uq.dtype)
    nope, rope_n, rope_s = [], [], []
    for h in range(MLA_HEADS):
        base = h * hd
        nope.append(w_mla_uq[:, base:base + MLA_NOPE])
        x1 = w_mla_uq[:, base + MLA_NOPE:base + MLA_NOPE + half]
        x2 = w_mla_uq[:, base + MLA_NOPE + half:base + hd]
        rope_n += [x1, x2, zq]
        rope_s += [x2, x1, zq]
    wuq = jnp.concatenate(nope + rope_n + rope_s, axis=1).astype(BF16)
    kvd = MLA_NOPE + MLA_V
    wkt = jnp.concatenate([w_mla_ukv[:, h * kvd:h * kvd + MLA_NOPE] for h in range(MLA_HEADS)],
                          axis=1).T.astype(BF16)
    wv = jnp.concatenate([w_mla_ukv[:, h * kvd + MLA_NOPE:(h + 1) * kvd] for h in range(MLA_HEADS)],
                         axis=1).astype(BF16)
    nff = D_FF // D_MODEL
    return dict(
        w_in3=w_in3, w_t=w_t, wuq=wuq, wkt=wkt, wv=wv,
        wb=w_branch.astype(BF16), wo=w_out.astype(BF16), wxq=w_xq.astype(BF16), wxo=w_xo.astype(BF16),
        wmem=jnp.concatenate([w_mem_k, w_mem_v], axis=1).astype(BF16),
        wu3=w_up.astype(BF16).reshape(D_MODEL, nff, D_MODEL).transpose(1, 0, 2),
        wd3=w_down.astype(BF16).reshape(nff, D_MODEL, D_MODEL))


def _rope_tables(pos, reps):
    half = MLA_ROPE // 2
    freq = ROPE_THETA ** (-jnp.arange(half, dtype=F32) / half)
    ang = pos.astype(F32)[:, None] * freq[None, :]
    cos, sin = jnp.cos(ang), jnp.sin(ang)
    z = jnp.zeros((pos.shape[0], LANE - MLA_ROPE), F32)
    cos_r = jnp.tile(jnp.concatenate([cos, cos, z], axis=1), (reps, 1))
    sin_r = jnp.tile(jnp.concatenate([-sin, sin, z], axis=1), (reps, 1))
    return cos_r, sin_r, cos_r[:, :MLA_ROPE].T, sin_r[:, :MLA_ROPE].T


def _tile(n, pref):
    t = min(n, pref)
    assert n % t == 0
    return t


def _layer(x, bsz, s, pos0, w, lb, b_fox, g_q, g_kv, g_hout, g_norm, mem_k, mem_v, past, cfg, layer):
    m = bsz * s
    g = lambda i: g_norm[i][None, :]
    tm_in = _tile(s, cfg["tm_in"])
    tm_p = _tile(s, cfg["tm_prep"])
    cos_r, sin_r, cos_c, sin_c = _rope_tables(pos0 + jnp.arange(s), 1)
    big, kt, vt, kpet, logft = in_proj(x, g(0), w["w_in3"], w["w_t"], cos_c, sin_c, b_fox[:, None],
                                       bsz, s, tm_in)
    qx, knt, v, ckv_n = mla_prep(big, cos_r, sin_r, g_q[None, :], g_kv[None, :],
                                 w["wuq"], w["wkt"], w["wv"], bsz, s, tm_p)
    if past is None:
        t = _tile(s, cfg["t_attn"])
        cumt = fox_cumsum(logft)
        o_fox = fox_prompt(big, kt, vt, cumt, bsz, s, t)
        o_mla = mla_prompt(qx, knt, kpet, v, bsz, s, t)
        o_hg, hg_state = hgrn(big, lb[None, :], g_hout[None, :], bsz, s, CHUNK,
                              _tile(s, cfg["hg_rows"]))
    else:
        c_kt, c_vt, c_cumt, c_knt, c_kpt, c_v, c_hg = past
        p = c_kt.shape[2]
        cumt = fox_cumsum(logft)
        o_fox = fox_sample(big, kt, vt, c_kt, c_vt, cumt, c_cumt, bsz, s, p, layer)
        o_mla = mla_sample(qx, knt, kpet, v, c_knt, c_kpt, c_v, bsz, s, p, layer)
        o_hg, hg_state = hgrn(big, lb[None, :], g_hout[None, :], bsz, s, s, s, s0=c_hg)
    x = merge_out(o_fox, o_mla, o_hg, big, x, w["wb"], w["wo"], g(1), _tile(m, cfg["tm_merge"]))
    x = cross_block(x, mem_k, mem_v, w["wxq"], w["wxo"], g(2), g(3), bsz, s, _tile(s, cfg["tm_cross"]))
    x = mlp_block(x, w["wu3"], w["wd3"], g(4), g(5), _tile(m, cfg["tm_mlp"]))
    state = (kt, vt, logft, ckv_n.reshape(bsz, s, MLA_KV_RANK), kpet, hg_state)
    return x, state


def _from_feature_major(stacked, heads):
    a = jnp.swapaxes(stacked, 2, 3)
    if heads:
        a = a.reshape(a.shape[:3] + (heads, a.shape[3] // heads))
    return a


def _assemble_states(states):
    kt, vt, logft, ckv, kpet, hg = [jnp.stack(a) for a in zip(*states)]
    return (_from_feature_major(kt, FOX_HEADS), _from_feature_major(vt, FOX_HEADS),
            _from_feature_major(logft, 0), ckv, _from_feature_major(kpet, 0), hg)


_CFG = dict(tm_in=1024, tm_prep=512, t_attn=512, hg_rows=256, tm_merge=512, tm_cross=512,
            tm_mlp=512, tm_mem=512, tm_expand=1024)


def kernel(x_prompt, x_sample, cache_fox_k, cache_fox_v, cache_fox_logf, cache_mla_ckv, cache_mla_kpe,
           state_hgrn, cache_mem_k, cache_mem_v, mem_prompt, w_in, b_fox, g_mla_q, w_mla_uq, g_mla_kv,
           w_mla_ukv, g_hgrn_out, lb_hgrn, w_branch, w_out, w_xq, w_mem_k, w_mem_v, w_xo, w_up, w_down,
           g_norm):
    cfg = _CFG
    depth = w_in.shape[0]
    lb_p = jax.nn.softmax(lb_hgrn.astype(F32), axis=0)
    lb_all = jnp.cumsum(lb_p, axis=0) - lb_p[0]
    ws = [_prep_layer_weights(w_in[l], w_mla_uq[l], w_mla_ukv[l], w_branch[l], w_out[l], w_xq[l],
                              w_mem_k[l], w_mem_v[l], w_xo[l], w_up[l], w_down[l]) for l in range(depth)]

    def run_layer(x, bsz, s, pos0, l, mk, mv, past):
        return _layer(x, bsz, s, pos0, ws[l], lb_all[l], b_fox[l], g_mla_q[l], g_mla_kv[l],
                      g_hgrn_out[l], g_norm[l], mk, mv, past, cfg, l)

    bp, sp, _ = x_prompt.shape
    x = x_prompt.reshape(bp * sp, D_MODEL)
    mem = mem_prompt.reshape(bp * N_MEM, D_MODEL)
    p_states, p_mem = [], []
    for l in range(depth):
        mk, mv = mem_kv(mem, ws[l]["wmem"], _tile(bp * N_MEM, cfg["tm_mem"]))
        x, st = run_layer(x, bp, sp, 0, l, mk, mv, None)
        p_states.append(st)
        p_mem.append((mk.reshape(bp, N_MEM, X_HEADS, X_DIM), mv.reshape(bp, N_MEM, X_HEADS, X_DIM)))
    y_prompt = x.reshape(bp, sp, D_MODEL)
    p_out = _assemble_states(p_states) + tuple(jnp.stack(a) for a in zip(*p_mem))

    bs, ts, _ = x_sample.shape
    p = cache_fox_k.shape[2]
    fm = lambda c: jnp.moveaxis(c, 2, -1)
    c_kt = fm(cache_fox_k).reshape(depth * bs, BRANCH_W, p)
    c_vt = fm(cache_fox_v).reshape(depth * bs, BRANCH_W, p)
    c_kpt = fm(cache_mla_kpe).reshape(depth * bs, MLA_ROPE, p)
    c_cumt = fox_cumsum(fm(cache_fox_logf).reshape(depth * bs, FOX_HEADS, p))
    x = x_sample.reshape(bs * ts, D_MODEL)
    s_states = []
    for l in range(depth):
        c_knt, c_v = latent_expand(cache_mla_ckv[l].reshape(bs * p, MLA_KV_RANK), ws[l]["wkt"],
                                   ws[l]["wv"], bs, p, _tile(p, cfg["tm_expand"]))
        past = (c_kt, c_vt, c_cumt, c_knt, c_kpt, c_v, state_hgrn[l])
        x, st = run_layer(x, bs, ts, p, l, cache_mem_k[l].reshape(bs * N_MEM, X_HEADS * X_DIM),
                          cache_mem_v[l].reshape(bs * N_MEM, X_HEADS * X_DIM), past)
        s_states.append(st)
    y_sample = x.reshape(bs, ts, D_MODEL)
    return (y_prompt, y_sample, *p_out, *_assemble_states(s_states))
```

```python
import functools

import numpy as np
import jax
import jax.numpy as jnp
from jax import lax
from jax.experimental import pallas as pl
from jax.experimental.pallas import tpu as pltpu

F32 = jnp.float32
BF16 = jnp.bfloat16

D_MODEL = 1024
CHUNK = 64
N_MEM = 256
EPS = 1e-6
NEG_BIG = -1e30
EXP_CLIP = 80.0
FOX_HEADS = 8
FOX_DIM = 64
MLA_HEADS = 4
MLA_Q_RANK = 384
MLA_KV_RANK = 256
MLA_NOPE = 128
MLA_ROPE = 64
MLA_V = 128
ROPE_THETA = 10000.0
HG_HEADS = 4
HG_DK = 128
HG_DV = 128
X_HEADS = 4
X_DIM = 128
D_FF = 4 * D_MODEL
BRANCH_W = 512
IN_SIZES = (512, 512, 512, FOX_HEADS, MLA_Q_RANK, MLA_KV_RANK, MLA_ROPE, 512, 512, 512, 512,
            D_MODEL, D_MODEL, D_MODEL)

LANE = 128
SUB_BLOCK = 16
VMEM_LIMIT = 56 * 1024 * 1024
LOG2E = 1.4426950408889634

OFF_HQ, OFF_HF, OFF_HI, OFF_HG, OFF_GA, OFF_FQ, OFF_CQ, OFF_CKV, NP_IN = (
    0, 512, 1024, 1536, 2048, 5120, 5760, 6144, 6400)
IN_TN = 1280
T_FK, T_FV, T_KPE, T_KPE_SW, T_FF, NT_IN = 0, 512, 1024, 1088, 1152, 1160


def _params(sem, vmem=VMEM_LIMIT):
    return pltpu.CompilerParams(dimension_semantics=sem, vmem_limit_bytes=vmem)


def _dot(a, b):
    return jnp.dot(a, b, preferred_element_type=F32)


def _dot_nt(a, b):
    return lax.dot_general(a, b, (((1,), (1,)), ((), ())), preferred_element_type=F32)


def _dot_tn(a, b):
    return lax.dot_general(a, b, (((0,), (0,)), ((), ())), preferred_element_type=F32)


def _rms(x, g):
    y = x * lax.rsqrt(jnp.mean(x * x, axis=-1, keepdims=True) + EPS)
    return y * g


def _log_sigmoid(z):
    return jnp.minimum(z, 0.0) - jnp.log(1.0 + jnp.exp(-jnp.abs(z)))


def _sigmoid(z):
    return 1.0 / (1.0 + jnp.exp(-z))


def _resident(shape):
    nd = len(shape)
    return pl.BlockSpec(shape, lambda *_: (0,) * nd, pipeline_mode=pl.Buffered(1))


def _in_proj_kernel(x_ref, g_ref, w_ref, wt_ref, cos_ref, sin_ref, bf_ref, *rest):
    big_ref, kt_ref, vt_ref, kpe_ref, lf_ref, h_ref = rest[-6:]
    j = pl.program_id(1)

    @pl.when(j == 0)
    def _():
        h = _rms(x_ref[...], g_ref[...]).astype(BF16)
        h_ref[...] = h
        yt = _dot_nt(wt_ref[...], h)
        kt_ref[0] = yt[T_FK:T_FK + BRANCH_W]
        vt_ref[0] = yt[T_FV:T_FV + BRANCH_W]
        kpe_ref[0] = (yt[T_KPE:T_KPE + MLA_ROPE] * cos_ref[...]
                      + yt[T_KPE_SW:T_KPE_SW + MLA_ROPE] * sin_ref[...])
        lf_ref[0] = _log_sigmoid(yt[T_FF:T_FF + FOX_HEADS] + bf_ref[...])

    big_ref[...] = _dot(h_ref[...], w_ref[j])


def in_proj(x, g, w3, wt, cos_t, sin_t, b_col, bsz, s, tm, layer, depth, prev):
    m, k = x.shape
    nj, _, tn = w3.shape
    nt = s // tm
    ntab = cos_t.shape[1] // tm
    feats = (BRANCH_W, BRANCH_W, MLA_ROPE, FOX_HEADS)
    tspec = lambda rows: pl.BlockSpec((None, 1, rows, tm), lambda i, j: (layer, i // nt, 0, i % nt))
    n_in = 7
    return pl.pallas_call(
        _in_proj_kernel,
        out_shape=(jax.ShapeDtypeStruct((m, nj * tn), F32),)
        + tuple(jax.ShapeDtypeStruct((depth, bsz, f, s), F32) for f in feats),
        grid=(m // tm, nj),
        in_specs=[pl.BlockSpec((tm, k), lambda i, j: (i, 0)),
                  pl.BlockSpec((1, k), lambda i, j: (0, 0)),
                  _resident((nj, k, tn)),
                  _resident(wt.shape),
                  pl.BlockSpec((MLA_ROPE, tm), lambda i, j: (0, i % ntab)),
                  pl.BlockSpec((MLA_ROPE, tm), lambda i, j: (0, i % ntab)),
                  pl.BlockSpec((FOX_HEADS, 1), lambda i, j: (0, 0))]
        + [pl.BlockSpec(memory_space=pl.ANY)] * len(prev),
        out_specs=(pl.BlockSpec((tm, tn), lambda i, j: (i, j)),) + tuple(tspec(f) for f in feats),
        scratch_shapes=[pltpu.VMEM((tm, k), BF16)],
        input_output_aliases={n_in + i: 1 + i for i in range(len(prev))},
        compiler_params=_params(("parallel", "arbitrary")),
        name="in_proj",
    )(x, g, w3, wt, cos_t, sin_t, b_col, *prev)


def _mla_prep_kernel(cq_ref, ckv_ref, cs_ref, sn_ref, gq_ref, gkv_ref, wuq_ref, wkt_ref, wv_ref, *rest):
    qx_ref, knt_ref, v_ref, ckvn_ref = rest[-4:]
    cos_t = cs_ref[...]
    sin_t = sn_ref[...]
    qn = _rms(cq_ref[...], gq_ref[...]).astype(BF16)
    qall = _dot(qn, wuq_ref[...]) * ((MLA_NOPE + MLA_ROPE) ** -0.5 * LOG2E)
    for h in range(MLA_HEADS):
        lo = h * LANE
        qr = (qall[:, 512 + lo:512 + lo + LANE] * cos_t
              + qall[:, 1024 + lo:1024 + lo + LANE] * sin_t)
        qx_ref[:, 2 * lo:2 * lo + LANE] = qall[:, lo:lo + LANE].astype(BF16)
        qx_ref[:, 2 * lo + LANE:2 * lo + 2 * LANE] = qr.astype(BF16)
    ckvn = _rms(ckv_ref[...], gkv_ref[...])
    ckvn_ref[...] = ckvn
    cb = ckvn.astype(BF16)
    knt_ref[0] = _dot_nt(wkt_ref[...], cb).astype(BF16)
    v_ref[...] = _dot(cb, wv_ref[...]).astype(BF16)


def mla_prep(big, cos_t, sin_t, gq, gkv, wuq, wkt, wv, bsz, s, tm, layer, depth, prev):
    m = big.shape[0]
    nt = s // tm
    ntab = cos_t.shape[0] // tm
    row = lambda w: pl.BlockSpec((tm, w), lambda i: (i, 0))
    full = lambda a: pl.BlockSpec(a.shape, lambda i: (0,) * a.ndim)
    n_in = 9
    return pl.pallas_call(
        _mla_prep_kernel,
        out_shape=(jax.ShapeDtypeStruct((m, 1024), BF16),
                   jax.ShapeDtypeStruct((bsz, 512, s), BF16),
                   jax.ShapeDtypeStruct((m, 512), BF16),
                   jax.ShapeDtypeStruct((depth, m, MLA_KV_RANK), F32)),
        grid=(m // tm,),
        in_specs=[pl.BlockSpec((tm, MLA_Q_RANK), lambda i: (i, OFF_CQ // MLA_Q_RANK)),
                  pl.BlockSpec((tm, MLA_KV_RANK), lambda i: (i, OFF_CKV // MLA_KV_RANK)),
                  pl.BlockSpec((tm, LANE), lambda i: (i % ntab, 0)),
                  pl.BlockSpec((tm, LANE), lambda i: (i % ntab, 0)),
                  full(gq), full(gkv), full(wuq), full(wkt), full(wv)]
        + [pl.BlockSpec(memory_space=pl.ANY)] * len(prev),
        out_specs=(row(1024), pl.BlockSpec((1, 512, tm), lambda i: (i // nt, 0, i % nt)),
                   row(512), pl.BlockSpec((None, tm, MLA_KV_RANK), lambda i: (layer, i, 0))),
        input_output_aliases={n_in + i: 3 + i for i in range(len(prev))},
        compiler_params=_params(("parallel",)),
        name="mla_prep",
    )(big, big, cos_t, sin_t, gq, gkv, wuq, wkt, wv, *prev)


def _cumsum_kernel(x_ref, c_ref, *, w):
    s = x_ref.shape[2]
    r = lax.broadcasted_iota(jnp.int32, (w, w), 0)
    c = lax.broadcasted_iota(jnp.int32, (w, w), 1)
    upper = (r <= c).astype(F32)
    local = [jnp.dot(x_ref[0, :, g * w:(g + 1) * w], upper, preferred_element_type=F32,
                     precision=lax.Precision.HIGHEST) for g in range(s // w)]
    carry = jnp.zeros((x_ref.shape[1], 1), F32)
    for g, cum in enumerate(local):
        cum = cum + carry
        c_ref[0, :, g * w:(g + 1) * w] = cum
        carry = cum[:, w - 1:w]


def fox_cumsum(x, row0, bsz):
    _, h, s = x.shape
    return pl.pallas_call(
        functools.partial(_cumsum_kernel, w=min(LANE, s)),
        out_shape=jax.ShapeDtypeStruct((bsz, h, s), F32),
        grid=(bsz,),
        in_specs=[pl.BlockSpec((1, h, s), lambda i: (row0 + i, 0, 0))],
        out_specs=pl.BlockSpec((1, h, s), lambda i: (i, 0, 0)),
        compiler_params=_params(("parallel",)),
        name="fox_cumsum",
    )(x)


def _pair_rows_mask(hh):
    sub = lax.broadcasted_iota(jnp.int32, (LANE, 1), 0)
    return (sub < FOX_DIM) if hh == 0 else (sub >= FOX_DIM)


def _fox_finish(acc0, acc1):
    lane = lax.broadcasted_iota(jnp.int32, (1, LANE), 1)
    o0 = acc0 / pltpu.roll(acc0, FOX_DIM, axis=1)
    o1 = acc1 / pltpu.roll(acc1, FOX_DIM, axis=1)
    return jnp.where(lane < FOX_DIM, o0, o1)


def _fox_prompt_kernel(q_ref, kt_ref, vt_ref, ct_ref, o_ref, *, t):
    hp = pl.program_id(1)
    qi = pl.program_id(2)
    lane = lax.broadcasted_iota(jnp.int32, (1, LANE), 1)
    lo = lane < FOX_DIM
    q = q_ref[...] * (FOX_DIM ** -0.5 * LOG2E)
    qs = (jnp.where(lo, q, 0.0).astype(BF16), jnp.where(lo, 0.0, q).astype(BF16))
    d0 = pl.multiple_of(qi * t, t)
    cref = tuple(ct_ref[0, 2 * hp + hh, :, pl.ds(d0, LANE)][:, 0:1] for hh in range(2))

    def chunk(j, carry, causal):
        c0 = pl.multiple_of(j * t, t)
        kt = kt_ref[0, :, pl.ds(c0, t)].astype(BF16)
        vt = vt_ref[0, :, pl.ds(c0, t)]
        ss = []
        for hh in range(2):
            bias = (cref[hh] - ct_ref[0, 2 * hp + hh, :, pl.ds(c0, t)]) * LOG2E
            s = _dot(qs[hh], kt) + bias
            ss.append(s if causal is None else jnp.where(causal, s, NEG_BIG))
        vts = [jnp.where(_pair_rows_mask(hh), vt, 1.0).astype(BF16) for hh in range(2)]
        m_new = [jnp.maximum(carry[hh][0], jnp.max(ss[hh], axis=1, keepdims=True)) for hh in range(2)]
        ps = [jnp.exp2(ss[hh] - m_new[hh]).astype(BF16) for hh in range(2)]
        pvs = [_dot_nt(ps[hh], vts[hh]) for hh in range(2)]
        return tuple((m_new[hh], jnp.exp2(carry[hh][0] - m_new[hh]) * carry[hh][1] + pvs[hh])
                     for hh in range(2))

    init = tuple((jnp.full((t, 1), NEG_BIG, F32), jnp.zeros((t, LANE), F32)) for _ in range(2))
    st = lax.fori_loop(0, qi, lambda j, c: chunk(j, c, None), init)
    causal = (lax.broadcasted_iota(jnp.int32, (t, t), 0) >= lax.broadcasted_iota(jnp.int32, (t, t), 1))
    st = chunk(qi, st, causal)
    o_ref[...] = _fox_finish(st[0][1], st[1][1]).astype(BF16)


def fox_prompt(big, kt, vt, cumt, row0, bsz, s, t):
    m = big.shape[0]
    nq = s // t
    return pl.pallas_call(
        functools.partial(_fox_prompt_kernel, t=t),
        out_shape=jax.ShapeDtypeStruct((m, BRANCH_W), BF16),
        grid=(bsz, FOX_HEADS // 2, nq),
        in_specs=[pl.BlockSpec((t, LANE), lambda b, h, i: (b * nq + i, OFF_FQ // LANE + h)),
                  pl.BlockSpec((1, LANE, s), lambda b, h, i: (row0 + b, h, 0)),
                  pl.BlockSpec((1, LANE, s), lambda b, h, i: (row0 + b, h, 0)),
                  pl.BlockSpec((1, FOX_HEADS, 1, s), lambda b, h, i: (b, 0, 0, 0))],
        out_specs=pl.BlockSpec((t, LANE), lambda b, h, i: (b * nq + i, h)),
        compiler_params=_params(("parallel", "parallel", "arbitrary")),
        name="fox_prompt",
    )(big, kt, vt, cumt.reshape(bsz, FOX_HEADS, 1, s))


def _mla_keys(knt, kpt):
    n = knt.shape[1]
    return jnp.concatenate([knt, kpt.astype(BF16), jnp.zeros((LANE - MLA_ROPE, n), BF16)], axis=0)


def _mla_prompt_kernel(q_ref, knt_ref, kpt_ref, v_ref, o_ref, *, t):
    qi = pl.program_id(2)
    hs = range(2)
    qs = [q_ref[:, hh * 2 * LANE:(hh + 1) * 2 * LANE] for hh in hs]

    def chunk(j, carry, mask):
        c0 = pl.multiple_of(j * t, t)
        kpt = kpt_ref[0, :, pl.ds(c0, t)]
        ss = [_dot(qs[hh], _mla_keys(knt_ref[0, hh * LANE:(hh + 1) * LANE, pl.ds(c0, t)], kpt)) for hh in hs]
        if mask is not None:
            ss = [jnp.where(mask, s, NEG_BIG) for s in ss]
        m_new = [jnp.maximum(carry[hh][0], jnp.max(ss[hh], axis=1, keepdims=True)) for hh in hs]
        ps = [jnp.exp2(ss[hh] - m_new[hh]) for hh in hs]
        pvs = [_dot(ps[hh].astype(BF16), v_ref[pl.ds(c0, t), hh * LANE:(hh + 1) * LANE]) for hh in hs]
        out = []
        for hh in hs:
            alpha = jnp.exp2(carry[hh][0] - m_new[hh])
            out.append((m_new[hh], alpha * carry[hh][1] + jnp.sum(ps[hh], axis=1, keepdims=True),
                        alpha * carry[hh][2] + pvs[hh]))
        return tuple(out)

    init = tuple((jnp.full((t, 1), NEG_BIG, F32), jnp.zeros((t, 1), F32), jnp.zeros((t, LANE), F32))
                 for _ in hs)
    st = lax.fori_loop(0, qi, lambda j, c: chunk(j, c, None), init)
    rc = lax.broadcasted_iota(jnp.int32, (t, t), 0) // CHUNK
    cc = lax.broadcasted_iota(jnp.int32, (t, t), 1) // CHUNK
    st = chunk(qi, st, rc >= cc)
    o_ref[...] = jnp.concatenate([st[hh][2] / st[hh][1] for hh in hs], axis=1).astype(BF16)


def mla_prompt(qx, knt, kpt, v, row0, bsz, s, t):
    m = qx.shape[0]
    nq = s // t
    return pl.pallas_call(
        functools.partial(_mla_prompt_kernel, t=t),
        out_shape=jax.ShapeDtypeStruct((m, BRANCH_W), BF16),
        grid=(bsz, MLA_HEADS // 2, nq),
        in_specs=[pl.BlockSpec((t, 4 * LANE), lambda b, h, i: (b * nq + i, h)),
                  pl.BlockSpec((1, 2 * LANE, s), lambda b, h, i: (b, h, 0)),
                  pl.BlockSpec((1, MLA_ROPE, s), lambda b, h, i: (row0 + b, 0, 0)),
                  pl.BlockSpec((s, 2 * LANE), lambda b, h, i: (b, h))],
        out_specs=pl.BlockSpec((t, 2 * LANE), lambda b, h, i: (b * nq + i, h)),
        compiler_params=_params(("parallel", "parallel", "arbitrary")),
        name="mla_prompt",
    )(qx, knt, kpt, v)


def _fox_sample_kernel(q_ref, ktn_ref, vtn_ref, ktc_ref, vtc_ref, ctn_ref, ctc_ref, o_ref, *, t, p):
    hp = pl.program_id(1)
    lane = lax.broadcasted_iota(jnp.int32, (1, LANE), 1)
    lo = lane < FOX_DIM
    q = q_ref[...] * (FOX_DIM ** -0.5 * LOG2E)
    kt_c = ktc_ref[0].astype(BF16)
    kt_n = ktn_ref[0].astype(BF16)
    vt_c = vtc_ref[0]
    vt_n = vtn_ref[0]
    causal = (lax.broadcasted_iota(jnp.int32, (t, t), 0) >= lax.broadcasted_iota(jnp.int32, (t, t), 1))
    accs = []
    for hh in range(2):
        head = 2 * hp + hh
        qh = (jnp.where(lo, q, 0.0) if hh == 0 else jnp.where(lo, 0.0, q)).astype(BF16)
        cc = ctc_ref[0, pl.ds(head, 1), :]
        ctot = cc[:, p - 1:p]
        s_c = _dot(qh, kt_c) + (ctot - cc) * LOG2E
        s_n = _dot(qh, kt_n) - ctn_ref[0, pl.ds(head, 1), :] * LOG2E
        s_n = jnp.where(causal, s_n, NEG_BIG)
        m = jnp.maximum(jnp.max(s_c, axis=1, keepdims=True), jnp.max(s_n, axis=1, keepdims=True))
        rows = _pair_rows_mask(hh)
        accs.append(_dot_nt(jnp.exp2(s_c - m).astype(BF16), jnp.where(rows, vt_c, 1.0).astype(BF16))
                    + _dot_nt(jnp.exp2(s_n - m).astype(BF16), jnp.where(rows, vt_n, 1.0).astype(BF16)))
    o_ref[...] = _fox_finish(accs[0], accs[1]).astype(BF16)


def fox_sample(big, kt_n, vt_n, kt_c, vt_c, cumt_n, cumt_c, bsz, t, p, layer):
    cidx = lambda b, h: (layer * bsz + b, h, 0)
    return pl.pallas_call(
        functools.partial(_fox_sample_kernel, t=t, p=p),
        out_shape=jax.ShapeDtypeStruct((bsz * t, BRANCH_W), BF16),
        grid=(bsz, FOX_HEADS // 2),
        in_specs=[pl.BlockSpec((t, LANE), lambda b, h: (b, OFF_FQ // LANE + h)),
                  pl.BlockSpec((1, LANE, t), cidx),
                  pl.BlockSpec((1, LANE, t), cidx),
                  pl.BlockSpec((1, LANE, p), cidx),
                  pl.BlockSpec((1, LANE, p), cidx),
                  pl.BlockSpec((1, FOX_HEADS, t), lambda b, h: (b, 0, 0)),
                  pl.BlockSpec((1, FOX_HEADS, p), lambda b, h: (layer * bsz + b, 0, 0))],
        out_specs=pl.BlockSpec((t, LANE), lambda b, h: (b, h)),
        compiler_params=_params(("parallel", "parallel")),
        name="fox_sample",
    )(big, kt_n, vt_n, kt_c, vt_c, cumt_n, cumt_c)


def _mla_sample_kernel(q_ref, kntn_ref, kptn_ref, vn_ref, kntc_ref, kptc_ref, vc_ref, o_ref, *, t, p):
    q = q_ref[...]
    s_c = _dot(q, _mla_keys(kntc_ref[0], kptc_ref[0]))
    s_n = _dot(q, _mla_keys(kntn_ref[0], kptn_ref[0]))
    qc = (p + lax.broadcasted_iota(jnp.int32, (t, t), 0)) // CHUNK
    kc = (p + lax.broadcasted_iota(jnp.int32, (t, t), 1)) // CHUNK
    s_n = jnp.where(qc >= kc, s_n, NEG_BIG)
    m = jnp.maximum(jnp.max(s_c, axis=1, keepdims=True), jnp.max(s_n, axis=1, keepdims=True))
    p_c = jnp.exp2(s_c - m)
    p_n = jnp.exp2(s_n - m)
    l = jnp.sum(p_c, axis=1, keepdims=True) + jnp.sum(p_n, axis=1, keepdims=True)
    o = _dot(p_c.astype(BF16), vc_ref[...]) + _dot(p_n.astype(BF16), vn_ref[...])
    o_ref[...] = (o / l).astype(BF16)


def mla_sample(qx, knt_n, kpt_n, v_n, knt_c, kpt_c, v_c, bsz, t, p, layer):
    assert (p - 1) // CHUNK <= p // CHUNK
    return pl.pallas_call(
        functools.partial(_mla_sample_kernel, t=t, p=p),
        out_shape=jax.ShapeDtypeStruct((bsz * t, BRANCH_W), BF16),
        grid=(bsz, MLA_HEADS),
        in_specs=[pl.BlockSpec((t, 2 * LANE), lambda b, h: (b, h)),
                  pl.BlockSpec((1, LANE, t), lambda b, h: (b, h, 0)),
                  pl.BlockSpec((1, MLA_ROPE, t), lambda b, h: (layer * bsz + b, 0, 0)),
                  pl.BlockSpec((t, LANE), lambda b, h: (b, h)),
                  pl.BlockSpec((1, LANE, p), lambda b, h: (b, h, 0)),
                  pl.BlockSpec((1, MLA_ROPE, p), lambda b, h: (layer * bsz + b, 0, 0)),
                  pl.BlockSpec((p, LANE), lambda b, h: (b, h))],
        out_specs=pl.BlockSpec((t, LANE), lambda b, h: (b, h)),
        compiler_params=_params(("parallel", "parallel")),
        name="mla_sample",
    )(qx, knt_n, kpt_n, v_n, knt_c, kpt_c, v_c)


def _latent_expand_kernel(c_ref, wkt_ref, wv_ref, knt_ref, v_ref):
    cb = c_ref[...].astype(BF16)
    knt_ref[0] = _dot_nt(wkt_ref[...], cb).astype(BF16)
    v_ref[...] = _dot(cb, wv_ref[...]).astype(BF16)


def latent_expand(ckv, wkt, wv, bsz, p, tm, layer):
    m = bsz * p
    nt = p // tm
    return pl.pallas_call(
        _latent_expand_kernel,
        out_shape=(jax.ShapeDtypeStruct((bsz, 512, p), BF16), jax.ShapeDtypeStruct((m, 512), BF16)),
        grid=(m // tm,),
        in_specs=[pl.BlockSpec((tm, MLA_KV_RANK), lambda i: (layer * (m // tm) + i, 0)),
                  pl.BlockSpec(wkt.shape, lambda i: (0, 0)), pl.BlockSpec(wv.shape, lambda i: (0, 0))],
        out_specs=(pl.BlockSpec((1, 512, tm), lambda i: (i // nt, 0, i % nt)),
                   pl.BlockSpec((tm, 512), lambda i: (i, 0))),
        compiler_params=_params(("parallel",)),
        name="latent_expand",
    )(ckv, wkt, wv)


def _hgrn_gates(z, lb, tri):
    logf = _log_sigmoid(z) + jnp.log(1.0 + lb * jnp.exp(jnp.minimum(-z, EXP_CLIP)))
    k = (1.0 - lb) * (1.0 / (1.0 + jnp.exp(z)))
    h1 = logf.astype(BF16)
    r1 = logf - h1.astype(F32)
    h2 = r1.astype(BF16)
    h3 = (r1 - h2.astype(F32)).astype(BF16)
    parts = _dot(tri, jnp.concatenate([h1, h2, h3], axis=1))
    lc = ((parts[:, :LANE] + parts[:, LANE:2 * LANE]) + parts[:, 2 * LANE:]) * LOG2E
    return k, lc


def _hgrn_local(q, z, lb, tri, v_b, ln, sel):
    k, lc = _hgrn_gates(z, lb, tri)
    nchunk = q.shape[0] // ln
    nsb = q.shape[0] // SUB_BLOCK
    per = ln // SUB_BLOCK
    half = SUB_BLOCK // 2
    rows = lambda a, i: a[i * SUB_BLOCK:(i + 1) * SUB_BLOCK, :]
    lcb = [jnp.zeros((1, LANE), F32) if i % per == 0 else lc[i * SUB_BLOCK - 1:i * SUB_BLOCK, :]
           for i in range(nsb)]
    lcb_rows = jnp.concatenate([jnp.broadcast_to(b, (SUB_BLOCK, LANE)) for b in lcb], axis=0)
    last = [lc[(c + 1) * ln - 1:(c + 1) * ln, :] for c in range(nchunk)]
    last_rows = jnp.concatenate([jnp.broadcast_to(b, (ln, LANE)) for b in last], axis=0)
    qh = (q * jnp.exp2(lc - lcb_rows)).astype(BF16)
    qe = (q * jnp.exp2(lc)).astype(BF16)
    kdec = (k * jnp.exp2(last_rows - lc)).astype(BF16)
    a_off = {}
    for i in range(nsb):
        n = (i % per) * SUB_BLOCK
        if n:
            c0 = i * SUB_BLOCK - n
            kt = (k[c0:c0 + n, :] * jnp.exp2(lcb[i] - lc[c0:c0 + n, :])).astype(BF16)
            a_off[i] = _dot_nt(rows(qh, i), kt)
    yield None
    pieces = []
    for i in range(nsb):
        q_i, k_i, lc_i = rows(q, i), rows(k, i), rows(lc, i)
        cols = []
        for s in range(SUB_BLOCK):
            lo = 0 if s < half else half
            d = lc_i[lo:, :] - lc_i[s:s + 1, :]
            d = (jnp.concatenate([jnp.minimum(d[:half, :], 0.0), d[half:, :]], axis=0) if s < half
                 else jnp.minimum(d, 0.0))
            w = (q_i[lo:, :] * k_i[s:s + 1, :]) * jnp.exp2(d)
            if lo:
                w = jnp.concatenate([jnp.zeros((lo, LANE), F32), w], axis=0)
            cols.append(w.astype(BF16))
        pieces.append(jnp.concatenate(cols, axis=1))
    a_all = _dot(jnp.concatenate(pieces, axis=0), sel)
    inc = [_dot_tn(v_b[c * ln:(c + 1) * ln, :], kdec[c * ln:(c + 1) * ln, :]) for c in range(nchunk)]
    yield None
    pair_ok = (lax.broadcasted_iota(jnp.int32, (SUB_BLOCK, LANE), 0)
               >= lax.broadcasted_iota(jnp.int32, (SUB_BLOCK, LANE), 1))
    off = []
    for i in range(nsb):
        n = (i % per) * SUB_BLOCK
        off.append(_dot(a_off[i].astype(BF16), v_b[i * SUB_BLOCK - n:i * SUB_BLOCK, :]) if n
                   else jnp.zeros((SUB_BLOCK, LANE), F32))
    diag = [_dot(jnp.where(pair_ok, rows(a_all, i), 0.0)[:, :SUB_BLOCK].astype(BF16), rows(v_b, i))
            for i in range(nsb)]
    local = jnp.concatenate(off, axis=0) + jnp.concatenate(diag, axis=0)
    dec = [jnp.exp2(b) for b in last]
    yield local, qe, inc, dec


def _hgrn_kernel(*refs, ln, nchunk, has_init):
    refs = list(refs)
    hq_ref, hf_ref, hi_ref, hg_ref, lb_ref, go_ref, sel_ref = refs[:7]
    s0_ref = refs[7] if has_init else None
    o_ref, sout_ref, st_ref = refs[-3:]
    step = pl.program_id(1)
    nrows = ln * nchunk

    @pl.when(step == 0)
    def _():
        for h in range(HG_HEADS):
            st_ref[h] = s0_ref[0, h].T if has_init else jnp.zeros((HG_DV, HG_DK), F32)

    ri = lax.broadcasted_iota(jnp.int32, (nrows, nrows), 0)
    ci = lax.broadcasted_iota(jnp.int32, (nrows, nrows), 1)
    tri = ((ri >= ci) & (ri // ln == ci // ln)).astype(BF16)
    def finish(h, local, qe, inc, dec):
        cs = slice(h * LANE, (h + 1) * LANE)
        st = st_ref[h]
        parts = []
        for c in range(nchunk):
            parts.append(_dot_nt(qe[c * ln:(c + 1) * ln, :], st.astype(BF16)))
            st = st * dec[c] + inc[c]
        st_ref[h] = st
        o = local + jnp.concatenate(parts, axis=0)
        o_ref[:, cs] = (_rms(o, go_ref[...]) * _sigmoid(hg_ref[:, cs])).astype(BF16)

    def start(h):
        cs = slice(h * LANE, (h + 1) * LANE)
        gen = _hgrn_local(hq_ref[:, cs], hf_ref[:, cs], lb_ref[:, cs], tri, hi_ref[:, cs].astype(BF16),
                          ln, sel_ref[...])
        next(gen)
        return gen

    gens = {0: start(0)}
    for h in range(HG_HEADS):
        if h + 1 < HG_HEADS:
            gens[h + 1] = start(h + 1)
        next(gens[h])
        if h > 0:
            finish(h - 1, *next(gens.pop(h - 1)))
    finish(HG_HEADS - 1, *next(gens.pop(HG_HEADS - 1)))

    @pl.when(step == pl.num_programs(1) - 1)
    def _():
        for h in range(HG_HEADS):
            sout_ref[0, h] = st_ref[h].T


def hgrn(big, lb, g_out, bsz, s, ln, rows, s0=None, s0_row0=0):
    m = big.shape[0]
    ns = s // rows
    has_init = s0 is not None
    sel = (np.arange(SUB_BLOCK * LANE)[:, None] // LANE == np.arange(LANE)[None, :])
    sel = jnp.asarray(sel, BF16)
    blk = lambda off: pl.BlockSpec((rows, BRANCH_W), lambda b, i: (b * ns + i, off // BRANCH_W))
    ins = [big, big, big, big, lb, g_out, sel]
    specs = [blk(OFF_HQ), blk(OFF_HF), blk(OFF_HI), blk(OFF_HG),
             pl.BlockSpec((1, BRANCH_W), lambda b, i: (0, 0)),
             pl.BlockSpec((1, HG_DV), lambda b, i: (0, 0)),
             pl.BlockSpec(sel.shape, lambda b, i: (0, 0))]
    if has_init:
        ins.append(s0)
        specs.append(pl.BlockSpec((1, HG_HEADS, HG_DK, HG_DV), lambda b, i: (s0_row0 + b, 0, 0, 0)))
    return pl.pallas_call(
        functools.partial(_hgrn_kernel, ln=ln, nchunk=rows // ln, has_init=has_init),
        out_shape=(jax.ShapeDtypeStruct((m, BRANCH_W), BF16),
                   jax.ShapeDtypeStruct((bsz, HG_HEADS, HG_DK, HG_DV), F32)),
        grid=(bsz, ns),
        in_specs=specs,
        out_specs=(pl.BlockSpec((rows, BRANCH_W), lambda b, i: (b * ns + i, 0)),
                   pl.BlockSpec((1, HG_HEADS, HG_DK, HG_DV), lambda b, i: (b, 0, 0, 0))),
        scratch_shapes=[pltpu.VMEM((HG_HEADS, HG_DV, HG_DK), F32)],
        compiler_params=_params(("parallel", "arbitrary")),
        name="hgrn",
    )(*ins)


def _merge_kernel(of_ref, om_ref, oh_ref, ga_ref, gb_ref, gc_ref, x_ref, wb_ref, wo_ref, g_ref, o_ref):
    merged = (_sigmoid(ga_ref[...]) * _dot(of_ref[...], wb_ref[0])
              + _sigmoid(gb_ref[...]) * _dot(om_ref[...], wb_ref[1])
              + _sigmoid(gc_ref[...]) * _dot(oh_ref[...], wb_ref[2]))
    y = _dot(merged.astype(BF16), wo_ref[...])
    o_ref[...] = x_ref[...] + _rms(y, g_ref[...])


def merge_out(o_fox, o_mla, o_hg, big, x, wb, wo, g, tm):
    m = x.shape[0]
    row = lambda w: pl.BlockSpec((tm, w), lambda i: (i, 0))
    gate = lambda off: pl.BlockSpec((tm, D_MODEL), lambda i: (i, off // D_MODEL))
    return pl.pallas_call(
        _merge_kernel,
        out_shape=jax.ShapeDtypeStruct((m, D_MODEL), F32),
        grid=(m // tm,),
        in_specs=[row(BRANCH_W), row(BRANCH_W), row(BRANCH_W),
                  gate(OFF_GA), gate(OFF_GA + D_MODEL), gate(OFF_GA + 2 * D_MODEL),
                  row(D_MODEL), _resident(wb.shape), _resident(wo.shape),
                  pl.BlockSpec((1, D_MODEL), lambda i: (0, 0))],
        out_specs=row(D_MODEL),
        compiler_params=_params(("parallel",)),
        name="merge_out",
    )(o_fox, o_mla, o_hg, big, big, big, x, wb, wo, g)


def _matmul2_kernel(x_ref, w_ref, a_ref, b_ref):
    y = _dot(x_ref[...].astype(BF16), w_ref[...])
    n = a_ref.shape[1]
    a_ref[...] = y[:, :n]
    b_ref[...] = y[:, n:]


def mem_kv(mem, w, tm):
    m, k = mem.shape
    n = w.shape[1] // 2
    row = lambda w_: pl.BlockSpec((tm, w_), lambda i: (i, 0))
    return pl.pallas_call(
        _matmul2_kernel,
        out_shape=(jax.ShapeDtypeStruct((m, n), F32), jax.ShapeDtypeStruct((m, n), F32)),
        grid=(m // tm,),
        in_specs=[row(k), _resident(w.shape)],
        out_specs=(row(n), row(n)),
        compiler_params=_params(("parallel",)),
        name="mem_kv",
    )(mem, w)


def _cross_kernel(x_ref, mk_ref, mv_ref, wq_ref, wo_ref, g2_ref, g3_ref, o_ref):
    x = x_ref[...]
    h = _rms(x, g2_ref[...]).astype(BF16)
    q = _dot(h, wq_ref[...])
    scale = X_DIM ** -0.5
    outs = []
    for hd in range(X_HEADS):
        cs = slice(hd * X_DIM, (hd + 1) * X_DIM)
        s = _dot_nt(q[:, cs].astype(BF16), mk_ref[:, cs].astype(BF16)) * scale
        p = jnp.exp(s - jnp.max(s, axis=1, keepdims=True))
        l = jnp.sum(p, axis=1, keepdims=True)
        outs.append(_dot(p.astype(BF16), mv_ref[:, cs].astype(BF16)) / l)
    ox = jnp.concatenate(outs, axis=1).astype(BF16)
    o_ref[...] = x + _rms(_dot(ox, wo_ref[...]), g3_ref[...])


def cross_block(x, mk, mv, wq, wo, g2, g3, bsz, s, tm, mem_row0):
    m = x.shape[0]
    nt = s // tm
    vec = pl.BlockSpec((1, D_MODEL), lambda b, i: (0, 0))
    return pl.pallas_call(
        _cross_kernel,
        out_shape=jax.ShapeDtypeStruct((m, D_MODEL), F32),
        grid=(bsz, nt),
        in_specs=[pl.BlockSpec((tm, D_MODEL), lambda b, i: (b * nt + i, 0)),
                  pl.BlockSpec((N_MEM, X_HEADS * X_DIM), lambda b, i: (mem_row0 + b, 0)),
                  pl.BlockSpec((N_MEM, X_HEADS * X_DIM), lambda b, i: (mem_row0 + b, 0)),
                  _resident(wq.shape), _resident(wo.shape), vec, vec],
        out_specs=pl.BlockSpec((tm, D_MODEL), lambda b, i: (b * nt + i, 0)),
        compiler_params=_params(("parallel", "parallel")),
        name="cross_attn",
    )(x, mk, mv, wq, wo, g2, g3)


def _mlp_kernel(x_ref, wu_ref, wd_ref, g4_ref, g5_ref, o_ref, h_ref, acc_ref):
    j = pl.program_id(1)

    @pl.when(j == 0)
    def _():
        h_ref[...] = _rms(x_ref[...], g4_ref[...]).astype(BF16)
        acc_ref[...] = jnp.zeros_like(acc_ref)

    u = jnp.square(jnp.maximum(_dot(h_ref[...], wu_ref[j]), 0.0)).astype(BF16)
    acc_ref[...] += _dot(u, wd_ref[j])

    @pl.when(j == pl.num_programs(1) - 1)
    def _():
        o_ref[...] = x_ref[...] + _rms(acc_ref[...], g5_ref[...])


def mlp_block(x, wu3, wd3, g4, g5, tm):
    m = x.shape[0]
    nj = wu3.shape[0]
    vec = pl.BlockSpec((1, D_MODEL), lambda i, j: (0, 0))
    return pl.pallas_call(
        _mlp_kernel,
        out_shape=jax.ShapeDtypeStruct((m, D_MODEL), F32),
        grid=(m // tm, nj),
        in_specs=[pl.BlockSpec((tm, D_MODEL), lambda i, j: (i, 0)),
                  _resident(wu3.shape), _resident(wd3.shape), vec, vec],
        out_specs=pl.BlockSpec((tm, D_MODEL), lambda i, j: (i, 0)),
        scratch_shapes=[pltpu.VMEM((tm, D_MODEL), BF16), pltpu.VMEM((tm, D_MODEL), F32)],
        compiler_params=_params(("parallel", "arbitrary")),
        name="mlp",
    )(x, wu3, wd3, g4, g5)


def _prep_layer_weights(w_in, w_mla_uq, w_mla_ukv, w_branch, w_out, w_xq, w_mem_k, w_mem_v, w_xo,
                        w_up, w_down):
    idx = np.cumsum((0,) + IN_SIZES)
    seg = lambda i: w_in[:, idx[i]:idx[i + 1]]
    fq, fk, fv, ff, cq, ckv, kpe, hq, hf, hi, hg, ga, gb, gc = (seg(i) for i in range(14))
    half = MLA_ROPE // 2
    kpe_sw = jnp.concatenate([kpe[:, half:], kpe[:, :half]], axis=1)
    pad = jnp.zeros((D_MODEL, OFF_CQ - OFF_FQ - BRANCH_W), w_in.dtype)
    w_p = jnp.concatenate([hq, hf, hi, hg, ga, gb, gc, fq, pad, cq, ckv], axis=1).astype(BF16)
    w_in3 = w_p.reshape(D_MODEL, NP_IN // IN_TN, IN_TN).transpose(1, 0, 2)
    w_t = jnp.concatenate([fk, fv, kpe, kpe_sw, ff], axis=1).T.astype(BF16)
    hd = MLA_NOPE + MLA_ROPE
    zq = jnp.zeros((MLA_Q_RANK, LANE - MLA_ROPE), w_mla_uq.dtype)
    nope, rope_n, rope_s = [], [], []
    for h in range(MLA_HEADS):
        base = h * hd
        nope.append(w_mla_uq[:, base:base + MLA_NOPE])
        x1 = w_mla_uq[:, base + MLA_NOPE:base + MLA_NOPE + half]
        x2 = w_mla_uq[:, base + MLA_NOPE + half:base + hd]
        rope_n += [x1, x2, zq]
        rope_s += [x2, x1, zq]
    wuq = jnp.concatenate(nope + rope_n + rope_s, axis=1).astype(BF16)
    kvd = MLA_NOPE + MLA_V
    wkt = jnp.concatenate([w_mla_ukv[:, h * kvd:h * kvd + MLA_NOPE] for h in range(MLA_HEADS)],
                          axis=1).T.astype(BF16)
    wv = jnp.concatenate([w_mla_ukv[:, h * kvd + MLA_NOPE:(h + 1) * kvd] for h in range(MLA_HEADS)],
                         axis=1).astype(BF16)
    nff = D_FF // D_MODEL
    return dict(
        w_in3=w_in3, w_t=w_t, wuq=wuq, wkt=wkt, wv=wv,
        wb=w_branch.astype(BF16), wo=w_out.astype(BF16), wxq=w_xq.astype(BF16), wxo=w_xo.astype(BF16),
        wmem=jnp.concatenate([w_mem_k, w_mem_v], axis=1).astype(BF16),
        wu3=w_up.astype(BF16).reshape(D_MODEL, nff, D_MODEL).transpose(1, 0, 2),
        wd3=w_down.astype(BF16).reshape(nff, D_MODEL, D_MODEL))


def _rope_tables(pos, reps):
    half = MLA_ROPE // 2
    freq = ROPE_THETA ** (-jnp.arange(half, dtype=F32) / half)
    ang = pos.astype(F32)[:, None] * freq[None, :]
    cos, sin = jnp.cos(ang), jnp.sin(ang)
    z = jnp.zeros((pos.shape[0], LANE - MLA_ROPE), F32)
    cos_r = jnp.tile(jnp.concatenate([cos, cos, z], axis=1), (reps, 1))
    sin_r = jnp.tile(jnp.concatenate([-sin, sin, z], axis=1), (reps, 1))
    return cos_r, sin_r, cos_r[:, :MLA_ROPE].T, sin_r[:, :MLA_ROPE].T


def _tile(n, pref):
    t = min(n, pref)
    assert n % t == 0
    return t


def _layer(x, bsz, s, pos0, w, lb, b_fox, g_q, g_kv, g_hout, g_norm, mem_k, mem_v, mem_row0, past, cfg,
           layer, depth, shared):
    m = bsz * s
    g = lambda i: g_norm[i][None, :]
    tm_in = _tile(s, cfg["tm_in"])
    tm_p = _tile(s, cfg["tm_prep"])
    cos_r, sin_r, cos_c, sin_c = _rope_tables(pos0 + jnp.arange(s), 1)
    big, kt, vt, kpet, logft = in_proj(x, g(0), w["w_in3"], w["w_t"], cos_c, sin_c, b_fox[:, None],
                                       bsz, s, tm_in, layer, depth, shared[:4])
    qx, knt, v, ckv_n = mla_prep(big, cos_r, sin_r, g_q[None, :], g_kv[None, :],
                                 w["wuq"], w["wkt"], w["wv"], bsz, s, tm_p, layer, depth, shared[4:])
    row0 = layer * bsz
    rows = lambda a: a.reshape((depth * bsz,) + a.shape[2:])
    cumt = fox_cumsum(rows(logft), row0, bsz)
    if past is None:
        t = _tile(s, cfg["t_attn"])
        o_fox = fox_prompt(big, rows(kt), rows(vt), cumt, row0, bsz, s, t)
        o_mla = mla_prompt(qx, knt, rows(kpet), v, row0, bsz, s, t)
        o_hg, hg_state = hgrn(big, lb[None, :], g_hout[None, :], bsz, s, CHUNK,
                              _tile(s, cfg["hg_rows"]))
    else:
        c_kt, c_vt, c_cumt, c_knt, c_kpt, c_v, c_hg = past
        p = c_kt.shape[2]
        o_fox = fox_sample(big, rows(kt), rows(vt), c_kt, c_vt, cumt, c_cumt, bsz, s, p, layer)
        o_mla = mla_sample(qx, knt, rows(kpet), v, c_knt, c_kpt, c_v, bsz, s, p, layer)
        o_hg, hg_state = hgrn(big, lb[None, :], g_hout[None, :], bsz, s, s, s, s0=c_hg, s0_row0=row0)
    x = merge_out(o_fox, o_mla, o_hg, big, x, w["wb"], w["wo"], g(1), _tile(m, cfg["tm_merge"]))
    x = cross_block(x, mem_k, mem_v, w["wxq"], w["wxo"], g(2), g(3), bsz, s, _tile(s, cfg["tm_cross"]),
                    mem_row0)
    x = mlp_block(x, w["wu3"], w["wd3"], g(4), g(5), _tile(m, cfg["tm_mlp"]))
    return x, (kt, vt, kpet, logft, ckv_n), hg_state


def _from_feature_major(stacked, heads):
    a = jnp.swapaxes(stacked, 2, 3)
    if heads:
        a = a.reshape(a.shape[:3] + (heads, a.shape[3] // heads))
    return a


def _assemble_states(shared, hg_states, bsz, s):
    kt, vt, kpet, logft, ckv = shared
    return (_from_feature_major(kt, FOX_HEADS), _from_feature_major(vt, FOX_HEADS),
            _from_feature_major(logft, 0), ckv.reshape(ckv.shape[0], bsz, s, MLA_KV_RANK),
            _from_feature_major(kpet, 0), jnp.stack(hg_states))


_CFG = dict(tm_in=1024, tm_prep=512, t_attn=512, hg_rows=256, tm_merge=512, tm_cross=512,
            tm_mlp=1024, tm_mem=512, tm_expand=1024)


def kernel(x_prompt, x_sample, cache_fox_k, cache_fox_v, cache_fox_logf, cache_mla_ckv, cache_mla_kpe,
           state_hgrn, cache_mem_k, cache_mem_v, mem_prompt, w_in, b_fox, g_mla_q, w_mla_uq, g_mla_kv,
           w_mla_ukv, g_hgrn_out, lb_hgrn, w_branch, w_out, w_xq, w_mem_k, w_mem_v, w_xo, w_up, w_down,
           g_norm):
    cfg = _CFG
    depth = w_in.shape[0]
    lb_p = jax.nn.softmax(lb_hgrn.astype(F32), axis=0)
    lb_all = jnp.cumsum(lb_p, axis=0) - lb_p[0]
    ws = [_prep_layer_weights(w_in[l], w_mla_uq[l], w_mla_ukv[l], w_branch[l], w_out[l], w_xq[l],
                              w_mem_k[l], w_mem_v[l], w_xo[l], w_up[l], w_down[l]) for l in range(depth)]

    def run_layer(x, bsz, s, pos0, l, mk, mv, mem_row0, past, shared):
        return _layer(x, bsz, s, pos0, ws[l], lb_all[l], b_fox[l], g_mla_q[l], g_mla_kv[l],
                      g_hgrn_out[l], g_norm[l], mk, mv, mem_row0, past, cfg, l, depth, shared)

    bp, sp, _ = x_prompt.shape
    x = x_prompt.reshape(bp * sp, D_MODEL)
    mem = mem_prompt.reshape(bp * N_MEM, D_MODEL)
    shared, hg_states, p_mem = (), [], []
    for l in range(depth):
        mk, mv = mem_kv(mem, ws[l]["wmem"], _tile(bp * N_MEM, cfg["tm_mem"]))
        x, shared, hg = run_layer(x, bp, sp, 0, l, mk, mv, 0, None, shared)
        hg_states.append(hg)
        p_mem.append((mk.reshape(bp, N_MEM, X_HEADS, X_DIM), mv.reshape(bp, N_MEM, X_HEADS, X_DIM)))
    y_prompt = x.reshape(bp, sp, D_MODEL)
    p_out = _assemble_states(shared, hg_states, bp, sp) + tuple(jnp.stack(a) for a in zip(*p_mem))

    bs, ts, _ = x_sample.shape
    p = cache_fox_k.shape[2]
    fm = lambda c: jnp.moveaxis(c, 2, -1)
    c_kt = fm(cache_fox_k).reshape(depth * bs, BRANCH_W, p)
    c_vt = fm(cache_fox_v).reshape(depth * bs, BRANCH_W, p)
    c_kpt = fm(cache_mla_kpe).reshape(depth * bs, MLA_ROPE, p)
    c_cumt = fox_cumsum(fm(cache_fox_logf).reshape(depth * bs, FOX_HEADS, p), 0, depth * bs)
    c_ckv = cache_mla_ckv.reshape(depth * bs * p, MLA_KV_RANK)
    c_hg = state_hgrn.reshape((depth * bs,) + state_hgrn.shape[2:])
    c_mk = cache_mem_k.reshape(depth * bs * N_MEM, X_HEADS * X_DIM)
    c_mv = cache_mem_v.reshape(depth * bs * N_MEM, X_HEADS * X_DIM)
    x = x_sample.reshape(bs * ts, D_MODEL)
    shared, hg_states = (), []
    for l in range(depth):
        c_knt, c_v = latent_expand(c_ckv, ws[l]["wkt"], ws[l]["wv"], bs, p, _tile(p, cfg["tm_expand"]), l)
        past = (c_kt, c_vt, c_cumt, c_knt, c_kpt, c_v, c_hg)
        x, shared, hg = run_layer(x, bs, ts, p, l, c_mk, c_mv, l * bs, past, shared)
        hg_states.append(hg)
    y_sample = x.reshape(bs, ts, D_MODEL)
    return (y_prompt, y_sample, *p_out, *_assemble_states(shared, hg_states, bs, ts))
```

```python
import functools

import numpy as np
import jax
import jax.numpy as jnp
from jax import lax
from jax.experimental import pallas as pl
from jax.experimental.pallas import tpu as pltpu

F32 = jnp.float32
BF16 = jnp.bfloat16

D_MODEL = 1024
CHUNK = 64
N_MEM = 256
EPS = 1e-6
NEG_BIG = -1e30
EXP_CLIP = 80.0
FOX_HEADS = 8
FOX_DIM = 64
MLA_HEADS = 4
MLA_Q_RANK = 384
MLA_KV_RANK = 256
MLA_NOPE = 128
MLA_ROPE = 64
MLA_V = 128
ROPE_THETA = 10000.0
HG_HEADS = 4
HG_DK = 128
HG_DV = 128
X_HEADS = 4
X_DIM = 128
D_FF = 4 * D_MODEL
BRANCH_W = 512
IN_SIZES = (512, 512, 512, FOX_HEADS, MLA_Q_RANK, MLA_KV_RANK, MLA_ROPE, 512, 512, 512, 512,
            D_MODEL, D_MODEL, D_MODEL)

LANE = 128
SUB_BLOCK = 16
VMEM_LIMIT = 56 * 1024 * 1024
LOG2E = 1.4426950408889634

OFF_HQ, OFF_HF, OFF_HI, OFF_HG, OFF_GA, OFF_FQ, OFF_CQ, OFF_CKV, NP_IN = (
    0, 512, 1024, 1536, 2048, 5120, 5760, 6144, 6400)
IN_TN = 1280
T_FK, T_FV, T_KPE, T_KPE_SW, T_FF, NT_IN = 0, 512, 1024, 1088, 1152, 1160


def _params(sem, vmem=VMEM_LIMIT):
    return pltpu.CompilerParams(dimension_semantics=sem, vmem_limit_bytes=vmem)


def _dot(a, b):
    return jnp.dot(a, b, preferred_element_type=F32)


def _dot_nt(a, b):
    return lax.dot_general(a, b, (((1,), (1,)), ((), ())), preferred_element_type=F32)


def _dot_tn(a, b):
    return lax.dot_general(a, b, (((0,), (0,)), ((), ())), preferred_element_type=F32)


def _rms(x, g):
    y = x * lax.rsqrt(jnp.mean(x * x, axis=-1, keepdims=True) + EPS)
    return y * g


def _log_sigmoid(z):
    return jnp.minimum(z, 0.0) - jnp.log(1.0 + jnp.exp(-jnp.abs(z)))


def _sigmoid(z):
    return 1.0 / (1.0 + jnp.exp(-z))


def _resident(shape):
    nd = len(shape)
    return pl.BlockSpec(shape, lambda *_: (0,) * nd, pipeline_mode=pl.Buffered(1))


def _in_proj_kernel(x_ref, g_ref, w_ref, wt_ref, cos_ref, sin_ref, bf_ref, *rest):
    big_ref, kt_ref, vt_ref, kpe_ref, lf_ref, h_ref = rest[-6:]
    j = pl.program_id(1)

    @pl.when(j == 0)
    def _():
        h = _rms(x_ref[...], g_ref[...]).astype(BF16)
        h_ref[...] = h
        yt = _dot_nt(wt_ref[...], h)
        kt_ref[0] = yt[T_FK:T_FK + BRANCH_W]
        vt_ref[0] = yt[T_FV:T_FV + BRANCH_W]
        kpe_ref[0] = (yt[T_KPE:T_KPE + MLA_ROPE] * cos_ref[...]
                      + yt[T_KPE_SW:T_KPE_SW + MLA_ROPE] * sin_ref[...])
        lf_ref[0] = _log_sigmoid(yt[T_FF:T_FF + FOX_HEADS] + bf_ref[...])

    big_ref[...] = _dot(h_ref[...], w_ref[j])


def in_proj(x, g, w3, wt, cos_t, sin_t, b_col, bsz, s, tm, layer, depth, prev):
    m, k = x.shape
    nj, _, tn = w3.shape
    nt = s // tm
    ntab = cos_t.shape[1] // tm
    feats = (BRANCH_W, BRANCH_W, MLA_ROPE, FOX_HEADS)
    tspec = lambda rows: pl.BlockSpec((None, 1, rows, tm), lambda i, j: (layer, i // nt, 0, i % nt))
    n_in = 7
    return pl.pallas_call(
        _in_proj_kernel,
        out_shape=(jax.ShapeDtypeStruct((m, nj * tn), F32),)
        + tuple(jax.ShapeDtypeStruct((depth, bsz, f, s), F32) for f in feats),
        grid=(m // tm, nj),
        in_specs=[pl.BlockSpec((tm, k), lambda i, j: (i, 0)),
                  pl.BlockSpec((1, k), lambda i, j: (0, 0)),
                  _resident((nj, k, tn)),
                  _resident(wt.shape),
                  pl.BlockSpec((MLA_ROPE, tm), lambda i, j: (0, i % ntab)),
                  pl.BlockSpec((MLA_ROPE, tm), lambda i, j: (0, i % ntab)),
                  pl.BlockSpec((FOX_HEADS, 1), lambda i, j: (0, 0))]
        + [pl.BlockSpec(memory_space=pl.ANY)] * len(prev),
        out_specs=(pl.BlockSpec((tm, tn), lambda i, j: (i, j)),) + tuple(tspec(f) for f in feats),
        scratch_shapes=[pltpu.VMEM((tm, k), BF16)],
        input_output_aliases={n_in + i: 1 + i for i in range(len(prev))},
        compiler_params=_params(("parallel", "arbitrary")),
        name="in_proj",
    )(x, g, w3, wt, cos_t, sin_t, b_col, *prev)


def _mla_prep_kernel(cq_ref, ckv_ref, cs_ref, sn_ref, gq_ref, gkv_ref, wuq_ref, wkt_ref, wv_ref, *rest):
    qx_ref, knt_ref, v_ref, ckvn_ref = rest[-4:]
    cos_t = cs_ref[...]
    sin_t = sn_ref[...]
    qn = _rms(cq_ref[...], gq_ref[...]).astype(BF16)
    qall = _dot(qn, wuq_ref[...]) * ((MLA_NOPE + MLA_ROPE) ** -0.5 * LOG2E)
    for h in range(MLA_HEADS):
        lo = h * LANE
        qr = (qall[:, 512 + lo:512 + lo + LANE] * cos_t
              + qall[:, 1024 + lo:1024 + lo + LANE] * sin_t)
        qx_ref[:, 2 * lo:2 * lo + LANE] = qall[:, lo:lo + LANE].astype(BF16)
        qx_ref[:, 2 * lo + LANE:2 * lo + 2 * LANE] = qr.astype(BF16)
    ckvn = _rms(ckv_ref[...], gkv_ref[...])
    ckvn_ref[...] = ckvn
    cb = ckvn.astype(BF16)
    knt_ref[0] = _dot_nt(wkt_ref[...], cb).astype(BF16)
    v_ref[...] = _dot(cb, wv_ref[...]).astype(BF16)


def mla_prep(big, cos_t, sin_t, gq, gkv, wuq, wkt, wv, bsz, s, tm, layer, depth, prev):
    m = big.shape[0]
    nt = s // tm
    ntab = cos_t.shape[0] // tm
    row = lambda w: pl.BlockSpec((tm, w), lambda i: (i, 0))
    full = lambda a: pl.BlockSpec(a.shape, lambda i: (0,) * a.ndim)
    n_in = 9
    return pl.pallas_call(
        _mla_prep_kernel,
        out_shape=(jax.ShapeDtypeStruct((m, 1024), BF16),
                   jax.ShapeDtypeStruct((bsz, 512, s), BF16),
                   jax.ShapeDtypeStruct((m, 512), BF16),
                   jax.ShapeDtypeStruct((depth, m, MLA_KV_RANK), F32)),
        grid=(m // tm,),
        in_specs=[pl.BlockSpec((tm, MLA_Q_RANK), lambda i: (i, OFF_CQ // MLA_Q_RANK)),
                  pl.BlockSpec((tm, MLA_KV_RANK), lambda i: (i, OFF_CKV // MLA_KV_RANK)),
                  pl.BlockSpec((tm, LANE), lambda i: (i % ntab, 0)),
                  pl.BlockSpec((tm, LANE), lambda i: (i % ntab, 0)),
                  full(gq), full(gkv), full(wuq), full(wkt), full(wv)]
        + [pl.BlockSpec(memory_space=pl.ANY)] * len(prev),
        out_specs=(row(1024), pl.BlockSpec((1, 512, tm), lambda i: (i // nt, 0, i % nt)),
                   row(512), pl.BlockSpec((None, tm, MLA_KV_RANK), lambda i: (layer, i, 0))),
        input_output_aliases={n_in + i: 3 + i for i in range(len(prev))},
        compiler_params=_params(("parallel",)),
        name="mla_prep",
    )(big, big, cos_t, sin_t, gq, gkv, wuq, wkt, wv, *prev)


def _cumsum_kernel(x_ref, c_ref, *, w):
    s = x_ref.shape[2]
    r = lax.broadcasted_iota(jnp.int32, (w, w), 0)
    c = lax.broadcasted_iota(jnp.int32, (w, w), 1)
    upper = (r <= c).astype(F32)
    local = [jnp.dot(x_ref[0, :, g * w:(g + 1) * w], upper, preferred_element_type=F32,
                     precision=lax.Precision.HIGHEST) for g in range(s // w)]
    carry = jnp.zeros((x_ref.shape[1], 1), F32)
    for g, cum in enumerate(local):
        cum = cum + carry
        c_ref[0, :, g * w:(g + 1) * w] = cum
        carry = cum[:, w - 1:w]


def fox_cumsum(x, row0, bsz):
    _, h, s = x.shape
    return pl.pallas_call(
        functools.partial(_cumsum_kernel, w=min(LANE, s)),
        out_shape=jax.ShapeDtypeStruct((bsz, h, s), F32),
        grid=(bsz,),
        in_specs=[pl.BlockSpec((1, h, s), lambda i: (row0 + i, 0, 0))],
        out_specs=pl.BlockSpec((1, h, s), lambda i: (i, 0, 0)),
        compiler_params=_params(("parallel",)),
        name="fox_cumsum",
    )(x)


def _pair_rows_mask(hh):
    sub = lax.broadcasted_iota(jnp.int32, (LANE, 1), 0)
    return (sub < FOX_DIM) if hh == 0 else (sub >= FOX_DIM)


def _fox_finish(acc0, acc1):
    lane = lax.broadcasted_iota(jnp.int32, (1, LANE), 1)
    o0 = acc0 / pltpu.roll(acc0, FOX_DIM, axis=1)
    o1 = acc1 / pltpu.roll(acc1, FOX_DIM, axis=1)
    return jnp.where(lane < FOX_DIM, o0, o1)


def _fox_prompt_kernel(q_ref, kt_ref, vt_ref, ct_ref, o_ref, *, t):
    hp = pl.program_id(1)
    qi = pl.program_id(2)
    lane = lax.broadcasted_iota(jnp.int32, (1, LANE), 1)
    lo = lane < FOX_DIM
    q = q_ref[...] * (FOX_DIM ** -0.5 * LOG2E)
    qs = (jnp.where(lo, q, 0.0).astype(BF16), jnp.where(lo, 0.0, q).astype(BF16))
    d0 = pl.multiple_of(qi * t, t)
    cref = tuple(ct_ref[0, 2 * hp + hh, :, pl.ds(d0, LANE)][:, 0:1] for hh in range(2))

    def step(carry, qrows, c0, w, mask):
        kt = kt_ref[0, :, pl.ds(c0, w)].astype(BF16)
        vt = vt_ref[0, :, pl.ds(c0, w)]
        ss = []
        for hh in range(2):
            bias = (cref[hh] - ct_ref[0, 2 * hp + hh, :, pl.ds(c0, w)]) * LOG2E
            s = _dot(qrows[hh], kt) + bias
            ss.append(s if mask is None else jnp.where(mask, s, NEG_BIG))
        vts = [jnp.where(_pair_rows_mask(hh), vt, 1.0).astype(BF16) for hh in range(2)]
        m_new = [jnp.maximum(carry[hh][0], jnp.max(ss[hh], axis=1, keepdims=True)) for hh in range(2)]
        ps = [jnp.exp2(ss[hh] - m_new[hh]).astype(BF16) for hh in range(2)]
        pvs = [_dot_nt(ps[hh], vts[hh]) for hh in range(2)]
        return tuple((m_new[hh], jnp.exp2(carry[hh][0] - m_new[hh]) * carry[hh][1] + pvs[hh])
                     for hh in range(2))

    init = tuple((jnp.full((t, 1), NEG_BIG, F32), jnp.zeros((t, LANE), F32)) for _ in range(2))
    st = lax.fori_loop(0, qi, lambda j, c: step(c, qs, pl.multiple_of(j * t, t), t, None), init)
    causal = (lax.broadcasted_iota(jnp.int32, (t, t), 0) >= lax.broadcasted_iota(jnp.int32, (t, t), 1))
    st = step(st, qs, d0, t, causal)
    o_ref[...] = _fox_finish(st[0][1], st[1][1]).astype(BF16)


def fox_prompt(big, kt, vt, cumt, row0, bsz, s, t):
    m = big.shape[0]
    nq = s // t
    return pl.pallas_call(
        functools.partial(_fox_prompt_kernel, t=t),
        out_shape=jax.ShapeDtypeStruct((m, BRANCH_W), BF16),
        grid=(bsz, FOX_HEADS // 2, nq),
        in_specs=[pl.BlockSpec((t, LANE), lambda b, h, i: (b * nq + i, OFF_FQ // LANE + h)),
                  pl.BlockSpec((1, LANE, s), lambda b, h, i: (row0 + b, h, 0)),
                  pl.BlockSpec((1, LANE, s), lambda b, h, i: (row0 + b, h, 0)),
                  pl.BlockSpec((1, FOX_HEADS, 1, s), lambda b, h, i: (b, 0, 0, 0))],
        out_specs=pl.BlockSpec((t, LANE), lambda b, h, i: (b * nq + i, h)),
        compiler_params=_params(("parallel", "parallel", "arbitrary")),
        name="fox_prompt",
    )(big, kt, vt, cumt.reshape(bsz, FOX_HEADS, 1, s))


def _mla_keys(knt, kpt):
    n = knt.shape[1]
    return jnp.concatenate([knt, kpt.astype(BF16), jnp.zeros((LANE - MLA_ROPE, n), BF16)], axis=0)


def _mla_prompt_kernel(q_ref, knt_ref, kpt_ref, v_ref, o_ref, *, t):
    qi = pl.program_id(2)
    hs = range(2)
    qs = [q_ref[:, hh * 2 * LANE:(hh + 1) * 2 * LANE] for hh in hs]

    def step(carry, qrows, c0, w, mask):
        kpt = kpt_ref[0, :, pl.ds(c0, w)]
        ss = [_dot(qrows[hh], _mla_keys(knt_ref[0, hh * LANE:(hh + 1) * LANE, pl.ds(c0, w)], kpt))
              for hh in hs]
        if mask is not None:
            ss = [jnp.where(mask, s, NEG_BIG) for s in ss]
        m_new = [jnp.maximum(carry[hh][0], jnp.max(ss[hh], axis=1, keepdims=True)) for hh in hs]
        ps = [jnp.exp2(ss[hh] - m_new[hh]) for hh in hs]
        pvs = [_dot(ps[hh].astype(BF16), v_ref[pl.ds(c0, w), hh * LANE:(hh + 1) * LANE]) for hh in hs]
        out = []
        for hh in hs:
            alpha = jnp.exp2(carry[hh][0] - m_new[hh])
            out.append((m_new[hh], alpha * carry[hh][1] + jnp.sum(ps[hh], axis=1, keepdims=True),
                        alpha * carry[hh][2] + pvs[hh]))
        return tuple(out)

    init = tuple((jnp.full((t, 1), NEG_BIG, F32), jnp.zeros((t, 1), F32), jnp.zeros((t, LANE), F32))
                 for _ in hs)
    st = lax.fori_loop(0, qi, lambda j, c: step(c, qs, pl.multiple_of(j * t, t), t, None), init)
    rc = lax.broadcasted_iota(jnp.int32, (t, t), 0) // CHUNK
    cc = lax.broadcasted_iota(jnp.int32, (t, t), 1) // CHUNK
    st = step(st, qs, pl.multiple_of(qi * t, t), t, rc >= cc)
    o_ref[...] = jnp.concatenate([st[hh][2] / st[hh][1] for hh in hs], axis=1).astype(BF16)


def mla_prompt(qx, knt, kpt, v, row0, bsz, s, t):
    assert t % CHUNK == 0
    m = qx.shape[0]
    nq = s // t
    return pl.pallas_call(
        functools.partial(_mla_prompt_kernel, t=t),
        out_shape=jax.ShapeDtypeStruct((m, BRANCH_W), BF16),
        grid=(bsz, MLA_HEADS // 2, nq),
        in_specs=[pl.BlockSpec((t, 4 * LANE), lambda b, h, i: (b * nq + i, h)),
                  pl.BlockSpec((1, 2 * LANE, s), lambda b, h, i: (b, h, 0)),
                  pl.BlockSpec((1, MLA_ROPE, s), lambda b, h, i: (row0 + b, 0, 0)),
                  pl.BlockSpec((s, 2 * LANE), lambda b, h, i: (b, h))],
        out_specs=pl.BlockSpec((t, 2 * LANE), lambda b, h, i: (b * nq + i, h)),
        compiler_params=_params(("parallel", "parallel", "arbitrary")),
        name="mla_prompt",
    )(qx, knt, kpt, v)


def _fox_sample_kernel(q_ref, ktn_ref, vtn_ref, ktc_ref, vtc_ref, ctn_ref, ctc_ref, o_ref, *, t, p):
    hp = pl.program_id(1)
    lane = lax.broadcasted_iota(jnp.int32, (1, LANE), 1)
    lo = lane < FOX_DIM
    q = q_ref[...] * (FOX_DIM ** -0.5 * LOG2E)
    kt_c = ktc_ref[0].astype(BF16)
    kt_n = ktn_ref[0].astype(BF16)
    vt_c = vtc_ref[0]
    vt_n = vtn_ref[0]
    causal = (lax.broadcasted_iota(jnp.int32, (t, t), 0) >= lax.broadcasted_iota(jnp.int32, (t, t), 1))
    accs = []
    for hh in range(2):
        head = 2 * hp + hh
        qh = (jnp.where(lo, q, 0.0) if hh == 0 else jnp.where(lo, 0.0, q)).astype(BF16)
        cc = ctc_ref[0, pl.ds(head, 1), :]
        ctot = cc[:, p - 1:p]
        s_c = _dot(qh, kt_c) + (ctot - cc) * LOG2E
        s_n = _dot(qh, kt_n) - ctn_ref[0, pl.ds(head, 1), :] * LOG2E
        s_n = jnp.where(causal, s_n, NEG_BIG)
        m = jnp.maximum(jnp.max(s_c, axis=1, keepdims=True), jnp.max(s_n, axis=1, keepdims=True))
        rows = _pair_rows_mask(hh)
        accs.append(_dot_nt(jnp.exp2(s_c - m).astype(BF16), jnp.where(rows, vt_c, 1.0).astype(BF16))
                    + _dot_nt(jnp.exp2(s_n - m).astype(BF16), jnp.where(rows, vt_n, 1.0).astype(BF16)))
    o_ref[...] = _fox_finish(accs[0], accs[1]).astype(BF16)


def fox_sample(big, kt_n, vt_n, kt_c, vt_c, cumt_n, cumt_c, bsz, t, p, layer):
    cidx = lambda b, h: (layer * bsz + b, h, 0)
    return pl.pallas_call(
        functools.partial(_fox_sample_kernel, t=t, p=p),
        out_shape=jax.ShapeDtypeStruct((bsz * t, BRANCH_W), BF16),
        grid=(bsz, FOX_HEADS // 2),
        in_specs=[pl.BlockSpec((t, LANE), lambda b, h: (b, OFF_FQ // LANE + h)),
                  pl.BlockSpec((1, LANE, t), cidx),
                  pl.BlockSpec((1, LANE, t), cidx),
                  pl.BlockSpec((1, LANE, p), cidx),
                  pl.BlockSpec((1, LANE, p), cidx),
                  pl.BlockSpec((1, FOX_HEADS, t), lambda b, h: (b, 0, 0)),
                  pl.BlockSpec((1, FOX_HEADS, p), lambda b, h: (layer * bsz + b, 0, 0))],
        out_specs=pl.BlockSpec((t, LANE), lambda b, h: (b, h)),
        compiler_params=_params(("parallel", "parallel")),
        name="fox_sample",
    )(big, kt_n, vt_n, kt_c, vt_c, cumt_n, cumt_c)


def _mla_sample_kernel(q_ref, kntn_ref, kptn_ref, vn_ref, kntc_ref, kptc_ref, vc_ref, o_ref, *, t, p):
    q = q_ref[...]
    s_c = _dot(q, _mla_keys(kntc_ref[0], kptc_ref[0]))
    s_n = _dot(q, _mla_keys(kntn_ref[0], kptn_ref[0]))
    qc = (p + lax.broadcasted_iota(jnp.int32, (t, t), 0)) // CHUNK
    kc = (p + lax.broadcasted_iota(jnp.int32, (t, t), 1)) // CHUNK
    s_n = jnp.where(qc >= kc, s_n, NEG_BIG)
    m = jnp.maximum(jnp.max(s_c, axis=1, keepdims=True), jnp.max(s_n, axis=1, keepdims=True))
    p_c = jnp.exp2(s_c - m)
    p_n = jnp.exp2(s_n - m)
    l = jnp.sum(p_c, axis=1, keepdims=True) + jnp.sum(p_n, axis=1, keepdims=True)
    o = _dot(p_c.astype(BF16), vc_ref[...]) + _dot(p_n.astype(BF16), vn_ref[...])
    o_ref[...] = (o / l).astype(BF16)


def mla_sample(qx, knt_n, kpt_n, v_n, knt_c, kpt_c, v_c, bsz, t, p, layer):
    assert (p - 1) // CHUNK <= p // CHUNK
    return pl.pallas_call(
        functools.partial(_mla_sample_kernel, t=t, p=p),
        out_shape=jax.ShapeDtypeStruct((bsz * t, BRANCH_W), BF16),
        grid=(bsz, MLA_HEADS),
        in_specs=[pl.BlockSpec((t, 2 * LANE), lambda b, h: (b, h)),
                  pl.BlockSpec((1, LANE, t), lambda b, h: (b, h, 0)),
                  pl.BlockSpec((1, MLA_ROPE, t), lambda b, h: (layer * bsz + b, 0, 0)),
                  pl.BlockSpec((t, LANE), lambda b, h: (b, h)),
                  pl.BlockSpec((1, LANE, p), lambda b, h: (b, h, 0)),
                  pl.BlockSpec((1, MLA_ROPE, p), lambda b, h: (layer * bsz + b, 0, 0)),
                  pl.BlockSpec((p, LANE), lambda b, h: (b, h))],
        out_specs=pl.BlockSpec((t, LANE), lambda b, h: (b, h)),
        compiler_params=_params(("parallel", "parallel")),
        name="mla_sample",
    )(qx, knt_n, kpt_n, v_n, knt_c, kpt_c, v_c)


def _latent_expand_kernel(c_ref, wkt_ref, wv_ref, knt_ref, v_ref):
    cb = c_ref[...].astype(BF16)
    knt_ref[0] = _dot_nt(wkt_ref[...], cb).astype(BF16)
    v_ref[...] = _dot(cb, wv_ref[...]).astype(BF16)


def latent_expand(ckv, wkt, wv, bsz, p, tm, layer):
    m = bsz * p
    nt = p // tm
    return pl.pallas_call(
        _latent_expand_kernel,
        out_shape=(jax.ShapeDtypeStruct((bsz, 512, p), BF16), jax.ShapeDtypeStruct((m, 512), BF16)),
        grid=(m // tm,),
        in_specs=[pl.BlockSpec((tm, MLA_KV_RANK), lambda i: (layer * (m // tm) + i, 0)),
                  pl.BlockSpec(wkt.shape, lambda i: (0, 0)), pl.BlockSpec(wv.shape, lambda i: (0, 0))],
        out_specs=(pl.BlockSpec((1, 512, tm), lambda i: (i // nt, 0, i % nt)),
                   pl.BlockSpec((tm, 512), lambda i: (i, 0))),
        compiler_params=_params(("parallel",)),
        name="latent_expand",
    )(ckv, wkt, wv)


def _hgrn_gates(z, lb, tri):
    logf = _log_sigmoid(z) + jnp.log(1.0 + lb * jnp.exp(jnp.minimum(-z, EXP_CLIP)))
    k = (1.0 - lb) * (1.0 / (1.0 + jnp.exp(z)))
    h1 = logf.astype(BF16)
    r1 = logf - h1.astype(F32)
    h2 = r1.astype(BF16)
    h3 = (r1 - h2.astype(F32)).astype(BF16)
    parts = _dot(tri, jnp.concatenate([h1, h2, h3], axis=1))
    lc = ((parts[:, :LANE] + parts[:, LANE:2 * LANE]) + parts[:, 2 * LANE:]) * LOG2E
    return k, lc


def _hgrn_local(q, z, lb, tri, v_b, ln, sel):
    k, lc = _hgrn_gates(z, lb, tri)
    nchunk = q.shape[0] // ln
    nsb = q.shape[0] // SUB_BLOCK
    per = ln // SUB_BLOCK
    half = SUB_BLOCK // 2
    rows = lambda a, i: a[i * SUB_BLOCK:(i + 1) * SUB_BLOCK, :]
    lcb = [jnp.zeros((1, LANE), F32) if i % per == 0 else lc[i * SUB_BLOCK - 1:i * SUB_BLOCK, :]
           for i in range(nsb)]
    lcb_rows = jnp.concatenate([jnp.broadcast_to(b, (SUB_BLOCK, LANE)) for b in lcb], axis=0)
    last = [lc[(c + 1) * ln - 1:(c + 1) * ln, :] for c in range(nchunk)]
    last_rows = jnp.concatenate([jnp.broadcast_to(b, (ln, LANE)) for b in last], axis=0)
    qh = (q * jnp.exp2(lc - lcb_rows)).astype(BF16)
    qe = (q * jnp.exp2(lc)).astype(BF16)
    kdec = (k * jnp.exp2(last_rows - lc)).astype(BF16)
    a_off = {}
    for i in range(nsb):
        n = (i % per) * SUB_BLOCK
        if n:
            c0 = i * SUB_BLOCK - n
            kt = (k[c0:c0 + n, :] * jnp.exp2(lcb[i] - lc[c0:c0 + n, :])).astype(BF16)
            a_off[i] = _dot_nt(rows(qh, i), kt)
    yield None
    pieces = []
    for i in range(nsb):
        q_i, k_i, lc_i = rows(q, i), rows(k, i), rows(lc, i)
        cols = []
        for s in range(SUB_BLOCK):
            lo = 0 if s < half else half
            d = lc_i[lo:, :] - lc_i[s:s + 1, :]
            d = (jnp.concatenate([jnp.minimum(d[:half, :], 0.0), d[half:, :]], axis=0) if s < half
                 else jnp.minimum(d, 0.0))
            w = (q_i[lo:, :] * k_i[s:s + 1, :]) * jnp.exp2(d)
            if lo:
                w = jnp.concatenate([jnp.zeros((lo, LANE), F32), w], axis=0)
            cols.append(w.astype(BF16))
        pieces.append(jnp.concatenate(cols, axis=1))
    a_all = _dot(jnp.concatenate(pieces, axis=0), sel)
    inc = [_dot_tn(v_b[c * ln:(c + 1) * ln, :], kdec[c * ln:(c + 1) * ln, :]) for c in range(nchunk)]
    yield None
    pair_ok = (lax.broadcasted_iota(jnp.int32, (SUB_BLOCK, LANE), 0)
               >= lax.broadcasted_iota(jnp.int32, (SUB_BLOCK, LANE), 1))
    off = []
    for i in range(nsb):
        n = (i % per) * SUB_BLOCK
        off.append(_dot(a_off[i].astype(BF16), v_b[i * SUB_BLOCK - n:i * SUB_BLOCK, :]) if n
                   else jnp.zeros((SUB_BLOCK, LANE), F32))
    diag = [_dot(jnp.where(pair_ok, rows(a_all, i), 0.0)[:, :SUB_BLOCK].astype(BF16), rows(v_b, i))
            for i in range(nsb)]
    local = jnp.concatenate(off, axis=0) + jnp.concatenate(diag, axis=0)
    dec = [jnp.exp2(b) for b in last]
    yield local, qe, inc, dec


def _hgrn_kernel(*refs, ln, nchunk, has_init):
    refs = list(refs)
    hq_ref, hf_ref, hi_ref, hg_ref, lb_ref, go_ref, sel_ref = refs[:7]
    s0_ref = refs[7] if has_init else None
    o_ref, sout_ref, st_ref = refs[-3:]
    step = pl.program_id(1)
    nrows = ln * nchunk

    @pl.when(step == 0)
    def _():
        for h in range(HG_HEADS):
            st_ref[h] = s0_ref[0, h].T if has_init else jnp.zeros((HG_DV, HG_DK), F32)

    ri = lax.broadcasted_iota(jnp.int32, (nrows, nrows), 0)
    ci = lax.broadcasted_iota(jnp.int32, (nrows, nrows), 1)
    tri = ((ri >= ci) & (ri // ln == ci // ln)).astype(BF16)
    def finish(h, local, qe, inc, dec):
        cs = slice(h * LANE, (h + 1) * LANE)
        st = st_ref[h]
        parts = []
        for c in range(nchunk):
            parts.append(_dot_nt(qe[c * ln:(c + 1) * ln, :], st.astype(BF16)))
            st = st * dec[c] + inc[c]
        st_ref[h] = st
        o = local + jnp.concatenate(parts, axis=0)
        o_ref[:, cs] = (_rms(o, go_ref[...]) * _sigmoid(hg_ref[:, cs])).astype(BF16)

    def start(h):
        cs = slice(h * LANE, (h + 1) * LANE)
        gen = _hgrn_local(hq_ref[:, cs], hf_ref[:, cs], lb_ref[:, cs], tri, hi_ref[:, cs].astype(BF16),
                          ln, sel_ref[...])
        next(gen)
        return gen

    gens = {0: start(0)}
    for h in range(HG_HEADS):
        if h + 1 < HG_HEADS:
            gens[h + 1] = start(h + 1)
        next(gens[h])
        if h > 0:
            finish(h - 1, *next(gens.pop(h - 1)))
    finish(HG_HEADS - 1, *next(gens.pop(HG_HEADS - 1)))

    @pl.when(step == pl.num_programs(1) - 1)
    def _():
        for h in range(HG_HEADS):
            sout_ref[0, h] = st_ref[h].T


def hgrn(big, lb, g_out, bsz, s, ln, rows, s0=None, s0_row0=0):
    m = big.shape[0]
    ns = s // rows
    has_init = s0 is not None
    sel = (np.arange(SUB_BLOCK * LANE)[:, None] // LANE == np.arange(LANE)[None, :])
    sel = jnp.asarray(sel, BF16)
    blk = lambda off: pl.BlockSpec((rows, BRANCH_W), lambda b, i: (b * ns + i, off // BRANCH_W))
    ins = [big, big, big, big, lb, g_out, sel]
    specs = [blk(OFF_HQ), blk(OFF_HF), blk(OFF_HI), blk(OFF_HG),
             pl.BlockSpec((1, BRANCH_W), lambda b, i: (0, 0)),
             pl.BlockSpec((1, HG_DV), lambda b, i: (0, 0)),
             pl.BlockSpec(sel.shape, lambda b, i: (0, 0))]
    if has_init:
        ins.append(s0)
        specs.append(pl.BlockSpec((1, HG_HEADS, HG_DK, HG_DV), lambda b, i: (s0_row0 + b, 0, 0, 0)))
    return pl.pallas_call(
        functools.partial(_hgrn_kernel, ln=ln, nchunk=rows // ln, has_init=has_init),
        out_shape=(jax.ShapeDtypeStruct((m, BRANCH_W), BF16),
                   jax.ShapeDtypeStruct((bsz, HG_HEADS, HG_DK, HG_DV), F32)),
        grid=(bsz, ns),
        in_specs=specs,
        out_specs=(pl.BlockSpec((rows, BRANCH_W), lambda b, i: (b * ns + i, 0)),
                   pl.BlockSpec((1, HG_HEADS, HG_DK, HG_DV), lambda b, i: (b, 0, 0, 0))),
        scratch_shapes=[pltpu.VMEM((HG_HEADS, HG_DV, HG_DK), F32)],
        compiler_params=_params(("parallel", "arbitrary")),
        name="hgrn",
    )(*ins)


def _merge_kernel(of_ref, om_ref, oh_ref, ga_ref, gb_ref, gc_ref, x_ref, wb_ref, wo_ref, g_ref, o_ref):
    merged = (_sigmoid(ga_ref[...]) * _dot(of_ref[...], wb_ref[0])
              + _sigmoid(gb_ref[...]) * _dot(om_ref[...], wb_ref[1])
              + _sigmoid(gc_ref[...]) * _dot(oh_ref[...], wb_ref[2]))
    y = _dot(merged.astype(BF16), wo_ref[...])
    o_ref[...] = x_ref[...] + _rms(y, g_ref[...])


def merge_out(o_fox, o_mla, o_hg, big, x, wb, wo, g, tm):
    m = x.shape[0]
    row = lambda w: pl.BlockSpec((tm, w), lambda i: (i, 0))
    gate = lambda off: pl.BlockSpec((tm, D_MODEL), lambda i: (i, off // D_MODEL))
    return pl.pallas_call(
        _merge_kernel,
        out_shape=jax.ShapeDtypeStruct((m, D_MODEL), F32),
        grid=(m // tm,),
        in_specs=[row(BRANCH_W), row(BRANCH_W), row(BRANCH_W),
                  gate(OFF_GA), gate(OFF_GA + D_MODEL), gate(OFF_GA + 2 * D_MODEL),
                  row(D_MODEL), _resident(wb.shape), _resident(wo.shape),
                  pl.BlockSpec((1, D_MODEL), lambda i: (0, 0))],
        out_specs=row(D_MODEL),
        compiler_params=_params(("parallel",)),
        name="merge_out",
    )(o_fox, o_mla, o_hg, big, big, big, x, wb, wo, g)


def _matmul2_kernel(x_ref, w_ref, a_ref, b_ref):
    y = _dot(x_ref[...].astype(BF16), w_ref[...])
    n = a_ref.shape[1]
    a_ref[...] = y[:, :n]
    b_ref[...] = y[:, n:]


def mem_kv(mem, w, tm):
    m, k = mem.shape
    n = w.shape[1] // 2
    row = lambda w_: pl.BlockSpec((tm, w_), lambda i: (i, 0))
    return pl.pallas_call(
        _matmul2_kernel,
        out_shape=(jax.ShapeDtypeStruct((m, n), F32), jax.ShapeDtypeStruct((m, n), F32)),
        grid=(m // tm,),
        in_specs=[row(k), _resident(w.shape)],
        out_specs=(row(n), row(n)),
        compiler_params=_params(("parallel",)),
        name="mem_kv",
    )(mem, w)


def _cross_kernel(x_ref, mk_ref, mv_ref, wq_ref, wo_ref, g2_ref, g3_ref, o_ref):
    x = x_ref[...]
    h = _rms(x, g2_ref[...]).astype(BF16)
    q = _dot(h, wq_ref[...])
    qb = (q * (X_DIM ** -0.5 * LOG2E)).astype(BF16)
    cols = [slice(hd * X_DIM, (hd + 1) * X_DIM) for hd in range(X_HEADS)]
    ss = [_dot_nt(qb[:, cs], mk_ref[:, cs].astype(BF16)) for cs in cols]
    ps = [jnp.exp2(s - jnp.max(s, axis=1, keepdims=True)) for s in ss]
    pvs = [_dot(p.astype(BF16), mv_ref[:, cs].astype(BF16)) for p, cs in zip(ps, cols)]
    outs = [pv / jnp.sum(p, axis=1, keepdims=True) for pv, p in zip(pvs, ps)]
    ox = jnp.concatenate(outs, axis=1).astype(BF16)
    o_ref[...] = x + _rms(_dot(ox, wo_ref[...]), g3_ref[...])


def cross_block(x, mk, mv, wq, wo, g2, g3, bsz, s, tm, mem_row0):
    m = x.shape[0]
    nt = s // tm
    vec = pl.BlockSpec((1, D_MODEL), lambda b, i: (0, 0))
    return pl.pallas_call(
        _cross_kernel,
        out_shape=jax.ShapeDtypeStruct((m, D_MODEL), F32),
        grid=(bsz, nt),
        in_specs=[pl.BlockSpec((tm, D_MODEL), lambda b, i: (b * nt + i, 0)),
                  pl.BlockSpec((N_MEM, X_HEADS * X_DIM), lambda b, i: (mem_row0 + b, 0)),
                  pl.BlockSpec((N_MEM, X_HEADS * X_DIM), lambda b, i: (mem_row0 + b, 0)),
                  _resident(wq.shape), _resident(wo.shape), vec, vec],
        out_specs=pl.BlockSpec((tm, D_MODEL), lambda b, i: (b * nt + i, 0)),
        compiler_params=_params(("parallel", "parallel")),
        name="cross_attn",
    )(x, mk, mv, wq, wo, g2, g3)


def _mlp_kernel(x_ref, wu_ref, wd_ref, g4_ref, g5_ref, o_ref, h_ref, acc_ref):
    j = pl.program_id(1)

    @pl.when(j == 0)
    def _():
        h_ref[...] = _rms(x_ref[...], g4_ref[...]).astype(BF16)
        acc_ref[...] = jnp.zeros_like(acc_ref)

    u = jnp.square(jnp.maximum(_dot(h_ref[...], wu_ref[j]), 0.0)).astype(BF16)
    acc_ref[...] += _dot(u, wd_ref[j])

    @pl.when(j == pl.num_programs(1) - 1)
    def _():
        o_ref[...] = x_ref[...] + _rms(acc_ref[...], g5_ref[...])


def mlp_block(x, wu3, wd3, g4, g5, tm):
    m = x.shape[0]
    nj = wu3.shape[0]
    vec = pl.BlockSpec((1, D_MODEL), lambda i, j: (0, 0))
    return pl.pallas_call(
        _mlp_kernel,
        out_shape=jax.ShapeDtypeStruct((m, D_MODEL), F32),
        grid=(m // tm, nj),
        in_specs=[pl.BlockSpec((tm, D_MODEL), lambda i, j: (i, 0)),
                  _resident(wu3.shape), _resident(wd3.shape), vec, vec],
        out_specs=pl.BlockSpec((tm, D_MODEL), lambda i, j: (i, 0)),
        scratch_shapes=[pltpu.VMEM((tm, D_MODEL), BF16), pltpu.VMEM((tm, D_MODEL), F32)],
        compiler_params=_params(("parallel", "arbitrary")),
        name="mlp",
    )(x, wu3, wd3, g4, g5)


def _prep_layer_weights(w_in, w_mla_uq, w_mla_ukv, w_branch, w_out, w_xq, w_mem_k, w_mem_v, w_xo,
                        w_up, w_down):
    idx = np.cumsum((0,) + IN_SIZES)
    seg = lambda i: w_in[:, idx[i]:idx[i + 1]]
    fq, fk, fv, ff, cq, ckv, kpe, hq, hf, hi, hg, ga, gb, gc = (seg(i) for i in range(14))
    half = MLA_ROPE // 2
    kpe_sw = jnp.concatenate([kpe[:, half:], kpe[:, :half]], axis=1)
    pad = jnp.zeros((D_MODEL, OFF_CQ - OFF_FQ - BRANCH_W), w_in.dtype)
    w_p = jnp.concatenate([hq, hf, hi, hg, ga, gb, gc, fq, pad, cq, ckv], axis=1).astype(BF16)
    w_in3 = w_p.reshape(D_MODEL, NP_IN // IN_TN, IN_TN).transpose(1, 0, 2)
    w_t = jnp.concatenate([fk, fv, kpe, kpe_sw, ff], axis=1).T.astype(BF16)
    hd = MLA_NOPE + MLA_ROPE
    zq = jnp.zeros((MLA_Q_RANK, LANE - MLA_ROPE), w_mla_uq.dtype)
    nope, rope_n, rope_s = [], [], []
    for h in range(MLA_HEADS):
        base = h * hd
        nope.append(w_mla_uq[:, base:base + MLA_NOPE])
        x1 = w_mla_uq[:, base + MLA_NOPE:base + MLA_NOPE + half]
        x2 = w_mla_uq[:, base + MLA_NOPE + half:base + hd]
        rope_n += [x1, x2, zq]
        rope_s += [x2, x1, zq]
    wuq = jnp.concatenate(nope + rope_n + rope_s, axis=1).astype(BF16)
    kvd = MLA_NOPE + MLA_V
    wkt = jnp.concatenate([w_mla_ukv[:, h * kvd:h * kvd + MLA_NOPE] for h in range(MLA_HEADS)],
                          axis=1).T.astype(BF16)
    wv = jnp.concatenate([w_mla_ukv[:, h * kvd + MLA_NOPE:(h + 1) * kvd] for h in range(MLA_HEADS)],
                         axis=1).astype(BF16)
    nff = D_FF // D_MODEL
    return dict(
        w_in3=w_in3, w_t=w_t, wuq=wuq, wkt=wkt, wv=wv,
        wb=w_branch.astype(BF16), wo=w_out.astype(BF16), wxq=w_xq.astype(BF16), wxo=w_xo.astype(BF16),
        wmem=jnp.concatenate([w_mem_k, w_mem_v], axis=1).astype(BF16),
        wu3=w_up.astype(BF16).reshape(D_MODEL, nff, D_MODEL).transpose(1, 0, 2),
        wd3=w_down.astype(BF16).reshape(nff, D_MODEL, D_MODEL))


def _rope_tables(pos, reps):
    half = MLA_ROPE // 2
    freq = ROPE_THETA ** (-jnp.arange(half, dtype=F32) / half)
    ang = pos.astype(F32)[:, None] * freq[None, :]
    cos, sin = jnp.cos(ang), jnp.sin(ang)
    z = jnp.zeros((pos.shape[0], LANE - MLA_ROPE), F32)
    cos_r = jnp.tile(jnp.concatenate([cos, cos, z], axis=1), (reps, 1))
    sin_r = jnp.tile(jnp.concatenate([-sin, sin, z], axis=1), (reps, 1))
    return cos_r, sin_r, cos_r[:, :MLA_ROPE].T, sin_r[:, :MLA_ROPE].T


def _tile(n, pref):
    t = min(n, pref)
    assert n % t == 0
    return t


def _layer(x, bsz, s, pos0, w, lb, b_fox, g_q, g_kv, g_hout, g_norm, mem_k, mem_v, mem_row0, past, cfg,
           layer, depth, shared):
    m = bsz * s
    g = lambda i: g_norm[i][None, :]
    tm_in = _tile(s, cfg["tm_in"])
    tm_p = _tile(s, cfg["tm_prep"])
    cos_r, sin_r, cos_c, sin_c = _rope_tables(pos0 + jnp.arange(s), 1)
    big, kt, vt, kpet, logft = in_proj(x, g(0), w["w_in3"], w["w_t"], cos_c, sin_c, b_fox[:, None],
                                       bsz, s, tm_in, layer, depth, shared[:4])
    qx, knt, v, ckv_n = mla_prep(big, cos_r, sin_r, g_q[None, :], g_kv[None, :],
                                 w["wuq"], w["wkt"], w["wv"], bsz, s, tm_p, layer, depth, shared[4:])
    row0 = layer * bsz
    rows = lambda a: a.reshape((depth * bsz,) + a.shape[2:])
    cumt = fox_cumsum(rows(logft), row0, bsz)
    if past is None:
        t = _tile(s, cfg["t_attn"])
        o_fox = fox_prompt(big, rows(kt), rows(vt), cumt, row0, bsz, s, t)
        o_mla = mla_prompt(qx, knt, rows(kpet), v, row0, bsz, s, t)
        o_hg, hg_state = hgrn(big, lb[None, :], g_hout[None, :], bsz, s, CHUNK,
                              _tile(s, cfg["hg_rows"]))
    else:
        c_kt, c_vt, c_cumt, c_knt, c_kpt, c_v, c_hg = past
        p = c_kt.shape[2]
        o_fox = fox_sample(big, rows(kt), rows(vt), c_kt, c_vt, cumt, c_cumt, bsz, s, p, layer)
        o_mla = mla_sample(qx, knt, rows(kpet), v, c_knt, c_kpt, c_v, bsz, s, p, layer)
        o_hg, hg_state = hgrn(big, lb[None, :], g_hout[None, :], bsz, s, s, s, s0=c_hg, s0_row0=row0)
    x = merge_out(o_fox, o_mla, o_hg, big, x, w["wb"], w["wo"], g(1), _tile(m, cfg["tm_merge"]))
    x = cross_block(x, mem_k, mem_v, w["wxq"], w["wxo"], g(2), g(3), bsz, s, _tile(s, cfg["tm_cross"]),
                    mem_row0)
    x = mlp_block(x, w["wu3"], w["wd3"], g(4), g(5), _tile(m, cfg["tm_mlp"]))
    return x, (kt, vt, kpet, logft, ckv_n), hg_state


def _from_feature_major(stacked, heads):
    a = jnp.swapaxes(stacked, 2, 3)
    if heads:
        a = a.reshape(a.shape[:3] + (heads, a.shape[3] // heads))
    return a


def _assemble_states(shared, hg_states, bsz, s):
    kt, vt, kpet, logft, ckv = shared
    return (_from_feature_major(kt, FOX_HEADS), _from_feature_major(vt, FOX_HEADS),
            _from_feature_major(logft, 0), ckv.reshape(ckv.shape[0], bsz, s, MLA_KV_RANK),
            _from_feature_major(kpet, 0), jnp.stack(hg_states))


_CFG = dict(tm_in=1024, tm_prep=512, t_attn=512, hg_rows=512, tm_merge=512, tm_cross=512,
            tm_mlp=1024, tm_mem=512, tm_expand=1024)


def kernel(x_prompt, x_sample, cache_fox_k, cache_fox_v, cache_fox_logf, cache_mla_ckv, cache_mla_kpe,
           state_hgrn, cache_mem_k, cache_mem_v, mem_prompt, w_in, b_fox, g_mla_q, w_mla_uq, g_mla_kv,
           w_mla_ukv, g_hgrn_out, lb_hgrn, w_branch, w_out, w_xq, w_mem_k, w_mem_v, w_xo, w_up, w_down,
           g_norm):
    cfg = _CFG
    depth = w_in.shape[0]
    lb_p = jax.nn.softmax(lb_hgrn.astype(F32), axis=0)
    lb_all = jnp.cumsum(lb_p, axis=0) - lb_p[0]
    ws = [_prep_layer_weights(w_in[l], w_mla_uq[l], w_mla_ukv[l], w_branch[l], w_out[l], w_xq[l],
                              w_mem_k[l], w_mem_v[l], w_xo[l], w_up[l], w_down[l]) for l in range(depth)]

    def run_layer(x, bsz, s, pos0, l, mk, mv, mem_row0, past, shared):
        return _layer(x, bsz, s, pos0, ws[l], lb_all[l], b_fox[l], g_mla_q[l], g_mla_kv[l],
                      g_hgrn_out[l], g_norm[l], mk, mv, mem_row0, past, cfg, l, depth, shared)

    bp, sp, _ = x_prompt.shape
    x = x_prompt.reshape(bp * sp, D_MODEL)
    mem = mem_prompt.reshape(bp * N_MEM, D_MODEL)
    shared, hg_states, p_mem = (), [], []
    for l in range(depth):
        mk, mv = mem_kv(mem, ws[l]["wmem"], _tile(bp * N_MEM, cfg["tm_mem"]))
        x, shared, hg = run_layer(x, bp, sp, 0, l, mk, mv, 0, None, shared)
        hg_states.append(hg)
        p_mem.append((mk.reshape(bp, N_MEM, X_HEADS, X_DIM), mv.reshape(bp, N_MEM, X_HEADS, X_DIM)))
    y_prompt = x.reshape(bp, sp, D_MODEL)
    p_out = _assemble_states(shared, hg_states, bp, sp) + tuple(jnp.stack(a) for a in zip(*p_mem))

    bs, ts, _ = x_sample.shape
    p = cache_fox_k.shape[2]
    fm = lambda c: jnp.moveaxis(c, 2, -1)
    c_kt = fm(cache_fox_k).reshape(depth * bs, BRANCH_W, p)
    c_vt = fm(cache_fox_v).reshape(depth * bs, BRANCH_W, p)
    c_kpt = fm(cache_mla_kpe).reshape(depth * bs, MLA_ROPE, p)
    c_cumt = fox_cumsum(fm(cache_fox_logf).reshape(depth * bs, FOX_HEADS, p), 0, depth * bs)
    c_ckv = cache_mla_ckv.reshape(depth * bs * p, MLA_KV_RANK)
    c_hg = state_hgrn.reshape((depth * bs,) + state_hgrn.shape[2:])
    c_mk = cache_mem_k.reshape(depth * bs * N_MEM, X_HEADS * X_DIM)
    c_mv = cache_mem_v.reshape(depth * bs * N_MEM, X_HEADS * X_DIM)
    x = x_sample.reshape(bs * ts, D_MODEL)
    shared, hg_states = (), []
    for l in range(depth):
        c_knt, c_v = latent_expand(c_ckv, ws[l]["wkt"], ws[l]["wv"], bs, p, _tile(p, cfg["tm_expand"]), l)
        past = (c_kt, c_vt, c_cumt, c_knt, c_kpt, c_v, c_hg)
        x, shared, hg = run_layer(x, bs, ts, p, l, c_mk, c_mv, l * bs, past, shared)
        hg_states.append(hg)
    y_sample = x.reshape(bs, ts, D_MODEL)
    return (y_prompt, y_sample, *p_out, *_assemble_states(shared, hg_states, bs, ts))
```

```python
import functools

import numpy as np
import jax
import jax.numpy as jnp
from jax import lax
from jax.experimental import pallas as pl
from jax.experimental.pallas import tpu as pltpu

F32 = jnp.float32
BF16 = jnp.bfloat16

D_MODEL = 1024
CHUNK = 64
N_MEM = 256
EPS = 1e-6
NEG_BIG = -1e30
EXP_CLIP = 80.0
FOX_HEADS = 8
FOX_DIM = 64
MLA_HEADS = 4
MLA_Q_RANK = 384
MLA_KV_RANK = 256
MLA_NOPE = 128
MLA_ROPE = 64
MLA_V = 128
ROPE_THETA = 10000.0
HG_HEADS = 4
HG_DK = 128
HG_DV = 128
X_HEADS = 4
X_DIM = 128
D_FF = 4 * D_MODEL
BRANCH_W = 512
IN_SIZES = (512, 512, 512, FOX_HEADS, MLA_Q_RANK, MLA_KV_RANK, MLA_ROPE, 512, 512, 512, 512,
            D_MODEL, D_MODEL, D_MODEL)

LANE = 128
SUB_BLOCK = 16
VMEM_LIMIT = 56 * 1024 * 1024
LOG2E = 1.4426950408889634

OFF_HQ, OFF_HF, OFF_HI, OFF_HG, OFF_FQ, OFF_CQ, OFF_CKV, NP_IN = (
    0, 512, 1024, 1536, 2048, 2688, 3072, 3328)
IN_TN = 1664
T_FK, T_FV, T_KPE, T_KPE_SW, T_FF, NT_IN = 0, 512, 1024, 1088, 1152, 1160


def _params(sem, vmem=VMEM_LIMIT):
    return pltpu.CompilerParams(dimension_semantics=sem, vmem_limit_bytes=vmem)


def _dot(a, b):
    return jnp.dot(a, b, preferred_element_type=F32)


def _dot_nt(a, b):
    return lax.dot_general(a, b, (((1,), (1,)), ((), ())), preferred_element_type=F32)


def _dot_tn(a, b):
    return lax.dot_general(a, b, (((0,), (0,)), ((), ())), preferred_element_type=F32)


def _rms(x, g):
    y = x * lax.rsqrt(jnp.mean(x * x, axis=-1, keepdims=True) + EPS)
    return y * g


def _log_sigmoid(z):
    return jnp.minimum(z, 0.0) - jnp.log(1.0 + jnp.exp(-jnp.abs(z)))


def _sigmoid(z):
    return 1.0 / (1.0 + jnp.exp(-z))


def _resident(shape):
    nd = len(shape)
    return pl.BlockSpec(shape, lambda *_: (0,) * nd, pipeline_mode=pl.Buffered(1))


def _in_proj_kernel(x_ref, g_ref, w_ref, wt_ref, cos_ref, sin_ref, bf_ref, *rest):
    big_ref, kt_ref, vt_ref, kpe_ref, lf_ref, h_ref = rest[-6:]
    j = pl.program_id(1)

    @pl.when(j == 0)
    def _():
        h = _rms(x_ref[...], g_ref[...]).astype(BF16)
        h_ref[...] = h
        yt = _dot_nt(wt_ref[...], h)
        kt_ref[0] = yt[T_FK:T_FK + BRANCH_W]
        vt_ref[0] = yt[T_FV:T_FV + BRANCH_W]
        kpe_ref[0] = (yt[T_KPE:T_KPE + MLA_ROPE] * cos_ref[...]
                      + yt[T_KPE_SW:T_KPE_SW + MLA_ROPE] * sin_ref[...])
        lf_ref[0] = _log_sigmoid(yt[T_FF:T_FF + FOX_HEADS] + bf_ref[...])

    tn = big_ref.shape[1]
    big_ref[...] = _dot(h_ref[...], w_ref[:, pl.ds(pl.multiple_of(j * tn, LANE), tn)])


def in_proj(x, g, w3, wt, cos_t, sin_t, b_col, bsz, s, tm, layer, depth, prev):
    m, k = x.shape
    tn = IN_TN
    nj = w3.shape[1] // tn
    nt = s // tm
    ntab = cos_t.shape[1] // tm
    feats = (BRANCH_W, BRANCH_W, MLA_ROPE, FOX_HEADS)
    tspec = lambda rows: pl.BlockSpec((None, 1, rows, tm), lambda i, j: (layer, i // nt, 0, i % nt))
    n_in = 7
    return pl.pallas_call(
        _in_proj_kernel,
        out_shape=(jax.ShapeDtypeStruct((m, nj * tn), F32),)
        + tuple(jax.ShapeDtypeStruct((depth, bsz, f, s), F32) for f in feats),
        grid=(m // tm, nj),
        in_specs=[pl.BlockSpec((tm, k), lambda i, j: (i, 0)),
                  pl.BlockSpec((1, k), lambda i, j: (0, 0)),
                  _resident(w3.shape),
                  _resident(wt.shape),
                  pl.BlockSpec((MLA_ROPE, tm), lambda i, j: (0, i % ntab)),
                  pl.BlockSpec((MLA_ROPE, tm), lambda i, j: (0, i % ntab)),
                  pl.BlockSpec((FOX_HEADS, 1), lambda i, j: (0, 0))]
        + [pl.BlockSpec(memory_space=pl.ANY)] * len(prev),
        out_specs=(pl.BlockSpec((tm, tn), lambda i, j: (i, j)),) + tuple(tspec(f) for f in feats),
        scratch_shapes=[pltpu.VMEM((tm, k), BF16)],
        input_output_aliases={n_in + i: 1 + i for i in range(len(prev))},
        compiler_params=_params(("parallel", "arbitrary")),
        name="in_proj",
    )(x, g, w3, wt, cos_t, sin_t, b_col, *prev)


def _mla_prep_kernel(cq_ref, ckv_ref, cs_ref, sn_ref, gq_ref, gkv_ref, wuq_ref, wkt_ref, wv_ref, *rest):
    qx_ref, knt_ref, v_ref, ckvn_ref = rest[-4:]
    cos_t = cs_ref[...]
    sin_t = sn_ref[...]
    qn = _rms(cq_ref[...], gq_ref[...]).astype(BF16)
    qall = _dot(qn, wuq_ref[...]) * ((MLA_NOPE + MLA_ROPE) ** -0.5 * LOG2E)
    for h in range(MLA_HEADS):
        lo = h * LANE
        qr = (qall[:, 512 + lo:512 + lo + LANE] * cos_t
              + qall[:, 1024 + lo:1024 + lo + LANE] * sin_t)
        qx_ref[:, 2 * lo:2 * lo + LANE] = qall[:, lo:lo + LANE].astype(BF16)
        qx_ref[:, 2 * lo + LANE:2 * lo + 2 * LANE] = qr.astype(BF16)
    ckvn = _rms(ckv_ref[...], gkv_ref[...])
    ckvn_ref[...] = ckvn
    cb = ckvn.astype(BF16)
    knt_ref[0] = _dot_nt(wkt_ref[...], cb).astype(BF16)
    v_ref[...] = _dot(cb, wv_ref[...]).astype(BF16)


def mla_prep(big, cos_t, sin_t, gq, gkv, wuq, wkt, wv, bsz, s, tm, layer, depth, prev):
    m = big.shape[0]
    nt = s // tm
    ntab = cos_t.shape[0] // tm
    row = lambda w: pl.BlockSpec((tm, w), lambda i: (i, 0))
    full = lambda a: pl.BlockSpec(a.shape, lambda i: (0,) * a.ndim)
    n_in = 9
    return pl.pallas_call(
        _mla_prep_kernel,
        out_shape=(jax.ShapeDtypeStruct((m, 1024), BF16),
                   jax.ShapeDtypeStruct((bsz, 512, s), BF16),
                   jax.ShapeDtypeStruct((m, 512), BF16),
                   jax.ShapeDtypeStruct((depth, m, MLA_KV_RANK), F32)),
        grid=(m // tm,),
        in_specs=[pl.BlockSpec((tm, MLA_Q_RANK), lambda i: (i, OFF_CQ // MLA_Q_RANK)),
                  pl.BlockSpec((tm, MLA_KV_RANK), lambda i: (i, OFF_CKV // MLA_KV_RANK)),
                  pl.BlockSpec((tm, LANE), lambda i: (i % ntab, 0)),
                  pl.BlockSpec((tm, LANE), lambda i: (i % ntab, 0)),
                  full(gq), full(gkv), full(wuq), full(wkt), full(wv)]
        + [pl.BlockSpec(memory_space=pl.ANY)] * len(prev),
        out_specs=(row(1024), pl.BlockSpec((1, 512, tm), lambda i: (i // nt, 0, i % nt)),
                   row(512), pl.BlockSpec((None, tm, MLA_KV_RANK), lambda i: (layer, i, 0))),
        input_output_aliases={n_in + i: 3 + i for i in range(len(prev))},
        compiler_params=_params(("parallel",)),
        name="mla_prep",
    )(big, big, cos_t, sin_t, gq, gkv, wuq, wkt, wv, *prev)


def _cumsum_kernel(x_ref, c_ref, *, w):
    s = x_ref.shape[2]
    r = lax.broadcasted_iota(jnp.int32, (w, w), 0)
    c = lax.broadcasted_iota(jnp.int32, (w, w), 1)
    upper = (r <= c).astype(F32)
    local = [jnp.dot(x_ref[0, :, g * w:(g + 1) * w], upper, preferred_element_type=F32,
                     precision=lax.Precision.HIGHEST) for g in range(s // w)]
    carry = jnp.zeros((x_ref.shape[1], 1), F32)
    for g, cum in enumerate(local):
        cum = cum + carry
        c_ref[0, :, g * w:(g + 1) * w] = cum
        carry = cum[:, w - 1:w]


def fox_cumsum(x, row0, bsz):
    _, h, s = x.shape
    return pl.pallas_call(
        functools.partial(_cumsum_kernel, w=min(LANE, s)),
        out_shape=jax.ShapeDtypeStruct((bsz, h, s), F32),
        grid=(bsz,),
        in_specs=[pl.BlockSpec((1, h, s), lambda i: (row0 + i, 0, 0))],
        out_specs=pl.BlockSpec((1, h, s), lambda i: (i, 0, 0)),
        compiler_params=_params(("parallel",)),
        name="fox_cumsum",
    )(x)


def _pair_rows_mask(hh):
    sub = lax.broadcasted_iota(jnp.int32, (LANE, 1), 0)
    return (sub < FOX_DIM) if hh == 0 else (sub >= FOX_DIM)


def _fox_finish(acc0, acc1):
    lane = lax.broadcasted_iota(jnp.int32, (1, LANE), 1)
    o0 = acc0 / pltpu.roll(acc0, FOX_DIM, axis=1)
    o1 = acc1 / pltpu.roll(acc1, FOX_DIM, axis=1)
    return jnp.where(lane < FOX_DIM, o0, o1)


def _fox_prompt_kernel(q_ref, kt_ref, vt_ref, ct_ref, o_ref, *, t):
    hp = pl.program_id(1)
    qi = pl.program_id(2)
    lane = lax.broadcasted_iota(jnp.int32, (1, LANE), 1)
    lo = lane < FOX_DIM
    q = q_ref[...] * (FOX_DIM ** -0.5 * LOG2E)
    qs = (jnp.where(lo, q, 0.0).astype(BF16), jnp.where(lo, 0.0, q).astype(BF16))
    d0 = pl.multiple_of(qi * t, t)
    cref = tuple(ct_ref[0, 2 * hp + hh, :, pl.ds(d0, LANE)][:, 0:1] for hh in range(2))

    def step(carry, qrows, c0, w, mask):
        kt = kt_ref[0, :, pl.ds(c0, w)].astype(BF16)
        vt = vt_ref[0, :, pl.ds(c0, w)]
        ss = []
        for hh in range(2):
            bias = (cref[hh] - ct_ref[0, 2 * hp + hh, :, pl.ds(c0, w)]) * LOG2E
            s = _dot(qrows[hh], kt) + bias
            ss.append(s if mask is None else jnp.where(mask, s, NEG_BIG))
        vts = [jnp.where(_pair_rows_mask(hh), vt, 1.0).astype(BF16) for hh in range(2)]
        m_new = [jnp.maximum(carry[hh][0], jnp.max(ss[hh], axis=1, keepdims=True)) for hh in range(2)]
        ps = [jnp.exp2(ss[hh] - m_new[hh]).astype(BF16) for hh in range(2)]
        pvs = [_dot_nt(ps[hh], vts[hh]) for hh in range(2)]
        return tuple((m_new[hh], jnp.exp2(carry[hh][0] - m_new[hh]) * carry[hh][1] + pvs[hh])
                     for hh in range(2))

    init = tuple((jnp.full((t, 1), NEG_BIG, F32), jnp.zeros((t, LANE), F32)) for _ in range(2))
    st = lax.fori_loop(0, qi, lambda j, c: step(c, qs, pl.multiple_of(j * t, t), t, None), init)
    causal = (lax.broadcasted_iota(jnp.int32, (t, t), 0) >= lax.broadcasted_iota(jnp.int32, (t, t), 1))
    st = step(st, qs, d0, t, causal)
    o_ref[...] = _fox_finish(st[0][1], st[1][1]).astype(BF16)


def fox_prompt(big, kt, vt, cumt, row0, bsz, s, t):
    m = big.shape[0]
    nq = s // t
    return pl.pallas_call(
        functools.partial(_fox_prompt_kernel, t=t),
        out_shape=jax.ShapeDtypeStruct((m, BRANCH_W), BF16),
        grid=(bsz, FOX_HEADS // 2, nq),
        in_specs=[pl.BlockSpec((t, LANE), lambda b, h, i: (b * nq + i, OFF_FQ // LANE + h)),
                  pl.BlockSpec((1, LANE, s), lambda b, h, i: (row0 + b, h, 0)),
                  pl.BlockSpec((1, LANE, s), lambda b, h, i: (row0 + b, h, 0)),
                  pl.BlockSpec((1, FOX_HEADS, 1, s), lambda b, h, i: (b, 0, 0, 0))],
        out_specs=pl.BlockSpec((t, LANE), lambda b, h, i: (b * nq + i, h)),
        compiler_params=_params(("parallel", "parallel", "arbitrary")),
        name="fox_prompt",
    )(big, kt, vt, cumt.reshape(bsz, FOX_HEADS, 1, s))


def _mla_keys(knt, kpt):
    n = knt.shape[1]
    return jnp.concatenate([knt, kpt.astype(BF16), jnp.zeros((LANE - MLA_ROPE, n), BF16)], axis=0)


def _mla_prompt_kernel(q_ref, knt_ref, kpt_ref, v_ref, o_ref, *, t):
    qi = pl.program_id(2)
    hs = range(2)
    qs = [q_ref[:, hh * 2 * LANE:(hh + 1) * 2 * LANE] for hh in hs]

    def step(carry, qrows, c0, w, mask):
        kpt = kpt_ref[0, :, pl.ds(c0, w)]
        ss = [_dot(qrows[hh], _mla_keys(knt_ref[0, hh * LANE:(hh + 1) * LANE, pl.ds(c0, w)], kpt))
              for hh in hs]
        if mask is not None:
            ss = [jnp.where(mask, s, NEG_BIG) for s in ss]
        m_new = [jnp.maximum(carry[hh][0], jnp.max(ss[hh], axis=1, keepdims=True)) for hh in hs]
        ps = [jnp.exp2(ss[hh] - m_new[hh]) for hh in hs]
        pvs = [_dot(ps[hh].astype(BF16), v_ref[pl.ds(c0, w), hh * LANE:(hh + 1) * LANE]) for hh in hs]
        out = []
        for hh in hs:
            alpha = jnp.exp2(carry[hh][0] - m_new[hh])
            out.append((m_new[hh], alpha * carry[hh][1] + jnp.sum(ps[hh], axis=1, keepdims=True),
                        alpha * carry[hh][2] + pvs[hh]))
        return tuple(out)

    init = tuple((jnp.full((t, 1), NEG_BIG, F32), jnp.zeros((t, 1), F32), jnp.zeros((t, LANE), F32))
                 for _ in hs)
    st = lax.fori_loop(0, qi, lambda j, c: step(c, qs, pl.multiple_of(j * t, t), t, None), init)
    rc = lax.broadcasted_iota(jnp.int32, (t, t), 0) // CHUNK
    cc = lax.broadcasted_iota(jnp.int32, (t, t), 1) // CHUNK
    st = step(st, qs, pl.multiple_of(qi * t, t), t, rc >= cc)
    o_ref[...] = jnp.concatenate([st[hh][2] / st[hh][1] for hh in hs], axis=1).astype(BF16)


def mla_prompt(qx, knt, kpt, v, row0, bsz, s, t):
    assert t % CHUNK == 0
    m = qx.shape[0]
    nq = s // t
    return pl.pallas_call(
        functools.partial(_mla_prompt_kernel, t=t),
        out_shape=jax.ShapeDtypeStruct((m, BRANCH_W), BF16),
        grid=(bsz, MLA_HEADS // 2, nq),
        in_specs=[pl.BlockSpec((t, 4 * LANE), lambda b, h, i: (b * nq + i, h)),
                  pl.BlockSpec((1, 2 * LANE, s), lambda b, h, i: (b, h, 0)),
                  pl.BlockSpec((1, MLA_ROPE, s), lambda b, h, i: (row0 + b, 0, 0)),
                  pl.BlockSpec((s, 2 * LANE), lambda b, h, i: (b, h))],
        out_specs=pl.BlockSpec((t, 2 * LANE), lambda b, h, i: (b * nq + i, h)),
        compiler_params=_params(("parallel", "parallel", "arbitrary")),
        name="mla_prompt",
    )(qx, knt, kpt, v)


def _fox_sample_kernel(q_ref, ktn_ref, vtn_ref, ktc_ref, vtc_ref, ctn_ref, ctc_ref, o_ref, *, t, p):
    hp = pl.program_id(1)
    lane = lax.broadcasted_iota(jnp.int32, (1, LANE), 1)
    lo = lane < FOX_DIM
    q = q_ref[...] * (FOX_DIM ** -0.5 * LOG2E)
    kt_c = ktc_ref[0].astype(BF16)
    kt_n = ktn_ref[0].astype(BF16)
    vt_c = vtc_ref[0]
    vt_n = vtn_ref[0]
    causal = (lax.broadcasted_iota(jnp.int32, (t, t), 0) >= lax.broadcasted_iota(jnp.int32, (t, t), 1))
    accs = []
    for hh in range(2):
        head = 2 * hp + hh
        qh = (jnp.where(lo, q, 0.0) if hh == 0 else jnp.where(lo, 0.0, q)).astype(BF16)
        cc = ctc_ref[0, pl.ds(head, 1), :]
        ctot = cc[:, p - 1:p]
        s_c = _dot(qh, kt_c) + (ctot - cc) * LOG2E
        s_n = _dot(qh, kt_n) - ctn_ref[0, pl.ds(head, 1), :] * LOG2E
        s_n = jnp.where(causal, s_n, NEG_BIG)
        m = jnp.maximum(jnp.max(s_c, axis=1, keepdims=True), jnp.max(s_n, axis=1, keepdims=True))
        rows = _pair_rows_mask(hh)
        accs.append(_dot_nt(jnp.exp2(s_c - m).astype(BF16), jnp.where(rows, vt_c, 1.0).astype(BF16))
                    + _dot_nt(jnp.exp2(s_n - m).astype(BF16), jnp.where(rows, vt_n, 1.0).astype(BF16)))
    o_ref[...] = _fox_finish(accs[0], accs[1]).astype(BF16)


def fox_sample(big, kt_n, vt_n, kt_c, vt_c, cumt_n, cumt_c, bsz, t, p, layer):
    cidx = lambda b, h: (layer * bsz + b, h, 0)
    return pl.pallas_call(
        functools.partial(_fox_sample_kernel, t=t, p=p),
        out_shape=jax.ShapeDtypeStruct((bsz * t, BRANCH_W), BF16),
        grid=(bsz, FOX_HEADS // 2),
        in_specs=[pl.BlockSpec((t, LANE), lambda b, h: (b, OFF_FQ // LANE + h)),
                  pl.BlockSpec((1, LANE, t), cidx),
                  pl.BlockSpec((1, LANE, t), cidx),
                  pl.BlockSpec((1, LANE, p), cidx),
                  pl.BlockSpec((1, LANE, p), cidx),
                  pl.BlockSpec((1, FOX_HEADS, t), lambda b, h: (b, 0, 0)),
                  pl.BlockSpec((1, FOX_HEADS, p), lambda b, h: (layer * bsz + b, 0, 0))],
        out_specs=pl.BlockSpec((t, LANE), lambda b, h: (b, h)),
        compiler_params=_params(("parallel", "parallel")),
        name="fox_sample",
    )(big, kt_n, vt_n, kt_c, vt_c, cumt_n, cumt_c)


def _mla_sample_kernel(q_ref, kntn_ref, kptn_ref, vn_ref, kntc_ref, kptc_ref, vc_ref, o_ref, *, t, p):
    q = q_ref[...]
    s_c = _dot(q, _mla_keys(kntc_ref[0], kptc_ref[0]))
    s_n = _dot(q, _mla_keys(kntn_ref[0], kptn_ref[0]))
    qc = (p + lax.broadcasted_iota(jnp.int32, (t, t), 0)) // CHUNK
    kc = (p + lax.broadcasted_iota(jnp.int32, (t, t), 1)) // CHUNK
    s_n = jnp.where(qc >= kc, s_n, NEG_BIG)
    m = jnp.maximum(jnp.max(s_c, axis=1, keepdims=True), jnp.max(s_n, axis=1, keepdims=True))
    p_c = jnp.exp2(s_c - m)
    p_n = jnp.exp2(s_n - m)
    l = jnp.sum(p_c, axis=1, keepdims=True) + jnp.sum(p_n, axis=1, keepdims=True)
    o = _dot(p_c.astype(BF16), vc_ref[...]) + _dot(p_n.astype(BF16), vn_ref[...])
    o_ref[...] = (o / l).astype(BF16)


def mla_sample(qx, knt_n, kpt_n, v_n, knt_c, kpt_c, v_c, bsz, t, p, layer):
    assert (p - 1) // CHUNK <= p // CHUNK
    return pl.pallas_call(
        functools.partial(_mla_sample_kernel, t=t, p=p),
        out_shape=jax.ShapeDtypeStruct((bsz * t, BRANCH_W), BF16),
        grid=(bsz, MLA_HEADS),
        in_specs=[pl.BlockSpec((t, 2 * LANE), lambda b, h: (b, h)),
                  pl.BlockSpec((1, LANE, t), lambda b, h: (b, h, 0)),
                  pl.BlockSpec((1, MLA_ROPE, t), lambda b, h: (layer * bsz + b, 0, 0)),
                  pl.BlockSpec((t, LANE), lambda b, h: (b, h)),
                  pl.BlockSpec((1, LANE, p), lambda b, h: (b, h, 0)),
                  pl.BlockSpec((1, MLA_ROPE, p), lambda b, h: (layer * bsz + b, 0, 0)),
                  pl.BlockSpec((p, LANE), lambda b, h: (b, h))],
        out_specs=pl.BlockSpec((t, LANE), lambda b, h: (b, h)),
        compiler_params=_params(("parallel", "parallel")),
        name="mla_sample",
    )(qx, knt_n, kpt_n, v_n, knt_c, kpt_c, v_c)


def _latent_expand_kernel(c_ref, wkt_ref, wv_ref, knt_ref, v_ref):
    cb = c_ref[...].astype(BF16)
    knt_ref[0] = _dot_nt(wkt_ref[...], cb).astype(BF16)
    v_ref[...] = _dot(cb, wv_ref[...]).astype(BF16)


def latent_expand(ckv, wkt, wv, bsz, p, tm, layer):
    m = bsz * p
    nt = p // tm
    return pl.pallas_call(
        _latent_expand_kernel,
        out_shape=(jax.ShapeDtypeStruct((bsz, 512, p), BF16), jax.ShapeDtypeStruct((m, 512), BF16)),
        grid=(m // tm,),
        in_specs=[pl.BlockSpec((tm, MLA_KV_RANK), lambda i: (layer * (m // tm) + i, 0)),
                  pl.BlockSpec(wkt.shape, lambda i: (0, 0)), pl.BlockSpec(wv.shape, lambda i: (0, 0))],
        out_specs=(pl.BlockSpec((1, 512, tm), lambda i: (i // nt, 0, i % nt)),
                   pl.BlockSpec((tm, 512), lambda i: (i, 0))),
        compiler_params=_params(("parallel",)),
        name="latent_expand",
    )(ckv, wkt, wv)


def _hgrn_gates(z, lb, tri):
    logf = _log_sigmoid(z) + jnp.log(1.0 + lb * jnp.exp(jnp.minimum(-z, EXP_CLIP)))
    k = (1.0 - lb) * (1.0 / (1.0 + jnp.exp(z)))
    h1 = logf.astype(BF16)
    r1 = logf - h1.astype(F32)
    h2 = r1.astype(BF16)
    h3 = (r1 - h2.astype(F32)).astype(BF16)
    parts = _dot(tri, jnp.concatenate([h1, h2, h3], axis=1))
    lc = ((parts[:, :LANE] + parts[:, LANE:2 * LANE]) + parts[:, 2 * LANE:]) * LOG2E
    return k, lc


def _hgrn_local(q, z, lb, tri, v_b, ln, sel):
    k, lc = _hgrn_gates(z, lb, tri)
    nchunk = q.shape[0] // ln
    nsb = q.shape[0] // SUB_BLOCK
    per = ln // SUB_BLOCK
    half = SUB_BLOCK // 2
    rows = lambda a, i: a[i * SUB_BLOCK:(i + 1) * SUB_BLOCK, :]
    lcb = [jnp.zeros((1, LANE), F32) if i % per == 0 else lc[i * SUB_BLOCK - 1:i * SUB_BLOCK, :]
           for i in range(nsb)]
    lcb_rows = jnp.concatenate([jnp.broadcast_to(b, (SUB_BLOCK, LANE)) for b in lcb], axis=0)
    last = [lc[(c + 1) * ln - 1:(c + 1) * ln, :] for c in range(nchunk)]
    last_rows = jnp.concatenate([jnp.broadcast_to(b, (ln, LANE)) for b in last], axis=0)
    qh = (q * jnp.exp2(lc - lcb_rows)).astype(BF16)
    qe = (q * jnp.exp2(lc)).astype(BF16)
    kdec = (k * jnp.exp2(last_rows - lc)).astype(BF16)
    a_off = {}
    for i in range(nsb):
        n = (i % per) * SUB_BLOCK
        if n:
            c0 = i * SUB_BLOCK - n
            kt = (k[c0:c0 + n, :] * jnp.exp2(lcb[i] - lc[c0:c0 + n, :])).astype(BF16)
            a_off[i] = _dot_nt(rows(qh, i), kt)
    yield None
    pieces = []
    for i in range(nsb):
        q_i, k_i, lc_i = rows(q, i), rows(k, i), rows(lc, i)
        cols = []
        for s in range(SUB_BLOCK):
            lo = 0 if s < half else half
            d = lc_i[lo:, :] - lc_i[s:s + 1, :]
            d = (jnp.concatenate([jnp.minimum(d[:half, :], 0.0), d[half:, :]], axis=0) if s < half
                 else jnp.minimum(d, 0.0))
            w = (q_i[lo:, :] * k_i[s:s + 1, :]) * jnp.exp2(d)
            if lo:
                w = jnp.concatenate([jnp.zeros((lo, LANE), F32), w], axis=0)
            cols.append(w.astype(BF16))
        pieces.append(jnp.concatenate(cols, axis=1))
    a_all = _dot(jnp.concatenate(pieces, axis=0), sel)
    inc = [_dot_tn(v_b[c * ln:(c + 1) * ln, :], kdec[c * ln:(c + 1) * ln, :]) for c in range(nchunk)]
    yield None
    pair_ok = (lax.broadcasted_iota(jnp.int32, (SUB_BLOCK, LANE), 0)
               >= lax.broadcasted_iota(jnp.int32, (SUB_BLOCK, LANE), 1))
    off = []
    for i in range(nsb):
        n = (i % per) * SUB_BLOCK
        off.append(_dot(a_off[i].astype(BF16), v_b[i * SUB_BLOCK - n:i * SUB_BLOCK, :]) if n
                   else jnp.zeros((SUB_BLOCK, LANE), F32))
    diag = [_dot(jnp.where(pair_ok, rows(a_all, i), 0.0)[:, :SUB_BLOCK].astype(BF16), rows(v_b, i))
            for i in range(nsb)]
    local = jnp.concatenate(off, axis=0) + jnp.concatenate(diag, axis=0)
    dec = [jnp.exp2(b) for b in last]
    yield local, qe, inc, dec


def _hgrn_kernel(*refs, ln, nchunk, has_init):
    refs = list(refs)
    hq_ref, hf_ref, hi_ref, hg_ref, lb_ref, go_ref, sel_ref = refs[:7]
    s0_ref = refs[7] if has_init else None
    o_ref, sout_ref, st_ref = refs[-3:]
    step = pl.program_id(1)
    nrows = ln * nchunk

    @pl.when(step == 0)
    def _():
        for h in range(HG_HEADS):
            st_ref[h] = s0_ref[0, h].T if has_init else jnp.zeros((HG_DV, HG_DK), F32)

    ri = lax.broadcasted_iota(jnp.int32, (nrows, nrows), 0)
    ci = lax.broadcasted_iota(jnp.int32, (nrows, nrows), 1)
    tri = ((ri >= ci) & (ri // ln == ci // ln)).astype(BF16)
    def finish(h, local, qe, inc, dec):
        cs = slice(h * LANE, (h + 1) * LANE)
        st = st_ref[h]
        parts = []
        for c in range(nchunk):
            parts.append(_dot_nt(qe[c * ln:(c + 1) * ln, :], st.astype(BF16)))
            st = st * dec[c] + inc[c]
        st_ref[h] = st
        o = local + jnp.concatenate(parts, axis=0)
        o_ref[:, cs] = (_rms(o, go_ref[...]) * _sigmoid(hg_ref[:, cs])).astype(BF16)

    def start(h):
        cs = slice(h * LANE, (h + 1) * LANE)
        gen = _hgrn_local(hq_ref[:, cs], hf_ref[:, cs], lb_ref[:, cs], tri, hi_ref[:, cs].astype(BF16),
                          ln, sel_ref[...])
        next(gen)
        return gen

    gens = {0: start(0)}
    for h in range(HG_HEADS):
        if h + 1 < HG_HEADS:
            gens[h + 1] = start(h + 1)
        next(gens[h])
        if h > 0:
            finish(h - 1, *next(gens.pop(h - 1)))
    finish(HG_HEADS - 1, *next(gens.pop(HG_HEADS - 1)))

    @pl.when(step == pl.num_programs(1) - 1)
    def _():
        for h in range(HG_HEADS):
            sout_ref[0, h] = st_ref[h].T


def hgrn(big, lb, g_out, bsz, s, ln, rows, s0=None, s0_row0=0):
    m = big.shape[0]
    ns = s // rows
    has_init = s0 is not None
    sel = (np.arange(SUB_BLOCK * LANE)[:, None] // LANE == np.arange(LANE)[None, :])
    sel = jnp.asarray(sel, BF16)
    blk = lambda off: pl.BlockSpec((rows, BRANCH_W), lambda b, i: (b * ns + i, off // BRANCH_W))
    ins = [big, big, big, big, lb, g_out, sel]
    specs = [blk(OFF_HQ), blk(OFF_HF), blk(OFF_HI), blk(OFF_HG),
             pl.BlockSpec((1, BRANCH_W), lambda b, i: (0, 0)),
             pl.BlockSpec((1, HG_DV), lambda b, i: (0, 0)),
             pl.BlockSpec(sel.shape, lambda b, i: (0, 0))]
    if has_init:
        ins.append(s0)
        specs.append(pl.BlockSpec((1, HG_HEADS, HG_DK, HG_DV), lambda b, i: (s0_row0 + b, 0, 0, 0)))
    return pl.pallas_call(
        functools.partial(_hgrn_kernel, ln=ln, nchunk=rows // ln, has_init=has_init),
        out_shape=(jax.ShapeDtypeStruct((m, BRANCH_W), BF16),
                   jax.ShapeDtypeStruct((bsz, HG_HEADS, HG_DK, HG_DV), F32)),
        grid=(bsz, ns),
        in_specs=specs,
        out_specs=(pl.BlockSpec((rows, BRANCH_W), lambda b, i: (b * ns + i, 0)),
                   pl.BlockSpec((1, HG_HEADS, HG_DK, HG_DV), lambda b, i: (b, 0, 0, 0))),
        scratch_shapes=[pltpu.VMEM((HG_HEADS, HG_DV, HG_DK), F32)],
        compiler_params=_params(("parallel", "arbitrary")),
        name="hgrn",
    )(*ins)


def _merge_kernel(of_ref, om_ref, oh_ref, x_ref, wg_ref, wb_ref, wo_ref, g0_ref, g1_ref, o_ref):
    x = x_ref[...]
    h = _rms(x, g0_ref[...]).astype(BF16)
    branches = (of_ref, om_ref, oh_ref)
    gates = [_dot(h, wg_ref[i]) for i in range(3)]
    outs = [_dot(branches[i][...], wb_ref[i]) for i in range(3)]
    merged = (_sigmoid(gates[0]) * outs[0] + _sigmoid(gates[1]) * outs[1]) + _sigmoid(gates[2]) * outs[2]
    y = _dot(merged.astype(BF16), wo_ref[...])
    o_ref[...] = x + _rms(y, g1_ref[...])


def merge_out(o_fox, o_mla, o_hg, x, wg, wb, wo, g0, g1, tm):
    m = x.shape[0]
    row = lambda w: pl.BlockSpec((tm, w), lambda i: (i, 0))
    vec = pl.BlockSpec((1, D_MODEL), lambda i: (0, 0))
    return pl.pallas_call(
        _merge_kernel,
        out_shape=jax.ShapeDtypeStruct((m, D_MODEL), F32),
        grid=(m // tm,),
        in_specs=[row(BRANCH_W), row(BRANCH_W), row(BRANCH_W), row(D_MODEL),
                  _resident(wg.shape), _resident(wb.shape), _resident(wo.shape), vec, vec],
        out_specs=row(D_MODEL),
        compiler_params=_params(("parallel",)),
        name="merge_out",
    )(o_fox, o_mla, o_hg, x, wg, wb, wo, g0, g1)


def _matmul2_kernel(x_ref, w_ref, a_ref, b_ref):
    y = _dot(x_ref[...].astype(BF16), w_ref[...])
    n = a_ref.shape[1]
    a_ref[...] = y[:, :n]
    b_ref[...] = y[:, n:]


def mem_kv(mem, w, tm):
    m, k = mem.shape
    n = w.shape[1] // 2
    row = lambda w_: pl.BlockSpec((tm, w_), lambda i: (i, 0))
    return pl.pallas_call(
        _matmul2_kernel,
        out_shape=(jax.ShapeDtypeStruct((m, n), F32), jax.ShapeDtypeStruct((m, n), F32)),
        grid=(m // tm,),
        in_specs=[row(k), _resident(w.shape)],
        out_specs=(row(n), row(n)),
        compiler_params=_params(("parallel",)),
        name="mem_kv",
    )(mem, w)


def _cross_kernel(x_ref, mk_ref, mv_ref, wq_ref, wo_ref, g2_ref, g3_ref, o_ref):
    x = x_ref[...]
    h = _rms(x, g2_ref[...]).astype(BF16)
    q = _dot(h, wq_ref[...])
    qb = (q * (X_DIM ** -0.5 * LOG2E)).astype(BF16)
    cols = [slice(hd * X_DIM, (hd + 1) * X_DIM) for hd in range(X_HEADS)]
    ss = [_dot_nt(qb[:, cs], mk_ref[:, cs].astype(BF16)) for cs in cols]
    ps = [jnp.exp2(s - jnp.max(s, axis=1, keepdims=True)) for s in ss]
    pvs = [_dot(p.astype(BF16), mv_ref[:, cs].astype(BF16)) for p, cs in zip(ps, cols)]
    outs = [pv / jnp.sum(p, axis=1, keepdims=True) for pv, p in zip(pvs, ps)]
    ox = jnp.concatenate(outs, axis=1).astype(BF16)
    o_ref[...] = x + _rms(_dot(ox, wo_ref[...]), g3_ref[...])


def cross_block(x, mk, mv, wq, wo, g2, g3, bsz, s, tm, mem_row0):
    m = x.shape[0]
    nt = s // tm
    vec = pl.BlockSpec((1, D_MODEL), lambda b, i: (0, 0))
    return pl.pallas_call(
        _cross_kernel,
        out_shape=jax.ShapeDtypeStruct((m, D_MODEL), F32),
        grid=(bsz, nt),
        in_specs=[pl.BlockSpec((tm, D_MODEL), lambda b, i: (b * nt + i, 0)),
                  pl.BlockSpec((N_MEM, X_HEADS * X_DIM), lambda b, i: (mem_row0 + b, 0)),
                  pl.BlockSpec((N_MEM, X_HEADS * X_DIM), lambda b, i: (mem_row0 + b, 0)),
                  _resident(wq.shape), _resident(wo.shape), vec, vec],
        out_specs=pl.BlockSpec((tm, D_MODEL), lambda b, i: (b * nt + i, 0)),
        compiler_params=_params(("parallel", "parallel")),
        name="cross_attn",
    )(x, mk, mv, wq, wo, g2, g3)


def _mlp_kernel(x_ref, wu_ref, wd_ref, g4_ref, g5_ref, o_ref, h_ref, acc_ref):
    j = pl.program_id(1)

    @pl.when(j == 0)
    def _():
        h_ref[...] = _rms(x_ref[...], g4_ref[...]).astype(BF16)
        acc_ref[...] = jnp.zeros_like(acc_ref)

    cols = pl.ds(pl.multiple_of(j * D_MODEL, D_MODEL), D_MODEL)
    u = jnp.square(jnp.maximum(_dot(h_ref[...], wu_ref[:, cols]), 0.0)).astype(BF16)
    acc_ref[...] += _dot(u, wd_ref[j])

    @pl.when(j == pl.num_programs(1) - 1)
    def _():
        o_ref[...] = x_ref[...] + _rms(acc_ref[...], g5_ref[...])


def mlp_block(x, wu3, wd3, g4, g5, tm):
    m = x.shape[0]
    nj = wd3.shape[0]
    vec = pl.BlockSpec((1, D_MODEL), lambda i, j: (0, 0))
    return pl.pallas_call(
        _mlp_kernel,
        out_shape=jax.ShapeDtypeStruct((m, D_MODEL), F32),
        grid=(m // tm, nj),
        in_specs=[pl.BlockSpec((tm, D_MODEL), lambda i, j: (i, 0)),
                  _resident(wu3.shape), _resident(wd3.shape), vec, vec],
        out_specs=pl.BlockSpec((tm, D_MODEL), lambda i, j: (i, 0)),
        scratch_shapes=[pltpu.VMEM((tm, D_MODEL), BF16), pltpu.VMEM((tm, D_MODEL), F32)],
        compiler_params=_params(("parallel", "arbitrary")),
        name="mlp",
    )(x, wu3, wd3, g4, g5)


def _prep_layer_weights(w_in, w_mla_uq, w_mla_ukv, w_branch, w_out, w_xq, w_mem_k, w_mem_v, w_xo,
                        w_up, w_down):
    idx = np.cumsum((0,) + IN_SIZES)
    seg = lambda i: w_in[:, idx[i]:idx[i + 1]]
    fq, fk, fv, ff, cq, ckv, kpe, hq, hf, hi, hg, ga, gb, gc = (seg(i) for i in range(14))
    half = MLA_ROPE // 2
    kpe_sw = jnp.concatenate([kpe[:, half:], kpe[:, :half]], axis=1)
    pad = jnp.zeros((D_MODEL, OFF_CQ - OFF_FQ - BRANCH_W), w_in.dtype)
    w_p = jnp.concatenate([hq, hf, hi, hg, fq, pad, cq, ckv], axis=1).astype(BF16)
    w_gate = jnp.stack([ga, gb, gc]).astype(BF16)
    w_t = jnp.concatenate([fk, fv, kpe, kpe_sw, ff], axis=1).T.astype(BF16)
    hd = MLA_NOPE + MLA_ROPE
    zq = jnp.zeros((MLA_Q_RANK, LANE - MLA_ROPE), w_mla_uq.dtype)
    nope, rope_n, rope_s = [], [], []
    for h in range(MLA_HEADS):
        base = h * hd
        nope.append(w_mla_uq[:, base:base + MLA_NOPE])
        x1 = w_mla_uq[:, base + MLA_NOPE:base + MLA_NOPE + half]
        x2 = w_mla_uq[:, base + MLA_NOPE + half:base + hd]
        rope_n += [x1, x2, zq]
        rope_s += [x2, x1, zq]
    wuq = jnp.concatenate(nope + rope_n + rope_s, axis=1).astype(BF16)
    kvd = MLA_NOPE + MLA_V
    wkt = jnp.concatenate([w_mla_ukv[:, h * kvd:h * kvd + MLA_NOPE] for h in range(MLA_HEADS)],
                          axis=1).T.astype(BF16)
    wv = jnp.concatenate([w_mla_ukv[:, h * kvd + MLA_NOPE:(h + 1) * kvd] for h in range(MLA_HEADS)],
                         axis=1).astype(BF16)
    nff = D_FF // D_MODEL
    return dict(
        w_in3=w_p, w_t=w_t, w_gate=w_gate, wuq=wuq, wkt=wkt, wv=wv,
        wb=w_branch.astype(BF16), wo=w_out.astype(BF16), wxq=w_xq.astype(BF16), wxo=w_xo.astype(BF16),
        wmem=jnp.concatenate([w_mem_k, w_mem_v], axis=1).astype(BF16),
        wu3=w_up.astype(BF16),
        wd3=w_down.astype(BF16).reshape(nff, D_MODEL, D_MODEL))


def _rope_tables(pos, reps):
    half = MLA_ROPE // 2
    freq = ROPE_THETA ** (-jnp.arange(half, dtype=F32) / half)
    ang = pos.astype(F32)[:, None] * freq[None, :]
    cos, sin = jnp.cos(ang), jnp.sin(ang)
    z = jnp.zeros((pos.shape[0], LANE - MLA_ROPE), F32)
    cos_r = jnp.tile(jnp.concatenate([cos, cos, z], axis=1), (reps, 1))
    sin_r = jnp.tile(jnp.concatenate([-sin, sin, z], axis=1), (reps, 1))
    return cos_r, sin_r, cos_r[:, :MLA_ROPE].T, sin_r[:, :MLA_ROPE].T


def _tile(n, pref):
    t = min(n, pref)
    assert n % t == 0
    return t


def _layer(x, bsz, s, pos0, w, lb, b_fox, g_q, g_kv, g_hout, g_norm, mem_k, mem_v, mem_row0, past, cfg,
           layer, depth, shared):
    m = bsz * s
    g = lambda i: g_norm[i][None, :]
    tm_in = _tile(s, cfg["tm_in"])
    tm_p = _tile(s, cfg["tm_prep"])
    cos_r, sin_r, cos_c, sin_c = _rope_tables(pos0 + jnp.arange(s), 1)
    big, kt, vt, kpet, logft = in_proj(x, g(0), w["w_in3"], w["w_t"], cos_c, sin_c, b_fox[:, None],
                                       bsz, s, tm_in, layer, depth, shared[:4])
    qx, knt, v, ckv_n = mla_prep(big, cos_r, sin_r, g_q[None, :], g_kv[None, :],
                                 w["wuq"], w["wkt"], w["wv"], bsz, s, tm_p, layer, depth, shared[4:])
    row0 = layer * bsz
    rows = lambda a: a.reshape((depth * bsz,) + a.shape[2:])
    cumt = fox_cumsum(rows(logft), row0, bsz)
    if past is None:
        t = _tile(s, cfg["t_attn"])
        o_fox = fox_prompt(big, rows(kt), rows(vt), cumt, row0, bsz, s, t)
        o_mla = mla_prompt(qx, knt, rows(kpet), v, row0, bsz, s, t)
        o_hg, hg_state = hgrn(big, lb[None, :], g_hout[None, :], bsz, s, CHUNK,
                              _tile(s, cfg["hg_rows"]))
    else:
        c_kt, c_vt, c_cumt, c_knt, c_kpt, c_v, c_hg = past
        p = c_kt.shape[2]
        o_fox = fox_sample(big, rows(kt), rows(vt), c_kt, c_vt, cumt, c_cumt, bsz, s, p, layer)
        o_mla = mla_sample(qx, knt, rows(kpet), v, c_knt, c_kpt, c_v, bsz, s, p, layer)
        o_hg, hg_state = hgrn(big, lb[None, :], g_hout[None, :], bsz, s, s, s, s0=c_hg, s0_row0=row0)
    x = merge_out(o_fox, o_mla, o_hg, x, w["w_gate"], w["wb"], w["wo"], g(0), g(1),
                  _tile(m, cfg["tm_merge"]))
    x = cross_block(x, mem_k, mem_v, w["wxq"], w["wxo"], g(2), g(3), bsz, s, _tile(s, cfg["tm_cross"]),
                    mem_row0)
    x = mlp_block(x, w["wu3"], w["wd3"], g(4), g(5), _tile(m, cfg["tm_mlp"]))
    return x, (kt, vt, kpet, logft, ckv_n), hg_state


def _from_feature_major(stacked, heads):
    a = jnp.swapaxes(stacked, 2, 3)
    if heads:
        a = a.reshape(a.shape[:3] + (heads, a.shape[3] // heads))
    return a


def _assemble_states(shared, hg_states, bsz, s):
    kt, vt, kpet, logft, ckv = shared
    return (_from_feature_major(kt, FOX_HEADS), _from_feature_major(vt, FOX_HEADS),
            _from_feature_major(logft, 0), ckv.reshape(ckv.shape[0], bsz, s, MLA_KV_RANK),
            _from_feature_major(kpet, 0), jnp.stack(hg_states))


_CFG = dict(tm_in=1024, tm_prep=512, t_attn=512, hg_rows=512, tm_merge=512, tm_cross=512,
            tm_mlp=1024, tm_mem=512, tm_expand=1024)


def kernel(x_prompt, x_sample, cache_fox_k, cache_fox_v, cache_fox_logf, cache_mla_ckv, cache_mla_kpe,
           state_hgrn, cache_mem_k, cache_mem_v, mem_prompt, w_in, b_fox, g_mla_q, w_mla_uq, g_mla_kv,
           w_mla_ukv, g_hgrn_out, lb_hgrn, w_branch, w_out, w_xq, w_mem_k, w_mem_v, w_xo, w_up, w_down,
           g_norm):
    cfg = _CFG
    depth = w_in.shape[0]
    lb_p = jax.nn.softmax(lb_hgrn.astype(F32), axis=0)
    lb_all = jnp.cumsum(lb_p, axis=0) - lb_p[0]
    ws = [_prep_layer_weights(w_in[l], w_mla_uq[l], w_mla_ukv[l], w_branch[l], w_out[l], w_xq[l],
                              w_mem_k[l], w_mem_v[l], w_xo[l], w_up[l], w_down[l]) for l in range(depth)]

    def run_layer(x, bsz, s, pos0, l, mk, mv, mem_row0, past, shared):
        return _layer(x, bsz, s, pos0, ws[l], lb_all[l], b_fox[l], g_mla_q[l], g_mla_kv[l],
                      g_hgrn_out[l], g_norm[l], mk, mv, mem_row0, past, cfg, l, depth, shared)

    bp, sp, _ = x_prompt.shape
    x = x_prompt.reshape(bp * sp, D_MODEL)
    mem = mem_prompt.reshape(bp * N_MEM, D_MODEL)
    shared, hg_states, p_mem = (), [], []
    for l in range(depth):
        mk, mv = mem_kv(mem, ws[l]["wmem"], _tile(bp * N_MEM, cfg["tm_mem"]))
        x, shared, hg = run_layer(x, bp, sp, 0, l, mk, mv, 0, None, shared)
        hg_states.append(hg)
        p_mem.append((mk.reshape(bp, N_MEM, X_HEADS, X_DIM), mv.reshape(bp, N_MEM, X_HEADS, X_DIM)))
    y_prompt = x.reshape(bp, sp, D_MODEL)
    p_out = _assemble_states(shared, hg_states, bp, sp) + tuple(jnp.stack(a) for a in zip(*p_mem))

    bs, ts, _ = x_sample.shape
    p = cache_fox_k.shape[2]
    fm = lambda c: jnp.moveaxis(c, 2, -1)
    c_kt = fm(cache_fox_k).reshape(depth * bs, BRANCH_W, p)
    c_vt = fm(cache_fox_v).reshape(depth * bs, BRANCH_W, p)
    c_kpt = fm(cache_mla_kpe).reshape(depth * bs, MLA_ROPE, p)
    c_cumt = fox_cumsum(fm(cache_fox_logf).reshape(depth * bs, FOX_HEADS, p), 0, depth * bs)
    c_ckv = cache_mla_ckv.reshape(depth * bs * p, MLA_KV_RANK)
    c_hg = state_hgrn.reshape((depth * bs,) + state_hgrn.shape[2:])
    c_mk = cache_mem_k.reshape(depth * bs * N_MEM, X_HEADS * X_DIM)
    c_mv = cache_mem_v.reshape(depth * bs * N_MEM, X_HEADS * X_DIM)
    x = x_sample.reshape(bs * ts, D_MODEL)
    shared, hg_states = (), []
    for l in range(depth):
        c_knt, c_v = latent_expand(c_ckv, ws[l]["wkt"], ws[l]["wv"], bs, p, _tile(p, cfg["tm_expand"]), l)
        past = (c_kt, c_vt, c_cumt, c_knt, c_kpt, c_v, c_hg)
        x, shared, hg = run_layer(x, bs, ts, p, l, c_mk, c_mv, l * bs, past, shared)
        hg_states.append(hg)
    y_sample = x.reshape(bs, ts, D_MODEL)
    return (y_prompt, y_sample, *p_out, *_assemble_states(shared, hg_states, bs, ts))
```

```python
import functools

import numpy as np
import jax
import jax.numpy as jnp
from jax import lax
from jax.experimental import pallas as pl
from jax.experimental.pallas import tpu as pltpu

F32 = jnp.float32
BF16 = jnp.bfloat16

D_MODEL = 1024
CHUNK = 64
N_MEM = 256
EPS = 1e-6
NEG_BIG = -1e30
EXP_CLIP = 80.0
FOX_HEADS = 8
FOX_DIM = 64
MLA_HEADS = 4
MLA_Q_RANK = 384
MLA_KV_RANK = 256
MLA_NOPE = 128
MLA_ROPE = 64
MLA_V = 128
ROPE_THETA = 10000.0
HG_HEADS = 4
HG_DK = 128
HG_DV = 128
X_HEADS = 4
X_DIM = 128
D_FF = 4 * D_MODEL
BRANCH_W = 512
IN_SIZES = (512, 512, 512, FOX_HEADS, MLA_Q_RANK, MLA_KV_RANK, MLA_ROPE, 512, 512, 512, 512,
            D_MODEL, D_MODEL, D_MODEL)

LANE = 128
SUB_BLOCK = 16
VMEM_LIMIT = 56 * 1024 * 1024
LOG2E = 1.4426950408889634

OFF_HQ, OFF_HF, OFF_HI, OFF_HG, OFF_FQ, OFF_CQ, OFF_CKV, NP_IN = (
    0, 512, 1024, 1536, 2048, 2688, 3072, 3328)
IN_TN = 1664
T_FK, T_FV, T_KPE, T_KPE_SW, T_FF, NT_IN = 0, 512, 1024, 1088, 1152, 1160


def _params(sem, vmem=VMEM_LIMIT):
    return pltpu.CompilerParams(dimension_semantics=sem, vmem_limit_bytes=vmem)


def _dot(a, b):
    return jnp.dot(a, b, preferred_element_type=F32)


def _dot_nt(a, b):
    return lax.dot_general(a, b, (((1,), (1,)), ((), ())), preferred_element_type=F32)


def _dot_tn(a, b):
    return lax.dot_general(a, b, (((0,), (0,)), ((), ())), preferred_element_type=F32)


def _rms(x, g):
    y = x * lax.rsqrt(jnp.mean(x * x, axis=-1, keepdims=True) + EPS)
    return y * g


def _log_sigmoid(z):
    return jnp.minimum(z, 0.0) - jnp.log(1.0 + jnp.exp(-jnp.abs(z)))


def _sigmoid(z):
    return 1.0 / (1.0 + jnp.exp(-z))


def _resident(shape):
    nd = len(shape)
    return pl.BlockSpec(shape, lambda *_: (0,) * nd, pipeline_mode=pl.Buffered(1))


def _in_proj_kernel(x_ref, g_ref, w_ref, wt_ref, cos_ref, sin_ref, bf_ref, *rest):
    big_ref, kt_ref, vt_ref, kpe_ref, lf_ref, h_ref = rest[-6:]
    j = pl.program_id(1)

    @pl.when(j == 0)
    def _():
        h = _rms(x_ref[...], g_ref[...]).astype(BF16)
        h_ref[...] = h
        yt = _dot_nt(wt_ref[...], h)
        kt_ref[0] = yt[T_FK:T_FK + BRANCH_W]
        vt_ref[0] = yt[T_FV:T_FV + BRANCH_W]
        kpe_ref[0] = (yt[T_KPE:T_KPE + MLA_ROPE] * cos_ref[...]
                      + yt[T_KPE_SW:T_KPE_SW + MLA_ROPE] * sin_ref[...])
        lf_ref[0] = _log_sigmoid(yt[T_FF:T_FF + FOX_HEADS] + bf_ref[...])

    tn = big_ref.shape[1]
    big_ref[...] = _dot(h_ref[...], w_ref[:, pl.ds(pl.multiple_of(j * tn, LANE), tn)])


def in_proj(x, g, w3, wt, cos_t, sin_t, b_col, bsz, s, tm, layer, depth, prev):
    m, k = x.shape
    tn = IN_TN
    nj = w3.shape[1] // tn
    nt = s // tm
    ntab = cos_t.shape[1] // tm
    feats = (BRANCH_W, BRANCH_W, MLA_ROPE, FOX_HEADS)
    tspec = lambda rows: pl.BlockSpec((None, 1, rows, tm), lambda i, j: (layer, i // nt, 0, i % nt))
    n_in = 7
    return pl.pallas_call(
        _in_proj_kernel,
        out_shape=(jax.ShapeDtypeStruct((m, nj * tn), F32),)
        + tuple(jax.ShapeDtypeStruct((depth, bsz, f, s), F32) for f in feats),
        grid=(m // tm, nj),
        in_specs=[pl.BlockSpec((tm, k), lambda i, j: (i, 0)),
                  pl.BlockSpec((1, k), lambda i, j: (0, 0)),
                  _resident(w3.shape),
                  _resident(wt.shape),
                  pl.BlockSpec((MLA_ROPE, tm), lambda i, j: (0, i % ntab)),
                  pl.BlockSpec((MLA_ROPE, tm), lambda i, j: (0, i % ntab)),
                  pl.BlockSpec((FOX_HEADS, 1), lambda i, j: (0, 0))]
        + [pl.BlockSpec(memory_space=pl.ANY)] * len(prev),
        out_specs=(pl.BlockSpec((tm, tn), lambda i, j: (i, j)),) + tuple(tspec(f) for f in feats),
        scratch_shapes=[pltpu.VMEM((tm, k), BF16)],
        input_output_aliases={n_in + i: 1 + i for i in range(len(prev))},
        compiler_params=_params(("parallel", "arbitrary")),
        name="in_proj",
    )(x, g, w3, wt, cos_t, sin_t, b_col, *prev)


def _mla_prep_kernel(cq_ref, ckv_ref, cs_ref, sn_ref, gq_ref, gkv_ref, wuq_ref, wkt_ref, wv_ref, *rest):
    qx_ref, knt_ref, v_ref, ckvn_ref = rest[-4:]
    cos_t = cs_ref[...]
    sin_t = sn_ref[...]
    qn = _rms(cq_ref[...], gq_ref[...]).astype(BF16)
    qall = _dot(qn, wuq_ref[...]) * ((MLA_NOPE + MLA_ROPE) ** -0.5 * LOG2E)
    for h in range(MLA_HEADS):
        lo = h * LANE
        qr = (qall[:, 512 + lo:512 + lo + LANE] * cos_t
              + qall[:, 1024 + lo:1024 + lo + LANE] * sin_t)
        qx_ref[:, 2 * lo:2 * lo + LANE] = qall[:, lo:lo + LANE].astype(BF16)
        qx_ref[:, 2 * lo + LANE:2 * lo + 2 * LANE] = qr.astype(BF16)
    ckvn = _rms(ckv_ref[...], gkv_ref[...])
    ckvn_ref[...] = ckvn
    cb = ckvn.astype(BF16)
    knt_ref[0] = _dot_nt(wkt_ref[...], cb).astype(BF16)
    v_ref[...] = _dot(cb, wv_ref[...]).astype(BF16)


def mla_prep(big, cos_t, sin_t, gq, gkv, wuq, wkt, wv, bsz, s, tm, layer, depth, prev):
    m = big.shape[0]
    nt = s // tm
    ntab = cos_t.shape[0] // tm
    row = lambda w: pl.BlockSpec((tm, w), lambda i: (i, 0))
    full = lambda a: pl.BlockSpec(a.shape, lambda i: (0,) * a.ndim)
    n_in = 9
    return pl.pallas_call(
        _mla_prep_kernel,
        out_shape=(jax.ShapeDtypeStruct((m, 1024), BF16),
                   jax.ShapeDtypeStruct((bsz, 512, s), BF16),
                   jax.ShapeDtypeStruct((m, 512), BF16),
                   jax.ShapeDtypeStruct((depth, m, MLA_KV_RANK), F32)),
        grid=(m // tm,),
        in_specs=[pl.BlockSpec((tm, MLA_Q_RANK), lambda i: (i, OFF_CQ // MLA_Q_RANK)),
                  pl.BlockSpec((tm, MLA_KV_RANK), lambda i: (i, OFF_CKV // MLA_KV_RANK)),
                  pl.BlockSpec((tm, LANE), lambda i: (i % ntab, 0)),
                  pl.BlockSpec((tm, LANE), lambda i: (i % ntab, 0)),
                  full(gq), full(gkv), full(wuq), full(wkt), full(wv)]
        + [pl.BlockSpec(memory_space=pl.ANY)] * len(prev),
        out_specs=(row(1024), pl.BlockSpec((1, 512, tm), lambda i: (i // nt, 0, i % nt)),
                   row(512), pl.BlockSpec((None, tm, MLA_KV_RANK), lambda i: (layer, i, 0))),
        input_output_aliases={n_in + i: 3 + i for i in range(len(prev))},
        compiler_params=_params(("parallel",)),
        name="mla_prep",
    )(big, big, cos_t, sin_t, gq, gkv, wuq, wkt, wv, *prev)


def _cumsum_kernel(x_ref, c_ref, *, w):
    s = x_ref.shape[2]
    r = lax.broadcasted_iota(jnp.int32, (w, w), 0)
    c = lax.broadcasted_iota(jnp.int32, (w, w), 1)
    upper = (r <= c).astype(F32)
    local = [jnp.dot(x_ref[0, :, g * w:(g + 1) * w], upper, preferred_element_type=F32,
                     precision=lax.Precision.HIGHEST) for g in range(s // w)]
    carry = jnp.zeros((x_ref.shape[1], 1), F32)
    for g, cum in enumerate(local):
        cum = cum + carry
        c_ref[0, :, g * w:(g + 1) * w] = cum
        carry = cum[:, w - 1:w]


def fox_cumsum(x, row0, bsz):
    _, h, s = x.shape
    return pl.pallas_call(
        functools.partial(_cumsum_kernel, w=min(LANE, s)),
        out_shape=jax.ShapeDtypeStruct((bsz, h, s), F32),
        grid=(bsz,),
        in_specs=[pl.BlockSpec((1, h, s), lambda i: (row0 + i, 0, 0))],
        out_specs=pl.BlockSpec((1, h, s), lambda i: (i, 0, 0)),
        compiler_params=_params(("parallel",)),
        name="fox_cumsum",
    )(x)


def _pair_rows_mask(hh):
    sub = lax.broadcasted_iota(jnp.int32, (LANE, 1), 0)
    return (sub < FOX_DIM) if hh == 0 else (sub >= FOX_DIM)


def _fox_finish(acc0, acc1):
    lane = lax.broadcasted_iota(jnp.int32, (1, LANE), 1)
    o0 = acc0 / pltpu.roll(acc0, FOX_DIM, axis=1)
    o1 = acc1 / pltpu.roll(acc1, FOX_DIM, axis=1)
    return jnp.where(lane < FOX_DIM, o0, o1)


def _causal_schedule(nq):
    todo = {i: list(range(i + 1)) for i in range(nq)}
    order = []
    while any(todo.values()):
        for i in reversed(range(nq)):
            if todo[i]:
                order.append((i, todo[i].pop(0)))
    return order


def _fox_prompt_kernel(q_ref, kt_ref, vt_ref, ct_ref, o_ref, *, t, nq):
    hp = pl.program_id(1)
    lane = lax.broadcasted_iota(jnp.int32, (1, LANE), 1)
    lo = lane < FOX_DIM
    span = lambda i: slice(i * t, (i + 1) * t)
    ct = [ct_ref[0, 2 * hp + hh] for hh in range(2)]
    qs, cref = [], []
    for i in range(nq):
        q = q_ref[span(i), :] * (FOX_DIM ** -0.5 * LOG2E)
        qs.append((jnp.where(lo, q, 0.0).astype(BF16), jnp.where(lo, 0.0, q).astype(BF16)))
        cref.append([c[:, i * t:i * t + 1] for c in ct])
    kts = [kt_ref[0, :, span(j)].astype(BF16) for j in range(nq)]
    vts = [[jnp.where(_pair_rows_mask(hh), vt_ref[0, :, span(j)], 1.0).astype(BF16) for hh in range(2)]
           for j in range(nq)]
    causal = (lax.broadcasted_iota(jnp.int32, (t, t), 0) >= lax.broadcasted_iota(jnp.int32, (t, t), 1))

    def scores(i, j):
        out = []
        for hh in range(2):
            s = _dot(qs[i][hh], kts[j]) + (cref[i][hh] - ct[hh][:, span(j)]) * LOG2E
            out.append(jnp.where(causal, s, NEG_BIG) if i == j else s)
        return out

    state = [[(jnp.full((t, 1), NEG_BIG, F32), jnp.zeros((t, LANE), F32)) for _ in range(2)]
             for _ in range(nq)]
    order = _causal_schedule(nq)
    ss = scores(*order[0])
    for n, (i, j) in enumerate(order):
        nxt = scores(*order[n + 1]) if n + 1 < len(order) else None
        m_new = [jnp.maximum(state[i][hh][0], jnp.max(ss[hh], axis=1, keepdims=True)) for hh in range(2)]
        ps = [jnp.exp2(ss[hh] - m_new[hh]).astype(BF16) for hh in range(2)]
        pvs = [_dot_nt(ps[hh], vts[j][hh]) for hh in range(2)]
        state[i] = [(m_new[hh], jnp.exp2(state[i][hh][0] - m_new[hh]) * state[i][hh][1] + pvs[hh])
                    for hh in range(2)]
        ss = nxt
    for i in range(nq):
        o_ref[span(i), :] = _fox_finish(state[i][0][1], state[i][1][1]).astype(BF16)


def fox_prompt(big, kt, vt, cumt, row0, bsz, s, t):
    m = big.shape[0]
    return pl.pallas_call(
        functools.partial(_fox_prompt_kernel, t=t, nq=s // t),
        out_shape=jax.ShapeDtypeStruct((m, BRANCH_W), BF16),
        grid=(bsz, FOX_HEADS // 2),
        in_specs=[pl.BlockSpec((s, LANE), lambda b, h: (b, OFF_FQ // LANE + h)),
                  pl.BlockSpec((1, LANE, s), lambda b, h: (row0 + b, h, 0)),
                  pl.BlockSpec((1, LANE, s), lambda b, h: (row0 + b, h, 0)),
                  pl.BlockSpec((1, FOX_HEADS, 1, s), lambda b, h: (b, 0, 0, 0))],
        out_specs=pl.BlockSpec((s, LANE), lambda b, h: (b, h)),
        compiler_params=_params(("parallel", "parallel")),
        name="fox_prompt",
    )(big, kt, vt, cumt.reshape(bsz, FOX_HEADS, 1, s))


def _mla_keys(knt, kpt):
    n = knt.shape[1]
    return jnp.concatenate([knt, kpt.astype(BF16), jnp.zeros((LANE - MLA_ROPE, n), BF16)], axis=0)


def _mla_prompt_kernel(q_ref, knt_ref, kpt_ref, v_ref, o_ref, *, t, nq):
    hs = range(2)
    span = lambda i: slice(i * t, (i + 1) * t)
    cols = lambda hh, w: slice(hh * w, (hh + 1) * w)
    keys = [[_mla_keys(knt_ref[0, cols(hh, LANE), span(j)], kpt_ref[0, :, span(j)]) for hh in hs]
            for j in range(nq)]
    rc = lax.broadcasted_iota(jnp.int32, (t, t), 0) // CHUNK
    cc = lax.broadcasted_iota(jnp.int32, (t, t), 1) // CHUNK
    mask = rc >= cc

    def scores(i, j):
        ss = [_dot(q_ref[span(i), cols(hh, 2 * LANE)], keys[j][hh]) for hh in hs]
        return [jnp.where(mask, s, NEG_BIG) for s in ss] if i == j else ss

    state = [[(jnp.full((t, 1), NEG_BIG, F32), jnp.zeros((t, 1), F32), jnp.zeros((t, LANE), F32))
              for _ in hs] for _ in range(nq)]
    order = _causal_schedule(nq)
    ss = scores(*order[0])
    for n, (i, j) in enumerate(order):
        nxt = scores(*order[n + 1]) if n + 1 < len(order) else None
        m_new = [jnp.maximum(state[i][hh][0], jnp.max(ss[hh], axis=1, keepdims=True)) for hh in hs]
        ps = [jnp.exp2(ss[hh] - m_new[hh]) for hh in hs]
        pvs = [_dot(ps[hh].astype(BF16), v_ref[span(j), cols(hh, LANE)]) for hh in hs]
        new = []
        for hh in hs:
            alpha = jnp.exp2(state[i][hh][0] - m_new[hh])
            new.append((m_new[hh], alpha * state[i][hh][1] + jnp.sum(ps[hh], axis=1, keepdims=True),
                        alpha * state[i][hh][2] + pvs[hh]))
        state[i] = new
        ss = nxt
    for i in range(nq):
        o_ref[span(i), :] = jnp.concatenate([state[i][hh][2] / state[i][hh][1] for hh in hs],
                                            axis=1).astype(BF16)


def mla_prompt(qx, knt, kpt, v, row0, bsz, s, t):
    assert t % CHUNK == 0
    m = qx.shape[0]
    return pl.pallas_call(
        functools.partial(_mla_prompt_kernel, t=t, nq=s // t),
        out_shape=jax.ShapeDtypeStruct((m, BRANCH_W), BF16),
        grid=(bsz, MLA_HEADS // 2),
        in_specs=[pl.BlockSpec((s, 4 * LANE), lambda b, h: (b, h)),
                  pl.BlockSpec((1, 2 * LANE, s), lambda b, h: (b, h, 0)),
                  pl.BlockSpec((1, MLA_ROPE, s), lambda b, h: (row0 + b, 0, 0)),
                  pl.BlockSpec((s, 2 * LANE), lambda b, h: (b, h))],
        out_specs=pl.BlockSpec((s, 2 * LANE), lambda b, h: (b, h)),
        compiler_params=_params(("parallel", "parallel")),
        name="mla_prompt",
    )(qx, knt, kpt, v)


def _fox_sample_kernel(q_ref, ktn_ref, vtn_ref, ktc_ref, vtc_ref, ctn_ref, ctc_ref, o_ref, *, t, p):
    hp = pl.program_id(1)
    lane = lax.broadcasted_iota(jnp.int32, (1, LANE), 1)
    lo = lane < FOX_DIM
    q = q_ref[...] * (FOX_DIM ** -0.5 * LOG2E)
    kt_c = ktc_ref[0].astype(BF16)
    kt_n = ktn_ref[0].astype(BF16)
    vt_c = vtc_ref[0]
    vt_n = vtn_ref[0]
    causal = (lax.broadcasted_iota(jnp.int32, (t, t), 0) >= lax.broadcasted_iota(jnp.int32, (t, t), 1))
    accs = []
    for hh in range(2):
        head = 2 * hp + hh
        qh = (jnp.where(lo, q, 0.0) if hh == 0 else jnp.where(lo, 0.0, q)).astype(BF16)
        cc = ctc_ref[0, pl.ds(head, 1), :]
        ctot = cc[:, p - 1:p]
        s_c = _dot(qh, kt_c) + (ctot - cc) * LOG2E
        s_n = _dot(qh, kt_n) - ctn_ref[0, pl.ds(head, 1), :] * LOG2E
        s_n = jnp.where(causal, s_n, NEG_BIG)
        m = jnp.maximum(jnp.max(s_c, axis=1, keepdims=True), jnp.max(s_n, axis=1, keepdims=True))
        rows = _pair_rows_mask(hh)
        accs.append(_dot_nt(jnp.exp2(s_c - m).astype(BF16), jnp.where(rows, vt_c, 1.0).astype(BF16))
                    + _dot_nt(jnp.exp2(s_n - m).astype(BF16), jnp.where(rows, vt_n, 1.0).astype(BF16)))
    o_ref[...] = _fox_finish(accs[0], accs[1]).astype(BF16)


def fox_sample(big, kt_n, vt_n, kt_c, vt_c, cumt_n, cumt_c, bsz, t, p, layer):
    cidx = lambda b, h: (layer * bsz + b, h, 0)
    return pl.pallas_call(
        functools.partial(_fox_sample_kernel, t=t, p=p),
        out_shape=jax.ShapeDtypeStruct((bsz * t, BRANCH_W), BF16),
        grid=(bsz, FOX_HEADS // 2),
        in_specs=[pl.BlockSpec((t, LANE), lambda b, h: (b, OFF_FQ // LANE + h)),
                  pl.BlockSpec((1, LANE, t), cidx),
                  pl.BlockSpec((1, LANE, t), cidx),
                  pl.BlockSpec((1, LANE, p), cidx),
                  pl.BlockSpec((1, LANE, p), cidx),
                  pl.BlockSpec((1, FOX_HEADS, t), lambda b, h: (b, 0, 0)),
                  pl.BlockSpec((1, FOX_HEADS, p), lambda b, h: (layer * bsz + b, 0, 0))],
        out_specs=pl.BlockSpec((t, LANE), lambda b, h: (b, h)),
        compiler_params=_params(("parallel", "parallel")),
        name="fox_sample",
    )(big, kt_n, vt_n, kt_c, vt_c, cumt_n, cumt_c)


def _mla_sample_kernel(q_ref, kntn_ref, kptn_ref, vn_ref, kntc_ref, kptc_ref, vc_ref, o_ref, *, t, p):
    q = q_ref[...]
    s_c = _dot(q, _mla_keys(kntc_ref[0], kptc_ref[0]))
    s_n = _dot(q, _mla_keys(kntn_ref[0], kptn_ref[0]))
    qc = (p + lax.broadcasted_iota(jnp.int32, (t, t), 0)) // CHUNK
    kc = (p + lax.broadcasted_iota(jnp.int32, (t, t), 1)) // CHUNK
    s_n = jnp.where(qc >= kc, s_n, NEG_BIG)
    m = jnp.maximum(jnp.max(s_c, axis=1, keepdims=True), jnp.max(s_n, axis=1, keepdims=True))
    p_c = jnp.exp2(s_c - m)
    p_n = jnp.exp2(s_n - m)
    l = jnp.sum(p_c, axis=1, keepdims=True) + jnp.sum(p_n, axis=1, keepdims=True)
    o = _dot(p_c.astype(BF16), vc_ref[...]) + _dot(p_n.astype(BF16), vn_ref[...])
    o_ref[...] = (o / l).astype(BF16)


def mla_sample(qx, knt_n, kpt_n, v_n, knt_c, kpt_c, v_c, bsz, t, p, layer):
    assert (p - 1) // CHUNK <= p // CHUNK
    return pl.pallas_call(
        functools.partial(_mla_sample_kernel, t=t, p=p),
        out_shape=jax.ShapeDtypeStruct((bsz * t, BRANCH_W), BF16),
        grid=(bsz, MLA_HEADS),
        in_specs=[pl.BlockSpec((t, 2 * LANE), lambda b, h: (b, h)),
                  pl.BlockSpec((1, LANE, t), lambda b, h: (b, h, 0)),
                  pl.BlockSpec((1, MLA_ROPE, t), lambda b, h: (layer * bsz + b, 0, 0)),
                  pl.BlockSpec((t, LANE), lambda b, h: (b, h)),
                  pl.BlockSpec((1, LANE, p), lambda b, h: (b, h, 0)),
                  pl.BlockSpec((1, MLA_ROPE, p), lambda b, h: (layer * bsz + b, 0, 0)),
                  pl.BlockSpec((p, LANE), lambda b, h: (b, h))],
        out_specs=pl.BlockSpec((t, LANE), lambda b, h: (b, h)),
        compiler_params=_params(("parallel", "parallel")),
        name="mla_sample",
    )(qx, knt_n, kpt_n, v_n, knt_c, kpt_c, v_c)


def _latent_expand_kernel(c_ref, wkt_ref, wv_ref, knt_ref, v_ref):
    cb = c_ref[...].astype(BF16)
    knt_ref[0] = _dot_nt(wkt_ref[...], cb).astype(BF16)
    v_ref[...] = _dot(cb, wv_ref[...]).astype(BF16)


def latent_expand(ckv, wkt, wv, bsz, p, tm, layer):
    m = bsz * p
    nt = p // tm
    return pl.pallas_call(
        _latent_expand_kernel,
        out_shape=(jax.ShapeDtypeStruct((bsz, 512, p), BF16), jax.ShapeDtypeStruct((m, 512), BF16)),
        grid=(m // tm,),
        in_specs=[pl.BlockSpec((tm, MLA_KV_RANK), lambda i: (layer * (m // tm) + i, 0)),
                  pl.BlockSpec(wkt.shape, lambda i: (0, 0)), pl.BlockSpec(wv.shape, lambda i: (0, 0))],
        out_specs=(pl.BlockSpec((1, 512, tm), lambda i: (i // nt, 0, i % nt)),
                   pl.BlockSpec((tm, 512), lambda i: (i, 0))),
        compiler_params=_params(("parallel",)),
        name="latent_expand",
    )(ckv, wkt, wv)


def _hgrn_gates(z, lb, tri):
    logf = _log_sigmoid(z) + jnp.log(1.0 + lb * jnp.exp(jnp.minimum(-z, EXP_CLIP)))
    k = (1.0 - lb) * (1.0 / (1.0 + jnp.exp(z)))
    h1 = logf.astype(BF16)
    r1 = logf - h1.astype(F32)
    h2 = r1.astype(BF16)
    h3 = (r1 - h2.astype(F32)).astype(BF16)
    parts = _dot(tri, jnp.concatenate([h1, h2, h3], axis=1))
    lc = ((parts[:, :LANE] + parts[:, LANE:2 * LANE]) + parts[:, 2 * LANE:]) * LOG2E
    return k, lc


def _hgrn_local(q, z, lb, tri, v_b, ln, sel):
    k, lc = _hgrn_gates(z, lb, tri)
    nchunk = q.shape[0] // ln
    nsb = q.shape[0] // SUB_BLOCK
    per = ln // SUB_BLOCK
    half = SUB_BLOCK // 2
    rows = lambda a, i: a[i * SUB_BLOCK:(i + 1) * SUB_BLOCK, :]
    lcb = [jnp.zeros((1, LANE), F32) if i % per == 0 else lc[i * SUB_BLOCK - 1:i * SUB_BLOCK, :]
           for i in range(nsb)]
    lcb_rows = jnp.concatenate([jnp.broadcast_to(b, (SUB_BLOCK, LANE)) for b in lcb], axis=0)
    last = [lc[(c + 1) * ln - 1:(c + 1) * ln, :] for c in range(nchunk)]
    last_rows = jnp.concatenate([jnp.broadcast_to(b, (ln, LANE)) for b in last], axis=0)
    qh = (q * jnp.exp2(lc - lcb_rows)).astype(BF16)
    qe = (q * jnp.exp2(lc)).astype(BF16)
    kdec = (k * jnp.exp2(last_rows - lc)).astype(BF16)
    a_off = {}
    for i in range(nsb):
        n = (i % per) * SUB_BLOCK
        if n:
            c0 = i * SUB_BLOCK - n
            kt = (k[c0:c0 + n, :] * jnp.exp2(lcb[i] - lc[c0:c0 + n, :])).astype(BF16)
            a_off[i] = _dot_nt(rows(qh, i), kt)
    yield None
    pieces = []
    for i in range(nsb):
        q_i, k_i, lc_i = rows(q, i), rows(k, i), rows(lc, i)
        cols = []
        for s in range(SUB_BLOCK):
            lo = 0 if s < half else half
            d = lc_i[lo:, :] - lc_i[s:s + 1, :]
            d = (jnp.concatenate([jnp.minimum(d[:half, :], 0.0), d[half:, :]], axis=0) if s < half
                 else jnp.minimum(d, 0.0))
            w = (q_i[lo:, :] * k_i[s:s + 1, :]) * jnp.exp2(d)
            if lo:
                w = jnp.concatenate([jnp.zeros((lo, LANE), F32), w], axis=0)
            cols.append(w.astype(BF16))
        pieces.append(jnp.concatenate(cols, axis=1))
    a_all = _dot(jnp.concatenate(pieces, axis=0), sel)
    inc = [_dot_tn(v_b[c * ln:(c + 1) * ln, :], kdec[c * ln:(c + 1) * ln, :]) for c in range(nchunk)]
    yield None
    pair_ok = (lax.broadcasted_iota(jnp.int32, (SUB_BLOCK, LANE), 0)
               >= lax.broadcasted_iota(jnp.int32, (SUB_BLOCK, LANE), 1))
    off = []
    for i in range(nsb):
        n = (i % per) * SUB_BLOCK
        off.append(_dot(a_off[i].astype(BF16), v_b[i * SUB_BLOCK - n:i * SUB_BLOCK, :]) if n
                   else jnp.zeros((SUB_BLOCK, LANE), F32))
    diag = [_dot(jnp.where(pair_ok, rows(a_all, i), 0.0)[:, :SUB_BLOCK].astype(BF16), rows(v_b, i))
            for i in range(nsb)]
    local = jnp.concatenate(off, axis=0) + jnp.concatenate(diag, axis=0)
    dec = [jnp.exp2(b) for b in last]
    yield local, qe, inc, dec


def _hgrn_kernel(*refs, ln, nchunk, has_init):
    refs = list(refs)
    hq_ref, hf_ref, hi_ref, hg_ref, lb_ref, go_ref, sel_ref = refs[:7]
    s0_ref = refs[7] if has_init else None
    o_ref, sout_ref, st_ref = refs[-3:]
    step = pl.program_id(1)
    nrows = ln * nchunk

    @pl.when(step == 0)
    def _():
        for h in range(HG_HEADS):
            st_ref[h] = s0_ref[0, h].T if has_init else jnp.zeros((HG_DV, HG_DK), F32)

    ri = lax.broadcasted_iota(jnp.int32, (nrows, nrows), 0)
    ci = lax.broadcasted_iota(jnp.int32, (nrows, nrows), 1)
    tri = ((ri >= ci) & (ri // ln == ci // ln)).astype(BF16)
    def finish(h, local, qe, inc, dec):
        cs = slice(h * LANE, (h + 1) * LANE)
        st = st_ref[h]
        parts = []
        for c in range(nchunk):
            parts.append(_dot_nt(qe[c * ln:(c + 1) * ln, :], st.astype(BF16)))
            st = st * dec[c] + inc[c]
        st_ref[h] = st
        o = local + jnp.concatenate(parts, axis=0)
        o_ref[:, cs] = (_rms(o, go_ref[...]) * _sigmoid(hg_ref[:, cs])).astype(BF16)

    def start(h):
        cs = slice(h * LANE, (h + 1) * LANE)
        gen = _hgrn_local(hq_ref[:, cs], hf_ref[:, cs], lb_ref[:, cs], tri, hi_ref[:, cs].astype(BF16),
                          ln, sel_ref[...])
        next(gen)
        return gen

    gens = {0: start(0)}
    for h in range(HG_HEADS):
        if h + 1 < HG_HEADS:
            gens[h + 1] = start(h + 1)
        next(gens[h])
        if h > 0:
            finish(h - 1, *next(gens.pop(h - 1)))
    finish(HG_HEADS - 1, *next(gens.pop(HG_HEADS - 1)))

    @pl.when(step == pl.num_programs(1) - 1)
    def _():
        for h in range(HG_HEADS):
            sout_ref[0, h] = st_ref[h].T


def hgrn(big, lb, g_out, bsz, s, ln, rows, s0=None, s0_row0=0):
    m = big.shape[0]
    ns = s // rows
    has_init = s0 is not None
    sel = (np.arange(SUB_BLOCK * LANE)[:, None] // LANE == np.arange(LANE)[None, :])
    sel = jnp.asarray(sel, BF16)
    blk = lambda off: pl.BlockSpec((rows, BRANCH_W), lambda b, i: (b * ns + i, off // BRANCH_W))
    ins = [big, big, big, big, lb, g_out, sel]
    specs = [blk(OFF_HQ), blk(OFF_HF), blk(OFF_HI), blk(OFF_HG),
             pl.BlockSpec((1, BRANCH_W), lambda b, i: (0, 0)),
             pl.BlockSpec((1, HG_DV), lambda b, i: (0, 0)),
             pl.BlockSpec(sel.shape, lambda b, i: (0, 0))]
    if has_init:
        ins.append(s0)
        specs.append(pl.BlockSpec((1, HG_HEADS, HG_DK, HG_DV), lambda b, i: (s0_row0 + b, 0, 0, 0)))
    return pl.pallas_call(
        functools.partial(_hgrn_kernel, ln=ln, nchunk=rows // ln, has_init=has_init),
        out_shape=(jax.ShapeDtypeStruct((m, BRANCH_W), BF16),
                   jax.ShapeDtypeStruct((bsz, HG_HEADS, HG_DK, HG_DV), F32)),
        grid=(bsz, ns),
        in_specs=specs,
        out_specs=(pl.BlockSpec((rows, BRANCH_W), lambda b, i: (b * ns + i, 0)),
                   pl.BlockSpec((1, HG_HEADS, HG_DK, HG_DV), lambda b, i: (b, 0, 0, 0))),
        scratch_shapes=[pltpu.VMEM((HG_HEADS, HG_DV, HG_DK), F32)],
        compiler_params=_params(("parallel", "arbitrary")),
        name="hgrn",
    )(*ins)


def _merge_kernel(of_ref, om_ref, oh_ref, x_ref, wg_ref, wb_ref, wo_ref, g0_ref, g1_ref, o_ref):
    x = x_ref[...]
    h = _rms(x, g0_ref[...]).astype(BF16)
    branches = (of_ref, om_ref, oh_ref)
    gates = [_dot(h, wg_ref[i]) for i in range(3)]
    outs = [_dot(branches[i][...], wb_ref[i]) for i in range(3)]
    merged = (_sigmoid(gates[0]) * outs[0] + _sigmoid(gates[1]) * outs[1]) + _sigmoid(gates[2]) * outs[2]
    y = _dot(merged.astype(BF16), wo_ref[...])
    o_ref[...] = x + _rms(y, g1_ref[...])


def merge_out(o_fox, o_mla, o_hg, x, wg, wb, wo, g0, g1, tm):
    m = x.shape[0]
    row = lambda w: pl.BlockSpec((tm, w), lambda i: (i, 0))
    vec = pl.BlockSpec((1, D_MODEL), lambda i: (0, 0))
    return pl.pallas_call(
        _merge_kernel,
        out_shape=jax.ShapeDtypeStruct((m, D_MODEL), F32),
        grid=(m // tm,),
        in_specs=[row(BRANCH_W), row(BRANCH_W), row(BRANCH_W), row(D_MODEL),
                  _resident(wg.shape), _resident(wb.shape), _resident(wo.shape), vec, vec],
        out_specs=row(D_MODEL),
        compiler_params=_params(("parallel",)),
        name="merge_out",
    )(o_fox, o_mla, o_hg, x, wg, wb, wo, g0, g1)


def _matmul2_kernel(x_ref, w_ref, a_ref, b_ref):
    y = _dot(x_ref[...].astype(BF16), w_ref[...])
    n = a_ref.shape[1]
    a_ref[...] = y[:, :n]
    b_ref[...] = y[:, n:]


def mem_kv(mem, w, tm):
    m, k = mem.shape
    n = w.shape[1] // 2
    row = lambda w_: pl.BlockSpec((tm, w_), lambda i: (i, 0))
    return pl.pallas_call(
        _matmul2_kernel,
        out_shape=(jax.ShapeDtypeStruct((m, n), F32), jax.ShapeDtypeStruct((m, n), F32)),
        grid=(m // tm,),
        in_specs=[row(k), _resident(w.shape)],
        out_specs=(row(n), row(n)),
        compiler_params=_params(("parallel",)),
        name="mem_kv",
    )(mem, w)


def _cross_kernel(x_ref, mk_ref, mv_ref, wq_ref, wo_ref, g2_ref, g3_ref, o_ref):
    x = x_ref[...]
    h = _rms(x, g2_ref[...]).astype(BF16)
    q = _dot(h, wq_ref[...])
    qb = (q * (X_DIM ** -0.5 * LOG2E)).astype(BF16)
    cols = [slice(hd * X_DIM, (hd + 1) * X_DIM) for hd in range(X_HEADS)]
    ss = [_dot_nt(qb[:, cs], mk_ref[:, cs].astype(BF16)) for cs in cols]
    ps = [jnp.exp2(s - jnp.max(s, axis=1, keepdims=True)) for s in ss]
    pvs = [_dot(p.astype(BF16), mv_ref[:, cs].astype(BF16)) for p, cs in zip(ps, cols)]
    outs = [pv / jnp.sum(p, axis=1, keepdims=True) for pv, p in zip(pvs, ps)]
    ox = jnp.concatenate(outs, axis=1).astype(BF16)
    o_ref[...] = x + _rms(_dot(ox, wo_ref[...]), g3_ref[...])


def cross_block(x, mk, mv, wq, wo, g2, g3, bsz, s, tm, mem_row0):
    m = x.shape[0]
    nt = s // tm
    vec = pl.BlockSpec((1, D_MODEL), lambda b, i: (0, 0))
    return pl.pallas_call(
        _cross_kernel,
        out_shape=jax.ShapeDtypeStruct((m, D_MODEL), F32),
        grid=(bsz, nt),
        in_specs=[pl.BlockSpec((tm, D_MODEL), lambda b, i: (b * nt + i, 0)),
                  pl.BlockSpec((N_MEM, X_HEADS * X_DIM), lambda b, i: (mem_row0 + b, 0)),
                  pl.BlockSpec((N_MEM, X_HEADS * X_DIM), lambda b, i: (mem_row0 + b, 0)),
                  _resident(wq.shape), _resident(wo.shape), vec, vec],
        out_specs=pl.BlockSpec((tm, D_MODEL), lambda b, i: (b * nt + i, 0)),
        compiler_params=_params(("parallel", "parallel")),
        name="cross_attn",
    )(x, mk, mv, wq, wo, g2, g3)


def _mlp_kernel(x_ref, wu_ref, wd_ref, g4_ref, g5_ref, o_ref, h_ref, acc_ref):
    j = pl.program_id(1)

    @pl.when(j == 0)
    def _():
        h_ref[...] = _rms(x_ref[...], g4_ref[...]).astype(BF16)
        acc_ref[...] = jnp.zeros_like(acc_ref)

    cols = pl.ds(pl.multiple_of(j * D_MODEL, D_MODEL), D_MODEL)
    u = jnp.square(jnp.maximum(_dot(h_ref[...], wu_ref[:, cols]), 0.0)).astype(BF16)
    acc_ref[...] += _dot(u, wd_ref[j])

    @pl.when(j == pl.num_programs(1) - 1)
    def _():
        o_ref[...] = x_ref[...] + _rms(acc_ref[...], g5_ref[...])


def mlp_block(x, wu3, wd3, g4, g5, tm):
    m = x.shape[0]
    nj = wd3.shape[0]
    vec = pl.BlockSpec((1, D_MODEL), lambda i, j: (0, 0))
    return pl.pallas_call(
        _mlp_kernel,
        out_shape=jax.ShapeDtypeStruct((m, D_MODEL), F32),
        grid=(m // tm, nj),
        in_specs=[pl.BlockSpec((tm, D_MODEL), lambda i, j: (i, 0)),
                  _resident(wu3.shape), _resident(wd3.shape), vec, vec],
        out_specs=pl.BlockSpec((tm, D_MODEL), lambda i, j: (i, 0)),
        scratch_shapes=[pltpu.VMEM((tm, D_MODEL), BF16), pltpu.VMEM((tm, D_MODEL), F32)],
        compiler_params=_params(("parallel", "arbitrary")),
        name="mlp",
    )(x, wu3, wd3, g4, g5)


def _prep_layer_weights(w_in, w_mla_uq, w_mla_ukv, w_branch, w_out, w_xq, w_mem_k, w_mem_v, w_xo,
                        w_up, w_down):
    idx = np.cumsum((0,) + IN_SIZES)
    seg = lambda i: w_in[:, idx[i]:idx[i + 1]]
    fq, fk, fv, ff, cq, ckv, kpe, hq, hf, hi, hg, ga, gb, gc = (seg(i) for i in range(14))
    half = MLA_ROPE // 2
    kpe_sw = jnp.concatenate([kpe[:, half:], kpe[:, :half]], axis=1)
    pad = jnp.zeros((D_MODEL, OFF_CQ - OFF_FQ - BRANCH_W), w_in.dtype)
    w_p = jnp.concatenate([hq, hf, hi, hg, fq, pad, cq, ckv], axis=1).astype(BF16)
    w_gate = jnp.stack([ga, gb, gc]).astype(BF16)
    w_t = jnp.concatenate([fk, fv, kpe, kpe_sw, ff], axis=1).T.astype(BF16)
    hd = MLA_NOPE + MLA_ROPE
    zq = jnp.zeros((MLA_Q_RANK, LANE - MLA_ROPE), w_mla_uq.dtype)
    nope, rope_n, rope_s = [], [], []
    for h in range(MLA_HEADS):
        base = h * hd
        nope.append(w_mla_uq[:, base:base + MLA_NOPE])
        x1 = w_mla_uq[:, base + MLA_NOPE:base + MLA_NOPE + half]
        x2 = w_mla_uq[:, base + MLA_NOPE + half:base + hd]
        rope_n += [x1, x2, zq]
        rope_s += [x2, x1, zq]
    wuq = jnp.concatenate(nope + rope_n + rope_s, axis=1).astype(BF16)
    kvd = MLA_NOPE + MLA_V
    wkt = jnp.concatenate([w_mla_ukv[:, h * kvd:h * kvd + MLA_NOPE] for h in range(MLA_HEADS)],
                          axis=1).T.astype(BF16)
    wv = jnp.concatenate([w_mla_ukv[:, h * kvd + MLA_NOPE:(h + 1) * kvd] for h in range(MLA_HEADS)],
                         axis=1).astype(BF16)
    nff = D_FF // D_MODEL
    return dict(
        w_in3=w_p, w_t=w_t, w_gate=w_gate, wuq=wuq, wkt=wkt, wv=wv,
        wb=w_branch.astype(BF16), wo=w_out.astype(BF16), wxq=w_xq.astype(BF16), wxo=w_xo.astype(BF16),
        wmem=jnp.concatenate([w_mem_k, w_mem_v], axis=1).astype(BF16),
        wu3=w_up.astype(BF16),
        wd3=w_down.astype(BF16).reshape(nff, D_MODEL, D_MODEL))


def _rope_tables(pos, reps):
    half = MLA_ROPE // 2
    freq = ROPE_THETA ** (-jnp.arange(half, dtype=F32) / half)
    ang = pos.astype(F32)[:, None] * freq[None, :]
    cos, sin = jnp.cos(ang), jnp.sin(ang)
    z = jnp.zeros((pos.shape[0], LANE - MLA_ROPE), F32)
    cos_r = jnp.tile(jnp.concatenate([cos, cos, z], axis=1), (reps, 1))
    sin_r = jnp.tile(jnp.concatenate([-sin, sin, z], axis=1), (reps, 1))
    return cos_r, sin_r, cos_r[:, :MLA_ROPE].T, sin_r[:, :MLA_ROPE].T


def _tile(n, pref):
    t = min(n, pref)
    assert n % t == 0
    return t


def _layer(x, bsz, s, pos0, w, lb, b_fox, g_q, g_kv, g_hout, g_norm, mem_k, mem_v, mem_row0, past, cfg,
           layer, depth, shared):
    m = bsz * s
    g = lambda i: g_norm[i][None, :]
    tm_in = _tile(s, cfg["tm_in"])
    tm_p = _tile(s, cfg["tm_prep"])
    cos_r, sin_r, cos_c, sin_c = _rope_tables(pos0 + jnp.arange(s), 1)
    big, kt, vt, kpet, logft = in_proj(x, g(0), w["w_in3"], w["w_t"], cos_c, sin_c, b_fox[:, None],
                                       bsz, s, tm_in, layer, depth, shared[:4])
    qx, knt, v, ckv_n = mla_prep(big, cos_r, sin_r, g_q[None, :], g_kv[None, :],
                                 w["wuq"], w["wkt"], w["wv"], bsz, s, tm_p, layer, depth, shared[4:])
    row0 = layer * bsz
    rows = lambda a: a.reshape((depth * bsz,) + a.shape[2:])
    cumt = fox_cumsum(rows(logft), row0, bsz)
    if past is None:
        t = _tile(s, cfg["t_attn"])
        o_fox = fox_prompt(big, rows(kt), rows(vt), cumt, row0, bsz, s, t)
        o_mla = mla_prompt(qx, knt, rows(kpet), v, row0, bsz, s, t)
        o_hg, hg_state = hgrn(big, lb[None, :], g_hout[None, :], bsz, s, CHUNK,
                              _tile(s, cfg["hg_rows"]))
    else:
        c_kt, c_vt, c_cumt, c_knt, c_kpt, c_v, c_hg = past
        p = c_kt.shape[2]
        o_fox = fox_sample(big, rows(kt), rows(vt), c_kt, c_vt, cumt, c_cumt, bsz, s, p, layer)
        o_mla = mla_sample(qx, knt, rows(kpet), v, c_knt, c_kpt, c_v, bsz, s, p, layer)
        o_hg, hg_state = hgrn(big, lb[None, :], g_hout[None, :], bsz, s, s, s, s0=c_hg, s0_row0=row0)
    x = merge_out(o_fox, o_mla, o_hg, x, w["w_gate"], w["wb"], w["wo"], g(0), g(1),
                  _tile(m, cfg["tm_merge"]))
    x = cross_block(x, mem_k, mem_v, w["wxq"], w["wxo"], g(2), g(3), bsz, s, _tile(s, cfg["tm_cross"]),
                    mem_row0)
    x = mlp_block(x, w["wu3"], w["wd3"], g(4), g(5), _tile(m, cfg["tm_mlp"]))
    return x, (kt, vt, kpet, logft, ckv_n), hg_state


def _from_feature_major(stacked, heads):
    a = jnp.swapaxes(stacked, 2, 3)
    if heads:
        a = a.reshape(a.shape[:3] + (heads, a.shape[3] // heads))
    return a


def _assemble_states(shared, hg_states, bsz, s):
    kt, vt, kpet, logft, ckv = shared
    return (_from_feature_major(kt, FOX_HEADS), _from_feature_major(vt, FOX_HEADS),
            _from_feature_major(logft, 0), ckv.reshape(ckv.shape[0], bsz, s, MLA_KV_RANK),
            _from_feature_major(kpet, 0), jnp.stack(hg_states))


_CFG = dict(tm_in=1024, tm_prep=512, t_attn=512, hg_rows=512, tm_merge=512, tm_cross=512,
            tm_mlp=1024, tm_mem=512, tm_expand=1024)


def kernel(x_prompt, x_sample, cache_fox_k, cache_fox_v, cache_fox_logf, cache_mla_ckv, cache_mla_kpe,
           state_hgrn, cache_mem_k, cache_mem_v, mem_prompt, w_in, b_fox, g_mla_q, w_mla_uq, g_mla_kv,
           w_mla_ukv, g_hgrn_out, lb_hgrn, w_branch, w_out, w_xq, w_mem_k, w_mem_v, w_xo, w_up, w_down,
           g_norm):
    cfg = _CFG
    depth = w_in.shape[0]
    lb_p = jax.nn.softmax(lb_hgrn.astype(F32), axis=0)
    lb_all = jnp.cumsum(lb_p, axis=0) - lb_p[0]
    ws = [_prep_layer_weights(w_in[l], w_mla_uq[l], w_mla_ukv[l], w_branch[l], w_out[l], w_xq[l],
                              w_mem_k[l], w_mem_v[l], w_xo[l], w_up[l], w_down[l]) for l in range(depth)]

    def run_layer(x, bsz, s, pos0, l, mk, mv, mem_row0, past, shared):
        return _layer(x, bsz, s, pos0, ws[l], lb_all[l], b_fox[l], g_mla_q[l], g_mla_kv[l],
                      g_hgrn_out[l], g_norm[l], mk, mv, mem_row0, past, cfg, l, depth, shared)

    bp, sp, _ = x_prompt.shape
    x = x_prompt.reshape(bp * sp, D_MODEL)
    mem = mem_prompt.reshape(bp * N_MEM, D_MODEL)
    shared, hg_states, p_mem = (), [], []
    for l in range(depth):
        mk, mv = mem_kv(mem, ws[l]["wmem"], _tile(bp * N_MEM, cfg["tm_mem"]))
        x, shared, hg = run_layer(x, bp, sp, 0, l, mk, mv, 0, None, shared)
        hg_states.append(hg)
        p_mem.append((mk.reshape(bp, N_MEM, X_HEADS, X_DIM), mv.reshape(bp, N_MEM, X_HEADS, X_DIM)))
    y_prompt = x.reshape(bp, sp, D_MODEL)
    p_out = _assemble_states(shared, hg_states, bp, sp) + tuple(jnp.stack(a) for a in zip(*p_mem))

    bs, ts, _ = x_sample.shape
    p = cache_fox_k.shape[2]
    fm = lambda c: jnp.moveaxis(c, 2, -1)
    c_kt = fm(cache_fox_k).reshape(depth * bs, BRANCH_W, p)
    c_vt = fm(cache_fox_v).reshape(depth * bs, BRANCH_W, p)
    c_kpt = fm(cache_mla_kpe).reshape(depth * bs, MLA_ROPE, p)
    c_cumt = fox_cumsum(fm(cache_fox_logf).reshape(depth * bs, FOX_HEADS, p), 0, depth * bs)
    c_ckv = cache_mla_ckv.reshape(depth * bs * p, MLA_KV_RANK)
    c_hg = state_hgrn.reshape((depth * bs,) + state_hgrn.shape[2:])
    c_mk = cache_mem_k.reshape(depth * bs * N_MEM, X_HEADS * X_DIM)
    c_mv = cache_mem_v.reshape(depth * bs * N_MEM, X_HEADS * X_DIM)
    x = x_sample.reshape(bs * ts, D_MODEL)
    shared, hg_states = (), []
    for l in range(depth):
        c_knt, c_v = latent_expand(c_ckv, ws[l]["wkt"], ws[l]["wv"], bs, p, _tile(p, cfg["tm_expand"]), l)
        past = (c_kt, c_vt, c_cumt, c_knt, c_kpt, c_v, c_hg)
        x, shared, hg = run_layer(x, bs, ts, p, l, c_mk, c_mv, l * bs, past, shared)
        hg_states.append(hg)
    y_sample = x.reshape(bs, ts, D_MODEL)
    return (y_prompt, y_sample, *p_out, *_assemble_states(shared, hg_states, bs, ts))
```

```python
import functools

import numpy as np
import jax
import jax.numpy as jnp
from jax import lax
from jax.experimental import pallas as pl
from jax.experimental.pallas import tpu as pltpu

F32 = jnp.float32
BF16 = jnp.bfloat16

D_MODEL = 1024
CHUNK = 64
N_MEM = 256
EPS = 1e-6
NEG_BIG = -1e30
EXP_CLIP = 80.0
FOX_HEADS = 8
FOX_DIM = 64
MLA_HEADS = 4
MLA_Q_RANK = 384
MLA_KV_RANK = 256
MLA_NOPE = 128
MLA_ROPE = 64
MLA_V = 128
ROPE_THETA = 10000.0
HG_HEADS = 4
HG_DK = 128
HG_DV = 128
X_HEADS = 4
X_DIM = 128
D_FF = 4 * D_MODEL
BRANCH_W = 512
IN_SIZES = (512, 512, 512, FOX_HEADS, MLA_Q_RANK, MLA_KV_RANK, MLA_ROPE, 512, 512, 512, 512,
            D_MODEL, D_MODEL, D_MODEL)

LANE = 128
SUB_BLOCK = 16
VMEM_LIMIT = 56 * 1024 * 1024
LOG2E = 1.4426950408889634

OFF_HQ, OFF_HF, OFF_HI, OFF_HG, OFF_FQ, OFF_CQ, OFF_CKV, NP_IN = (
    0, 512, 1024, 1536, 2048, 2688, 3072, 3328)
IN_TN = 1664
T_FK, T_FV, T_KPE, T_KPE_SW, T_FF, NT_IN = 0, 512, 1024, 1088, 1152, 1160


def _params(sem, vmem=VMEM_LIMIT):
    return pltpu.CompilerParams(dimension_semantics=sem, vmem_limit_bytes=vmem)


def _dot(a, b):
    return jnp.dot(a, b, preferred_element_type=F32)


def _dot_nt(a, b):
    return lax.dot_general(a, b, (((1,), (1,)), ((), ())), preferred_element_type=F32)


def _dot_tn(a, b):
    return lax.dot_general(a, b, (((0,), (0,)), ((), ())), preferred_element_type=F32)


def _rms(x, g):
    y = x * lax.rsqrt(jnp.mean(x * x, axis=-1, keepdims=True) + EPS)
    return y * g


def _log_sigmoid(z):
    return jnp.minimum(z, 0.0) - jnp.log(1.0 + jnp.exp(-jnp.abs(z)))


def _sigmoid(z):
    return 1.0 / (1.0 + jnp.exp(-z))


def _resident(shape):
    nd = len(shape)
    return pl.BlockSpec(shape, lambda *_: (0,) * nd, pipeline_mode=pl.Buffered(1))


def _in_proj_kernel(x_ref, g_ref, w_ref, wt_ref, cos_ref, sin_ref, bf_ref, *rest):
    big_ref, kt_ref, vt_ref, kpe_ref, lf_ref, h_ref = rest[-6:]
    j = pl.program_id(1)

    @pl.when(j == 0)
    def _():
        h = _rms(x_ref[...], g_ref[...]).astype(BF16)
        h_ref[...] = h
        yt = _dot_nt(wt_ref[...], h)
        kt_ref[0] = yt[T_FK:T_FK + BRANCH_W]
        vt_ref[0] = yt[T_FV:T_FV + BRANCH_W]
        kpe_ref[0] = (yt[T_KPE:T_KPE + MLA_ROPE] * cos_ref[...]
                      + yt[T_KPE_SW:T_KPE_SW + MLA_ROPE] * sin_ref[...])
        lf_ref[0] = _log_sigmoid(yt[T_FF:T_FF + FOX_HEADS] + bf_ref[...])

    tn = big_ref.shape[1]
    big_ref[...] = _dot(h_ref[...], w_ref[:, pl.ds(pl.multiple_of(j * tn, LANE), tn)])


def in_proj(x, g, w3, wt, cos_t, sin_t, b_col, bsz, s, tm, layer, depth, prev):
    m, k = x.shape
    tn = IN_TN
    nj = w3.shape[1] // tn
    nt = s // tm
    ntab = cos_t.shape[1] // tm
    feats = (BRANCH_W, BRANCH_W, MLA_ROPE, FOX_HEADS)
    tspec = lambda rows: pl.BlockSpec((None, 1, rows, tm), lambda i, j: (layer, i // nt, 0, i % nt))
    n_in = 7
    return pl.pallas_call(
        _in_proj_kernel,
        out_shape=(jax.ShapeDtypeStruct((m, nj * tn), F32),)
        + tuple(jax.ShapeDtypeStruct((depth, bsz, f, s), F32) for f in feats),
        grid=(m // tm, nj),
        in_specs=[pl.BlockSpec((tm, k), lambda i, j: (i, 0)),
                  pl.BlockSpec((1, k), lambda i, j: (0, 0)),
                  _resident(w3.shape),
                  _resident(wt.shape),
                  pl.BlockSpec((MLA_ROPE, tm), lambda i, j: (0, i % ntab)),
                  pl.BlockSpec((MLA_ROPE, tm), lambda i, j: (0, i % ntab)),
                  pl.BlockSpec((FOX_HEADS, 1), lambda i, j: (0, 0))]
        + [pl.BlockSpec(memory_space=pl.ANY)] * len(prev),
        out_specs=(pl.BlockSpec((tm, tn), lambda i, j: (i, j)),) + tuple(tspec(f) for f in feats),
        scratch_shapes=[pltpu.VMEM((tm, k), BF16)],
        input_output_aliases={n_in + i: 1 + i for i in range(len(prev))},
        compiler_params=_params(("parallel", "arbitrary")),
        name="in_proj",
    )(x, g, w3, wt, cos_t, sin_t, b_col, *prev)


def _mla_prep_kernel(cq_ref, ckv_ref, cs_ref, sn_ref, gq_ref, gkv_ref, wuq_ref, wkt_ref, wv_ref, *rest):
    qx_ref, knt_ref, v_ref, ckvn_ref = rest[-4:]
    cos_t = cs_ref[...]
    sin_t = sn_ref[...]
    qn = _rms(cq_ref[...], gq_ref[...]).astype(BF16)
    qall = _dot(qn, wuq_ref[...]) * ((MLA_NOPE + MLA_ROPE) ** -0.5 * LOG2E)
    for h in range(MLA_HEADS):
        lo = h * LANE
        qr = (qall[:, 512 + lo:512 + lo + LANE] * cos_t
              + qall[:, 1024 + lo:1024 + lo + LANE] * sin_t)
        qx_ref[:, 2 * lo:2 * lo + LANE] = qall[:, lo:lo + LANE].astype(BF16)
        qx_ref[:, 2 * lo + LANE:2 * lo + 2 * LANE] = qr.astype(BF16)
    ckvn = _rms(ckv_ref[...], gkv_ref[...])
    ckvn_ref[...] = ckvn
    cb = ckvn.astype(BF16)
    knt_ref[0] = _dot_nt(wkt_ref[...], cb).astype(BF16)
    v_ref[...] = _dot(cb, wv_ref[...]).astype(BF16)


def mla_prep(big, cos_t, sin_t, gq, gkv, wuq, wkt, wv, bsz, s, tm, layer, depth, prev):
    m = big.shape[0]
    nt = s // tm
    ntab = cos_t.shape[0] // tm
    row = lambda w: pl.BlockSpec((tm, w), lambda i: (i, 0))
    full = lambda a: pl.BlockSpec(a.shape, lambda i: (0,) * a.ndim)
    n_in = 9
    return pl.pallas_call(
        _mla_prep_kernel,
        out_shape=(jax.ShapeDtypeStruct((m, 1024), BF16),
                   jax.ShapeDtypeStruct((bsz, 512, s), BF16),
                   jax.ShapeDtypeStruct((m, 512), BF16),
                   jax.ShapeDtypeStruct((depth, m, MLA_KV_RANK), F32)),
        grid=(m // tm,),
        in_specs=[pl.BlockSpec((tm, MLA_Q_RANK), lambda i: (i, OFF_CQ // MLA_Q_RANK)),
                  pl.BlockSpec((tm, MLA_KV_RANK), lambda i: (i, OFF_CKV // MLA_KV_RANK)),
                  pl.BlockSpec((tm, LANE), lambda i: (i % ntab, 0)),
                  pl.BlockSpec((tm, LANE), lambda i: (i % ntab, 0)),
                  full(gq), full(gkv), full(wuq), full(wkt), full(wv)]
        + [pl.BlockSpec(memory_space=pl.ANY)] * len(prev),
        out_specs=(row(1024), pl.BlockSpec((1, 512, tm), lambda i: (i // nt, 0, i % nt)),
                   row(512), pl.BlockSpec((None, tm, MLA_KV_RANK), lambda i: (layer, i, 0))),
        input_output_aliases={n_in + i: 3 + i for i in range(len(prev))},
        compiler_params=_params(("parallel",)),
        name="mla_prep",
    )(big, big, cos_t, sin_t, gq, gkv, wuq, wkt, wv, *prev)


def _cumsum_kernel(x_ref, c_ref, *, w):
    s = x_ref.shape[1]
    r = lax.broadcasted_iota(jnp.int32, (w, w), 0)
    c = lax.broadcasted_iota(jnp.int32, (w, w), 1)
    upper = (r <= c).astype(F32)
    local = [jnp.dot(x_ref[:, g * w:(g + 1) * w], upper, preferred_element_type=F32,
                     precision=lax.Precision.HIGHEST) for g in range(s // w)]
    carry = jnp.zeros((x_ref.shape[0], 1), F32)
    for g, cum in enumerate(local):
        cum = cum + carry
        c_ref[:, g * w:(g + 1) * w] = cum
        carry = cum[:, w - 1:w]


def fox_cumsum(x, row0, bsz):
    n, h, s = x.shape
    rows = bsz * h
    assert (row0 * h) % rows == 0
    out = pl.pallas_call(
        functools.partial(_cumsum_kernel, w=min(LANE, s)),
        out_shape=jax.ShapeDtypeStruct((rows, s), F32),
        grid=(1,),
        in_specs=[pl.BlockSpec((rows, s), lambda i: (row0 * h // rows, 0))],
        out_specs=pl.BlockSpec((rows, s), lambda i: (0, 0)),
        compiler_params=_params(("arbitrary",)),
        name="fox_cumsum",
    )(x.reshape(n * h, s))
    return out.reshape(bsz, h, s)


def _pair_rows_mask(hh):
    sub = lax.broadcasted_iota(jnp.int32, (LANE, 1), 0)
    return (sub < FOX_DIM) if hh == 0 else (sub >= FOX_DIM)


def _fox_finish(acc0, acc1):
    lane = lax.broadcasted_iota(jnp.int32, (1, LANE), 1)
    o0 = acc0 / pltpu.roll(acc0, FOX_DIM, axis=1)
    o1 = acc1 / pltpu.roll(acc1, FOX_DIM, axis=1)
    return jnp.where(lane < FOX_DIM, o0, o1)


def _causal_schedule(nq, t):
    todo = {i: [(i, j, 0, t, 0) for j in range(i + 1)] for i in range(nq)}
    order = []
    while any(todo.values()):
        for i in reversed(range(nq)):
            if todo[i]:
                order.append(todo[i].pop(0))
    return order


def _rows_from(r0, old, new):
    return new if r0 == 0 else jnp.concatenate([old[:r0], new], axis=0)


def _fox_prompt_kernel(q_ref, kt_ref, vt_ref, ct_ref, o_ref, *, t, nq):
    hp = pl.program_id(1)
    lane = lax.broadcasted_iota(jnp.int32, (1, LANE), 1)
    lo = lane < FOX_DIM
    span = lambda i: slice(i * t, (i + 1) * t)
    ct = [ct_ref[0, 2 * hp + hh] for hh in range(2)]
    qs, cref = [], []
    for i in range(nq):
        q = q_ref[span(i), :] * (FOX_DIM ** -0.5 * LOG2E)
        qs.append((jnp.where(lo, q, 0.0).astype(BF16), jnp.where(lo, 0.0, q).astype(BF16)))
        cref.append([c[:, i * t:i * t + 1] for c in ct])
    kts = [kt_ref[0, :, span(j)].astype(BF16) for j in range(nq)]
    vts = [[jnp.where(_pair_rows_mask(hh), vt_ref[0, :, span(j)], 1.0).astype(BF16) for hh in range(2)]
           for j in range(nq)]
    def scores(i, j, k0, w, r0):
        out = []
        for hh in range(2):
            s = (_dot(qs[i][hh][r0:], kts[j][:, k0:k0 + w])
                 + (cref[i][hh] - ct[hh][:, j * t + k0:j * t + k0 + w]) * LOG2E)
            if i == j:
                r = lax.broadcasted_iota(jnp.int32, s.shape, 0)
                c = lax.broadcasted_iota(jnp.int32, s.shape, 1)
                s = jnp.where(r >= c, s, NEG_BIG)
            out.append(s)
        return out

    state = [[(jnp.full((t, 1), NEG_BIG, F32), jnp.zeros((t, LANE), F32)) for _ in range(2)]
             for _ in range(nq)]
    order = _causal_schedule(nq, t)
    ss = scores(*order[0])
    for n, (i, j, k0, w, r0) in enumerate(order):
        nxt = scores(*order[n + 1]) if n + 1 < len(order) else None
        m_old = [state[i][hh][0][r0:] for hh in range(2)]
        m_new = [jnp.maximum(m_old[hh], jnp.max(ss[hh], axis=1, keepdims=True)) for hh in range(2)]
        ps = [jnp.exp2(ss[hh] - m_new[hh]).astype(BF16) for hh in range(2)]
        pvs = [_dot_nt(ps[hh], vts[j][hh][:, k0:k0 + w]) for hh in range(2)]
        state[i] = [(_rows_from(r0, state[i][hh][0], m_new[hh]),
                     _rows_from(r0, state[i][hh][1],
                                jnp.exp2(m_old[hh] - m_new[hh]) * state[i][hh][1][r0:] + pvs[hh]))
                    for hh in range(2)]
        ss = nxt
    for i in range(nq):
        o_ref[span(i), :] = _fox_finish(state[i][0][1], state[i][1][1]).astype(BF16)


def fox_prompt(big, kt, vt, cumt, row0, bsz, s, t):
    m = big.shape[0]
    return pl.pallas_call(
        functools.partial(_fox_prompt_kernel, t=t, nq=s // t),
        out_shape=jax.ShapeDtypeStruct((m, BRANCH_W), BF16),
        grid=(bsz, FOX_HEADS // 2),
        in_specs=[pl.BlockSpec((s, LANE), lambda b, h: (b, OFF_FQ // LANE + h)),
                  pl.BlockSpec((1, LANE, s), lambda b, h: (row0 + b, h, 0)),
                  pl.BlockSpec((1, LANE, s), lambda b, h: (row0 + b, h, 0)),
                  pl.BlockSpec((1, FOX_HEADS, 1, s), lambda b, h: (b, 0, 0, 0))],
        out_specs=pl.BlockSpec((s, LANE), lambda b, h: (b, h)),
        compiler_params=_params(("parallel", "parallel")),
        name="fox_prompt",
    )(big, kt, vt, cumt.reshape(bsz, FOX_HEADS, 1, s))


def _mla_keys(knt, kpt):
    n = knt.shape[1]
    return jnp.concatenate([knt, kpt.astype(BF16), jnp.zeros((LANE - MLA_ROPE, n), BF16)], axis=0)


def _mla_prompt_kernel(q_ref, knt_ref, kpt_ref, v_ref, o_ref, *, t, nq):
    hs = range(2)
    span = lambda i: slice(i * t, (i + 1) * t)
    cols = lambda hh, w: slice(hh * w, (hh + 1) * w)
    keys = [[_mla_keys(knt_ref[0, cols(hh, LANE), span(j)], kpt_ref[0, :, span(j)]) for hh in hs]
            for j in range(nq)]
    def scores(i, j, k0, w, r0):
        ss = [_dot(q_ref[i * t + r0:(i + 1) * t, cols(hh, 2 * LANE)], keys[j][hh][:, k0:k0 + w]) for hh in hs]
        if i != j:
            return ss
        rc = lax.broadcasted_iota(jnp.int32, ss[0].shape, 0) // CHUNK
        cc = lax.broadcasted_iota(jnp.int32, ss[0].shape, 1) // CHUNK
        return [jnp.where(rc >= cc, s, NEG_BIG) for s in ss]

    state = [[(jnp.full((t, 1), NEG_BIG, F32), jnp.zeros((t, 1), F32), jnp.zeros((t, LANE), F32))
              for _ in hs] for _ in range(nq)]
    order = _causal_schedule(nq, t)
    ss = scores(*order[0])
    for n, (i, j, k0, w, r0) in enumerate(order):
        nxt = scores(*order[n + 1]) if n + 1 < len(order) else None
        old = [[a[r0:] for a in state[i][hh]] for hh in hs]
        m_new = [jnp.maximum(old[hh][0], jnp.max(ss[hh], axis=1, keepdims=True)) for hh in hs]
        ps = [jnp.exp2(ss[hh] - m_new[hh]) for hh in hs]
        pvs = [_dot(ps[hh].astype(BF16), v_ref[j * t + k0:j * t + k0 + w, cols(hh, LANE)]) for hh in hs]
        new = []
        for hh in hs:
            alpha = jnp.exp2(old[hh][0] - m_new[hh])
            upd = (m_new[hh], alpha * old[hh][1] + jnp.sum(ps[hh], axis=1, keepdims=True),
                   alpha * old[hh][2] + pvs[hh])
            new.append(tuple(_rows_from(r0, a, b) for a, b in zip(state[i][hh], upd)))
        state[i] = new
        ss = nxt
    for i in range(nq):
        o_ref[span(i), :] = jnp.concatenate([state[i][hh][2] / state[i][hh][1] for hh in hs],
                                            axis=1).astype(BF16)


def mla_prompt(qx, knt, kpt, v, row0, bsz, s, t):
    assert (t // 2) % CHUNK == 0 and (t // 2) % LANE == 0
    m = qx.shape[0]
    return pl.pallas_call(
        functools.partial(_mla_prompt_kernel, t=t, nq=s // t),
        out_shape=jax.ShapeDtypeStruct((m, BRANCH_W), BF16),
        grid=(bsz, MLA_HEADS // 2),
        in_specs=[pl.BlockSpec((s, 4 * LANE), lambda b, h: (b, h)),
                  pl.BlockSpec((1, 2 * LANE, s), lambda b, h: (b, h, 0)),
                  pl.BlockSpec((1, MLA_ROPE, s), lambda b, h: (row0 + b, 0, 0)),
                  pl.BlockSpec((s, 2 * LANE), lambda b, h: (b, h))],
        out_specs=pl.BlockSpec((s, 2 * LANE), lambda b, h: (b, h)),
        compiler_params=_params(("parallel", "parallel")),
        name="mla_prompt",
    )(qx, knt, kpt, v)


def _fox_sample_kernel(q_ref, ktn_ref, vtn_ref, ktc_ref, vtc_ref, ctn_ref, ctc_ref, o_ref, *, t, p):
    hp = pl.program_id(1)
    lane = lax.broadcasted_iota(jnp.int32, (1, LANE), 1)
    lo = lane < FOX_DIM
    q = q_ref[...] * (FOX_DIM ** -0.5 * LOG2E)
    kt_c = ktc_ref[0].astype(BF16)
    kt_n = ktn_ref[0].astype(BF16)
    vt_c = vtc_ref[0]
    vt_n = vtn_ref[0]
    causal = (lax.broadcasted_iota(jnp.int32, (t, t), 0) >= lax.broadcasted_iota(jnp.int32, (t, t), 1))
    accs = []
    for hh in range(2):
        head = 2 * hp + hh
        qh = (jnp.where(lo, q, 0.0) if hh == 0 else jnp.where(lo, 0.0, q)).astype(BF16)
        cc = ctc_ref[0, pl.ds(head, 1), :]
        ctot = cc[:, p - 1:p]
        s_c = _dot(qh, kt_c) + (ctot - cc) * LOG2E
        s_n = _dot(qh, kt_n) - ctn_ref[0, pl.ds(head, 1), :] * LOG2E
        s_n = jnp.where(causal, s_n, NEG_BIG)
        m = jnp.maximum(jnp.max(s_c, axis=1, keepdims=True), jnp.max(s_n, axis=1, keepdims=True))
        rows = _pair_rows_mask(hh)
        accs.append(_dot_nt(jnp.exp2(s_c - m).astype(BF16), jnp.where(rows, vt_c, 1.0).astype(BF16))
                    + _dot_nt(jnp.exp2(s_n - m).astype(BF16), jnp.where(rows, vt_n, 1.0).astype(BF16)))
    o_ref[...] = _fox_finish(accs[0], accs[1]).astype(BF16)


def fox_sample(big, kt_n, vt_n, kt_c, vt_c, cumt_n, cumt_c, bsz, t, p, layer):
    cidx = lambda b, h: (layer * bsz + b, h, 0)
    return pl.pallas_call(
        functools.partial(_fox_sample_kernel, t=t, p=p),
        out_shape=jax.ShapeDtypeStruct((bsz * t, BRANCH_W), BF16),
        grid=(bsz, FOX_HEADS // 2),
        in_specs=[pl.BlockSpec((t, LANE), lambda b, h: (b, OFF_FQ // LANE + h)),
                  pl.BlockSpec((1, LANE, t), cidx),
                  pl.BlockSpec((1, LANE, t), cidx),
                  pl.BlockSpec((1, LANE, p), cidx),
                  pl.BlockSpec((1, LANE, p), cidx),
                  pl.BlockSpec((1, FOX_HEADS, t), lambda b, h: (b, 0, 0)),
                  pl.BlockSpec((1, FOX_HEADS, p), lambda b, h: (layer * bsz + b, 0, 0))],
        out_specs=pl.BlockSpec((t, LANE), lambda b, h: (b, h)),
        compiler_params=_params(("parallel", "parallel")),
        name="fox_sample",
    )(big, kt_n, vt_n, kt_c, vt_c, cumt_n, cumt_c)


def _mla_sample_kernel(q_ref, kntn_ref, kptn_ref, vn_ref, kntc_ref, kptc_ref, vc_ref, o_ref, *, t, p):
    q = q_ref[...]
    s_c = _dot(q, _mla_keys(kntc_ref[0], kptc_ref[0]))
    s_n = _dot(q, _mla_keys(kntn_ref[0], kptn_ref[0]))
    qc = (p + lax.broadcasted_iota(jnp.int32, (t, t), 0)) // CHUNK
    kc = (p + lax.broadcasted_iota(jnp.int32, (t, t), 1)) // CHUNK
    s_n = jnp.where(qc >= kc, s_n, NEG_BIG)
    m = jnp.maximum(jnp.max(s_c, axis=1, keepdims=True), jnp.max(s_n, axis=1, keepdims=True))
    p_c = jnp.exp2(s_c - m)
    p_n = jnp.exp2(s_n - m)
    l = jnp.sum(p_c, axis=1, keepdims=True) + jnp.sum(p_n, axis=1, keepdims=True)
    o = _dot(p_c.astype(BF16), vc_ref[...]) + _dot(p_n.astype(BF16), vn_ref[...])
    o_ref[...] = (o / l).astype(BF16)


def mla_sample(qx, knt_n, kpt_n, v_n, knt_c, kpt_c, v_c, bsz, t, p, layer):
    assert (p - 1) // CHUNK <= p // CHUNK
    return pl.pallas_call(
        functools.partial(_mla_sample_kernel, t=t, p=p),
        out_shape=jax.ShapeDtypeStruct((bsz * t, BRANCH_W), BF16),
        grid=(bsz, MLA_HEADS),
        in_specs=[pl.BlockSpec((t, 2 * LANE), lambda b, h: (b, h)),
                  pl.BlockSpec((1, LANE, t), lambda b, h: (b, h, 0)),
                  pl.BlockSpec((1, MLA_ROPE, t), lambda b, h: (layer * bsz + b, 0, 0)),
                  pl.BlockSpec((t, LANE), lambda b, h: (b, h)),
                  pl.BlockSpec((1, LANE, p), lambda b, h: (b, h, 0)),
                  pl.BlockSpec((1, MLA_ROPE, p), lambda b, h: (layer * bsz + b, 0, 0)),
                  pl.BlockSpec((p, LANE), lambda b, h: (b, h))],
        out_specs=pl.BlockSpec((t, LANE), lambda b, h: (b, h)),
        compiler_params=_params(("parallel", "parallel")),
        name="mla_sample",
    )(qx, knt_n, kpt_n, v_n, knt_c, kpt_c, v_c)


def _latent_expand_kernel(c_ref, wkt_ref, wv_ref, knt_ref, v_ref):
    cb = c_ref[...].astype(BF16)
    knt_ref[0] = _dot_nt(wkt_ref[...], cb).astype(BF16)
    v_ref[...] = _dot(cb, wv_ref[...]).astype(BF16)


def latent_expand(ckv, wkt, wv, bsz, p, tm, layer):
    m = bsz * p
    nt = p // tm
    return pl.pallas_call(
        _latent_expand_kernel,
        out_shape=(jax.ShapeDtypeStruct((bsz, 512, p), BF16), jax.ShapeDtypeStruct((m, 512), BF16)),
        grid=(m // tm,),
        in_specs=[pl.BlockSpec((tm, MLA_KV_RANK), lambda i: (layer * (m // tm) + i, 0)),
                  pl.BlockSpec(wkt.shape, lambda i: (0, 0)), pl.BlockSpec(wv.shape, lambda i: (0, 0))],
        out_specs=(pl.BlockSpec((1, 512, tm), lambda i: (i // nt, 0, i % nt)),
                   pl.BlockSpec((tm, 512), lambda i: (i, 0))),
        compiler_params=_params(("parallel",)),
        name="latent_expand",
    )(ckv, wkt, wv)


def _hgrn_gates(z, lb, tri):
    logf = _log_sigmoid(z) + jnp.log(1.0 + lb * jnp.exp(jnp.minimum(-z, EXP_CLIP)))
    k = (1.0 - lb) * (1.0 / (1.0 + jnp.exp(z)))
    h1 = logf.astype(BF16)
    r1 = logf - h1.astype(F32)
    h2 = r1.astype(BF16)
    h3 = (r1 - h2.astype(F32)).astype(BF16)
    hcat = jnp.concatenate([h1, h2, h3], axis=1)
    g = tri.shape[0]
    parts = jnp.concatenate([_dot(tri, hcat[r:r + g, :]) for r in range(0, z.shape[0], g)], axis=0)
    lc = ((parts[:, :LANE] + parts[:, LANE:2 * LANE]) + parts[:, 2 * LANE:]) * LOG2E
    return k, lc


def _hgrn_local(q, z, lb, tri, v_b, ln, sel, k_ref, lc_ref):
    k, lc = _hgrn_gates(z, lb, tri)
    k_ref[...] = k
    lc_ref[...] = lc
    nchunk = q.shape[0] // ln
    nsb = q.shape[0] // SUB_BLOCK
    per = ln // SUB_BLOCK
    half = SUB_BLOCK // 2
    rows = lambda a, i: a[i * SUB_BLOCK:(i + 1) * SUB_BLOCK, :]
    lcb = [jnp.zeros((1, LANE), F32) if i % per == 0 else lc_ref[i * SUB_BLOCK - 1:i * SUB_BLOCK, :]
           for i in range(nsb)]
    lcb_rows = jnp.concatenate([jnp.broadcast_to(b, (SUB_BLOCK, LANE)) for b in lcb], axis=0)
    last = [lc_ref[(c + 1) * ln - 1:(c + 1) * ln, :] for c in range(nchunk)]
    last_rows = jnp.concatenate([jnp.broadcast_to(b, (ln, LANE)) for b in last], axis=0)
    qh = (q * jnp.exp2(lc - lcb_rows)).astype(BF16)
    qe = (q * jnp.exp2(lc)).astype(BF16)
    kdec = (k * jnp.exp2(last_rows - lc)).astype(BF16)
    a_off = {}
    for i in range(nsb):
        n = (i % per) * SUB_BLOCK
        if n:
            c0 = i * SUB_BLOCK - n
            kt = (k[c0:c0 + n, :] * jnp.exp2(lcb[i] - lc[c0:c0 + n, :])).astype(BF16)
            a_off[i] = _dot_nt(rows(qh, i), kt)
    yield None
    pieces = []
    for i in range(nsb):
        q_i, lc_i = rows(q, i), rows(lc, i)
        cols = []
        for s in range(SUB_BLOCK):
            lo = 0 if s < half else half
            r = i * SUB_BLOCK + s
            d = lc_i[lo:, :] - lc_ref[r:r + 1, :]
            d = (jnp.concatenate([jnp.minimum(d[:half, :], 0.0), d[half:, :]], axis=0) if s < half
                 else jnp.minimum(d, 0.0))
            w = (q_i[lo:, :] * k_ref[r:r + 1, :]) * jnp.exp2(d)
            if lo:
                w = jnp.concatenate([jnp.zeros((lo, LANE), F32), w], axis=0)
            cols.append(w.astype(BF16))
        pieces.append(jnp.concatenate(cols, axis=1))
    a_all = _dot(jnp.concatenate(pieces, axis=0), sel)
    inc = [_dot_tn(v_b[c * ln:(c + 1) * ln, :], kdec[c * ln:(c + 1) * ln, :]) for c in range(nchunk)]
    yield None
    pair_ok = (lax.broadcasted_iota(jnp.int32, (SUB_BLOCK, LANE), 0)
               >= lax.broadcasted_iota(jnp.int32, (SUB_BLOCK, LANE), 1))
    off = []
    for i in range(nsb):
        n = (i % per) * SUB_BLOCK
        off.append(_dot(a_off[i].astype(BF16), v_b[i * SUB_BLOCK - n:i * SUB_BLOCK, :]) if n
                   else jnp.zeros((SUB_BLOCK, LANE), F32))
    diag = [_dot(jnp.where(pair_ok, rows(a_all, i), 0.0)[:, :SUB_BLOCK].astype(BF16), rows(v_b, i))
            for i in range(nsb)]
    local = jnp.concatenate(off, axis=0) + jnp.concatenate(diag, axis=0)
    dec = [jnp.exp2(b) for b in last]
    yield local, qe, inc, dec


def _hgrn_kernel(*refs, ln, nchunk, has_init):
    refs = list(refs)
    hq_ref, hf_ref, hi_ref, hg_ref, lb_ref, go_ref, sel_ref = refs[:7]
    s0_ref = refs[7] if has_init else None
    o_ref, sout_ref, st_ref, k_ref, lc_ref = refs[-5:]
    step = pl.program_id(1)
    nrows = ln * nchunk

    @pl.when(step == 0)
    def _():
        for h in range(HG_HEADS):
            st_ref[h] = s0_ref[0, h].T if has_init else jnp.zeros((HG_DV, HG_DK), F32)

    ng = min(nrows, 2 * LANE)
    assert ng % ln == 0 and nrows % ng == 0
    ri = lax.broadcasted_iota(jnp.int32, (ng, ng), 0)
    ci = lax.broadcasted_iota(jnp.int32, (ng, ng), 1)
    tri = ((ri >= ci) & (ri // ln == ci // ln)).astype(BF16)

    def finish(h, local, qe, inc, dec):
        cs = slice(h * LANE, (h + 1) * LANE)
        st = st_ref[h]
        parts = []
        for c in range(nchunk):
            parts.append(_dot_nt(qe[c * ln:(c + 1) * ln, :], st.astype(BF16)))
            st = st * dec[c] + inc[c]
        st_ref[h] = st
        o = local + jnp.concatenate(parts, axis=0)
        o_ref[:, cs] = (_rms(o, go_ref[...]) * _sigmoid(hg_ref[:, cs])).astype(BF16)

    def start(h):
        cs = slice(h * LANE, (h + 1) * LANE)
        gen = _hgrn_local(hq_ref[:, cs], hf_ref[:, cs], lb_ref[:, cs], tri, hi_ref[:, cs].astype(BF16),
                          ln, sel_ref[...], k_ref.at[h], lc_ref.at[h])
        next(gen)
        return gen

    gens = {0: start(0)}
    for h in range(HG_HEADS):
        if h + 1 < HG_HEADS:
            gens[h + 1] = start(h + 1)
        next(gens[h])
        if h > 0:
            finish(h - 1, *next(gens.pop(h - 1)))
    finish(HG_HEADS - 1, *next(gens.pop(HG_HEADS - 1)))

    @pl.when(step == pl.num_programs(1) - 1)
    def _():
        for h in range(HG_HEADS):
            sout_ref[0, h] = st_ref[h].T


def hgrn(big, lb, g_out, bsz, s, ln, rows, s0=None, s0_row0=0):
    m = big.shape[0]
    ns = s // rows
    has_init = s0 is not None
    sel = (np.arange(SUB_BLOCK * LANE)[:, None] // LANE == np.arange(LANE)[None, :])
    sel = jnp.asarray(sel, BF16)
    blk = lambda off: pl.BlockSpec((rows, BRANCH_W), lambda b, i: (b * ns + i, off // BRANCH_W))
    ins = [big, big, big, big, lb, g_out, sel]
    specs = [blk(OFF_HQ), blk(OFF_HF), blk(OFF_HI), blk(OFF_HG),
             pl.BlockSpec((1, BRANCH_W), lambda b, i: (0, 0)),
             pl.BlockSpec((1, HG_DV), lambda b, i: (0, 0)),
             pl.BlockSpec(sel.shape, lambda b, i: (0, 0))]
    if has_init:
        ins.append(s0)
        specs.append(pl.BlockSpec((1, HG_HEADS, HG_DK, HG_DV), lambda b, i: (s0_row0 + b, 0, 0, 0)))
    return pl.pallas_call(
        functools.partial(_hgrn_kernel, ln=ln, nchunk=rows // ln, has_init=has_init),
        out_shape=(jax.ShapeDtypeStruct((m, BRANCH_W), BF16),
                   jax.ShapeDtypeStruct((bsz, HG_HEADS, HG_DK, HG_DV), F32)),
        grid=(bsz, ns),
        in_specs=specs,
        out_specs=(pl.BlockSpec((rows, BRANCH_W), lambda b, i: (b * ns + i, 0)),
                   pl.BlockSpec((1, HG_HEADS, HG_DK, HG_DV), lambda b, i: (b, 0, 0, 0))),
        scratch_shapes=[pltpu.VMEM((HG_HEADS, HG_DV, HG_DK), F32),
                        pltpu.VMEM((HG_HEADS, rows, LANE), F32),
                        pltpu.VMEM((HG_HEADS, rows, LANE), F32)],
        compiler_params=_params(("parallel", "arbitrary")),
        name="hgrn",
    )(*ins)


def _merge_kernel(of_ref, om_ref, oh_ref, x_ref, wg_ref, wb_ref, wo_ref, g0_ref, g1_ref, o_ref):
    x = x_ref[...]
    h = _rms(x, g0_ref[...]).astype(BF16)
    branches = (of_ref, om_ref, oh_ref)
    gates = [_dot(h, wg_ref[i]) for i in range(3)]
    outs = [_dot(branches[i][...], wb_ref[i]) for i in range(3)]
    merged = (_sigmoid(gates[0]) * outs[0] + _sigmoid(gates[1]) * outs[1]) + _sigmoid(gates[2]) * outs[2]
    y = _dot(merged.astype(BF16), wo_ref[...])
    o_ref[...] = x + _rms(y, g1_ref[...])


def merge_out(o_fox, o_mla, o_hg, x, wg, wb, wo, g0, g1, tm):
    m = x.shape[0]
    row = lambda w: pl.BlockSpec((tm, w), lambda i: (i, 0))
    vec = pl.BlockSpec((1, D_MODEL), lambda i: (0, 0))
    return pl.pallas_call(
        _merge_kernel,
        out_shape=jax.ShapeDtypeStruct((m, D_MODEL), F32),
        grid=(m // tm,),
        in_specs=[row(BRANCH_W), row(BRANCH_W), row(BRANCH_W), row(D_MODEL),
                  _resident(wg.shape), _resident(wb.shape), _resident(wo.shape), vec, vec],
        out_specs=row(D_MODEL),
        compiler_params=_params(("parallel",)),
        name="merge_out",
    )(o_fox, o_mla, o_hg, x, wg, wb, wo, g0, g1)


def _matmul2_kernel(x_ref, w_ref, a_ref, b_ref):
    y = _dot(x_ref[...].astype(BF16), w_ref[...])
    n = a_ref.shape[1]
    a_ref[...] = y[:, :n]
    b_ref[...] = y[:, n:]


def mem_kv(mem, w, tm):
    m, k = mem.shape
    n = w.shape[1] // 2
    row = lambda w_: pl.BlockSpec((tm, w_), lambda i: (i, 0))
    return pl.pallas_call(
        _matmul2_kernel,
        out_shape=(jax.ShapeDtypeStruct((m, n), F32), jax.ShapeDtypeStruct((m, n), F32)),
        grid=(m // tm,),
        in_specs=[row(k), _resident(w.shape)],
        out_specs=(row(n), row(n)),
        compiler_params=_params(("parallel",)),
        name="mem_kv",
    )(mem, w)


def _cross_kernel(x_ref, mk_ref, mv_ref, wq_ref, wo_ref, g2_ref, g3_ref, o_ref):
    x = x_ref[...]
    h = _rms(x, g2_ref[...]).astype(BF16)
    q = _dot(h, wq_ref[...])
    qb = (q * (X_DIM ** -0.5 * LOG2E)).astype(BF16)
    cols = [slice(hd * X_DIM, (hd + 1) * X_DIM) for hd in range(X_HEADS)]
    ss = [_dot_nt(qb[:, cs], mk_ref[:, cs].astype(BF16)) for cs in cols]
    ps = [jnp.exp2(s - jnp.max(s, axis=1, keepdims=True)) for s in ss]
    pvs = [_dot(p.astype(BF16), mv_ref[:, cs].astype(BF16)) for p, cs in zip(ps, cols)]
    outs = [pv / jnp.sum(p, axis=1, keepdims=True) for pv, p in zip(pvs, ps)]
    ox = jnp.concatenate(outs, axis=1).astype(BF16)
    o_ref[...] = x + _rms(_dot(ox, wo_ref[...]), g3_ref[...])


def cross_block(x, mk, mv, wq, wo, g2, g3, bsz, s, tm, mem_row0):
    m = x.shape[0]
    nt = s // tm
    vec = pl.BlockSpec((1, D_MODEL), lambda b, i: (0, 0))
    return pl.pallas_call(
        _cross_kernel,
        out_shape=jax.ShapeDtypeStruct((m, D_MODEL), F32),
        grid=(bsz, nt),
        in_specs=[pl.BlockSpec((tm, D_MODEL), lambda b, i: (b * nt + i, 0)),
                  pl.BlockSpec((N_MEM, X_HEADS * X_DIM), lambda b, i: (mem_row0 + b, 0)),
                  pl.BlockSpec((N_MEM, X_HEADS * X_DIM), lambda b, i: (mem_row0 + b, 0)),
                  _resident(wq.shape), _resident(wo.shape), vec, vec],
        out_specs=pl.BlockSpec((tm, D_MODEL), lambda b, i: (b * nt + i, 0)),
        compiler_params=_params(("parallel", "parallel")),
        name="cross_attn",
    )(x, mk, mv, wq, wo, g2, g3)


def _mlp_kernel(x_ref, wu_ref, wd_ref, g4_ref, g5_ref, o_ref, h_ref, acc_ref):
    j = pl.program_id(1)

    @pl.when(j == 0)
    def _():
        h_ref[...] = _rms(x_ref[...], g4_ref[...]).astype(BF16)
        acc_ref[...] = jnp.zeros_like(acc_ref)

    cols = pl.ds(pl.multiple_of(j * D_MODEL, D_MODEL), D_MODEL)
    u = jnp.square(jnp.maximum(_dot(h_ref[...], wu_ref[:, cols]), 0.0)).astype(BF16)
    acc_ref[...] += _dot(u, wd_ref[j])

    @pl.when(j == pl.num_programs(1) - 1)
    def _():
        o_ref[...] = x_ref[...] + _rms(acc_ref[...], g5_ref[...])


def mlp_block(x, wu3, wd3, g4, g5, tm):
    m = x.shape[0]
    nj = wd3.shape[0]
    vec = pl.BlockSpec((1, D_MODEL), lambda i, j: (0, 0))
    return pl.pallas_call(
        _mlp_kernel,
        out_shape=jax.ShapeDtypeStruct((m, D_MODEL), F32),
        grid=(m // tm, nj),
        in_specs=[pl.BlockSpec((tm, D_MODEL), lambda i, j: (i, 0)),
                  _resident(wu3.shape), _resident(wd3.shape), vec, vec],
        out_specs=pl.BlockSpec((tm, D_MODEL), lambda i, j: (i, 0)),
        scratch_shapes=[pltpu.VMEM((tm, D_MODEL), BF16), pltpu.VMEM((tm, D_MODEL), F32)],
        compiler_params=_params(("parallel", "arbitrary")),
        name="mlp",
    )(x, wu3, wd3, g4, g5)


def _prep_layer_weights(w_in, w_mla_uq, w_mla_ukv, w_branch, w_out, w_xq, w_mem_k, w_mem_v, w_xo,
                        w_up, w_down):
    idx = np.cumsum((0,) + IN_SIZES)
    seg = lambda i: w_in[:, idx[i]:idx[i + 1]]
    fq, fk, fv, ff, cq, ckv, kpe, hq, hf, hi, hg, ga, gb, gc = (seg(i) for i in range(14))
    half = MLA_ROPE // 2
    kpe_sw = jnp.concatenate([kpe[:, half:], kpe[:, :half]], axis=1)
    pad = jnp.zeros((D_MODEL, OFF_CQ - OFF_FQ - BRANCH_W), w_in.dtype)
    w_p = jnp.concatenate([hq, hf, hi, hg, fq, pad, cq, ckv], axis=1).astype(BF16)
    w_gate = jnp.stack([ga, gb, gc]).astype(BF16)
    w_t = jnp.concatenate([fk, fv, kpe, kpe_sw, ff], axis=1).T.astype(BF16)
    hd = MLA_NOPE + MLA_ROPE
    zq = jnp.zeros((MLA_Q_RANK, LANE - MLA_ROPE), w_mla_uq.dtype)
    nope, rope_n, rope_s = [], [], []
    for h in range(MLA_HEADS):
        base = h * hd
        nope.append(w_mla_uq[:, base:base + MLA_NOPE])
        x1 = w_mla_uq[:, base + MLA_NOPE:base + MLA_NOPE + half]
        x2 = w_mla_uq[:, base + MLA_NOPE + half:base + hd]
        rope_n += [x1, x2, zq]
        rope_s += [x2, x1, zq]
    wuq = jnp.concatenate(nope + rope_n + rope_s, axis=1).astype(BF16)
    kvd = MLA_NOPE + MLA_V
    wkt = jnp.concatenate([w_mla_ukv[:, h * kvd:h * kvd + MLA_NOPE] for h in range(MLA_HEADS)],
                          axis=1).T.astype(BF16)
    wv = jnp.concatenate([w_mla_ukv[:, h * kvd + MLA_NOPE:(h + 1) * kvd] for h in range(MLA_HEADS)],
                         axis=1).astype(BF16)
    nff = D_FF // D_MODEL
    return dict(
        w_in3=w_p, w_t=w_t, w_gate=w_gate, wuq=wuq, wkt=wkt, wv=wv,
        wb=w_branch.astype(BF16), wo=w_out.astype(BF16), wxq=w_xq.astype(BF16), wxo=w_xo.astype(BF16),
        wmem=jnp.concatenate([w_mem_k, w_mem_v], axis=1).astype(BF16),
        wu3=w_up.astype(BF16),
        wd3=w_down.astype(BF16).reshape(nff, D_MODEL, D_MODEL))


def _rope_tables(pos, reps):
    half = MLA_ROPE // 2
    freq = ROPE_THETA ** (-jnp.arange(half, dtype=F32) / half)
    ang = pos.astype(F32)[:, None] * freq[None, :]
    cos, sin = jnp.cos(ang), jnp.sin(ang)
    z = jnp.zeros((pos.shape[0], LANE - MLA_ROPE), F32)
    cos_r = jnp.tile(jnp.concatenate([cos, cos, z], axis=1), (reps, 1))
    sin_r = jnp.tile(jnp.concatenate([-sin, sin, z], axis=1), (reps, 1))
    return cos_r, sin_r, cos_r[:, :MLA_ROPE].T, sin_r[:, :MLA_ROPE].T


def _tile(n, pref):
    t = min(n, pref)
    assert n % t == 0
    return t


def _layer(x, bsz, s, pos0, w, lb, b_fox, g_q, g_kv, g_hout, g_norm, mem_k, mem_v, mem_row0, past, cfg,
           layer, depth, shared):
    m = bsz * s
    g = lambda i: g_norm[i][None, :]
    tm_in = _tile(s, cfg["tm_in"])
    tm_p = _tile(s, cfg["tm_prep"])
    cos_r, sin_r, cos_c, sin_c = _rope_tables(pos0 + jnp.arange(s), 1)
    big, kt, vt, kpet, logft = in_proj(x, g(0), w["w_in3"], w["w_t"], cos_c, sin_c, b_fox[:, None],
                                       bsz, s, tm_in, layer, depth, shared[:4])
    qx, knt, v, ckv_n = mla_prep(big, cos_r, sin_r, g_q[None, :], g_kv[None, :],
                                 w["wuq"], w["wkt"], w["wv"], bsz, s, tm_p, layer, depth, shared[4:])
    row0 = layer * bsz
    rows = lambda a: a.reshape((depth * bsz,) + a.shape[2:])
    cumt = fox_cumsum(rows(logft), row0, bsz)
    if past is None:
        t = _tile(s, cfg["t_attn"])
        o_fox = fox_prompt(big, rows(kt), rows(vt), cumt, row0, bsz, s, t)
        o_mla = mla_prompt(qx, knt, rows(kpet), v, row0, bsz, s, t)
        o_hg, hg_state = hgrn(big, lb[None, :], g_hout[None, :], bsz, s, CHUNK,
                              _tile(s, cfg["hg_rows"]))
    else:
        c_kt, c_vt, c_cumt, c_knt, c_kpt, c_v, c_hg = past
        p = c_kt.shape[2]
        o_fox = fox_sample(big, rows(kt), rows(vt), c_kt, c_vt, cumt, c_cumt, bsz, s, p, layer)
        o_mla = mla_sample(qx, knt, rows(kpet), v, c_knt, c_kpt, c_v, bsz, s, p, layer)
        o_hg, hg_state = hgrn(big, lb[None, :], g_hout[None, :], bsz, s, s, s, s0=c_hg, s0_row0=row0)
    x = merge_out(o_fox, o_mla, o_hg, x, w["w_gate"], w["wb"], w["wo"], g(0), g(1),
                  _tile(m, cfg["tm_merge"]))
    x = cross_block(x, mem_k, mem_v, w["wxq"], w["wxo"], g(2), g(3), bsz, s, _tile(s, cfg["tm_cross"]),
                    mem_row0)
    x = mlp_block(x, w["wu3"], w["wd3"], g(4), g(5), _tile(m, cfg["tm_mlp"]))
    return x, (kt, vt, kpet, logft, ckv_n), hg_state


def _from_feature_major(stacked, heads):
    a = jnp.swapaxes(stacked, 2, 3)
    if heads:
        a = a.reshape(a.shape[:3] + (heads, a.shape[3] // heads))
    return a


def _assemble_states(shared, hg_states, bsz, s):
    kt, vt, kpet, logft, ckv = shared
    return (_from_feature_major(kt, FOX_HEADS), _from_feature_major(vt, FOX_HEADS),
            _from_feature_major(logft, 0), ckv.reshape(ckv.shape[0], bsz, s, MLA_KV_RANK),
            _from_feature_major(kpet, 0), jnp.stack(hg_states))


_CFG = dict(tm_in=1024, tm_prep=512, t_attn=512, hg_rows=512, tm_merge=512, tm_cross=512,
            tm_mlp=1024, tm_mem=512, tm_expand=4096)


def kernel(x_prompt, x_sample, cache_fox_k, cache_fox_v, cache_fox_logf, cache_mla_ckv, cache_mla_kpe,
           state_hgrn, cache_mem_k, cache_mem_v, mem_prompt, w_in, b_fox, g_mla_q, w_mla_uq, g_mla_kv,
           w_mla_ukv, g_hgrn_out, lb_hgrn, w_branch, w_out, w_xq, w_mem_k, w_mem_v, w_xo, w_up, w_down,
           g_norm):
    cfg = _CFG
    depth = w_in.shape[0]
    lb_p = jax.nn.softmax(lb_hgrn.astype(F32), axis=0)
    lb_all = jnp.cumsum(lb_p, axis=0) - lb_p[0]
    ws = [_prep_layer_weights(w_in[l], w_mla_uq[l], w_mla_ukv[l], w_branch[l], w_out[l], w_xq[l],
                              w_mem_k[l], w_mem_v[l], w_xo[l], w_up[l], w_down[l]) for l in range(depth)]

    def run_layer(x, bsz, s, pos0, l, mk, mv, mem_row0, past, shared):
        return _layer(x, bsz, s, pos0, ws[l], lb_all[l], b_fox[l], g_mla_q[l], g_mla_kv[l],
                      g_hgrn_out[l], g_norm[l], mk, mv, mem_row0, past, cfg, l, depth, shared)

    bp, sp, _ = x_prompt.shape
    x = x_prompt.reshape(bp * sp, D_MODEL)
    mem = mem_prompt.reshape(bp * N_MEM, D_MODEL)
    shared, hg_states, p_mem = (), [], []
    for l in range(depth):
        mk, mv = mem_kv(mem, ws[l]["wmem"], _tile(bp * N_MEM, cfg["tm_mem"]))
        x, shared, hg = run_layer(x, bp, sp, 0, l, mk, mv, 0, None, shared)
        hg_states.append(hg)
        p_mem.append((mk.reshape(bp, N_MEM, X_HEADS, X_DIM), mv.reshape(bp, N_MEM, X_HEADS, X_DIM)))
    y_prompt = x.reshape(bp, sp, D_MODEL)
    p_out = _assemble_states(shared, hg_states, bp, sp) + tuple(jnp.stack(a) for a in zip(*p_mem))

    bs, ts, _ = x_sample.shape
    p = cache_fox_k.shape[2]
    fm = lambda c: jnp.moveaxis(c, 2, -1)
    c_kt = fm(cache_fox_k).reshape(depth * bs, BRANCH_W, p)
    c_vt = fm(cache_fox_v).reshape(depth * bs, BRANCH_W, p)
    c_kpt = fm(cache_mla_kpe).reshape(depth * bs, MLA_ROPE, p)
    c_cumt = fox_cumsum(fm(cache_fox_logf).reshape(depth * bs, FOX_HEADS, p), 0, depth * bs)
    c_ckv = cache_mla_ckv.reshape(depth * bs * p, MLA_KV_RANK)
    c_hg = state_hgrn.reshape((depth * bs,) + state_hgrn.shape[2:])
    c_mk = cache_mem_k.reshape(depth * bs * N_MEM, X_HEADS * X_DIM)
    c_mv = cache_mem_v.reshape(depth * bs * N_MEM, X_HEADS * X_DIM)
    x = x_sample.reshape(bs * ts, D_MODEL)
    shared, hg_states = (), []
    for l in range(depth):
        c_knt, c_v = latent_expand(c_ckv, ws[l]["wkt"], ws[l]["wv"], bs, p, _tile(p, cfg["tm_expand"]), l)
        past = (c_kt, c_vt, c_cumt, c_knt, c_kpt, c_v, c_hg)
        x, shared, hg = run_layer(x, bs, ts, p, l, c_mk, c_mv, l * bs, past, shared)
        hg_states.append(hg)
    y_sample = x.reshape(bs, ts, D_MODEL)
    return (y_prompt, y_sample, *p_out, *_assemble_states(shared, hg_states, bs, ts))
```

```python
import functools

import numpy as np
import jax
import jax.numpy as jnp
from jax import lax
from jax.experimental import pallas as pl
from jax.experimental.pallas import tpu as pltpu

F32 = jnp.float32
BF16 = jnp.bfloat16

D_MODEL = 1024
CHUNK = 64
N_MEM = 256
EPS = 1e-6
NEG_BIG = -1e30
EXP_CLIP = 80.0
FOX_HEADS = 8
FOX_DIM = 64
MLA_HEADS = 4
MLA_Q_RANK = 384
MLA_KV_RANK = 256
MLA_NOPE = 128
MLA_ROPE = 64
MLA_V = 128
ROPE_THETA = 10000.0
HG_HEADS = 4
HG_DK = 128
HG_DV = 128
X_HEADS = 4
X_DIM = 128
D_FF = 4 * D_MODEL
BRANCH_W = 512
IN_SIZES = (512, 512, 512, FOX_HEADS, MLA_Q_RANK, MLA_KV_RANK, MLA_ROPE, 512, 512, 512, 512,
            D_MODEL, D_MODEL, D_MODEL)

LANE = 128
SUB_BLOCK = 16
VMEM_LIMIT = 56 * 1024 * 1024
LOG2E = 1.4426950408889634

IN_TN = 1664
HQF_W, MLA_W, HV_W = 1024, MLA_Q_RANK + MLA_KV_RANK, 1536
OFF_FQ = 1024
T_FK, T_FV, T_KPE, T_KPE_SW, T_FF, NT_IN = 0, 512, 1024, 1088, 1152, 1160


def _params(sem, vmem=VMEM_LIMIT):
    return pltpu.CompilerParams(dimension_semantics=sem, vmem_limit_bytes=vmem)


def _dot(a, b):
    return jnp.dot(a, b, preferred_element_type=F32)


def _dot_nt(a, b):
    return lax.dot_general(a, b, (((1,), (1,)), ((), ())), preferred_element_type=F32)


def _dot_tn(a, b):
    return lax.dot_general(a, b, (((0,), (0,)), ((), ())), preferred_element_type=F32)


def _rms(x, g):
    y = x * lax.rsqrt(jnp.mean(x * x, axis=-1, keepdims=True) + EPS)
    return y * g


def _log_sigmoid(z):
    return jnp.minimum(z, 0.0) - jnp.log(1.0 + jnp.exp(-jnp.abs(z)))


def _sigmoid(z):
    return 1.0 / (1.0 + jnp.exp(-z))


def _resident(shape):
    nd = len(shape)
    return pl.BlockSpec(shape, lambda *_: (0,) * nd, pipeline_mode=pl.Buffered(1))


def _in_proj_kernel(x_ref, g_ref, w_ref, wt_ref, cos_ref, sin_ref, bf_ref, *rest):
    hqf_ref, mla_ref, hv_ref, kt_ref, vt_ref, kpe_ref, lf_ref, h_ref = rest[-8:]
    j = pl.program_id(1)

    @pl.when(j == 0)
    def _():
        h = _rms(x_ref[...], g_ref[...]).astype(BF16)
        h_ref[...] = h
        yt = _dot_nt(wt_ref[...], h)
        kt_ref[0] = yt[T_FK:T_FK + BRANCH_W]
        vt_ref[0] = yt[T_FV:T_FV + BRANCH_W]
        kpe_ref[0] = (yt[T_KPE:T_KPE + MLA_ROPE] * cos_ref[...]
                      + yt[T_KPE_SW:T_KPE_SW + MLA_ROPE] * sin_ref[...])
        lf_ref[0] = _log_sigmoid(yt[T_FF:T_FF + FOX_HEADS] + bf_ref[...])
        y = _dot(h, w_ref[:, :IN_TN])
        hqf_ref[...] = y[:, :HQF_W]
        mla_ref[...] = y[:, HQF_W:]

    @pl.when(j == 1)
    def _():
        hv_ref[...] = _dot(h_ref[...], w_ref[:, IN_TN:])[:, :HV_W].astype(BF16)


def in_proj(x, g, w3, wt, cos_t, sin_t, b_col, bsz, s, tm, layer, depth, prev):
    m, k = x.shape
    assert w3.shape[1] == 2 * IN_TN
    nt = s // tm
    ntab = cos_t.shape[1] // tm
    feats = (BRANCH_W, BRANCH_W, MLA_ROPE, FOX_HEADS)
    tspec = lambda rows: pl.BlockSpec((None, 1, rows, tm), lambda i, j: (layer, i // nt, 0, i % nt))
    row = lambda w: pl.BlockSpec((tm, w), lambda i, j: (i, 0))
    n_in, n_row = 7, 3
    return pl.pallas_call(
        _in_proj_kernel,
        out_shape=(jax.ShapeDtypeStruct((m, HQF_W), F32), jax.ShapeDtypeStruct((m, MLA_W), F32),
                   jax.ShapeDtypeStruct((m, HV_W), BF16))
        + tuple(jax.ShapeDtypeStruct((depth, bsz, f, s), F32) for f in feats),
        grid=(m // tm, 2),
        in_specs=[pl.BlockSpec((tm, k), lambda i, j: (i, 0)),
                  pl.BlockSpec((1, k), lambda i, j: (0, 0)),
                  _resident(w3.shape),
                  _resident(wt.shape),
                  pl.BlockSpec((MLA_ROPE, tm), lambda i, j: (0, i % ntab)),
                  pl.BlockSpec((MLA_ROPE, tm), lambda i, j: (0, i % ntab)),
                  pl.BlockSpec((FOX_HEADS, 1), lambda i, j: (0, 0))]
        + [pl.BlockSpec(memory_space=pl.ANY)] * len(prev),
        out_specs=(row(HQF_W), row(MLA_W), row(HV_W)) + tuple(tspec(f) for f in feats),
        scratch_shapes=[pltpu.VMEM((tm, k), BF16)],
        input_output_aliases={n_in + i: n_row + i for i in range(len(prev))},
        compiler_params=_params(("parallel", "arbitrary")),
        name="in_proj",
    )(x, g, w3, wt, cos_t, sin_t, b_col, *prev)


def _mla_prep_kernel(in_ref, cs_ref, sn_ref, gq_ref, gkv_ref, wuq_ref, wkt_ref, wv_ref, *rest):
    qx_ref, knt_ref, v_ref, ckvn_ref = rest[-4:]
    cos_t = cs_ref[...]
    sin_t = sn_ref[...]
    qn = _rms(in_ref[:, :MLA_Q_RANK], gq_ref[...]).astype(BF16)
    qall = _dot(qn, wuq_ref[...]) * ((MLA_NOPE + MLA_ROPE) ** -0.5 * LOG2E)
    for h in range(MLA_HEADS):
        lo = h * LANE
        qr = (qall[:, 512 + lo:512 + lo + LANE] * cos_t
              + qall[:, 1024 + lo:1024 + lo + LANE] * sin_t)
        qx_ref[:, 2 * lo:2 * lo + LANE] = qall[:, lo:lo + LANE].astype(BF16)
        qx_ref[:, 2 * lo + LANE:2 * lo + 2 * LANE] = qr.astype(BF16)
    ckvn = _rms(in_ref[:, MLA_Q_RANK:], gkv_ref[...])
    ckvn_ref[...] = ckvn
    cb = ckvn.astype(BF16)
    knt_ref[0] = _dot_nt(wkt_ref[...], cb).astype(BF16)
    v_ref[...] = _dot(cb, wv_ref[...]).astype(BF16)


def mla_prep(mla_in, cos_t, sin_t, gq, gkv, wuq, wkt, wv, bsz, s, tm, layer, depth, prev):
    m = mla_in.shape[0]
    nt = s // tm
    ntab = cos_t.shape[0] // tm
    row = lambda w: pl.BlockSpec((tm, w), lambda i: (i, 0))
    full = lambda a: pl.BlockSpec(a.shape, lambda i: (0,) * a.ndim)
    n_in = 8
    return pl.pallas_call(
        _mla_prep_kernel,
        out_shape=(jax.ShapeDtypeStruct((m, 1024), BF16),
                   jax.ShapeDtypeStruct((bsz, 512, s), BF16),
                   jax.ShapeDtypeStruct((m, 512), BF16),
                   jax.ShapeDtypeStruct((depth, m, MLA_KV_RANK), F32)),
        grid=(m // tm,),
        in_specs=[row(MLA_W),
                  pl.BlockSpec((tm, LANE), lambda i: (i % ntab, 0)),
                  pl.BlockSpec((tm, LANE), lambda i: (i % ntab, 0)),
                  full(gq), full(gkv), full(wuq), full(wkt), full(wv)]
        + [pl.BlockSpec(memory_space=pl.ANY)] * len(prev),
        out_specs=(row(1024), pl.BlockSpec((1, 512, tm), lambda i: (i // nt, 0, i % nt)),
                   row(512), pl.BlockSpec((None, tm, MLA_KV_RANK), lambda i: (layer, i, 0))),
        input_output_aliases={n_in + i: 3 + i for i in range(len(prev))},
        compiler_params=_params(("parallel",)),
        name="mla_prep",
    )(mla_in, cos_t, sin_t, gq, gkv, wuq, wkt, wv, *prev)


def _cumsum_kernel(x_ref, c_ref, *, w):
    s = x_ref.shape[1]
    r = lax.broadcasted_iota(jnp.int32, (w, w), 0)
    c = lax.broadcasted_iota(jnp.int32, (w, w), 1)
    upper = (r <= c).astype(F32)
    local = [jnp.dot(x_ref[:, g * w:(g + 1) * w], upper, preferred_element_type=F32,
                     precision=lax.Precision.HIGHEST) for g in range(s // w)]
    carry = jnp.zeros((x_ref.shape[0], 1), F32)
    for g, cum in enumerate(local):
        cum = cum + carry
        c_ref[:, g * w:(g + 1) * w] = cum
        carry = cum[:, w - 1:w]


def fox_cumsum(x, row0, bsz):
    n, h, s = x.shape
    rows = bsz * h
    assert (row0 * h) % rows == 0
    out = pl.pallas_call(
        functools.partial(_cumsum_kernel, w=min(LANE, s)),
        out_shape=jax.ShapeDtypeStruct((rows, s), F32),
        grid=(1,),
        in_specs=[pl.BlockSpec((rows, s), lambda i: (row0 * h // rows, 0))],
        out_specs=pl.BlockSpec((rows, s), lambda i: (0, 0)),
        compiler_params=_params(("arbitrary",)),
        name="fox_cumsum",
    )(x.reshape(n * h, s))
    return out.reshape(bsz, h, s)


def _pair_rows_mask(hh):
    sub = lax.broadcasted_iota(jnp.int32, (LANE, 1), 0)
    return (sub < FOX_DIM) if hh == 0 else (sub >= FOX_DIM)


def _fox_finish(acc0, acc1):
    lane = lax.broadcasted_iota(jnp.int32, (1, LANE), 1)
    o0 = acc0 / pltpu.roll(acc0, FOX_DIM, axis=1)
    o1 = acc1 / pltpu.roll(acc1, FOX_DIM, axis=1)
    return jnp.where(lane < FOX_DIM, o0, o1)


def _causal_schedule(nq, t):
    todo = {i: [(i, j, 0, t, 0) for j in range(i + 1)] for i in range(nq)}
    order = []
    while any(todo.values()):
        for i in reversed(range(nq)):
            if todo[i]:
                order.append(todo[i].pop(0))
    return order


def _rows_from(r0, old, new):
    return new if r0 == 0 else jnp.concatenate([old[:r0], new], axis=0)


def _fox_prompt_kernel(q_ref, kt_ref, vt_ref, ct_ref, o_ref, *, t, nq):
    hp = pl.program_id(1)
    lane = lax.broadcasted_iota(jnp.int32, (1, LANE), 1)
    lo = lane < FOX_DIM
    span = lambda i: slice(i * t, (i + 1) * t)
    ct = [ct_ref[0, 2 * hp + hh] for hh in range(2)]
    qs, cref = [], []
    for i in range(nq):
        q = q_ref[span(i), :].astype(F32) * (FOX_DIM ** -0.5 * LOG2E)
        qs.append((jnp.where(lo, q, 0.0).astype(BF16), jnp.where(lo, 0.0, q).astype(BF16)))
        cref.append([c[:, i * t:i * t + 1] for c in ct])
    kts = [kt_ref[0, :, span(j)].astype(BF16) for j in range(nq)]
    vts = [[jnp.where(_pair_rows_mask(hh), vt_ref[0, :, span(j)], 1.0).astype(BF16) for hh in range(2)]
           for j in range(nq)]
    def scores(i, j, k0, w, r0):
        out = []
        for hh in range(2):
            s = (_dot(qs[i][hh][r0:], kts[j][:, k0:k0 + w])
                 + (cref[i][hh] - ct[hh][:, j * t + k0:j * t + k0 + w]) * LOG2E)
            if i == j:
                r = lax.broadcasted_iota(jnp.int32, s.shape, 0)
                c = lax.broadcasted_iota(jnp.int32, s.shape, 1)
                s = jnp.where(r >= c, s, NEG_BIG)
            out.append(s)
        return out

    state = [[(jnp.full((t, 1), NEG_BIG, F32), jnp.zeros((t, LANE), F32)) for _ in range(2)]
             for _ in range(nq)]
    order = _causal_schedule(nq, t)
    ss = scores(*order[0])
    for n, (i, j, k0, w, r0) in enumerate(order):
        nxt = scores(*order[n + 1]) if n + 1 < len(order) else None
        m_old = [state[i][hh][0][r0:] for hh in range(2)]
        m_new = [jnp.maximum(m_old[hh], jnp.max(ss[hh], axis=1, keepdims=True)) for hh in range(2)]
        ps = [jnp.exp2(ss[hh] - m_new[hh]).astype(BF16) for hh in range(2)]
        pvs = [_dot_nt(ps[hh], vts[j][hh][:, k0:k0 + w]) for hh in range(2)]
        state[i] = [(_rows_from(r0, state[i][hh][0], m_new[hh]),
                     _rows_from(r0, state[i][hh][1],
                                jnp.exp2(m_old[hh] - m_new[hh]) * state[i][hh][1][r0:] + pvs[hh]))
                    for hh in range(2)]
        ss = nxt
    for i in range(nq):
        o_ref[span(i), :] = _fox_finish(state[i][0][1], state[i][1][1]).astype(BF16)


def fox_prompt(hv, kt, vt, cumt, row0, bsz, s, t):
    m = hv.shape[0]
    return pl.pallas_call(
        functools.partial(_fox_prompt_kernel, t=t, nq=s // t),
        out_shape=jax.ShapeDtypeStruct((m, BRANCH_W), BF16),
        grid=(bsz, FOX_HEADS // 2),
        in_specs=[pl.BlockSpec((s, LANE), lambda b, h: (b, OFF_FQ // LANE + h)),
                  pl.BlockSpec((1, LANE, s), lambda b, h: (row0 + b, h, 0)),
                  pl.BlockSpec((1, LANE, s), lambda b, h: (row0 + b, h, 0)),
                  pl.BlockSpec((1, FOX_HEADS, 1, s), lambda b, h: (b, 0, 0, 0))],
        out_specs=pl.BlockSpec((s, LANE), lambda b, h: (b, h)),
        compiler_params=_params(("parallel", "parallel")),
        name="fox_prompt",
    )(hv, kt, vt, cumt.reshape(bsz, FOX_HEADS, 1, s))


def _mla_keys(knt, kpt):
    n = knt.shape[1]
    return jnp.concatenate([knt, kpt.astype(BF16), jnp.zeros((LANE - MLA_ROPE, n), BF16)], axis=0)


def _mla_prompt_kernel(q_ref, knt_ref, kpt_ref, v_ref, o_ref, *, t, nq):
    hs = range(2)
    span = lambda i: slice(i * t, (i + 1) * t)
    cols = lambda hh, w: slice(hh * w, (hh + 1) * w)
    keys = [[_mla_keys(knt_ref[0, cols(hh, LANE), span(j)], kpt_ref[0, :, span(j)]) for hh in hs]
            for j in range(nq)]
    def scores(i, j, k0, w, r0):
        ss = [_dot(q_ref[i * t + r0:(i + 1) * t, cols(hh, 2 * LANE)], keys[j][hh][:, k0:k0 + w]) for hh in hs]
        if i != j:
            return ss
        rc = lax.broadcasted_iota(jnp.int32, ss[0].shape, 0) // CHUNK
        cc = lax.broadcasted_iota(jnp.int32, ss[0].shape, 1) // CHUNK
        return [jnp.where(rc >= cc, s, NEG_BIG) for s in ss]

    state = [[(jnp.full((t, 1), NEG_BIG, F32), jnp.zeros((t, 1), F32), jnp.zeros((t, LANE), F32))
              for _ in hs] for _ in range(nq)]
    order = _causal_schedule(nq, t)
    ss = scores(*order[0])
    for n, (i, j, k0, w, r0) in enumerate(order):
        nxt = scores(*order[n + 1]) if n + 1 < len(order) else None
        old = [[a[r0:] for a in state[i][hh]] for hh in hs]
        m_new = [jnp.maximum(old[hh][0], jnp.max(ss[hh], axis=1, keepdims=True)) for hh in hs]
        ps = [jnp.exp2(ss[hh] - m_new[hh]) for hh in hs]
        pvs = [_dot(ps[hh].astype(BF16), v_ref[j * t + k0:j * t + k0 + w, cols(hh, LANE)]) for hh in hs]
        new = []
        for hh in hs:
            alpha = jnp.exp2(old[hh][0] - m_new[hh])
            upd = (m_new[hh], alpha * old[hh][1] + jnp.sum(ps[hh], axis=1, keepdims=True),
                   alpha * old[hh][2] + pvs[hh])
            new.append(tuple(_rows_from(r0, a, b) for a, b in zip(state[i][hh], upd)))
        state[i] = new
        ss = nxt
    for i in range(nq):
        o_ref[span(i), :] = jnp.concatenate([state[i][hh][2] / state[i][hh][1] for hh in hs],
                                            axis=1).astype(BF16)


def mla_prompt(qx, knt, kpt, v, row0, bsz, s, t):
    assert (t // 2) % CHUNK == 0 and (t // 2) % LANE == 0
    m = qx.shape[0]
    return pl.pallas_call(
        functools.partial(_mla_prompt_kernel, t=t, nq=s // t),
        out_shape=jax.ShapeDtypeStruct((m, BRANCH_W), BF16),
        grid=(bsz, MLA_HEADS // 2),
        in_specs=[pl.BlockSpec((s, 4 * LANE), lambda b, h: (b, h)),
                  pl.BlockSpec((1, 2 * LANE, s), lambda b, h: (b, h, 0)),
                  pl.BlockSpec((1, MLA_ROPE, s), lambda b, h: (row0 + b, 0, 0)),
                  pl.BlockSpec((s, 2 * LANE), lambda b, h: (b, h))],
        out_specs=pl.BlockSpec((s, 2 * LANE), lambda b, h: (b, h)),
        compiler_params=_params(("parallel", "parallel")),
        name="mla_prompt",
    )(qx, knt, kpt, v)


def _fox_sample_kernel(q_ref, ktn_ref, vtn_ref, ktc_ref, vtc_ref, ctn_ref, ctc_ref, o_ref, *, t, p):
    hp = pl.program_id(1)
    lane = lax.broadcasted_iota(jnp.int32, (1, LANE), 1)
    lo = lane < FOX_DIM
    q = q_ref[...].astype(F32) * (FOX_DIM ** -0.5 * LOG2E)
    kt_c = ktc_ref[0].astype(BF16)
    kt_n = ktn_ref[0].astype(BF16)
    vt_c = vtc_ref[0]
    vt_n = vtn_ref[0]
    causal = (lax.broadcasted_iota(jnp.int32, (t, t), 0) >= lax.broadcasted_iota(jnp.int32, (t, t), 1))
    accs = []
    for hh in range(2):
        head = 2 * hp + hh
        qh = (jnp.where(lo, q, 0.0) if hh == 0 else jnp.where(lo, 0.0, q)).astype(BF16)
        cc = ctc_ref[0, pl.ds(head, 1), :]
        ctot = cc[:, p - 1:p]
        s_c = _dot(qh, kt_c) + (ctot - cc) * LOG2E
        s_n = _dot(qh, kt_n) - ctn_ref[0, pl.ds(head, 1), :] * LOG2E
        s_n = jnp.where(causal, s_n, NEG_BIG)
        m = jnp.maximum(jnp.max(s_c, axis=1, keepdims=True), jnp.max(s_n, axis=1, keepdims=True))
        rows = _pair_rows_mask(hh)
        accs.append(_dot_nt(jnp.exp2(s_c - m).astype(BF16), jnp.where(rows, vt_c, 1.0).astype(BF16))
                    + _dot_nt(jnp.exp2(s_n - m).astype(BF16), jnp.where(rows, vt_n, 1.0).astype(BF16)))
    o_ref[...] = _fox_finish(accs[0], accs[1]).astype(BF16)


def fox_sample(hv, kt_n, vt_n, kt_c, vt_c, cumt_n, cumt_c, bsz, t, p, layer):
    cidx = lambda b, h: (layer * bsz + b, h, 0)
    return pl.pallas_call(
        functools.partial(_fox_sample_kernel, t=t, p=p),
        out_shape=jax.ShapeDtypeStruct((bsz * t, BRANCH_W), BF16),
        grid=(bsz, FOX_HEADS // 2),
        in_specs=[pl.BlockSpec((t, LANE), lambda b, h: (b, OFF_FQ // LANE + h)),
                  pl.BlockSpec((1, LANE, t), cidx),
                  pl.BlockSpec((1, LANE, t), cidx),
                  pl.BlockSpec((1, LANE, p), cidx),
                  pl.BlockSpec((1, LANE, p), cidx),
                  pl.BlockSpec((1, FOX_HEADS, t), lambda b, h: (b, 0, 0)),
                  pl.BlockSpec((1, FOX_HEADS, p), lambda b, h: (layer * bsz + b, 0, 0))],
        out_specs=pl.BlockSpec((t, LANE), lambda b, h: (b, h)),
        compiler_params=_params(("parallel", "parallel")),
        name="fox_sample",
    )(hv, kt_n, vt_n, kt_c, vt_c, cumt_n, cumt_c)


def _mla_sample_kernel(q_ref, kntn_ref, kptn_ref, vn_ref, kntc_ref, kptc_ref, vc_ref, o_ref, *, t, p):
    q = q_ref[...]
    s_c = _dot(q, _mla_keys(kntc_ref[0], kptc_ref[0]))
    s_n = _dot(q, _mla_keys(kntn_ref[0], kptn_ref[0]))
    qc = (p + lax.broadcasted_iota(jnp.int32, (t, t), 0)) // CHUNK
    kc = (p + lax.broadcasted_iota(jnp.int32, (t, t), 1)) // CHUNK
    s_n = jnp.where(qc >= kc, s_n, NEG_BIG)
    m = jnp.maximum(jnp.max(s_c, axis=1, keepdims=True), jnp.max(s_n, axis=1, keepdims=True))
    p_c = jnp.exp2(s_c - m)
    p_n = jnp.exp2(s_n - m)
    l = jnp.sum(p_c, axis=1, keepdims=True) + jnp.sum(p_n, axis=1, keepdims=True)
    o = _dot(p_c.astype(BF16), vc_ref[...]) + _dot(p_n.astype(BF16), vn_ref[...])
    o_ref[...] = (o / l).astype(BF16)


def mla_sample(qx, knt_n, kpt_n, v_n, knt_c, kpt_c, v_c, bsz, t, p, layer):
    assert (p - 1) // CHUNK <= p // CHUNK
    return pl.pallas_call(
        functools.partial(_mla_sample_kernel, t=t, p=p),
        out_shape=jax.ShapeDtypeStruct((bsz * t, BRANCH_W), BF16),
        grid=(bsz, MLA_HEADS),
        in_specs=[pl.BlockSpec((t, 2 * LANE), lambda b, h: (b, h)),
                  pl.BlockSpec((1, LANE, t), lambda b, h: (b, h, 0)),
                  pl.BlockSpec((1, MLA_ROPE, t), lambda b, h: (layer * bsz + b, 0, 0)),
                  pl.BlockSpec((t, LANE), lambda b, h: (b, h)),
                  pl.BlockSpec((1, LANE, p), lambda b, h: (b, h, 0)),
                  pl.BlockSpec((1, MLA_ROPE, p), lambda b, h: (layer * bsz + b, 0, 0)),
                  pl.BlockSpec((p, LANE), lambda b, h: (b, h))],
        out_specs=pl.BlockSpec((t, LANE), lambda b, h: (b, h)),
        compiler_params=_params(("parallel", "parallel")),
        name="mla_sample",
    )(qx, knt_n, kpt_n, v_n, knt_c, kpt_c, v_c)


def _latent_expand_kernel(c_ref, wkt_ref, wv_ref, knt_ref, v_ref):
    cb = c_ref[...].astype(BF16)
    knt_ref[0] = _dot_nt(wkt_ref[...], cb).astype(BF16)
    v_ref[...] = _dot(cb, wv_ref[...]).astype(BF16)


def latent_expand(ckv, wkt, wv, bsz, p, tm, layer):
    m = bsz * p
    nt = p // tm
    return pl.pallas_call(
        _latent_expand_kernel,
        out_shape=(jax.ShapeDtypeStruct((bsz, 512, p), BF16), jax.ShapeDtypeStruct((m, 512), BF16)),
        grid=(m // tm,),
        in_specs=[pl.BlockSpec((tm, MLA_KV_RANK), lambda i: (layer * (m // tm) + i, 0)),
                  pl.BlockSpec(wkt.shape, lambda i: (0, 0)), pl.BlockSpec(wv.shape, lambda i: (0, 0))],
        out_specs=(pl.BlockSpec((1, 512, tm), lambda i: (i // nt, 0, i % nt)),
                   pl.BlockSpec((tm, 512), lambda i: (i, 0))),
        compiler_params=_params(("parallel",)),
        name="latent_expand",
    )(ckv, wkt, wv)


def _hgrn_gates(z, lb, tri):
    logf = _log_sigmoid(z) + jnp.log(1.0 + lb * jnp.exp(jnp.minimum(-z, EXP_CLIP)))
    k = (1.0 - lb) * (1.0 / (1.0 + jnp.exp(z)))
    h1 = logf.astype(BF16)
    r1 = logf - h1.astype(F32)
    h2 = r1.astype(BF16)
    h3 = (r1 - h2.astype(F32)).astype(BF16)
    hcat = jnp.concatenate([h1, h2, h3], axis=1)
    g = tri.shape[0]
    parts = jnp.concatenate([_dot(tri, hcat[r:r + g, :]) for r in range(0, z.shape[0], g)], axis=0)
    lc = ((parts[:, :LANE] + parts[:, LANE:2 * LANE]) + parts[:, 2 * LANE:]) * LOG2E
    return k, lc


def _hgrn_local(q, z, lb, tri, v_b, ln, sel, k_ref, lc_ref):
    k, lc = _hgrn_gates(z, lb, tri)
    k_ref[...] = k
    lc_ref[...] = lc
    nchunk = q.shape[0] // ln
    nsb = q.shape[0] // SUB_BLOCK
    per = ln // SUB_BLOCK
    half = SUB_BLOCK // 2
    rows = lambda a, i: a[i * SUB_BLOCK:(i + 1) * SUB_BLOCK, :]
    lcb = [jnp.zeros((1, LANE), F32) if i % per == 0 else lc_ref[i * SUB_BLOCK - 1:i * SUB_BLOCK, :]
           for i in range(nsb)]
    lcb_rows = jnp.concatenate([jnp.broadcast_to(b, (SUB_BLOCK, LANE)) for b in lcb], axis=0)
    last = [lc_ref[(c + 1) * ln - 1:(c + 1) * ln, :] for c in range(nchunk)]
    last_rows = jnp.concatenate([jnp.broadcast_to(b, (ln, LANE)) for b in last], axis=0)
    qh = (q * jnp.exp2(lc - lcb_rows)).astype(BF16)
    qe = (q * jnp.exp2(lc)).astype(BF16)
    kdec = (k * jnp.exp2(last_rows - lc)).astype(BF16)
    a_off = {}
    for i in range(nsb):
        n = (i % per) * SUB_BLOCK
        if n:
            c0 = i * SUB_BLOCK - n
            kt = (k[c0:c0 + n, :] * jnp.exp2(lcb[i] - lc[c0:c0 + n, :])).astype(BF16)
            a_off[i] = _dot_nt(rows(qh, i), kt)
    yield None
    pieces = []
    for i in range(nsb):
        q_i, lc_i = rows(q, i), rows(lc, i)
        cols = []
        for s in range(SUB_BLOCK):
            lo = 0 if s < half else half
            r = i * SUB_BLOCK + s
            d = lc_i[lo:, :] - lc_ref[r:r + 1, :]
            d = (jnp.concatenate([jnp.minimum(d[:half, :], 0.0), d[half:, :]], axis=0) if s < half
                 else jnp.minimum(d, 0.0))
            w = (q_i[lo:, :] * k_ref[r:r + 1, :]) * jnp.exp2(d)
            if lo:
                w = jnp.concatenate([jnp.zeros((lo, LANE), F32), w], axis=0)
            cols.append(w.astype(BF16))
        pieces.append(jnp.concatenate(cols, axis=1))
    a_all = _dot(jnp.concatenate(pieces, axis=0), sel)
    inc = [_dot_tn(v_b[c * ln:(c + 1) * ln, :], kdec[c * ln:(c + 1) * ln, :]) for c in range(nchunk)]
    yield None
    pair_ok = (lax.broadcasted_iota(jnp.int32, (SUB_BLOCK, LANE), 0)
               >= lax.broadcasted_iota(jnp.int32, (SUB_BLOCK, LANE), 1))
    off = []
    for i in range(nsb):
        n = (i % per) * SUB_BLOCK
        off.append(_dot(a_off[i].astype(BF16), v_b[i * SUB_BLOCK - n:i * SUB_BLOCK, :]) if n
                   else jnp.zeros((SUB_BLOCK, LANE), F32))
    diag = [_dot(jnp.where(pair_ok, rows(a_all, i), 0.0)[:, :SUB_BLOCK].astype(BF16), rows(v_b, i))
            for i in range(nsb)]
    local = jnp.concatenate(off, axis=0) + jnp.concatenate(diag, axis=0)
    dec = [jnp.exp2(b) for b in last]
    yield local, qe, inc, dec


def _hgrn_kernel(*refs, ln, nchunk, has_init):
    refs = list(refs)
    hq_ref, hf_ref, hi_ref, hg_ref, lb_ref, go_ref, sel_ref = refs[:7]
    s0_ref = refs[7] if has_init else None
    o_ref, sout_ref, st_ref, k_ref, lc_ref = refs[-5:]
    step = pl.program_id(1)
    nrows = ln * nchunk

    @pl.when(step == 0)
    def _():
        for h in range(HG_HEADS):
            st_ref[h] = s0_ref[0, h].T if has_init else jnp.zeros((HG_DV, HG_DK), F32)

    ng = min(nrows, 2 * LANE)
    assert ng % ln == 0 and nrows % ng == 0
    ri = lax.broadcasted_iota(jnp.int32, (ng, ng), 0)
    ci = lax.broadcasted_iota(jnp.int32, (ng, ng), 1)
    tri = ((ri >= ci) & (ri // ln == ci // ln)).astype(BF16)

    def finish(h, local, qe, inc, dec):
        cs = slice(h * LANE, (h + 1) * LANE)
        st = st_ref[h]
        parts = []
        for c in range(nchunk):
            parts.append(_dot_nt(qe[c * ln:(c + 1) * ln, :], st.astype(BF16)))
            st = st * dec[c] + inc[c]
        st_ref[h] = st
        o = local + jnp.concatenate(parts, axis=0)
        o_ref[:, cs] = (_rms(o, go_ref[...]) * _sigmoid(hg_ref[:, cs].astype(F32))).astype(BF16)

    def start(h):
        cs = slice(h * LANE, (h + 1) * LANE)
        gen = _hgrn_local(hq_ref[:, cs], hf_ref[:, cs], lb_ref[:, cs], tri, hi_ref[:, cs].astype(BF16),
                          ln, sel_ref[...], k_ref.at[h], lc_ref.at[h])
        next(gen)
        return gen

    gens = {0: start(0)}
    for h in range(HG_HEADS):
        if h + 1 < HG_HEADS:
            gens[h + 1] = start(h + 1)
        next(gens[h])
        if h > 0:
            finish(h - 1, *next(gens.pop(h - 1)))
    finish(HG_HEADS - 1, *next(gens.pop(HG_HEADS - 1)))

    @pl.when(step == pl.num_programs(1) - 1)
    def _():
        for h in range(HG_HEADS):
            sout_ref[0, h] = st_ref[h].T


def hgrn(hqf, hv, lb, g_out, bsz, s, ln, rows, s0=None, s0_row0=0):
    m = hqf.shape[0]
    ns = s // rows
    has_init = s0 is not None
    sel = (np.arange(SUB_BLOCK * LANE)[:, None] // LANE == np.arange(LANE)[None, :])
    sel = jnp.asarray(sel, BF16)
    blk = lambda col: pl.BlockSpec((rows, BRANCH_W), lambda b, i: (b * ns + i, col))
    ins = [hqf, hqf, hv, hv, lb, g_out, sel]
    specs = [blk(0), blk(1), blk(0), blk(1),
             pl.BlockSpec((1, BRANCH_W), lambda b, i: (0, 0)),
             pl.BlockSpec((1, HG_DV), lambda b, i: (0, 0)),
             pl.BlockSpec(sel.shape, lambda b, i: (0, 0))]
    if has_init:
        ins.append(s0)
        specs.append(pl.BlockSpec((1, HG_HEADS, HG_DK, HG_DV), lambda b, i: (s0_row0 + b, 0, 0, 0)))
    return pl.pallas_call(
        functools.partial(_hgrn_kernel, ln=ln, nchunk=rows // ln, has_init=has_init),
        out_shape=(jax.ShapeDtypeStruct((m, BRANCH_W), BF16),
                   jax.ShapeDtypeStruct((bsz, HG_HEADS, HG_DK, HG_DV), F32)),
        grid=(bsz, ns),
        in_specs=specs,
        out_specs=(pl.BlockSpec((rows, BRANCH_W), lambda b, i: (b * ns + i, 0)),
                   pl.BlockSpec((1, HG_HEADS, HG_DK, HG_DV), lambda b, i: (b, 0, 0, 0))),
        scratch_shapes=[pltpu.VMEM((HG_HEADS, HG_DV, HG_DK), F32),
                        pltpu.VMEM((HG_HEADS, rows, LANE), F32),
                        pltpu.VMEM((HG_HEADS, rows, LANE), F32)],
        compiler_params=_params(("parallel", "arbitrary")),
        name="hgrn",
    )(*ins)


def _merge_kernel(of_ref, om_ref, oh_ref, x_ref, wg_ref, wb_ref, wo_ref, g0_ref, g1_ref, o_ref):
    x = x_ref[...]
    h = _rms(x, g0_ref[...]).astype(BF16)
    branches = (of_ref, om_ref, oh_ref)
    gates = [_dot(h, wg_ref[i]) for i in range(3)]
    outs = [_dot(branches[i][...], wb_ref[i]) for i in range(3)]
    merged = (_sigmoid(gates[0]) * outs[0] + _sigmoid(gates[1]) * outs[1]) + _sigmoid(gates[2]) * outs[2]
    y = _dot(merged.astype(BF16), wo_ref[...])
    o_ref[...] = x + _rms(y, g1_ref[...])


def merge_out(o_fox, o_mla, o_hg, x, wg, wb, wo, g0, g1, tm):
    m = x.shape[0]
    row = lambda w: pl.BlockSpec((tm, w), lambda i: (i, 0))
    vec = pl.BlockSpec((1, D_MODEL), lambda i: (0, 0))
    return pl.pallas_call(
        _merge_kernel,
        out_shape=jax.ShapeDtypeStruct((m, D_MODEL), F32),
        grid=(m // tm,),
        in_specs=[row(BRANCH_W), row(BRANCH_W), row(BRANCH_W), row(D_MODEL),
                  _resident(wg.shape), _resident(wb.shape), _resident(wo.shape), vec, vec],
        out_specs=row(D_MODEL),
        compiler_params=_params(("parallel",)),
        name="merge_out",
    )(o_fox, o_mla, o_hg, x, wg, wb, wo, g0, g1)


def _matmul2_kernel(x_ref, w_ref, a_ref, b_ref):
    y = _dot(x_ref[...].astype(BF16), w_ref[...])
    n = a_ref.shape[1]
    a_ref[...] = y[:, :n]
    b_ref[...] = y[:, n:]


def mem_kv(mem, w, tm):
    m, k = mem.shape
    n = w.shape[1] // 2
    row = lambda w_: pl.BlockSpec((tm, w_), lambda i: (i, 0))
    return pl.pallas_call(
        _matmul2_kernel,
        out_shape=(jax.ShapeDtypeStruct((m, n), F32), jax.ShapeDtypeStruct((m, n), F32)),
        grid=(m // tm,),
        in_specs=[row(k), _resident(w.shape)],
        out_specs=(row(n), row(n)),
        compiler_params=_params(("parallel",)),
        name="mem_kv",
    )(mem, w)


def _cross_kernel(x_ref, mk_ref, mv_ref, wq_ref, wo_ref, g2_ref, g3_ref, o_ref):
    x = x_ref[...]
    h = _rms(x, g2_ref[...]).astype(BF16)
    q = _dot(h, wq_ref[...])
    qb = (q * (X_DIM ** -0.5 * LOG2E)).astype(BF16)
    cols = [slice(hd * X_DIM, (hd + 1) * X_DIM) for hd in range(X_HEADS)]
    ss = [_dot_nt(qb[:, cs], mk_ref[:, cs].astype(BF16)) for cs in cols]
    ps = [jnp.exp2(s - jnp.max(s, axis=1, keepdims=True)) for s in ss]
    pvs = [_dot(p.astype(BF16), mv_ref[:, cs].astype(BF16)) for p, cs in zip(ps, cols)]
    outs = [pv / jnp.sum(p, axis=1, keepdims=True) for pv, p in zip(pvs, ps)]
    ox = jnp.concatenate(outs, axis=1).astype(BF16)
    o_ref[...] = x + _rms(_dot(ox, wo_ref[...]), g3_ref[...])


def cross_block(x, mk, mv, wq, wo, g2, g3, bsz, s, tm, mem_row0):
    m = x.shape[0]
    nt = s // tm
    vec = pl.BlockSpec((1, D_MODEL), lambda b, i: (0, 0))
    return pl.pallas_call(
        _cross_kernel,
        out_shape=jax.ShapeDtypeStruct((m, D_MODEL), F32),
        grid=(bsz, nt),
        in_specs=[pl.BlockSpec((tm, D_MODEL), lambda b, i: (b * nt + i, 0)),
                  pl.BlockSpec((N_MEM, X_HEADS * X_DIM), lambda b, i: (mem_row0 + b, 0)),
                  pl.BlockSpec((N_MEM, X_HEADS * X_DIM), lambda b, i: (mem_row0 + b, 0)),
                  _resident(wq.shape), _resident(wo.shape), vec, vec],
        out_specs=pl.BlockSpec((tm, D_MODEL), lambda b, i: (b * nt + i, 0)),
        compiler_params=_params(("parallel", "parallel")),
        name="cross_attn",
    )(x, mk, mv, wq, wo, g2, g3)


def _mlp_kernel(x_ref, wu_ref, wd_ref, g4_ref, g5_ref, o_ref, h_ref, acc_ref):
    j = pl.program_id(1)

    @pl.when(j == 0)
    def _():
        h_ref[...] = _rms(x_ref[...], g4_ref[...]).astype(BF16)
        acc_ref[...] = jnp.zeros_like(acc_ref)

    cols = pl.ds(pl.multiple_of(j * D_MODEL, D_MODEL), D_MODEL)
    u = jnp.square(jnp.maximum(_dot(h_ref[...], wu_ref[:, cols]), 0.0)).astype(BF16)
    acc_ref[...] += _dot(u, wd_ref[j])

    @pl.when(j == pl.num_programs(1) - 1)
    def _():
        o_ref[...] = x_ref[...] + _rms(acc_ref[...], g5_ref[...])


def mlp_block(x, wu3, wd3, g4, g5, tm):
    m = x.shape[0]
    nj = wd3.shape[0]
    vec = pl.BlockSpec((1, D_MODEL), lambda i, j: (0, 0))
    return pl.pallas_call(
        _mlp_kernel,
        out_shape=jax.ShapeDtypeStruct((m, D_MODEL), F32),
        grid=(m // tm, nj),
        in_specs=[pl.BlockSpec((tm, D_MODEL), lambda i, j: (i, 0)),
                  _resident(wu3.shape), _resident(wd3.shape), vec, vec],
        out_specs=pl.BlockSpec((tm, D_MODEL), lambda i, j: (i, 0)),
        scratch_shapes=[pltpu.VMEM((tm, D_MODEL), BF16), pltpu.VMEM((tm, D_MODEL), F32)],
        compiler_params=_params(("parallel", "arbitrary")),
        name="mlp",
    )(x, wu3, wd3, g4, g5)


def _prep_layer_weights(w_in, w_mla_uq, w_mla_ukv, w_branch, w_out, w_xq, w_mem_k, w_mem_v, w_xo,
                        w_up, w_down):
    idx = np.cumsum((0,) + IN_SIZES)
    seg = lambda i: w_in[:, idx[i]:idx[i + 1]]
    fq, fk, fv, ff, cq, ckv, kpe, hq, hf, hi, hg, ga, gb, gc = (seg(i) for i in range(14))
    half = MLA_ROPE // 2
    kpe_sw = jnp.concatenate([kpe[:, half:], kpe[:, :half]], axis=1)
    pad = jnp.zeros((D_MODEL, IN_TN - HV_W), w_in.dtype)
    w_p = jnp.concatenate([hq, hf, cq, ckv, hi, hg, fq, pad], axis=1).astype(BF16)
    w_gate = jnp.stack([ga, gb, gc]).astype(BF16)
    w_t = jnp.concatenate([fk, fv, kpe, kpe_sw, ff], axis=1).T.astype(BF16)
    hd = MLA_NOPE + MLA_ROPE
    zq = jnp.zeros((MLA_Q_RANK, LANE - MLA_ROPE), w_mla_uq.dtype)
    nope, rope_n, rope_s = [], [], []
    for h in range(MLA_HEADS):
        base = h * hd
        nope.append(w_mla_uq[:, base:base + MLA_NOPE])
        x1 = w_mla_uq[:, base + MLA_NOPE:base + MLA_NOPE + half]
        x2 = w_mla_uq[:, base + MLA_NOPE + half:base + hd]
        rope_n += [x1, x2, zq]
        rope_s += [x2, x1, zq]
    wuq = jnp.concatenate(nope + rope_n + rope_s, axis=1).astype(BF16)
    kvd = MLA_NOPE + MLA_V
    wkt = jnp.concatenate([w_mla_ukv[:, h * kvd:h * kvd + MLA_NOPE] for h in range(MLA_HEADS)],
                          axis=1).T.astype(BF16)
    wv = jnp.concatenate([w_mla_ukv[:, h * kvd + MLA_NOPE:(h + 1) * kvd] for h in range(MLA_HEADS)],
                         axis=1).astype(BF16)
    nff = D_FF // D_MODEL
    return dict(
        w_in3=w_p, w_t=w_t, w_gate=w_gate, wuq=wuq, wkt=wkt, wv=wv,
        wb=w_branch.astype(BF16), wo=w_out.astype(BF16), wxq=w_xq.astype(BF16), wxo=w_xo.astype(BF16),
        wmem=jnp.concatenate([w_mem_k, w_mem_v], axis=1).astype(BF16),
        wu3=w_up.astype(BF16),
        wd3=w_down.astype(BF16).reshape(nff, D_MODEL, D_MODEL))


def _rope_tables(pos, reps):
    half = MLA_ROPE // 2
    freq = ROPE_THETA ** (-jnp.arange(half, dtype=F32) / half)
    ang = pos.astype(F32)[:, None] * freq[None, :]
    cos, sin = jnp.cos(ang), jnp.sin(ang)
    z = jnp.zeros((pos.shape[0], LANE - MLA_ROPE), F32)
    cos_r = jnp.tile(jnp.concatenate([cos, cos, z], axis=1), (reps, 1))
    sin_r = jnp.tile(jnp.concatenate([-sin, sin, z], axis=1), (reps, 1))
    return cos_r, sin_r, cos_r[:, :MLA_ROPE].T, sin_r[:, :MLA_ROPE].T


def _tile(n, pref):
    t = min(n, pref)
    assert n % t == 0
    return t


def _layer(x, bsz, s, pos0, w, lb, b_fox, g_q, g_kv, g_hout, g_norm, mem_k, mem_v, mem_row0, past, cfg,
           layer, depth, shared):
    m = bsz * s
    g = lambda i: g_norm[i][None, :]
    tm_in = _tile(s, cfg["tm_in"])
    tm_p = _tile(s, cfg["tm_prep"])
    cos_r, sin_r, cos_c, sin_c = _rope_tables(pos0 + jnp.arange(s), 1)
    hqf, mla_in, hv, kt, vt, kpet, logft = in_proj(x, g(0), w["w_in3"], w["w_t"], cos_c, sin_c,
                                                   b_fox[:, None], bsz, s, tm_in, layer, depth, shared[:4])
    qx, knt, v, ckv_n = mla_prep(mla_in, cos_r, sin_r, g_q[None, :], g_kv[None, :],
                                 w["wuq"], w["wkt"], w["wv"], bsz, s, tm_p, layer, depth, shared[4:])
    row0 = layer * bsz
    rows = lambda a: a.reshape((depth * bsz,) + a.shape[2:])
    cumt = fox_cumsum(rows(logft), row0, bsz)
    if past is None:
        t = _tile(s, cfg["t_attn"])
        o_fox = fox_prompt(hv, rows(kt), rows(vt), cumt, row0, bsz, s, t)
        o_mla = mla_prompt(qx, knt, rows(kpet), v, row0, bsz, s, t)
        o_hg, hg_state = hgrn(hqf, hv, lb[None, :], g_hout[None, :], bsz, s, CHUNK,
                              _tile(s, cfg["hg_rows"]))
    else:
        c_kt, c_vt, c_cumt, c_knt, c_kpt, c_v, c_hg = past
        p = c_kt.shape[2]
        o_fox = fox_sample(hv, rows(kt), rows(vt), c_kt, c_vt, cumt, c_cumt, bsz, s, p, layer)
        o_mla = mla_sample(qx, knt, rows(kpet), v, c_knt, c_kpt, c_v, bsz, s, p, layer)
        o_hg, hg_state = hgrn(hqf, hv, lb[None, :], g_hout[None, :], bsz, s, s, s, s0=c_hg, s0_row0=row0)
    x = merge_out(o_fox, o_mla, o_hg, x, w["w_gate"], w["wb"], w["wo"], g(0), g(1),
                  _tile(m, cfg["tm_merge"]))
    x = cross_block(x, mem_k, mem_v, w["wxq"], w["wxo"], g(2), g(3), bsz, s, _tile(s, cfg["tm_cross"]),
                    mem_row0)
    x = mlp_block(x, w["wu3"], w["wd3"], g(4), g(5), _tile(m, cfg["tm_mlp"]))
    return x, (kt, vt, kpet, logft, ckv_n), hg_state


def _from_feature_major(stacked, heads):
    a = jnp.swapaxes(stacked, 2, 3)
    if heads:
        a = a.reshape(a.shape[:3] + (heads, a.shape[3] // heads))
    return a


def _assemble_states(shared, hg_states, bsz, s):
    kt, vt, kpet, logft, ckv = shared
    return (_from_feature_major(kt, FOX_HEADS), _from_feature_major(vt, FOX_HEADS),
            _from_feature_major(logft, 0), ckv.reshape(ckv.shape[0], bsz, s, MLA_KV_RANK),
            _from_feature_major(kpet, 0), jnp.stack(hg_states))


_CFG = dict(tm_in=1024, tm_prep=512, t_attn=512, hg_rows=512, tm_merge=512, tm_cross=512,
            tm_mlp=1024, tm_mem=512, tm_expand=4096)


def kernel(x_prompt, x_sample, cache_fox_k, cache_fox_v, cache_fox_logf, cache_mla_ckv, cache_mla_kpe,
           state_hgrn, cache_mem_k, cache_mem_v, mem_prompt, w_in, b_fox, g_mla_q, w_mla_uq, g_mla_kv,
           w_mla_ukv, g_hgrn_out, lb_hgrn, w_branch, w_out, w_xq, w_mem_k, w_mem_v, w_xo, w_up, w_down,
           g_norm):
    cfg = _CFG
    depth = w_in.shape[0]
    lb_p = jax.nn.softmax(lb_hgrn.astype(F32), axis=0)
    lb_all = jnp.cumsum(lb_p, axis=0) - lb_p[0]
    ws = [_prep_layer_weights(w_in[l], w_mla_uq[l], w_mla_ukv[l], w_branch[l], w_out[l], w_xq[l],
                              w_mem_k[l], w_mem_v[l], w_xo[l], w_up[l], w_down[l]) for l in range(depth)]

    def run_layer(x, bsz, s, pos0, l, mk, mv, mem_row0, past, shared):
        return _layer(x, bsz, s, pos0, ws[l], lb_all[l], b_fox[l], g_mla_q[l], g_mla_kv[l],
                      g_hgrn_out[l], g_norm[l], mk, mv, mem_row0, past, cfg, l, depth, shared)

    bp, sp, _ = x_prompt.shape
    x = x_prompt.reshape(bp * sp, D_MODEL)
    mem = mem_prompt.reshape(bp * N_MEM, D_MODEL)
    shared, hg_states, p_mem = (), [], []
    for l in range(depth):
        mk, mv = mem_kv(mem, ws[l]["wmem"], _tile(bp * N_MEM, cfg["tm_mem"]))
        x, shared, hg = run_layer(x, bp, sp, 0, l, mk, mv, 0, None, shared)
        hg_states.append(hg)
        p_mem.append((mk.reshape(bp, N_MEM, X_HEADS, X_DIM), mv.reshape(bp, N_MEM, X_HEADS, X_DIM)))
    y_prompt = x.reshape(bp, sp, D_MODEL)
    p_out = _assemble_states(shared, hg_states, bp, sp) + tuple(jnp.stack(a) for a in zip(*p_mem))

    bs, ts, _ = x_sample.shape
    p = cache_fox_k.shape[2]
    fm = lambda c: jnp.moveaxis(c, 2, -1)
    c_kt = fm(cache_fox_k).reshape(depth * bs, BRANCH_W, p)
    c_vt = fm(cache_fox_v).reshape(depth * bs, BRANCH_W, p)
    c_kpt = fm(cache_mla_kpe).reshape(depth * bs, MLA_ROPE, p)
    c_cumt = fox_cumsum(fm(cache_fox_logf).reshape(depth * bs, FOX_HEADS, p), 0, depth * bs)
    c_ckv = cache_mla_ckv.reshape(depth * bs * p, MLA_KV_RANK)
    c_hg = state_hgrn.reshape((depth * bs,) + state_hgrn.shape[2:])
    c_mk = cache_mem_k.reshape(depth * bs * N_MEM, X_HEADS * X_DIM)
    c_mv = cache_mem_v.reshape(depth * bs * N_MEM, X_HEADS * X_DIM)
    x = x_sample.reshape(bs * ts, D_MODEL)
    shared, hg_states = (), []
    for l in range(depth):
        c_knt, c_v = latent_expand(c_ckv, ws[l]["wkt"], ws[l]["wv"], bs, p, _tile(p, cfg["tm_expand"]), l)
        past = (c_kt, c_vt, c_cumt, c_knt, c_kpt, c_v, c_hg)
        x, shared, hg = run_layer(x, bs, ts, p, l, c_mk, c_mv, l * bs, past, shared)
        hg_states.append(hg)
    y_sample = x.reshape(bs, ts, D_MODEL)
    return (y_prompt, y_sample, *p_out, *_assemble_states(shared, hg_states, bs, ts))
```

```python
import functools

import numpy as np
import jax
import jax.numpy as jnp
from jax import lax
from jax.experimental import pallas as pl
from jax.experimental.pallas import tpu as pltpu

F32 = jnp.float32
BF16 = jnp.bfloat16

D_MODEL = 1024
CHUNK = 64
N_MEM = 256
EPS = 1e-6
NEG_BIG = -1e30
EXP_CLIP = 80.0
FOX_HEADS = 8
FOX_DIM = 64
MLA_HEADS = 4
MLA_Q_RANK = 384
MLA_KV_RANK = 256
MLA_NOPE = 128
MLA_ROPE = 64
MLA_V = 128
ROPE_THETA = 10000.0
HG_HEADS = 4
HG_DK = 128
HG_DV = 128
X_HEADS = 4
X_DIM = 128
D_FF = 4 * D_MODEL
BRANCH_W = 512
IN_SIZES = (512, 512, 512, FOX_HEADS, MLA_Q_RANK, MLA_KV_RANK, MLA_ROPE, 512, 512, 512, 512,
            D_MODEL, D_MODEL, D_MODEL)

LANE = 128
SUB_BLOCK = 16
VMEM_LIMIT = 56 * 1024 * 1024
LOG2E = 1.4426950408889634

IN_TN = 1664
HQF_W, MLA_W, HV_W = 1024, MLA_Q_RANK + MLA_KV_RANK, 1536
OFF_FQ = 1024
T_FK, T_FV, T_KPE, T_KPE_SW, T_FF, NT_IN = 0, 512, 1024, 1088, 1152, 1160


def _params(sem, vmem=VMEM_LIMIT):
    return pltpu.CompilerParams(dimension_semantics=sem, vmem_limit_bytes=vmem)


def _dot(a, b):
    return jnp.dot(a, b, preferred_element_type=F32)


def _dot_nt(a, b):
    return lax.dot_general(a, b, (((1,), (1,)), ((), ())), preferred_element_type=F32)


def _dot_tn(a, b):
    return lax.dot_general(a, b, (((0,), (0,)), ((), ())), preferred_element_type=F32)


def _rms(x, g):
    y = x * lax.rsqrt(jnp.mean(x * x, axis=-1, keepdims=True) + EPS)
    return y * g


def _log_sigmoid(z):
    return jnp.minimum(z, 0.0) - jnp.log(1.0 + jnp.exp(-jnp.abs(z)))


def _sigmoid(z):
    return 1.0 / (1.0 + jnp.exp(-z))


def _resident(shape):
    nd = len(shape)
    return pl.BlockSpec(shape, lambda *_: (0,) * nd, pipeline_mode=pl.Buffered(1))


def _in_proj_kernel(x_ref, g_ref, w_ref, wt_ref, cos_ref, sin_ref, bf_ref, *rest):
    hqf_ref, mla_ref, hv_ref, kt_ref, vt_ref, kpe_ref, lf_ref, h_ref = rest[-8:]
    j = pl.program_id(1)

    @pl.when(j == 0)
    def _():
        h = _rms(x_ref[...], g_ref[...]).astype(BF16)
        h_ref[...] = h
        yt = _dot_nt(wt_ref[...], h)
        kt_ref[0] = yt[T_FK:T_FK + BRANCH_W]
        vt_ref[0] = yt[T_FV:T_FV + BRANCH_W]
        kpe_ref[0] = (yt[T_KPE:T_KPE + MLA_ROPE] * cos_ref[...]
                      + yt[T_KPE_SW:T_KPE_SW + MLA_ROPE] * sin_ref[...])
        lf_ref[0] = _log_sigmoid(yt[T_FF:T_FF + FOX_HEADS] + bf_ref[...])
        y = _dot(h, w_ref[:, :IN_TN])
        hqf_ref[...] = y[:, :HQF_W]
        mla_ref[...] = y[:, HQF_W:]

    @pl.when(j == 1)
    def _():
        hv_ref[...] = _dot(h_ref[...], w_ref[:, IN_TN:])[:, :HV_W].astype(BF16)


def in_proj(x, g, w3, wt, cos_t, sin_t, b_col, bsz, s, tm, layer, depth, prev):
    m, k = x.shape
    assert w3.shape[1] == 2 * IN_TN and wt.shape[0] == NT_IN
    nt = s // tm
    ntab = cos_t.shape[1] // tm
    feats = (BRANCH_W, BRANCH_W, MLA_ROPE, FOX_HEADS)
    tspec = lambda rows: pl.BlockSpec((None, 1, rows, tm), lambda i, j: (layer, i // nt, 0, i % nt))
    row = lambda w: pl.BlockSpec((tm, w), lambda i, j: (i, 0))
    n_in, n_row = 7, 3
    return pl.pallas_call(
        _in_proj_kernel,
        out_shape=(jax.ShapeDtypeStruct((m, HQF_W), F32), jax.ShapeDtypeStruct((m, MLA_W), F32),
                   jax.ShapeDtypeStruct((m, HV_W), BF16))
        + tuple(jax.ShapeDtypeStruct((depth, bsz, f, s), F32) for f in feats),
        grid=(m // tm, 2),
        in_specs=[pl.BlockSpec((tm, k), lambda i, j: (i, 0)),
                  pl.BlockSpec((1, k), lambda i, j: (0, 0)),
                  _resident(w3.shape),
                  _resident(wt.shape),
                  pl.BlockSpec((MLA_ROPE, tm), lambda i, j: (0, i % ntab)),
                  pl.BlockSpec((MLA_ROPE, tm), lambda i, j: (0, i % ntab)),
                  pl.BlockSpec((FOX_HEADS, 1), lambda i, j: (0, 0))]
        + [pl.BlockSpec(memory_space=pl.ANY)] * len(prev),
        out_specs=(row(HQF_W), row(MLA_W), row(HV_W)) + tuple(tspec(f) for f in feats),
        scratch_shapes=[pltpu.VMEM((tm, k), BF16)],
        input_output_aliases={n_in + i: n_row + i for i in range(len(prev))},
        compiler_params=_params(("parallel", "arbitrary")),
        name="in_proj",
    )(x, g, w3, wt, cos_t, sin_t, b_col, *prev)


def _mla_prep_kernel(in_ref, cs_ref, sn_ref, gq_ref, gkv_ref, wuq_ref, wkt_ref, wv_ref, *rest):
    qx_ref, knt_ref, v_ref, ckvn_ref = rest[-4:]
    cos_t = cs_ref[...]
    sin_t = sn_ref[...]
    qn = _rms(in_ref[:, :MLA_Q_RANK], gq_ref[...]).astype(BF16)
    qall = _dot(qn, wuq_ref[...]) * ((MLA_NOPE + MLA_ROPE) ** -0.5 * LOG2E)
    for h in range(MLA_HEADS):
        lo = h * LANE
        qr = (qall[:, 512 + lo:512 + lo + LANE] * cos_t
              + qall[:, 1024 + lo:1024 + lo + LANE] * sin_t)
        qx_ref[:, 2 * lo:2 * lo + LANE] = qall[:, lo:lo + LANE].astype(BF16)
        qx_ref[:, 2 * lo + LANE:2 * lo + 2 * LANE] = qr.astype(BF16)
    ckvn = _rms(in_ref[:, MLA_Q_RANK:], gkv_ref[...])
    ckvn_ref[...] = ckvn
    cb = ckvn.astype(BF16)
    knt_ref[0] = _dot_nt(wkt_ref[...], cb).astype(BF16)
    v_ref[...] = _dot(cb, wv_ref[...]).astype(BF16)


def mla_prep(mla_in, cos_t, sin_t, gq, gkv, wuq, wkt, wv, bsz, s, tm, layer, depth, prev):
    m = mla_in.shape[0]
    nt = s // tm
    ntab = cos_t.shape[0] // tm
    row = lambda w: pl.BlockSpec((tm, w), lambda i: (i, 0))
    full = lambda a: pl.BlockSpec(a.shape, lambda i: (0,) * a.ndim)
    n_in = 8
    return pl.pallas_call(
        _mla_prep_kernel,
        out_shape=(jax.ShapeDtypeStruct((m, 1024), BF16),
                   jax.ShapeDtypeStruct((bsz, 512, s), BF16),
                   jax.ShapeDtypeStruct((m, 512), BF16),
                   jax.ShapeDtypeStruct((depth, m, MLA_KV_RANK), F32)),
        grid=(m // tm,),
        in_specs=[row(MLA_W),
                  pl.BlockSpec((tm, LANE), lambda i: (i % ntab, 0)),
                  pl.BlockSpec((tm, LANE), lambda i: (i % ntab, 0)),
                  full(gq), full(gkv), full(wuq), full(wkt), full(wv)]
        + [pl.BlockSpec(memory_space=pl.ANY)] * len(prev),
        out_specs=(row(1024), pl.BlockSpec((1, 512, tm), lambda i: (i // nt, 0, i % nt)),
                   row(512), pl.BlockSpec((None, tm, MLA_KV_RANK), lambda i: (layer, i, 0))),
        input_output_aliases={n_in + i: 3 + i for i in range(len(prev))},
        compiler_params=_params(("parallel",)),
        name="mla_prep",
    )(mla_in, cos_t, sin_t, gq, gkv, wuq, wkt, wv, *prev)


def _cumsum_kernel(x_ref, c_ref, *, w):
    s = x_ref.shape[1]
    r = lax.broadcasted_iota(jnp.int32, (w, w), 0)
    c = lax.broadcasted_iota(jnp.int32, (w, w), 1)
    upper = (r <= c).astype(F32)
    local = [jnp.dot(x_ref[:, g * w:(g + 1) * w], upper, preferred_element_type=F32,
                     precision=lax.Precision.HIGHEST) for g in range(s // w)]
    carry = jnp.zeros((x_ref.shape[0], 1), F32)
    for g, cum in enumerate(local):
        cum = cum + carry
        c_ref[:, g * w:(g + 1) * w] = cum
        carry = cum[:, w - 1:w]


def fox_cumsum(x, row0, bsz):
    n, h, s = x.shape
    rows = bsz * h
    assert (row0 * h) % rows == 0
    out = pl.pallas_call(
        functools.partial(_cumsum_kernel, w=min(LANE, s)),
        out_shape=jax.ShapeDtypeStruct((rows, s), F32),
        grid=(1,),
        in_specs=[pl.BlockSpec((rows, s), lambda i: (row0 * h // rows, 0))],
        out_specs=pl.BlockSpec((rows, s), lambda i: (0, 0)),
        compiler_params=_params(("arbitrary",)),
        name="fox_cumsum",
    )(x.reshape(n * h, s))
    return out.reshape(bsz, h, s)


def _pair_rows_mask(hh):
    sub = lax.broadcasted_iota(jnp.int32, (LANE, 1), 0)
    return (sub < FOX_DIM) if hh == 0 else (sub >= FOX_DIM)


def _fox_finish(acc0, acc1):
    lane = lax.broadcasted_iota(jnp.int32, (1, LANE), 1)
    o0 = acc0 / pltpu.roll(acc0, FOX_DIM, axis=1)
    o1 = acc1 / pltpu.roll(acc1, FOX_DIM, axis=1)
    return jnp.where(lane < FOX_DIM, o0, o1)


def _causal_schedule(nq):
    todo = {i: list(range(i + 1)) for i in range(nq)}
    order = []
    while any(todo.values()):
        for i in reversed(range(nq)):
            if todo[i]:
                order.append((i, todo[i].pop(0)))
    return order


def _fox_prompt_kernel(q_ref, kt_ref, vt_ref, ct_ref, o_ref, *, t, nq):
    hp = pl.program_id(1)
    lane = lax.broadcasted_iota(jnp.int32, (1, LANE), 1)
    lo = lane < FOX_DIM
    span = lambda i: slice(i * t, (i + 1) * t)
    ct = [ct_ref[0, 2 * hp + hh] for hh in range(2)]
    qs, cref = [], []
    for i in range(nq):
        q = q_ref[span(i), :].astype(F32) * (FOX_DIM ** -0.5 * LOG2E)
        qs.append((jnp.where(lo, q, 0.0).astype(BF16), jnp.where(lo, 0.0, q).astype(BF16)))
        cref.append([c[:, i * t:i * t + 1] for c in ct])
    kts = [kt_ref[0, :, span(j)].astype(BF16) for j in range(nq)]
    vts = [[jnp.where(_pair_rows_mask(hh), vt_ref[0, :, span(j)], 1.0).astype(BF16) for hh in range(2)]
           for j in range(nq)]
    causal = (lax.broadcasted_iota(jnp.int32, (t, t), 0) >= lax.broadcasted_iota(jnp.int32, (t, t), 1))

    def scores(i, j):
        out = []
        for hh in range(2):
            s = _dot(qs[i][hh], kts[j]) + (cref[i][hh] - ct[hh][:, span(j)]) * LOG2E
            out.append(jnp.where(causal, s, NEG_BIG) if i == j else s)
        return out

    state = [[(jnp.full((t, 1), NEG_BIG, F32), jnp.zeros((t, LANE), F32)) for _ in range(2)]
             for _ in range(nq)]
    order = _causal_schedule(nq)
    ss = scores(*order[0])
    for n, (i, j) in enumerate(order):
        nxt = scores(*order[n + 1]) if n + 1 < len(order) else None
        m_new = [jnp.maximum(state[i][hh][0], jnp.max(ss[hh], axis=1, keepdims=True)) for hh in range(2)]
        ps = [jnp.exp2(ss[hh] - m_new[hh]).astype(BF16) for hh in range(2)]
        pvs = [_dot_nt(ps[hh], vts[j][hh]) for hh in range(2)]
        state[i] = [(m_new[hh], jnp.exp2(state[i][hh][0] - m_new[hh]) * state[i][hh][1] + pvs[hh])
                    for hh in range(2)]
        ss = nxt
    for i in range(nq):
        o_ref[span(i), :] = _fox_finish(state[i][0][1], state[i][1][1]).astype(BF16)


def fox_prompt(hv, kt, vt, cumt, row0, bsz, s, t):
    m = hv.shape[0]
    return pl.pallas_call(
        functools.partial(_fox_prompt_kernel, t=t, nq=s // t),
        out_shape=jax.ShapeDtypeStruct((m, BRANCH_W), BF16),
        grid=(bsz, FOX_HEADS // 2),
        in_specs=[pl.BlockSpec((s, LANE), lambda b, h: (b, OFF_FQ // LANE + h)),
                  pl.BlockSpec((1, LANE, s), lambda b, h: (row0 + b, h, 0)),
                  pl.BlockSpec((1, LANE, s), lambda b, h: (row0 + b, h, 0)),
                  pl.BlockSpec((1, FOX_HEADS, 1, s), lambda b, h: (b, 0, 0, 0))],
        out_specs=pl.BlockSpec((s, LANE), lambda b, h: (b, h)),
        compiler_params=_params(("parallel", "parallel")),
        name="fox_prompt",
    )(hv, kt, vt, cumt.reshape(bsz, FOX_HEADS, 1, s))


def _mla_keys(knt, kpt):
    n = knt.shape[1]
    return jnp.concatenate([knt, kpt.astype(BF16), jnp.zeros((LANE - MLA_ROPE, n), BF16)], axis=0)


def _mla_prompt_kernel(q_ref, knt_ref, kpt_ref, v_ref, o_ref, *, t, nq):
    hs = range(2)
    span = lambda i: slice(i * t, (i + 1) * t)
    cols = lambda hh, w: slice(hh * w, (hh + 1) * w)
    keys = [[_mla_keys(knt_ref[0, cols(hh, LANE), span(j)], kpt_ref[0, :, span(j)]) for hh in hs]
            for j in range(nq)]
    mask = (lax.broadcasted_iota(jnp.int32, (t, t), 0) // CHUNK
            >= lax.broadcasted_iota(jnp.int32, (t, t), 1) // CHUNK)

    def scores(i, j):
        ss = [_dot(q_ref[span(i), cols(hh, 2 * LANE)], keys[j][hh]) for hh in hs]
        return [jnp.where(mask, s, NEG_BIG) for s in ss] if i == j else ss

    state = [[(jnp.full((t, 1), NEG_BIG, F32), jnp.zeros((t, 1), F32), jnp.zeros((t, LANE), F32))
              for _ in hs] for _ in range(nq)]
    order = _causal_schedule(nq)
    ss = scores(*order[0])
    for n, (i, j) in enumerate(order):
        nxt = scores(*order[n + 1]) if n + 1 < len(order) else None
        m_new = [jnp.maximum(state[i][hh][0], jnp.max(ss[hh], axis=1, keepdims=True)) for hh in hs]
        ps = [jnp.exp2(ss[hh] - m_new[hh]) for hh in hs]
        pvs = [_dot(ps[hh].astype(BF16), v_ref[span(j), cols(hh, LANE)]) for hh in hs]
        new = []
        for hh in hs:
            alpha = jnp.exp2(state[i][hh][0] - m_new[hh])
            new.append((m_new[hh], alpha * state[i][hh][1] + jnp.sum(ps[hh], axis=1, keepdims=True),
                        alpha * state[i][hh][2] + pvs[hh]))
        state[i] = new
        ss = nxt
    for i in range(nq):
        o_ref[span(i), :] = jnp.concatenate([state[i][hh][2] / state[i][hh][1] for hh in hs],
                                            axis=1).astype(BF16)


def mla_prompt(qx, knt, kpt, v, row0, bsz, s, t):
    assert t % CHUNK == 0
    m = qx.shape[0]
    return pl.pallas_call(
        functools.partial(_mla_prompt_kernel, t=t, nq=s // t),
        out_shape=jax.ShapeDtypeStruct((m, BRANCH_W), BF16),
        grid=(bsz, MLA_HEADS // 2),
        in_specs=[pl.BlockSpec((s, 4 * LANE), lambda b, h: (b, h)),
                  pl.BlockSpec((1, 2 * LANE, s), lambda b, h: (b, h, 0)),
                  pl.BlockSpec((1, MLA_ROPE, s), lambda b, h: (row0 + b, 0, 0)),
                  pl.BlockSpec((s, 2 * LANE), lambda b, h: (b, h))],
        out_specs=pl.BlockSpec((s, 2 * LANE), lambda b, h: (b, h)),
        compiler_params=_params(("parallel", "parallel")),
        name="mla_prompt",
    )(qx, knt, kpt, v)


def _fox_sample_kernel(q_ref, ktn_ref, vtn_ref, ktc_ref, vtc_ref, ctn_ref, ctc_ref, o_ref, *, t, p):
    hp = pl.program_id(1)
    lane = lax.broadcasted_iota(jnp.int32, (1, LANE), 1)
    lo = lane < FOX_DIM
    q = q_ref[...].astype(F32) * (FOX_DIM ** -0.5 * LOG2E)
    kt_c = ktc_ref[0].astype(BF16)
    kt_n = ktn_ref[0].astype(BF16)
    vt_c = vtc_ref[0]
    vt_n = vtn_ref[0]
    causal = (lax.broadcasted_iota(jnp.int32, (t, t), 0) >= lax.broadcasted_iota(jnp.int32, (t, t), 1))
    accs = []
    for hh in range(2):
        head = 2 * hp + hh
        qh = (jnp.where(lo, q, 0.0) if hh == 0 else jnp.where(lo, 0.0, q)).astype(BF16)
        cc = ctc_ref[0, pl.ds(head, 1), :]
        ctot = cc[:, p - 1:p]
        s_c = _dot(qh, kt_c) + (ctot - cc) * LOG2E
        s_n = _dot(qh, kt_n) - ctn_ref[0, pl.ds(head, 1), :] * LOG2E
        s_n = jnp.where(causal, s_n, NEG_BIG)
        m = jnp.maximum(jnp.max(s_c, axis=1, keepdims=True), jnp.max(s_n, axis=1, keepdims=True))
        rows = _pair_rows_mask(hh)
        accs.append(_dot_nt(jnp.exp2(s_c - m).astype(BF16), jnp.where(rows, vt_c, 1.0).astype(BF16))
                    + _dot_nt(jnp.exp2(s_n - m).astype(BF16), jnp.where(rows, vt_n, 1.0).astype(BF16)))
    o_ref[...] = _fox_finish(accs[0], accs[1]).astype(BF16)


def fox_sample(hv, kt_n, vt_n, kt_c, vt_c, cumt_n, cumt_c, bsz, t, p, layer):
    cidx = lambda b, h: (layer * bsz + b, h, 0)
    return pl.pallas_call(
        functools.partial(_fox_sample_kernel, t=t, p=p),
        out_shape=jax.ShapeDtypeStruct((bsz * t, BRANCH_W), BF16),
        grid=(bsz, FOX_HEADS // 2),
        in_specs=[pl.BlockSpec((t, LANE), lambda b, h: (b, OFF_FQ // LANE + h)),
                  pl.BlockSpec((1, LANE, t), cidx),
                  pl.BlockSpec((1, LANE, t), cidx),
                  pl.BlockSpec((1, LANE, p), cidx),
                  pl.BlockSpec((1, LANE, p), cidx),
                  pl.BlockSpec((1, FOX_HEADS, t), lambda b, h: (b, 0, 0)),
                  pl.BlockSpec((1, FOX_HEADS, p), lambda b, h: (layer * bsz + b, 0, 0))],
        out_specs=pl.BlockSpec((t, LANE), lambda b, h: (b, h)),
        compiler_params=_params(("parallel", "parallel")),
        name="fox_sample",
    )(hv, kt_n, vt_n, kt_c, vt_c, cumt_n, cumt_c)


def _mla_sample_kernel(q_ref, kntn_ref, kptn_ref, vn_ref, kntc_ref, kptc_ref, vc_ref, o_ref, *, t, p):
    q = q_ref[...]
    s_c = _dot(q, _mla_keys(kntc_ref[0], kptc_ref[0]))
    s_n = _dot(q, _mla_keys(kntn_ref[0], kptn_ref[0]))
    qc = (p + lax.broadcasted_iota(jnp.int32, (t, t), 0)) // CHUNK
    kc = (p + lax.broadcasted_iota(jnp.int32, (t, t), 1)) // CHUNK
    s_n = jnp.where(qc >= kc, s_n, NEG_BIG)
    m = jnp.maximum(jnp.max(s_c, axis=1, keepdims=True), jnp.max(s_n, axis=1, keepdims=True))
    p_c = jnp.exp2(s_c - m)
    p_n = jnp.exp2(s_n - m)
    l = jnp.sum(p_c, axis=1, keepdims=True) + jnp.sum(p_n, axis=1, keepdims=True)
    o = _dot(p_c.astype(BF16), vc_ref[...]) + _dot(p_n.astype(BF16), vn_ref[...])
    o_ref[...] = (o / l).astype(BF16)


def mla_sample(qx, knt_n, kpt_n, v_n, knt_c, kpt_c, v_c, bsz, t, p, layer):
    assert (p - 1) // CHUNK <= p // CHUNK
    return pl.pallas_call(
        functools.partial(_mla_sample_kernel, t=t, p=p),
        out_shape=jax.ShapeDtypeStruct((bsz * t, BRANCH_W), BF16),
        grid=(bsz, MLA_HEADS),
        in_specs=[pl.BlockSpec((t, 2 * LANE), lambda b, h: (b, h)),
                  pl.BlockSpec((1, LANE, t), lambda b, h: (b, h, 0)),
                  pl.BlockSpec((1, MLA_ROPE, t), lambda b, h: (layer * bsz + b, 0, 0)),
                  pl.BlockSpec((t, LANE), lambda b, h: (b, h)),
                  pl.BlockSpec((1, LANE, p), lambda b, h: (b, h, 0)),
                  pl.BlockSpec((1, MLA_ROPE, p), lambda b, h: (layer * bsz + b, 0, 0)),
                  pl.BlockSpec((p, LANE), lambda b, h: (b, h))],
        out_specs=pl.BlockSpec((t, LANE), lambda b, h: (b, h)),
        compiler_params=_params(("parallel", "parallel")),
        name="mla_sample",
    )(qx, knt_n, kpt_n, v_n, knt_c, kpt_c, v_c)


def _latent_expand_kernel(c_ref, wkt_ref, wv_ref, knt_ref, v_ref):
    cb = c_ref[...].astype(BF16)
    knt_ref[0] = _dot_nt(wkt_ref[...], cb).astype(BF16)
    v_ref[...] = _dot(cb, wv_ref[...]).astype(BF16)


def latent_expand(ckv, wkt, wv, bsz, p, tm, layer):
    m = bsz * p
    nt = p // tm
    return pl.pallas_call(
        _latent_expand_kernel,
        out_shape=(jax.ShapeDtypeStruct((bsz, 512, p), BF16), jax.ShapeDtypeStruct((m, 512), BF16)),
        grid=(m // tm,),
        in_specs=[pl.BlockSpec((tm, MLA_KV_RANK), lambda i: (layer * (m // tm) + i, 0)),
                  pl.BlockSpec(wkt.shape, lambda i: (0, 0)), pl.BlockSpec(wv.shape, lambda i: (0, 0))],
        out_specs=(pl.BlockSpec((1, 512, tm), lambda i: (i // nt, 0, i % nt)),
                   pl.BlockSpec((tm, 512), lambda i: (i, 0))),
        compiler_params=_params(("parallel",)),
        name="latent_expand",
    )(ckv, wkt, wv)


def _hgrn_gates(z, lb, tri):
    logf = _log_sigmoid(z) + jnp.log(1.0 + lb * jnp.exp(jnp.minimum(-z, EXP_CLIP)))
    k = (1.0 - lb) * (1.0 / (1.0 + jnp.exp(z)))
    h1 = logf.astype(BF16)
    r1 = logf - h1.astype(F32)
    h2 = r1.astype(BF16)
    h3 = (r1 - h2.astype(F32)).astype(BF16)
    hcat = jnp.concatenate([h1, h2, h3], axis=1)
    g = tri.shape[0]
    parts = jnp.concatenate([_dot(tri, hcat[r:r + g, :]) for r in range(0, z.shape[0], g)], axis=0)
    lc = ((parts[:, :LANE] + parts[:, LANE:2 * LANE]) + parts[:, 2 * LANE:]) * LOG2E
    return k, lc


def _hgrn_local(q, z, lb, tri, v_b, ln, sel, k_ref, lc_ref):
    k, lc = _hgrn_gates(z, lb, tri)
    k_ref[...] = k
    lc_ref[...] = lc
    nchunk = q.shape[0] // ln
    nsb = q.shape[0] // SUB_BLOCK
    per = ln // SUB_BLOCK
    half = SUB_BLOCK // 2
    rows = lambda a, i: a[i * SUB_BLOCK:(i + 1) * SUB_BLOCK, :]
    lcb = [jnp.zeros((1, LANE), F32) if i % per == 0 else lc_ref[i * SUB_BLOCK - 1:i * SUB_BLOCK, :]
           for i in range(nsb)]
    lcb_rows = jnp.concatenate([jnp.broadcast_to(b, (SUB_BLOCK, LANE)) for b in lcb], axis=0)
    last = [lc_ref[(c + 1) * ln - 1:(c + 1) * ln, :] for c in range(nchunk)]
    last_rows = jnp.concatenate([jnp.broadcast_to(b, (ln, LANE)) for b in last], axis=0)
    qh = (q * jnp.exp2(lc - lcb_rows)).astype(BF16)
    qe = (q * jnp.exp2(lc)).astype(BF16)
    kdec = (k * jnp.exp2(last_rows - lc)).astype(BF16)
    a_off = {}
    for i in range(nsb):
        n = (i % per) * SUB_BLOCK
        if n:
            c0 = i * SUB_BLOCK - n
            kt = (k[c0:c0 + n, :] * jnp.exp2(lcb[i] - lc[c0:c0 + n, :])).astype(BF16)
            a_off[i] = _dot_nt(rows(qh, i), kt)
    yield None
    pieces = []
    for i in range(nsb):
        q_i, lc_i = rows(q, i), rows(lc, i)
        cols = []
        for s in range(SUB_BLOCK):
            lo = 0 if s < half else half
            r = i * SUB_BLOCK + s
            d = lc_i[lo:, :] - lc_ref[r:r + 1, :]
            d = (jnp.concatenate([jnp.minimum(d[:half, :], 0.0), d[half:, :]], axis=0) if s < half
                 else jnp.minimum(d, 0.0))
            w = (q_i[lo:, :] * k_ref[r:r + 1, :]) * jnp.exp2(d)
            if lo:
                w = jnp.concatenate([jnp.zeros((lo, LANE), F32), w], axis=0)
            cols.append(w.astype(BF16))
        pieces.append(jnp.concatenate(cols, axis=1))
    a_all = _dot(jnp.concatenate(pieces, axis=0), sel)
    inc = [_dot_tn(v_b[c * ln:(c + 1) * ln, :], kdec[c * ln:(c + 1) * ln, :]) for c in range(nchunk)]
    yield None
    pair_ok = (lax.broadcasted_iota(jnp.int32, (SUB_BLOCK, LANE), 0)
               >= lax.broadcasted_iota(jnp.int32, (SUB_BLOCK, LANE), 1))
    off = []
    for i in range(nsb):
        n = (i % per) * SUB_BLOCK
        off.append(_dot(a_off[i].astype(BF16), v_b[i * SUB_BLOCK - n:i * SUB_BLOCK, :]) if n
                   else jnp.zeros((SUB_BLOCK, LANE), F32))
    diag = [_dot(jnp.where(pair_ok, rows(a_all, i), 0.0)[:, :SUB_BLOCK].astype(BF16), rows(v_b, i))
            for i in range(nsb)]
    local = jnp.concatenate(off, axis=0) + jnp.concatenate(diag, axis=0)
    dec = [jnp.exp2(b) for b in last]
    yield local, qe, inc, dec


def _hgrn_kernel(*refs, ln, nchunk, has_init):
    refs = list(refs)
    hq_ref, hf_ref, hi_ref, hg_ref, lb_ref, go_ref, sel_ref = refs[:7]
    s0_ref = refs[7] if has_init else None
    o_ref, sout_ref, st_ref, k_ref, lc_ref = refs[-5:]
    step = pl.program_id(1)
    nrows = ln * nchunk

    @pl.when(step == 0)
    def _():
        for h in range(HG_HEADS):
            st_ref[h] = s0_ref[0, h].T if has_init else jnp.zeros((HG_DV, HG_DK), F32)

    ng = min(nrows, 2 * LANE)
    assert ng % ln == 0 and nrows % ng == 0
    ri = lax.broadcasted_iota(jnp.int32, (ng, ng), 0)
    ci = lax.broadcasted_iota(jnp.int32, (ng, ng), 1)
    tri = ((ri >= ci) & (ri // ln == ci // ln)).astype(BF16)

    def finish(h, local, qe, inc, dec):
        cs = slice(h * LANE, (h + 1) * LANE)
        st = st_ref[h]
        parts = []
        for c in range(nchunk):
            parts.append(_dot_nt(qe[c * ln:(c + 1) * ln, :], st.astype(BF16)))
            st = st * dec[c] + inc[c]
        st_ref[h] = st
        o = local + jnp.concatenate(parts, axis=0)
        o_ref[:, cs] = (_rms(o, go_ref[...]) * _sigmoid(hg_ref[:, cs].astype(F32))).astype(BF16)

    def start(h):
        cs = slice(h * LANE, (h + 1) * LANE)
        gen = _hgrn_local(hq_ref[:, cs], hf_ref[:, cs], lb_ref[:, cs], tri, hi_ref[:, cs].astype(BF16),
                          ln, sel_ref[...], k_ref.at[h], lc_ref.at[h])
        next(gen)
        return gen

    gens = {0: start(0)}
    for h in range(HG_HEADS):
        if h + 1 < HG_HEADS:
            gens[h + 1] = start(h + 1)
        next(gens[h])
        if h > 0:
            finish(h - 1, *next(gens.pop(h - 1)))
    finish(HG_HEADS - 1, *next(gens.pop(HG_HEADS - 1)))

    @pl.when(step == pl.num_programs(1) - 1)
    def _():
        for h in range(HG_HEADS):
            sout_ref[0, h] = st_ref[h].T


def hgrn(hqf, hv, lb, g_out, bsz, s, ln, rows, s0=None, s0_row0=0):
    m = hqf.shape[0]
    ns = s // rows
    has_init = s0 is not None
    sel = (np.arange(SUB_BLOCK * LANE)[:, None] // LANE == np.arange(LANE)[None, :])
    sel = jnp.asarray(sel, BF16)
    blk = lambda col: pl.BlockSpec((rows, BRANCH_W), lambda b, i: (b * ns + i, col))
    ins = [hqf, hqf, hv, hv, lb, g_out, sel]
    specs = [blk(0), blk(1), blk(0), blk(1),
             pl.BlockSpec((1, BRANCH_W), lambda b, i: (0, 0)),
             pl.BlockSpec((1, HG_DV), lambda b, i: (0, 0)),
             pl.BlockSpec(sel.shape, lambda b, i: (0, 0))]
    if has_init:
        ins.append(s0)
        specs.append(pl.BlockSpec((1, HG_HEADS, HG_DK, HG_DV), lambda b, i: (s0_row0 + b, 0, 0, 0)))
    return pl.pallas_call(
        functools.partial(_hgrn_kernel, ln=ln, nchunk=rows // ln, has_init=has_init),
        out_shape=(jax.ShapeDtypeStruct((m, BRANCH_W), BF16),
                   jax.ShapeDtypeStruct((bsz, HG_HEADS, HG_DK, HG_DV), F32)),
        grid=(bsz, ns),
        in_specs=specs,
        out_specs=(pl.BlockSpec((rows, BRANCH_W), lambda b, i: (b * ns + i, 0)),
                   pl.BlockSpec((1, HG_HEADS, HG_DK, HG_DV), lambda b, i: (b, 0, 0, 0))),
        scratch_shapes=[pltpu.VMEM((HG_HEADS, HG_DV, HG_DK), F32),
                        pltpu.VMEM((HG_HEADS, rows, LANE), F32),
                        pltpu.VMEM((HG_HEADS, rows, LANE), F32)],
        compiler_params=_params(("parallel", "arbitrary")),
        name="hgrn",
    )(*ins)


def _merge_kernel(of_ref, om_ref, oh_ref, x_ref, wg_ref, wb_ref, wo_ref, g0_ref, g1_ref, o_ref):
    x = x_ref[...]
    h = _rms(x, g0_ref[...]).astype(BF16)
    branches = (of_ref, om_ref, oh_ref)
    gates = [_dot(h, wg_ref[i]) for i in range(3)]
    outs = [_dot(branches[i][...], wb_ref[i]) for i in range(3)]
    merged = (_sigmoid(gates[0]) * outs[0] + _sigmoid(gates[1]) * outs[1]) + _sigmoid(gates[2]) * outs[2]
    y = _dot(merged.astype(BF16), wo_ref[...])
    o_ref[...] = x + _rms(y, g1_ref[...])


def merge_out(o_fox, o_mla, o_hg, x, wg, wb, wo, g0, g1, tm):
    m = x.shape[0]
    row = lambda w: pl.BlockSpec((tm, w), lambda i: (i, 0))
    vec = pl.BlockSpec((1, D_MODEL), lambda i: (0, 0))
    return pl.pallas_call(
        _merge_kernel,
        out_shape=jax.ShapeDtypeStruct((m, D_MODEL), F32),
        grid=(m // tm,),
        in_specs=[row(BRANCH_W), row(BRANCH_W), row(BRANCH_W), row(D_MODEL),
                  _resident(wg.shape), _resident(wb.shape), _resident(wo.shape), vec, vec],
        out_specs=row(D_MODEL),
        compiler_params=_params(("parallel",)),
        name="merge_out",
    )(o_fox, o_mla, o_hg, x, wg, wb, wo, g0, g1)


def _matmul2_kernel(x_ref, w_ref, a_ref, b_ref):
    y = _dot(x_ref[...].astype(BF16), w_ref[...])
    n = a_ref.shape[1]
    a_ref[...] = y[:, :n]
    b_ref[...] = y[:, n:]


def mem_kv(mem, w, tm):
    m, k = mem.shape
    n = w.shape[1] // 2
    row = lambda w_: pl.BlockSpec((tm, w_), lambda i: (i, 0))
    return pl.pallas_call(
        _matmul2_kernel,
        out_shape=(jax.ShapeDtypeStruct((m, n), F32), jax.ShapeDtypeStruct((m, n), F32)),
        grid=(m // tm,),
        in_specs=[row(k), _resident(w.shape)],
        out_specs=(row(n), row(n)),
        compiler_params=_params(("parallel",)),
        name="mem_kv",
    )(mem, w)


def _cross_kernel(x_ref, mk_ref, mv_ref, wq_ref, wo_ref, g2_ref, g3_ref, o_ref):
    x = x_ref[...]
    h = _rms(x, g2_ref[...]).astype(BF16)
    q = _dot(h, wq_ref[...])
    qb = (q * (X_DIM ** -0.5 * LOG2E)).astype(BF16)
    cols = [slice(hd * X_DIM, (hd + 1) * X_DIM) for hd in range(X_HEADS)]
    ss = [_dot_nt(qb[:, cs], mk_ref[:, cs].astype(BF16)) for cs in cols]
    ps = [jnp.exp2(s - jnp.max(s, axis=1, keepdims=True)) for s in ss]
    pvs = [_dot(p.astype(BF16), mv_ref[:, cs].astype(BF16)) for p, cs in zip(ps, cols)]
    outs = [pv / jnp.sum(p, axis=1, keepdims=True) for pv, p in zip(pvs, ps)]
    ox = jnp.concatenate(outs, axis=1).astype(BF16)
    o_ref[...] = x + _rms(_dot(ox, wo_ref[...]), g3_ref[...])


def cross_block(x, mk, mv, wq, wo, g2, g3, bsz, s, tm, mem_row0):
    m = x.shape[0]
    nt = s // tm
    vec = pl.BlockSpec((1, D_MODEL), lambda b, i: (0, 0))
    return pl.pallas_call(
        _cross_kernel,
        out_shape=jax.ShapeDtypeStruct((m, D_MODEL), F32),
        grid=(bsz, nt),
        in_specs=[pl.BlockSpec((tm, D_MODEL), lambda b, i: (b * nt + i, 0)),
                  pl.BlockSpec((N_MEM, X_HEADS * X_DIM), lambda b, i: (mem_row0 + b, 0)),
                  pl.BlockSpec((N_MEM, X_HEADS * X_DIM), lambda b, i: (mem_row0 + b, 0)),
                  _resident(wq.shape), _resident(wo.shape), vec, vec],
        out_specs=pl.BlockSpec((tm, D_MODEL), lambda b, i: (b * nt + i, 0)),
        compiler_params=_params(("parallel", "parallel")),
        name="cross_attn",
    )(x, mk, mv, wq, wo, g2, g3)


def _mlp_kernel(x_ref, wu_ref, wd_ref, g4_ref, g5_ref, o_ref, h_ref, acc_ref):
    j = pl.program_id(1)

    @pl.when(j == 0)
    def _():
        h_ref[...] = _rms(x_ref[...], g4_ref[...]).astype(BF16)
        acc_ref[...] = jnp.zeros_like(acc_ref)

    cols = pl.ds(pl.multiple_of(j * D_MODEL, D_MODEL), D_MODEL)
    u = jnp.square(jnp.maximum(_dot(h_ref[...], wu_ref[:, cols]), 0.0)).astype(BF16)
    acc_ref[...] += _dot(u, wd_ref[j])

    @pl.when(j == pl.num_programs(1) - 1)
    def _():
        o_ref[...] = x_ref[...] + _rms(acc_ref[...], g5_ref[...])


def mlp_block(x, wu3, wd3, g4, g5, tm):
    m = x.shape[0]
    nj = wd3.shape[0]
    vec = pl.BlockSpec((1, D_MODEL), lambda i, j: (0, 0))
    return pl.pallas_call(
        _mlp_kernel,
        out_shape=jax.ShapeDtypeStruct((m, D_MODEL), F32),
        grid=(m // tm, nj),
        in_specs=[pl.BlockSpec((tm, D_MODEL), lambda i, j: (i, 0)),
                  _resident(wu3.shape), _resident(wd3.shape), vec, vec],
        out_specs=pl.BlockSpec((tm, D_MODEL), lambda i, j: (i, 0)),
        scratch_shapes=[pltpu.VMEM((tm, D_MODEL), BF16), pltpu.VMEM((tm, D_MODEL), F32)],
        compiler_params=_params(("parallel", "arbitrary")),
        name="mlp",
    )(x, wu3, wd3, g4, g5)


def _prep_layer_weights(w_in, w_mla_uq, w_mla_ukv, w_branch, w_out, w_xq, w_mem_k, w_mem_v, w_xo,
                        w_up, w_down):
    idx = np.cumsum((0,) + IN_SIZES)
    seg = lambda i: w_in[:, idx[i]:idx[i + 1]]
    fq, fk, fv, ff, cq, ckv, kpe, hq, hf, hi, hg, ga, gb, gc = (seg(i) for i in range(14))
    half = MLA_ROPE // 2
    kpe_sw = jnp.concatenate([kpe[:, half:], kpe[:, :half]], axis=1)
    pad = jnp.zeros((D_MODEL, IN_TN - HV_W), w_in.dtype)
    w_p = jnp.concatenate([hq, hf, cq, ckv, hi, hg, fq, pad], axis=1).astype(BF16)
    w_gate = jnp.stack([ga, gb, gc]).astype(BF16)
    w_t = jnp.concatenate([fk, fv, kpe, kpe_sw, ff], axis=1).T.astype(BF16)
    hd = MLA_NOPE + MLA_ROPE
    zq = jnp.zeros((MLA_Q_RANK, LANE - MLA_ROPE), w_mla_uq.dtype)
    nope, rope_n, rope_s = [], [], []
    for h in range(MLA_HEADS):
        base = h * hd
        nope.append(w_mla_uq[:, base:base + MLA_NOPE])
        x1 = w_mla_uq[:, base + MLA_NOPE:base + MLA_NOPE + half]
        x2 = w_mla_uq[:, base + MLA_NOPE + half:base + hd]
        rope_n += [x1, x2, zq]
        rope_s += [x2, x1, zq]
    wuq = jnp.concatenate(nope + rope_n + rope_s, axis=1).astype(BF16)
    kvd = MLA_NOPE + MLA_V
    wkt = jnp.concatenate([w_mla_ukv[:, h * kvd:h * kvd + MLA_NOPE] for h in range(MLA_HEADS)],
                          axis=1).T.astype(BF16)
    wv = jnp.concatenate([w_mla_ukv[:, h * kvd + MLA_NOPE:(h + 1) * kvd] for h in range(MLA_HEADS)],
                         axis=1).astype(BF16)
    nff = D_FF // D_MODEL
    return dict(
        w_in3=w_p, w_t=w_t, w_gate=w_gate, wuq=wuq, wkt=wkt, wv=wv,
        wb=w_branch.astype(BF16), wo=w_out.astype(BF16), wxq=w_xq.astype(BF16), wxo=w_xo.astype(BF16),
        wmem=jnp.concatenate([w_mem_k, w_mem_v], axis=1).astype(BF16),
        wu3=w_up.astype(BF16),
        wd3=w_down.astype(BF16).reshape(nff, D_MODEL, D_MODEL))


def _rope_tables(pos):
    half = MLA_ROPE // 2
    freq = ROPE_THETA ** (-jnp.arange(half, dtype=F32) / half)
    ang = pos.astype(F32)[:, None] * freq[None, :]
    cos, sin = jnp.cos(ang), jnp.sin(ang)
    z = jnp.zeros((pos.shape[0], LANE - MLA_ROPE), F32)
    cos_r = jnp.concatenate([cos, cos, z], axis=1)
    sin_r = jnp.concatenate([-sin, sin, z], axis=1)
    return cos_r, sin_r, cos_r[:, :MLA_ROPE].T, sin_r[:, :MLA_ROPE].T


def _tile(n, pref):
    t = min(n, pref)
    assert n % t == 0
    return t


def _layer(x, bsz, s, pos0, w, lb, b_fox, g_q, g_kv, g_hout, g_norm, mem_k, mem_v, mem_row0, past, cfg,
           layer, depth, shared):
    m = bsz * s
    g = lambda i: g_norm[i][None, :]
    tm_in = _tile(s, cfg["tm_in"])
    tm_p = _tile(s, cfg["tm_prep"])
    cos_r, sin_r, cos_c, sin_c = _rope_tables(pos0 + jnp.arange(s))
    hqf, mla_in, hv, kt, vt, kpet, logft = in_proj(x, g(0), w["w_in3"], w["w_t"], cos_c, sin_c,
                                                   b_fox[:, None], bsz, s, tm_in, layer, depth, shared[:4])
    qx, knt, v, ckv_n = mla_prep(mla_in, cos_r, sin_r, g_q[None, :], g_kv[None, :],
                                 w["wuq"], w["wkt"], w["wv"], bsz, s, tm_p, layer, depth, shared[4:])
    row0 = layer * bsz
    rows = lambda a: a.reshape((depth * bsz,) + a.shape[2:])
    cumt = fox_cumsum(rows(logft), row0, bsz)
    if past is None:
        t = _tile(s, cfg["t_attn"])
        o_fox = fox_prompt(hv, rows(kt), rows(vt), cumt, row0, bsz, s, t)
        o_mla = mla_prompt(qx, knt, rows(kpet), v, row0, bsz, s, t)
        o_hg, hg_state = hgrn(hqf, hv, lb[None, :], g_hout[None, :], bsz, s, CHUNK,
                              _tile(s, cfg["hg_rows"]))
    else:
        c_kt, c_vt, c_cumt, c_knt, c_kpt, c_v, c_hg = past
        p = c_kt.shape[2]
        o_fox = fox_sample(hv, rows(kt), rows(vt), c_kt, c_vt, cumt, c_cumt, bsz, s, p, layer)
        o_mla = mla_sample(qx, knt, rows(kpet), v, c_knt, c_kpt, c_v, bsz, s, p, layer)
        o_hg, hg_state = hgrn(hqf, hv, lb[None, :], g_hout[None, :], bsz, s, s, s, s0=c_hg, s0_row0=row0)
    x = merge_out(o_fox, o_mla, o_hg, x, w["w_gate"], w["wb"], w["wo"], g(0), g(1),
                  _tile(m, cfg["tm_merge"]))
    x = cross_block(x, mem_k, mem_v, w["wxq"], w["wxo"], g(2), g(3), bsz, s, _tile(s, cfg["tm_cross"]),
                    mem_row0)
    x = mlp_block(x, w["wu3"], w["wd3"], g(4), g(5), _tile(m, cfg["tm_mlp"]))
    return x, (kt, vt, kpet, logft, ckv_n), hg_state


def _from_feature_major(stacked, heads):
    a = jnp.swapaxes(stacked, 2, 3)
    if heads:
        a = a.reshape(a.shape[:3] + (heads, a.shape[3] // heads))
    return a


def _assemble_states(shared, hg_states, bsz, s):
    kt, vt, kpet, logft, ckv = shared
    return (_from_feature_major(kt, FOX_HEADS), _from_feature_major(vt, FOX_HEADS),
            _from_feature_major(logft, 0), ckv.reshape(ckv.shape[0], bsz, s, MLA_KV_RANK),
            _from_feature_major(kpet, 0), jnp.stack(hg_states))


_CFG = dict(tm_in=1024, tm_prep=1024, t_attn=512, hg_rows=512, tm_merge=512, tm_cross=1024,
            tm_mlp=1024, tm_mem=512, tm_expand=4096)


def kernel(x_prompt, x_sample, cache_fox_k, cache_fox_v, cache_fox_logf, cache_mla_ckv, cache_mla_kpe,
           state_hgrn, cache_mem_k, cache_mem_v, mem_prompt, w_in, b_fox, g_mla_q, w_mla_uq, g_mla_kv,
           w_mla_ukv, g_hgrn_out, lb_hgrn, w_branch, w_out, w_xq, w_mem_k, w_mem_v, w_xo, w_up, w_down,
           g_norm):
    cfg = _CFG
    depth = w_in.shape[0]
    lb_p = jax.nn.softmax(lb_hgrn.astype(F32), axis=0)
    lb_all = jnp.cumsum(lb_p, axis=0) - lb_p[0]
    ws = [_prep_layer_weights(w_in[l], w_mla_uq[l], w_mla_ukv[l], w_branch[l], w_out[l], w_xq[l],
                              w_mem_k[l], w_mem_v[l], w_xo[l], w_up[l], w_down[l]) for l in range(depth)]

    def run_layer(x, bsz, s, pos0, l, mk, mv, mem_row0, past, shared):
        return _layer(x, bsz, s, pos0, ws[l], lb_all[l], b_fox[l], g_mla_q[l], g_mla_kv[l],
                      g_hgrn_out[l], g_norm[l], mk, mv, mem_row0, past, cfg, l, depth, shared)

    bp, sp, _ = x_prompt.shape
    x = x_prompt.reshape(bp * sp, D_MODEL)
    mem = mem_prompt.reshape(bp * N_MEM, D_MODEL)
    shared, hg_states, p_mem = (), [], []
    for l in range(depth):
        mk, mv = mem_kv(mem, ws[l]["wmem"], _tile(bp * N_MEM, cfg["tm_mem"]))
        x, shared, hg = run_layer(x, bp, sp, 0, l, mk, mv, 0, None, shared)
        hg_states.append(hg)
        p_mem.append((mk.reshape(bp, N_MEM, X_HEADS, X_DIM), mv.reshape(bp, N_MEM, X_HEADS, X_DIM)))
    y_prompt = x.reshape(bp, sp, D_MODEL)
    p_out = _assemble_states(shared, hg_states, bp, sp) + tuple(jnp.stack(a) for a in zip(*p_mem))

    bs, ts, _ = x_sample.shape
    p = cache_fox_k.shape[2]
    fm = lambda c: jnp.moveaxis(c, 2, -1)
    c_kt = fm(cache_fox_k).reshape(depth * bs, BRANCH_W, p)
    c_vt = fm(cache_fox_v).reshape(depth * bs, BRANCH_W, p)
    c_kpt = fm(cache_mla_kpe).reshape(depth * bs, MLA_ROPE, p)
    c_cumt = fox_cumsum(fm(cache_fox_logf).reshape(depth * bs, FOX_HEADS, p), 0, depth * bs)
    c_ckv = cache_mla_ckv.reshape(depth * bs * p, MLA_KV_RANK)
    c_hg = state_hgrn.reshape((depth * bs,) + state_hgrn.shape[2:])
    c_mk = cache_mem_k.reshape(depth * bs * N_MEM, X_HEADS * X_DIM)
    c_mv = cache_mem_v.reshape(depth * bs * N_MEM, X_HEADS * X_DIM)
    x = x_sample.reshape(bs * ts, D_MODEL)
    shared, hg_states = (), []
    for l in range(depth):
        c_knt, c_v = latent_expand(c_ckv, ws[l]["wkt"], ws[l]["wv"], bs, p, _tile(p, cfg["tm_expand"]), l)
        past = (c_kt, c_vt, c_cumt, c_knt, c_kpt, c_v, c_hg)
        x, shared, hg = run_layer(x, bs, ts, p, l, c_mk, c_mv, l * bs, past, shared)
        hg_states.append(hg)
    y_sample = x.reshape(bs, ts, D_MODEL)
    return (y_prompt, y_sample, *p_out, *_assemble_states(shared, hg_states, bs, ts))
```

```python
import functools

import numpy as np
import jax
import jax.numpy as jnp
from jax import lax
from jax.experimental import pallas as pl
from jax.experimental.pallas import tpu as pltpu

F32 = jnp.float32
BF16 = jnp.bfloat16

D_MODEL = 1024
CHUNK = 64
N_MEM = 256
EPS = 1e-6
NEG_BIG = -1e30
EXP_CLIP = 80.0
FOX_HEADS = 8
FOX_DIM = 64
MLA_HEADS = 4
MLA_Q_RANK = 384
MLA_KV_RANK = 256
MLA_NOPE = 128
MLA_ROPE = 64
MLA_V = 128
ROPE_THETA = 10000.0
HG_HEADS = 4
HG_DK = 128
HG_DV = 128
X_HEADS = 4
X_DIM = 128
D_FF = 4 * D_MODEL
BRANCH_W = 512
IN_SIZES = (512, 512, 512, FOX_HEADS, MLA_Q_RANK, MLA_KV_RANK, MLA_ROPE, 512, 512, 512, 512,
            D_MODEL, D_MODEL, D_MODEL)

LANE = 128
SUB_BLOCK = 16
VMEM_LIMIT = 56 * 1024 * 1024
LOG2E = 1.4426950408889634

IN_TN = 1664
HQF_W, MLA_W, HV_W = 1024, MLA_Q_RANK + MLA_KV_RANK, 1536
OFF_FQ = 1024
T_FK, T_FV, T_KPE, T_KPE_SW, T_FF, NT_IN = 0, 512, 1024, 1088, 1152, 1160


def _params(sem, vmem=VMEM_LIMIT):
    return pltpu.CompilerParams(dimension_semantics=sem, vmem_limit_bytes=vmem)


def _dot(a, b):
    return jnp.dot(a, b, preferred_element_type=F32)


def _dot_nt(a, b):
    return lax.dot_general(a, b, (((1,), (1,)), ((), ())), preferred_element_type=F32)


def _dot_tn(a, b):
    return lax.dot_general(a, b, (((0,), (0,)), ((), ())), preferred_element_type=F32)


def _rms(x, g):
    y = x * lax.rsqrt(jnp.mean(x * x, axis=-1, keepdims=True) + EPS)
    return y * g


def _log_sigmoid(z):
    return jnp.minimum(z, 0.0) - jnp.log(1.0 + jnp.exp(-jnp.abs(z)))


def _sigmoid(z):
    return 1.0 / (1.0 + jnp.exp(-z))


def _resident(shape):
    nd = len(shape)
    return pl.BlockSpec(shape, lambda *_: (0,) * nd, pipeline_mode=pl.Buffered(1))


def _in_proj_kernel(x_ref, g_ref, w_ref, wt_ref, cos_ref, sin_ref, bf_ref, *rest):
    hqf_ref, mla_ref, hv_ref, kt_ref, vt_ref, kpe_ref, lf_ref, h_ref = rest[-8:]
    j = pl.program_id(1)

    @pl.when(j == 0)
    def _():
        h = _rms(x_ref[...], g_ref[...]).astype(BF16)
        h_ref[...] = h
        yt = _dot_nt(wt_ref[...], h)
        kt_ref[0] = yt[T_FK:T_FK + BRANCH_W]
        vt_ref[0] = yt[T_FV:T_FV + BRANCH_W]
        kpe_ref[0] = (yt[T_KPE:T_KPE + MLA_ROPE] * cos_ref[...]
                      + yt[T_KPE_SW:T_KPE_SW + MLA_ROPE] * sin_ref[...])
        lf_ref[0] = _log_sigmoid(yt[T_FF:T_FF + FOX_HEADS] + bf_ref[...])
        y = _dot(h, w_ref[:, :IN_TN])
        hqf_ref[...] = y[:, :HQF_W]
        mla_ref[...] = y[:, HQF_W:]

    @pl.when(j == 1)
    def _():
        hv_ref[...] = _dot(h_ref[...], w_ref[:, IN_TN:])[:, :HV_W].astype(BF16)


def in_proj(x, g, w3, wt, cos_t, sin_t, b_col, bsz, s, tm, layer, depth, prev):
    m, k = x.shape
    assert w3.shape[1] == 2 * IN_TN and wt.shape[0] == NT_IN
    nt = s // tm
    ntab = cos_t.shape[1] // tm
    feats = (BRANCH_W, BRANCH_W, MLA_ROPE, FOX_HEADS)
    tspec = lambda rows: pl.BlockSpec((None, 1, rows, tm), lambda i, j: (layer, i // nt, 0, i % nt))
    row = lambda w: pl.BlockSpec((tm, w), lambda i, j: (i, 0))
    n_in, n_row = 7, 3
    return pl.pallas_call(
        _in_proj_kernel,
        out_shape=(jax.ShapeDtypeStruct((m, HQF_W), F32), jax.ShapeDtypeStruct((m, MLA_W), F32),
                   jax.ShapeDtypeStruct((m, HV_W), BF16))
        + tuple(jax.ShapeDtypeStruct((depth, bsz, f, s), F32) for f in feats),
        grid=(m // tm, 2),
        in_specs=[pl.BlockSpec((tm, k), lambda i, j: (i, 0)),
                  pl.BlockSpec((1, k), lambda i, j: (0, 0)),
                  _resident(w3.shape),
                  _resident(wt.shape),
                  pl.BlockSpec((MLA_ROPE, tm), lambda i, j: (0, i % ntab)),
                  pl.BlockSpec((MLA_ROPE, tm), lambda i, j: (0, i % ntab)),
                  pl.BlockSpec((FOX_HEADS, 1), lambda i, j: (0, 0))]
        + [pl.BlockSpec(memory_space=pl.ANY)] * len(prev),
        out_specs=(row(HQF_W), row(MLA_W), row(HV_W)) + tuple(tspec(f) for f in feats),
        scratch_shapes=[pltpu.VMEM((tm, k), BF16)],
        input_output_aliases={n_in + i: n_row + i for i in range(len(prev))},
        compiler_params=_params(("parallel", "arbitrary")),
        name="in_proj",
    )(x, g, w3, wt, cos_t, sin_t, b_col, *prev)


def _mla_prep_kernel(in_ref, cs_ref, sn_ref, gq_ref, gkv_ref, wuq_ref, wkt_ref, wv_ref, *rest):
    qx_ref, knt_ref, v_ref, ckvn_ref = rest[-4:]
    cos_t = cs_ref[...]
    sin_t = sn_ref[...]
    qn = _rms(in_ref[:, :MLA_Q_RANK], gq_ref[...]).astype(BF16)
    qall = _dot(qn, wuq_ref[...]) * ((MLA_NOPE + MLA_ROPE) ** -0.5 * LOG2E)
    for h in range(MLA_HEADS):
        lo = h * LANE
        qr = (qall[:, 512 + lo:512 + lo + LANE] * cos_t
              + qall[:, 1024 + lo:1024 + lo + LANE] * sin_t)
        qx_ref[:, 2 * lo:2 * lo + LANE] = qall[:, lo:lo + LANE].astype(BF16)
        qx_ref[:, 2 * lo + LANE:2 * lo + 2 * LANE] = qr.astype(BF16)
    ckvn = _rms(in_ref[:, MLA_Q_RANK:], gkv_ref[...])
    ckvn_ref[...] = ckvn
    cb = ckvn.astype(BF16)
    knt_ref[0] = _dot_nt(wkt_ref[...], cb).astype(BF16)
    v_ref[...] = _dot(cb, wv_ref[...]).astype(BF16)


def mla_prep(mla_in, cos_t, sin_t, gq, gkv, wuq, wkt, wv, bsz, s, tm, layer, depth, prev):
    m = mla_in.shape[0]
    nt = s // tm
    ntab = cos_t.shape[0] // tm
    row = lambda w: pl.BlockSpec((tm, w), lambda i: (i, 0))
    full = lambda a: pl.BlockSpec(a.shape, lambda i: (0,) * a.ndim)
    n_in = 8
    return pl.pallas_call(
        _mla_prep_kernel,
        out_shape=(jax.ShapeDtypeStruct((m, 1024), BF16),
                   jax.ShapeDtypeStruct((bsz, 512, s), BF16),
                   jax.ShapeDtypeStruct((m, 512), BF16),
                   jax.ShapeDtypeStruct((depth, m, MLA_KV_RANK), F32)),
        grid=(m // tm,),
        in_specs=[row(MLA_W),
                  pl.BlockSpec((tm, LANE), lambda i: (i % ntab, 0)),
                  pl.BlockSpec((tm, LANE), lambda i: (i % ntab, 0)),
                  full(gq), full(gkv), full(wuq), full(wkt), full(wv)]
        + [pl.BlockSpec(memory_space=pl.ANY)] * len(prev),
        out_specs=(row(1024), pl.BlockSpec((1, 512, tm), lambda i: (i // nt, 0, i % nt)),
                   row(512), pl.BlockSpec((None, tm, MLA_KV_RANK), lambda i: (layer, i, 0))),
        input_output_aliases={n_in + i: 3 + i for i in range(len(prev))},
        compiler_params=_params(("parallel",)),
        name="mla_prep",
    )(mla_in, cos_t, sin_t, gq, gkv, wuq, wkt, wv, *prev)


def _cumsum_kernel(x_ref, c_ref, *, w):
    s = x_ref.shape[1]
    r = lax.broadcasted_iota(jnp.int32, (w, w), 0)
    c = lax.broadcasted_iota(jnp.int32, (w, w), 1)
    upper = (r <= c).astype(F32)
    local = [jnp.dot(x_ref[:, g * w:(g + 1) * w], upper, preferred_element_type=F32,
                     precision=lax.Precision.HIGHEST) for g in range(s // w)]
    carry = jnp.zeros((x_ref.shape[0], 1), F32)
    for g, cum in enumerate(local):
        cum = cum + carry
        c_ref[:, g * w:(g + 1) * w] = cum
        carry = cum[:, w - 1:w]


def fox_cumsum(x, row0, bsz):
    n, h, s = x.shape
    rows = bsz * h
    assert (row0 * h) % rows == 0
    out = pl.pallas_call(
        functools.partial(_cumsum_kernel, w=min(LANE, s)),
        out_shape=jax.ShapeDtypeStruct((rows, s), F32),
        grid=(1,),
        in_specs=[pl.BlockSpec((rows, s), lambda i: (row0 * h // rows, 0))],
        out_specs=pl.BlockSpec((rows, s), lambda i: (0, 0)),
        compiler_params=_params(("arbitrary",)),
        name="fox_cumsum",
    )(x.reshape(n * h, s))
    return out.reshape(bsz, h, s)


def _pair_rows_mask(hh):
    sub = lax.broadcasted_iota(jnp.int32, (LANE, 1), 0)
    return (sub < FOX_DIM) if hh == 0 else (sub >= FOX_DIM)


def _fox_finish(acc0, acc1):
    lane = lax.broadcasted_iota(jnp.int32, (1, LANE), 1)
    o0 = acc0 / pltpu.roll(acc0, FOX_DIM, axis=1)
    o1 = acc1 / pltpu.roll(acc1, FOX_DIM, axis=1)
    return jnp.where(lane < FOX_DIM, o0, o1)


def _causal_schedule(nq):
    todo = {i: list(range(i + 1)) for i in range(nq)}
    order = []
    while any(todo.values()):
        for i in reversed(range(nq)):
            if todo[i]:
                order.append((i, todo[i].pop(0)))
    return order


def _fox_prompt_kernel(q_ref, kt_ref, vt_ref, ct_ref, o_ref, *, t, nq):
    hp = pl.program_id(1)
    lane = lax.broadcasted_iota(jnp.int32, (1, LANE), 1)
    lo = lane < FOX_DIM
    span = lambda i: slice(i * t, (i + 1) * t)
    ct = [ct_ref[0, 2 * hp + hh] for hh in range(2)]
    qs, cref = [], []
    for i in range(nq):
        q = q_ref[span(i), :].astype(F32) * (FOX_DIM ** -0.5 * LOG2E)
        qs.append((jnp.where(lo, q, 0.0).astype(BF16), jnp.where(lo, 0.0, q).astype(BF16)))
        cref.append([c[:, i * t:i * t + 1] for c in ct])
    kts = [kt_ref[0, :, span(j)].astype(BF16) for j in range(nq)]
    vts = [[jnp.where(_pair_rows_mask(hh), vt_ref[0, :, span(j)], 1.0).astype(BF16) for hh in range(2)]
           for j in range(nq)]
    causal = (lax.broadcasted_iota(jnp.int32, (t, t), 0) >= lax.broadcasted_iota(jnp.int32, (t, t), 1))

    def scores(i, j):
        out = []
        for hh in range(2):
            s = _dot(qs[i][hh], kts[j]) + (cref[i][hh] - ct[hh][:, span(j)]) * LOG2E
            out.append(jnp.where(causal, s, NEG_BIG) if i == j else s)
        return out

    state = [[(jnp.full((t, 1), NEG_BIG, F32), jnp.zeros((t, LANE), F32)) for _ in range(2)]
             for _ in range(nq)]
    order = _causal_schedule(nq)
    ss = scores(*order[0])
    for n, (i, j) in enumerate(order):
        nxt = scores(*order[n + 1]) if n + 1 < len(order) else None
        m_new = [jnp.maximum(state[i][hh][0], jnp.max(ss[hh], axis=1, keepdims=True)) for hh in range(2)]
        ps = [jnp.exp2(ss[hh] - m_new[hh]).astype(BF16) for hh in range(2)]
        pvs = [_dot_nt(ps[hh], vts[j][hh]) for hh in range(2)]
        state[i] = [(m_new[hh], jnp.exp2(state[i][hh][0] - m_new[hh]) * state[i][hh][1] + pvs[hh])
                    for hh in range(2)]
        ss = nxt
    for i in range(nq):
        o_ref[span(i), :] = _fox_finish(state[i][0][1], state[i][1][1]).astype(BF16)


def fox_prompt(hv, kt, vt, cumt, row0, bsz, s, t):
    m = hv.shape[0]
    return pl.pallas_call(
        functools.partial(_fox_prompt_kernel, t=t, nq=s // t),
        out_shape=jax.ShapeDtypeStruct((m, BRANCH_W), BF16),
        grid=(bsz, FOX_HEADS // 2),
        in_specs=[pl.BlockSpec((s, LANE), lambda b, h: (b, OFF_FQ // LANE + h)),
                  pl.BlockSpec((1, LANE, s), lambda b, h: (row0 + b, h, 0)),
                  pl.BlockSpec((1, LANE, s), lambda b, h: (row0 + b, h, 0)),
                  pl.BlockSpec((1, FOX_HEADS, 1, s), lambda b, h: (b, 0, 0, 0))],
        out_specs=pl.BlockSpec((s, LANE), lambda b, h: (b, h)),
        compiler_params=_params(("parallel", "parallel")),
        name="fox_prompt",
    )(hv, kt, vt, cumt.reshape(bsz, FOX_HEADS, 1, s))


def _mla_keys(knt, kpt):
    n = knt.shape[1]
    return jnp.concatenate([knt, kpt.astype(BF16), jnp.zeros((LANE - MLA_ROPE, n), BF16)], axis=0)


def _mla_prompt_kernel(q_ref, knt_ref, kpt_ref, v_ref, o_ref, *, t, nq):
    hs = range(2)
    span = lambda i: slice(i * t, (i + 1) * t)
    cols = lambda hh, w: slice(hh * w, (hh + 1) * w)
    keys = [[_mla_keys(knt_ref[0, cols(hh, LANE), span(j)], kpt_ref[0, :, span(j)]) for hh in hs]
            for j in range(nq)]
    mask = (lax.broadcasted_iota(jnp.int32, (t, t), 0) // CHUNK
            >= lax.broadcasted_iota(jnp.int32, (t, t), 1) // CHUNK)

    def scores(i, j):
        ss = [_dot(q_ref[span(i), cols(hh, 2 * LANE)], keys[j][hh]) for hh in hs]
        return [jnp.where(mask, s, NEG_BIG) for s in ss] if i == j else ss

    state = [[(jnp.full((t, 1), NEG_BIG, F32), jnp.zeros((t, 1), F32), jnp.zeros((t, LANE), F32))
              for _ in hs] for _ in range(nq)]
    order = _causal_schedule(nq)
    ss = scores(*order[0])
    for n, (i, j) in enumerate(order):
        nxt = scores(*order[n + 1]) if n + 1 < len(order) else None
        m_new = [jnp.maximum(state[i][hh][0], jnp.max(ss[hh], axis=1, keepdims=True)) for hh in hs]
        ps = [jnp.exp2(ss[hh] - m_new[hh]) for hh in hs]
        pvs = [_dot(ps[hh].astype(BF16), v_ref[span(j), cols(hh, LANE)]) for hh in hs]
        new = []
        for hh in hs:
            alpha = jnp.exp2(state[i][hh][0] - m_new[hh])
            new.append((m_new[hh], alpha * state[i][hh][1] + jnp.sum(ps[hh], axis=1, keepdims=True),
                        alpha * state[i][hh][2] + pvs[hh]))
        state[i] = new
        ss = nxt
    for i in range(nq):
        o_ref[span(i), :] = jnp.concatenate([state[i][hh][2] / state[i][hh][1] for hh in hs],
                                            axis=1).astype(BF16)


def mla_prompt(qx, knt, kpt, v, row0, bsz, s, t):
    assert t % CHUNK == 0
    m = qx.shape[0]
    return pl.pallas_call(
        functools.partial(_mla_prompt_kernel, t=t, nq=s // t),
        out_shape=jax.ShapeDtypeStruct((m, BRANCH_W), BF16),
        grid=(bsz, MLA_HEADS // 2),
        in_specs=[pl.BlockSpec((s, 4 * LANE), lambda b, h: (b, h)),
                  pl.BlockSpec((1, 2 * LANE, s), lambda b, h: (b, h, 0)),
                  pl.BlockSpec((1, MLA_ROPE, s), lambda b, h: (row0 + b, 0, 0)),
                  pl.BlockSpec((s, 2 * LANE), lambda b, h: (b, h))],
        out_specs=pl.BlockSpec((s, 2 * LANE), lambda b, h: (b, h)),
        compiler_params=_params(("parallel", "parallel")),
        name="mla_prompt",
    )(qx, knt, kpt, v)


def _fox_sample_kernel(q_ref, ktn_ref, vtn_ref, ktc_ref, vtc_ref, ctn_ref, ctc_ref, o_ref, *, t, p):
    lane = lax.broadcasted_iota(jnp.int32, (1, LANE), 1)
    lo = lane < FOX_DIM
    causal = (lax.broadcasted_iota(jnp.int32, (t, t), 0) >= lax.broadcasted_iota(jnp.int32, (t, t), 1))
    for hp in range(FOX_HEADS // 2):
        pair = slice(hp * LANE, (hp + 1) * LANE)
        q = q_ref[:, pair].astype(F32) * (FOX_DIM ** -0.5 * LOG2E)
        kt_c = ktc_ref[0, pair, :].astype(BF16)
        kt_n = ktn_ref[0, pair, :].astype(BF16)
        vt_c = vtc_ref[0, pair, :]
        vt_n = vtn_ref[0, pair, :]
        accs = []
        for hh in range(2):
            head = 2 * hp + hh
            qh = (jnp.where(lo, q, 0.0) if hh == 0 else jnp.where(lo, 0.0, q)).astype(BF16)
            cc = ctc_ref[0, head:head + 1, :]
            ctot = cc[:, p - 1:p]
            s_c = _dot(qh, kt_c) + (ctot - cc) * LOG2E
            s_n = _dot(qh, kt_n) - ctn_ref[0, head:head + 1, :] * LOG2E
            s_n = jnp.where(causal, s_n, NEG_BIG)
            m = jnp.maximum(jnp.max(s_c, axis=1, keepdims=True), jnp.max(s_n, axis=1, keepdims=True))
            rows = _pair_rows_mask(hh)
            accs.append(_dot_nt(jnp.exp2(s_c - m).astype(BF16), jnp.where(rows, vt_c, 1.0).astype(BF16))
                        + _dot_nt(jnp.exp2(s_n - m).astype(BF16), jnp.where(rows, vt_n, 1.0).astype(BF16)))
        o_ref[:, pair] = _fox_finish(accs[0], accs[1]).astype(BF16)


def fox_sample(hv, kt_n, vt_n, kt_c, vt_c, cumt_n, cumt_c, bsz, t, p, layer):
    cidx = lambda b: (layer * bsz + b, 0, 0)
    return pl.pallas_call(
        functools.partial(_fox_sample_kernel, t=t, p=p),
        out_shape=jax.ShapeDtypeStruct((bsz * t, BRANCH_W), BF16),
        grid=(bsz,),
        in_specs=[pl.BlockSpec((t, BRANCH_W), lambda b: (b, OFF_FQ // BRANCH_W)),
                  pl.BlockSpec((1, BRANCH_W, t), cidx),
                  pl.BlockSpec((1, BRANCH_W, t), cidx),
                  pl.BlockSpec((1, BRANCH_W, p), cidx),
                  pl.BlockSpec((1, BRANCH_W, p), cidx),
                  pl.BlockSpec((1, FOX_HEADS, t), lambda b: (b, 0, 0)),
                  pl.BlockSpec((1, FOX_HEADS, p), cidx)],
        out_specs=pl.BlockSpec((t, BRANCH_W), lambda b: (b, 0)),
        compiler_params=_params(("parallel",)),
        name="fox_sample",
    )(hv, kt_n, vt_n, kt_c, vt_c, cumt_n, cumt_c)


def _mla_sample_kernel(q_ref, kntn_ref, kptn_ref, vn_ref, kntc_ref, kptc_ref, vc_ref, o_ref, *, t, p):
    qc = (p + lax.broadcasted_iota(jnp.int32, (t, t), 0)) // CHUNK
    kc = (p + lax.broadcasted_iota(jnp.int32, (t, t), 1)) // CHUNK
    kpt_c, kpt_n = kptc_ref[0], kptn_ref[0]
    for h in range(MLA_HEADS):
        cs = slice(h * LANE, (h + 1) * LANE)
        q = q_ref[:, 2 * h * LANE:2 * (h + 1) * LANE]
        s_c = _dot(q, _mla_keys(kntc_ref[0, cs, :], kpt_c))
        s_n = _dot(q, _mla_keys(kntn_ref[0, cs, :], kpt_n))
        s_n = jnp.where(qc >= kc, s_n, NEG_BIG)
        m = jnp.maximum(jnp.max(s_c, axis=1, keepdims=True), jnp.max(s_n, axis=1, keepdims=True))
        p_c = jnp.exp2(s_c - m)
        p_n = jnp.exp2(s_n - m)
        l = jnp.sum(p_c, axis=1, keepdims=True) + jnp.sum(p_n, axis=1, keepdims=True)
        o = _dot(p_c.astype(BF16), vc_ref[:, cs]) + _dot(p_n.astype(BF16), vn_ref[:, cs])
        o_ref[:, cs] = (o / l).astype(BF16)


def mla_sample(qx, knt_n, kpt_n, v_n, knt_c, kpt_c, v_c, bsz, t, p, layer):
    assert (p - 1) // CHUNK <= p // CHUNK
    return pl.pallas_call(
        functools.partial(_mla_sample_kernel, t=t, p=p),
        out_shape=jax.ShapeDtypeStruct((bsz * t, BRANCH_W), BF16),
        grid=(bsz,),
        in_specs=[pl.BlockSpec((t, 8 * LANE), lambda b: (b, 0)),
                  pl.BlockSpec((1, BRANCH_W, t), lambda b: (b, 0, 0)),
                  pl.BlockSpec((1, MLA_ROPE, t), lambda b: (layer * bsz + b, 0, 0)),
                  pl.BlockSpec((t, BRANCH_W), lambda b: (b, 0)),
                  pl.BlockSpec((1, BRANCH_W, p), lambda b: (b, 0, 0)),
                  pl.BlockSpec((1, MLA_ROPE, p), lambda b: (layer * bsz + b, 0, 0)),
                  pl.BlockSpec((p, BRANCH_W), lambda b: (b, 0))],
        out_specs=pl.BlockSpec((t, BRANCH_W), lambda b: (b, 0)),
        compiler_params=_params(("parallel",)),
        name="mla_sample",
    )(qx, knt_n, kpt_n, v_n, knt_c, kpt_c, v_c)


def _latent_expand_kernel(c_ref, wkt_ref, wv_ref, knt_ref, v_ref):
    cb = c_ref[...].astype(BF16)
    knt_ref[0] = _dot_nt(wkt_ref[...], cb).astype(BF16)
    v_ref[...] = _dot(cb, wv_ref[...]).astype(BF16)


def latent_expand(ckv, wkt, wv, bsz, p, tm, layer):
    m = bsz * p
    nt = p // tm
    return pl.pallas_call(
        _latent_expand_kernel,
        out_shape=(jax.ShapeDtypeStruct((bsz, 512, p), BF16), jax.ShapeDtypeStruct((m, 512), BF16)),
        grid=(m // tm,),
        in_specs=[pl.BlockSpec((tm, MLA_KV_RANK), lambda i: (layer * (m // tm) + i, 0)),
                  pl.BlockSpec(wkt.shape, lambda i: (0, 0)), pl.BlockSpec(wv.shape, lambda i: (0, 0))],
        out_specs=(pl.BlockSpec((1, 512, tm), lambda i: (i // nt, 0, i % nt)),
                   pl.BlockSpec((tm, 512), lambda i: (i, 0))),
        compiler_params=_params(("parallel",)),
        name="latent_expand",
    )(ckv, wkt, wv)


def _hgrn_gates(z, lb, tri):
    logf = _log_sigmoid(z) + jnp.log(1.0 + lb * jnp.exp(jnp.minimum(-z, EXP_CLIP)))
    k = (1.0 - lb) * (1.0 / (1.0 + jnp.exp(z)))
    h1 = logf.astype(BF16)
    r1 = logf - h1.astype(F32)
    h2 = r1.astype(BF16)
    h3 = (r1 - h2.astype(F32)).astype(BF16)
    hcat = jnp.concatenate([h1, h2, h3], axis=1)
    g = tri.shape[0]
    parts = jnp.concatenate([_dot(tri, hcat[r:r + g, :]) for r in range(0, z.shape[0], g)], axis=0)
    lc = ((parts[:, :LANE] + parts[:, LANE:2 * LANE]) + parts[:, 2 * LANE:]) * LOG2E
    return k, lc


def _hgrn_local(q, z, lb, tri, v_b, ln, sel, k_ref, lc_ref):
    k, lc = _hgrn_gates(z, lb, tri)
    k_ref[...] = k
    lc_ref[...] = lc
    nchunk = q.shape[0] // ln
    nsb = q.shape[0] // SUB_BLOCK
    per = ln // SUB_BLOCK
    half = SUB_BLOCK // 2
    rows = lambda a, i: a[i * SUB_BLOCK:(i + 1) * SUB_BLOCK, :]
    lcb = [jnp.zeros((1, LANE), F32) if i % per == 0 else lc_ref[i * SUB_BLOCK - 1:i * SUB_BLOCK, :]
           for i in range(nsb)]
    lcb_rows = jnp.concatenate([jnp.broadcast_to(b, (SUB_BLOCK, LANE)) for b in lcb], axis=0)
    last = [lc_ref[(c + 1) * ln - 1:(c + 1) * ln, :] for c in range(nchunk)]
    last_rows = jnp.concatenate([jnp.broadcast_to(b, (ln, LANE)) for b in last], axis=0)
    qh = (q * jnp.exp2(lc - lcb_rows)).astype(BF16)
    qe = (q * jnp.exp2(lc)).astype(BF16)
    kdec = (k * jnp.exp2(last_rows - lc)).astype(BF16)
    a_off = {}
    for i in range(nsb):
        n = (i % per) * SUB_BLOCK
        if n:
            c0 = i * SUB_BLOCK - n
            kt = (k[c0:c0 + n, :] * jnp.exp2(lcb[i] - lc[c0:c0 + n, :])).astype(BF16)
            a_off[i] = _dot_nt(rows(qh, i), kt)
    yield None
    pieces = []
    for i in range(nsb):
        q_i, lc_i = rows(q, i), rows(lc, i)
        cols = []
        for s in range(SUB_BLOCK):
            lo = 0 if s < half else half
            r = i * SUB_BLOCK + s
            d = lc_i[lo:, :] - lc_ref[r:r + 1, :]
            d = (jnp.concatenate([jnp.minimum(d[:half, :], 0.0), d[half:, :]], axis=0) if s < half
                 else jnp.minimum(d, 0.0))
            w = (q_i[lo:, :] * k_ref[r:r + 1, :]) * jnp.exp2(d)
            if lo:
                w = jnp.concatenate([jnp.zeros((lo, LANE), F32), w], axis=0)
            cols.append(w.astype(BF16))
        pieces.append(jnp.concatenate(cols, axis=1))
    a_all = _dot(jnp.concatenate(pieces, axis=0), sel)
    inc = [_dot_tn(v_b[c * ln:(c + 1) * ln, :], kdec[c * ln:(c + 1) * ln, :]) for c in range(nchunk)]
    yield None
    pair_ok = (lax.broadcasted_iota(jnp.int32, (SUB_BLOCK, LANE), 0)
               >= lax.broadcasted_iota(jnp.int32, (SUB_BLOCK, LANE), 1))
    off = []
    for i in range(nsb):
        n = (i % per) * SUB_BLOCK
        off.append(_dot(a_off[i].astype(BF16), v_b[i * SUB_BLOCK - n:i * SUB_BLOCK, :]) if n
                   else jnp.zeros((SUB_BLOCK, LANE), F32))
    diag = [_dot(jnp.where(pair_ok, rows(a_all, i), 0.0)[:, :SUB_BLOCK].astype(BF16), rows(v_b, i))
            for i in range(nsb)]
    local = jnp.concatenate(off, axis=0) + jnp.concatenate(diag, axis=0)
    dec = [jnp.exp2(b) for b in last]
    yield local, qe, inc, dec


def _hgrn_kernel(*refs, ln, nchunk, has_init):
    refs = list(refs)
    hq_ref, hf_ref, hi_ref, hg_ref, lb_ref, go_ref, sel_ref = refs[:7]
    s0_ref = refs[7] if has_init else None
    o_ref, sout_ref, st_ref, k_ref, lc_ref = refs[-5:]
    step = pl.program_id(1)
    nrows = ln * nchunk

    @pl.when(step == 0)
    def _():
        for h in range(HG_HEADS):
            st_ref[h] = s0_ref[0, h].T if has_init else jnp.zeros((HG_DV, HG_DK), F32)

    ng = min(nrows, 2 * LANE)
    assert ng % ln == 0 and nrows % ng == 0
    ri = lax.broadcasted_iota(jnp.int32, (ng, ng), 0)
    ci = lax.broadcasted_iota(jnp.int32, (ng, ng), 1)
    tri = ((ri >= ci) & (ri // ln == ci // ln)).astype(BF16)

    def finish(h, local, qe, inc, dec):
        cs = slice(h * LANE, (h + 1) * LANE)
        st = st_ref[h]
        parts = []
        for c in range(nchunk):
            parts.append(_dot_nt(qe[c * ln:(c + 1) * ln, :], st.astype(BF16)))
            st = st * dec[c] + inc[c]
        st_ref[h] = st
        o = local + jnp.concatenate(parts, axis=0)
        o_ref[:, cs] = (_rms(o, go_ref[...]) * _sigmoid(hg_ref[:, cs].astype(F32))).astype(BF16)

    def start(h):
        cs = slice(h * LANE, (h + 1) * LANE)
        gen = _hgrn_local(hq_ref[:, cs], hf_ref[:, cs], lb_ref[:, cs], tri, hi_ref[:, cs].astype(BF16),
                          ln, sel_ref[...], k_ref.at[h], lc_ref.at[h])
        next(gen)
        return gen

    gens = {0: start(0)}
    for h in range(HG_HEADS):
        if h + 1 < HG_HEADS:
            gens[h + 1] = start(h + 1)
        next(gens[h])
        if h > 0:
            finish(h - 1, *next(gens.pop(h - 1)))
    finish(HG_HEADS - 1, *next(gens.pop(HG_HEADS - 1)))

    @pl.when(step == pl.num_programs(1) - 1)
    def _():
        for h in range(HG_HEADS):
            sout_ref[0, h] = st_ref[h].T


def hgrn(hqf, hv, lb, g_out, bsz, s, ln, rows, s0=None, s0_row0=0):
    m = hqf.shape[0]
    ns = s // rows
    has_init = s0 is not None
    sel = (np.arange(SUB_BLOCK * LANE)[:, None] // LANE == np.arange(LANE)[None, :])
    sel = jnp.asarray(sel, BF16)
    blk = lambda col: pl.BlockSpec((rows, BRANCH_W), lambda b, i: (b * ns + i, col))
    ins = [hqf, hqf, hv, hv, lb, g_out, sel]
    specs = [blk(0), blk(1), blk(0), blk(1),
             pl.BlockSpec((1, BRANCH_W), lambda b, i: (0, 0)),
             pl.BlockSpec((1, HG_DV), lambda b, i: (0, 0)),
             pl.BlockSpec(sel.shape, lambda b, i: (0, 0))]
    if has_init:
        ins.append(s0)
        specs.append(pl.BlockSpec((1, HG_HEADS, HG_DK, HG_DV), lambda b, i: (s0_row0 + b, 0, 0, 0)))
    return pl.pallas_call(
        functools.partial(_hgrn_kernel, ln=ln, nchunk=rows // ln, has_init=has_init),
        out_shape=(jax.ShapeDtypeStruct((m, BRANCH_W), BF16),
                   jax.ShapeDtypeStruct((bsz, HG_HEADS, HG_DK, HG_DV), F32)),
        grid=(bsz, ns),
        in_specs=specs,
        out_specs=(pl.BlockSpec((rows, BRANCH_W), lambda b, i: (b * ns + i, 0)),
                   pl.BlockSpec((1, HG_HEADS, HG_DK, HG_DV), lambda b, i: (b, 0, 0, 0))),
        scratch_shapes=[pltpu.VMEM((HG_HEADS, HG_DV, HG_DK), F32),
                        pltpu.VMEM((HG_HEADS, rows, LANE), F32),
                        pltpu.VMEM((HG_HEADS, rows, LANE), F32)],
        compiler_params=_params(("parallel", "arbitrary")),
        name="hgrn",
    )(*ins)


def _merge_kernel(of_ref, om_ref, oh_ref, x_ref, wg_ref, wb_ref, wo_ref, g0_ref, g1_ref, o_ref):
    x = x_ref[...]
    h = _rms(x, g0_ref[...]).astype(BF16)
    branches = (of_ref, om_ref, oh_ref)
    gates = [_dot(h, wg_ref[i]) for i in range(3)]
    outs = [_dot(branches[i][...], wb_ref[i]) for i in range(3)]
    merged = (_sigmoid(gates[0]) * outs[0] + _sigmoid(gates[1]) * outs[1]) + _sigmoid(gates[2]) * outs[2]
    y = _dot(merged.astype(BF16), wo_ref[...])
    o_ref[...] = x + _rms(y, g1_ref[...])


def merge_out(o_fox, o_mla, o_hg, x, wg, wb, wo, g0, g1, tm):
    m = x.shape[0]
    row = lambda w: pl.BlockSpec((tm, w), lambda i: (i, 0))
    vec = pl.BlockSpec((1, D_MODEL), lambda i: (0, 0))
    return pl.pallas_call(
        _merge_kernel,
        out_shape=jax.ShapeDtypeStruct((m, D_MODEL), F32),
        grid=(m // tm,),
        in_specs=[row(BRANCH_W), row(BRANCH_W), row(BRANCH_W), row(D_MODEL),
                  _resident(wg.shape), _resident(wb.shape), _resident(wo.shape), vec, vec],
        out_specs=row(D_MODEL),
        compiler_params=_params(("parallel",)),
        name="merge_out",
    )(o_fox, o_mla, o_hg, x, wg, wb, wo, g0, g1)


def _matmul2_kernel(x_ref, w_ref, a_ref, b_ref):
    y = _dot(x_ref[...].astype(BF16), w_ref[...])
    n = a_ref.shape[1]
    a_ref[...] = y[:, :n]
    b_ref[...] = y[:, n:]


def mem_kv(mem, w, tm):
    m, k = mem.shape
    n = w.shape[1] // 2
    row = lambda w_: pl.BlockSpec((tm, w_), lambda i: (i, 0))
    return pl.pallas_call(
        _matmul2_kernel,
        out_shape=(jax.ShapeDtypeStruct((m, n), F32), jax.ShapeDtypeStruct((m, n), F32)),
        grid=(m // tm,),
        in_specs=[row(k), _resident(w.shape)],
        out_specs=(row(n), row(n)),
        compiler_params=_params(("parallel",)),
        name="mem_kv",
    )(mem, w)


def _cross_kernel(x_ref, mk_ref, mv_ref, wq_ref, wo_ref, g2_ref, g3_ref, o_ref):
    x = x_ref[...]
    h = _rms(x, g2_ref[...]).astype(BF16)
    q = _dot(h, wq_ref[...])
    qb = (q * (X_DIM ** -0.5 * LOG2E)).astype(BF16)
    cols = [slice(hd * X_DIM, (hd + 1) * X_DIM) for hd in range(X_HEADS)]
    ss = [_dot_nt(qb[:, cs], mk_ref[:, cs].astype(BF16)) for cs in cols]
    ps = [jnp.exp2(s - jnp.max(s, axis=1, keepdims=True)) for s in ss]
    pvs = [_dot(p.astype(BF16), mv_ref[:, cs].astype(BF16)) for p, cs in zip(ps, cols)]
    outs = [pv / jnp.sum(p, axis=1, keepdims=True) for pv, p in zip(pvs, ps)]
    ox = jnp.concatenate(outs, axis=1).astype(BF16)
    o_ref[...] = x + _rms(_dot(ox, wo_ref[...]), g3_ref[...])


def cross_block(x, mk, mv, wq, wo, g2, g3, bsz, s, tm, mem_row0):
    m = x.shape[0]
    nt = s // tm
    vec = pl.BlockSpec((1, D_MODEL), lambda b, i: (0, 0))
    return pl.pallas_call(
        _cross_kernel,
        out_shape=jax.ShapeDtypeStruct((m, D_MODEL), F32),
        grid=(bsz, nt),
        in_specs=[pl.BlockSpec((tm, D_MODEL), lambda b, i: (b * nt + i, 0)),
                  pl.BlockSpec((N_MEM, X_HEADS * X_DIM), lambda b, i: (mem_row0 + b, 0)),
                  pl.BlockSpec((N_MEM, X_HEADS * X_DIM), lambda b, i: (mem_row0 + b, 0)),
                  _resident(wq.shape), _resident(wo.shape), vec, vec],
        out_specs=pl.BlockSpec((tm, D_MODEL), lambda b, i: (b * nt + i, 0)),
        compiler_params=_params(("parallel", "parallel")),
        name="cross_attn",
    )(x, mk, mv, wq, wo, g2, g3)


def _mlp_kernel(x_ref, wu_ref, wd_ref, g4_ref, g5_ref, o_ref, h_ref, acc_ref):
    j = pl.program_id(1)

    @pl.when(j == 0)
    def _():
        h_ref[...] = _rms(x_ref[...], g4_ref[...]).astype(BF16)
        acc_ref[...] = jnp.zeros_like(acc_ref)

    cols = pl.ds(pl.multiple_of(j * D_MODEL, D_MODEL), D_MODEL)
    u = jnp.square(jnp.maximum(_dot(h_ref[...], wu_ref[:, cols]), 0.0)).astype(BF16)
    acc_ref[...] += _dot(u, wd_ref[j])

    @pl.when(j == pl.num_programs(1) - 1)
    def _():
        o_ref[...] = x_ref[...] + _rms(acc_ref[...], g5_ref[...])


def mlp_block(x, wu3, wd3, g4, g5, tm):
    m = x.shape[0]
    nj = wd3.shape[0]
    vec = pl.BlockSpec((1, D_MODEL), lambda i, j: (0, 0))
    return pl.pallas_call(
        _mlp_kernel,
        out_shape=jax.ShapeDtypeStruct((m, D_MODEL), F32),
        grid=(m // tm, nj),
        in_specs=[pl.BlockSpec((tm, D_MODEL), lambda i, j: (i, 0)),
                  _resident(wu3.shape), _resident(wd3.shape), vec, vec],
        out_specs=pl.BlockSpec((tm, D_MODEL), lambda i, j: (i, 0)),
        scratch_shapes=[pltpu.VMEM((tm, D_MODEL), BF16), pltpu.VMEM((tm, D_MODEL), F32)],
        compiler_params=_params(("parallel", "arbitrary")),
        name="mlp",
    )(x, wu3, wd3, g4, g5)


def _prep_layer_weights(w_in, w_mla_uq, w_mla_ukv, w_branch, w_out, w_xq, w_mem_k, w_mem_v, w_xo,
                        w_up, w_down):
    idx = np.cumsum((0,) + IN_SIZES)
    seg = lambda i: w_in[:, idx[i]:idx[i + 1]]
    fq, fk, fv, ff, cq, ckv, kpe, hq, hf, hi, hg, ga, gb, gc = (seg(i) for i in range(14))
    half = MLA_ROPE // 2
    kpe_sw = jnp.concatenate([kpe[:, half:], kpe[:, :half]], axis=1)
    pad = jnp.zeros((D_MODEL, IN_TN - HV_W), w_in.dtype)
    w_p = jnp.concatenate([hq, hf, cq, ckv, hi, hg, fq, pad], axis=1).astype(BF16)
    w_gate = jnp.stack([ga, gb, gc]).astype(BF16)
    w_t = jnp.concatenate([fk, fv, kpe, kpe_sw, ff], axis=1).T.astype(BF16)
    hd = MLA_NOPE + MLA_ROPE
    zq = jnp.zeros((MLA_Q_RANK, LANE - MLA_ROPE), w_mla_uq.dtype)
    nope, rope_n, rope_s = [], [], []
    for h in range(MLA_HEADS):
        base = h * hd
        nope.append(w_mla_uq[:, base:base + MLA_NOPE])
        x1 = w_mla_uq[:, base + MLA_NOPE:base + MLA_NOPE + half]
        x2 = w_mla_uq[:, base + MLA_NOPE + half:base + hd]
        rope_n += [x1, x2, zq]
        rope_s += [x2, x1, zq]
    wuq = jnp.concatenate(nope + rope_n + rope_s, axis=1).astype(BF16)
    kvd = MLA_NOPE + MLA_V
    wkt = jnp.concatenate([w_mla_ukv[:, h * kvd:h * kvd + MLA_NOPE] for h in range(MLA_HEADS)],
                          axis=1).T.astype(BF16)
    wv = jnp.concatenate([w_mla_ukv[:, h * kvd + MLA_NOPE:(h + 1) * kvd] for h in range(MLA_HEADS)],
                         axis=1).astype(BF16)
    nff = D_FF // D_MODEL
    return dict(
        w_in3=w_p, w_t=w_t, w_gate=w_gate, wuq=wuq, wkt=wkt, wv=wv,
        wb=w_branch.astype(BF16), wo=w_out.astype(BF16), wxq=w_xq.astype(BF16), wxo=w_xo.astype(BF16),
        wmem=jnp.concatenate([w_mem_k, w_mem_v], axis=1).astype(BF16),
        wu3=w_up.astype(BF16),
        wd3=w_down.astype(BF16).reshape(nff, D_MODEL, D_MODEL))


def _rope_tables(pos):
    half = MLA_ROPE // 2
    freq = ROPE_THETA ** (-jnp.arange(half, dtype=F32) / half)
    ang = pos.astype(F32)[:, None] * freq[None, :]
    cos, sin = jnp.cos(ang), jnp.sin(ang)
    z = jnp.zeros((pos.shape[0], LANE - MLA_ROPE), F32)
    cos_r = jnp.concatenate([cos, cos, z], axis=1)
    sin_r = jnp.concatenate([-sin, sin, z], axis=1)
    return cos_r, sin_r, cos_r[:, :MLA_ROPE].T, sin_r[:, :MLA_ROPE].T


def _tile(n, pref):
    t = min(n, pref)
    assert n % t == 0
    return t


def _layer(x, bsz, s, pos0, w, lb, b_fox, g_q, g_kv, g_hout, g_norm, mem_k, mem_v, mem_row0, past, cfg,
           layer, depth, shared):
    m = bsz * s
    g = lambda i: g_norm[i][None, :]
    tm_in = _tile(s, cfg["tm_in"])
    tm_p = _tile(s, cfg["tm_prep"])
    cos_r, sin_r, cos_c, sin_c = _rope_tables(pos0 + jnp.arange(s))
    hqf, mla_in, hv, kt, vt, kpet, logft = in_proj(x, g(0), w["w_in3"], w["w_t"], cos_c, sin_c,
                                                   b_fox[:, None], bsz, s, tm_in, layer, depth, shared[:4])
    qx, knt, v, ckv_n = mla_prep(mla_in, cos_r, sin_r, g_q[None, :], g_kv[None, :],
                                 w["wuq"], w["wkt"], w["wv"], bsz, s, tm_p, layer, depth, shared[4:])
    row0 = layer * bsz
    rows = lambda a: a.reshape((depth * bsz,) + a.shape[2:])
    cumt = fox_cumsum(rows(logft), row0, bsz)
    if past is None:
        t = _tile(s, cfg["t_attn"])
        o_fox = fox_prompt(hv, rows(kt), rows(vt), cumt, row0, bsz, s, t)
        o_mla = mla_prompt(qx, knt, rows(kpet), v, row0, bsz, s, t)
        o_hg, hg_state = hgrn(hqf, hv, lb[None, :], g_hout[None, :], bsz, s, CHUNK,
                              _tile(s, cfg["hg_rows"]))
    else:
        c_kt, c_vt, c_cumt, c_knt, c_kpt, c_v, c_hg = past
        p = c_kt.shape[2]
        o_fox = fox_sample(hv, rows(kt), rows(vt), c_kt, c_vt, cumt, c_cumt, bsz, s, p, layer)
        o_mla = mla_sample(qx, knt, rows(kpet), v, c_knt, c_kpt, c_v, bsz, s, p, layer)
        o_hg, hg_state = hgrn(hqf, hv, lb[None, :], g_hout[None, :], bsz, s, s, s, s0=c_hg, s0_row0=row0)
    x = merge_out(o_fox, o_mla, o_hg, x, w["w_gate"], w["wb"], w["wo"], g(0), g(1),
                  _tile(m, cfg["tm_merge"]))
    x = cross_block(x, mem_k, mem_v, w["wxq"], w["wxo"], g(2), g(3), bsz, s, _tile(s, cfg["tm_cross"]),
                    mem_row0)
    x = mlp_block(x, w["wu3"], w["wd3"], g(4), g(5), _tile(m, cfg["tm_mlp"]))
    return x, (kt, vt, kpet, logft, ckv_n), hg_state


def _from_feature_major(stacked, heads):
    a = jnp.swapaxes(stacked, 2, 3)
    if heads:
        a = a.reshape(a.shape[:3] + (heads, a.shape[3] // heads))
    return a


def _assemble_states(shared, hg_states, bsz, s):
    kt, vt, kpet, logft, ckv = shared
    return (_from_feature_major(kt, FOX_HEADS), _from_feature_major(vt, FOX_HEADS),
            _from_feature_major(logft, 0), ckv.reshape(ckv.shape[0], bsz, s, MLA_KV_RANK),
            _from_feature_major(kpet, 0), jnp.stack(hg_states))


_CFG = dict(tm_in=1024, tm_prep=1024, t_attn=512, hg_rows=512, tm_merge=512, tm_cross=1024,
            tm_mlp=1024, tm_mem=512, tm_expand=4096)


def kernel(x_prompt, x_sample, cache_fox_k, cache_fox_v, cache_fox_logf, cache_mla_ckv, cache_mla_kpe,
           state_hgrn, cache_mem_k, cache_mem_v, mem_prompt, w_in, b_fox, g_mla_q, w_mla_uq, g_mla_kv,
           w_mla_ukv, g_hgrn_out, lb_hgrn, w_branch, w_out, w_xq, w_mem_k, w_mem_v, w_xo, w_up, w_down,
           g_norm):
    cfg = _CFG
    depth = w_in.shape[0]
    lb_p = jax.nn.softmax(lb_hgrn.astype(F32), axis=0)
    lb_all = jnp.cumsum(lb_p, axis=0) - lb_p[0]
    ws = [_prep_layer_weights(w_in[l], w_mla_uq[l], w_mla_ukv[l], w_branch[l], w_out[l], w_xq[l],
                              w_mem_k[l], w_mem_v[l], w_xo[l], w_up[l], w_down[l]) for l in range(depth)]

    def run_layer(x, bsz, s, pos0, l, mk, mv, mem_row0, past, shared):
        return _layer(x, bsz, s, pos0, ws[l], lb_all[l], b_fox[l], g_mla_q[l], g_mla_kv[l],
                      g_hgrn_out[l], g_norm[l], mk, mv, mem_row0, past, cfg, l, depth, shared)

    bp, sp, _ = x_prompt.shape
    x = x_prompt.reshape(bp * sp, D_MODEL)
    mem = mem_prompt.reshape(bp * N_MEM, D_MODEL)
    shared, hg_states, p_mem = (), [], []
    for l in range(depth):
        mk, mv = mem_kv(mem, ws[l]["wmem"], _tile(bp * N_MEM, cfg["tm_mem"]))
        x, shared, hg = run_layer(x, bp, sp, 0, l, mk, mv, 0, None, shared)
        hg_states.append(hg)
        p_mem.append((mk.reshape(bp, N_MEM, X_HEADS, X_DIM), mv.reshape(bp, N_MEM, X_HEADS, X_DIM)))
    y_prompt = x.reshape(bp, sp, D_MODEL)
    p_out = _assemble_states(shared, hg_states, bp, sp) + tuple(jnp.stack(a) for a in zip(*p_mem))

    bs, ts, _ = x_sample.shape
    p = cache_fox_k.shape[2]
    fm = lambda c: jnp.moveaxis(c, 2, -1)
    c_kt = fm(cache_fox_k).reshape(depth * bs, BRANCH_W, p)
    c_vt = fm(cache_fox_v).reshape(depth * bs, BRANCH_W, p)
    c_kpt = fm(cache_mla_kpe).reshape(depth * bs, MLA_ROPE, p)
    c_cumt = fox_cumsum(fm(cache_fox_logf).reshape(depth * bs, FOX_HEADS, p), 0, depth * bs)
    c_ckv = cache_mla_ckv.reshape(depth * bs * p, MLA_KV_RANK)
    c_hg = state_hgrn.reshape((depth * bs,) + state_hgrn.shape[2:])
    c_mk = cache_mem_k.reshape(depth * bs * N_MEM, X_HEADS * X_DIM)
    c_mv = cache_mem_v.reshape(depth * bs * N_MEM, X_HEADS * X_DIM)
    x = x_sample.reshape(bs * ts, D_MODEL)
    shared, hg_states = (), []
    for l in range(depth):
        c_knt, c_v = latent_expand(c_ckv, ws[l]["wkt"], ws[l]["wv"], bs, p, _tile(p, cfg["tm_expand"]), l)
        past = (c_kt, c_vt, c_cumt, c_knt, c_kpt, c_v, c_hg)
        x, shared, hg = run_layer(x, bs, ts, p, l, c_mk, c_mv, l * bs, past, shared)
        hg_states.append(hg)
    y_sample = x.reshape(bs, ts, D_MODEL)
    return (y_prompt, y_sample, *p_out, *_assemble_states(shared, hg_states, bs, ts))
```

```python
import functools

import numpy as np
import jax
import jax.numpy as jnp
from jax import lax
from jax.experimental import pallas as pl
from jax.experimental.pallas import tpu as pltpu

F32 = jnp.float32
BF16 = jnp.bfloat16

D_MODEL = 1024
CHUNK = 64
N_MEM = 256
EPS = 1e-6
NEG_BIG = -1e30
EXP_CLIP = 80.0
FOX_HEADS = 8
FOX_DIM = 64
MLA_HEADS = 4
MLA_Q_RANK = 384
MLA_KV_RANK = 256
MLA_NOPE = 128
MLA_ROPE = 64
MLA_V = 128
ROPE_THETA = 10000.0
HG_HEADS = 4
HG_DK = 128
HG_DV = 128
X_HEADS = 4
X_DIM = 128
D_FF = 4 * D_MODEL
BRANCH_W = 512
IN_SIZES = (512, 512, 512, FOX_HEADS, MLA_Q_RANK, MLA_KV_RANK, MLA_ROPE, 512, 512, 512, 512,
            D_MODEL, D_MODEL, D_MODEL)

LANE = 128
SUB_BLOCK = 16
VMEM_LIMIT = 56 * 1024 * 1024
LOG2E = 1.4426950408889634

IN_TN = 1664
HQF_W, MLA_W, HV_W = 1024, MLA_Q_RANK + MLA_KV_RANK, 1536
OFF_FQ = 1024
T_FK, T_FV, T_KPE, T_KPE_SW, T_FF, NT_IN = 0, 512, 1024, 1088, 1152, 1160


def _params(sem, vmem=VMEM_LIMIT):
    return pltpu.CompilerParams(dimension_semantics=sem, vmem_limit_bytes=vmem)


def _dot(a, b):
    return jnp.dot(a, b, preferred_element_type=F32)


def _dot_nt(a, b):
    return lax.dot_general(a, b, (((1,), (1,)), ((), ())), preferred_element_type=F32)


def _dot_tn(a, b):
    return lax.dot_general(a, b, (((0,), (0,)), ((), ())), preferred_element_type=F32)


def _rms(x, g):
    y = x * lax.rsqrt(jnp.mean(x * x, axis=-1, keepdims=True) + EPS)
    return y * g


def _log_sigmoid(z):
    return jnp.minimum(z, 0.0) - jnp.log(1.0 + jnp.exp(-jnp.abs(z)))


def _sigmoid(z):
    return 1.0 / (1.0 + jnp.exp(-z))


def _resident(shape):
    nd = len(shape)
    return pl.BlockSpec(shape, lambda *_: (0,) * nd, pipeline_mode=pl.Buffered(1))


def _in_proj_kernel(x_ref, g_ref, w_ref, wt_ref, cos_ref, sin_ref, bf_ref, *rest):
    hqf_ref, mla_ref, hv_ref, kt_ref, vt_ref, kpe_ref, lf_ref, h_ref = rest[-8:]
    j = pl.program_id(1)

    @pl.when(j == 0)
    def _():
        h = _rms(x_ref[...], g_ref[...]).astype(BF16)
        h_ref[...] = h
        yt = _dot_nt(wt_ref[...], h)
        kt_ref[0] = yt[T_FK:T_FK + BRANCH_W]
        vt_ref[0] = yt[T_FV:T_FV + BRANCH_W]
        kpe_ref[0] = (yt[T_KPE:T_KPE + MLA_ROPE] * cos_ref[...]
                      + yt[T_KPE_SW:T_KPE_SW + MLA_ROPE] * sin_ref[...])
        lf_ref[0] = _log_sigmoid(yt[T_FF:T_FF + FOX_HEADS] + bf_ref[...])
        y = _dot(h, w_ref[:, :IN_TN])
        hqf_ref[...] = y[:, :HQF_W]
        mla_ref[...] = y[:, HQF_W:]

    @pl.when(j == 1)
    def _():
        hv_ref[...] = _dot(h_ref[...], w_ref[:, IN_TN:])[:, :HV_W].astype(BF16)


def in_proj(x, g, w3, wt, cos_t, sin_t, b_col, bsz, s, tm, layer, depth, prev):
    m, k = x.shape
    assert w3.shape[1] == 2 * IN_TN and wt.shape[0] == NT_IN
    nt = s // tm
    ntab = cos_t.shape[1] // tm
    feats = (BRANCH_W, BRANCH_W, MLA_ROPE, FOX_HEADS)
    tspec = lambda rows: pl.BlockSpec((None, 1, rows, tm), lambda i, j: (layer, i // nt, 0, i % nt))
    row = lambda w: pl.BlockSpec((tm, w), lambda i, j: (i, 0))
    n_in, n_row = 7, 3
    return pl.pallas_call(
        _in_proj_kernel,
        out_shape=(jax.ShapeDtypeStruct((m, HQF_W), F32), jax.ShapeDtypeStruct((m, MLA_W), F32),
                   jax.ShapeDtypeStruct((m, HV_W), BF16))
        + tuple(jax.ShapeDtypeStruct((depth, bsz, f, s), F32) for f in feats),
        grid=(m // tm, 2),
        in_specs=[pl.BlockSpec((tm, k), lambda i, j: (i, 0)),
                  pl.BlockSpec((1, k), lambda i, j: (0, 0)),
                  _resident(w3.shape),
                  _resident(wt.shape),
                  pl.BlockSpec((MLA_ROPE, tm), lambda i, j: (0, i % ntab)),
                  pl.BlockSpec((MLA_ROPE, tm), lambda i, j: (0, i % ntab)),
                  pl.BlockSpec((FOX_HEADS, 1), lambda i, j: (0, 0))]
        + [pl.BlockSpec(memory_space=pl.ANY)] * len(prev),
        out_specs=(row(HQF_W), row(MLA_W), row(HV_W)) + tuple(tspec(f) for f in feats),
        scratch_shapes=[pltpu.VMEM((tm, k), BF16)],
        input_output_aliases={n_in + i: n_row + i for i in range(len(prev))},
        compiler_params=_params(("parallel", "arbitrary")),
        name="in_proj",
    )(x, g, w3, wt, cos_t, sin_t, b_col, *prev)


def _mla_prep_kernel(in_ref, cs_ref, sn_ref, gq_ref, gkv_ref, wuq_ref, wkt_ref, wv_ref, *rest):
    qx_ref, knt_ref, v_ref, ckvn_ref = rest[-4:]
    cos_t = cs_ref[...]
    sin_t = sn_ref[...]
    qn = _rms(in_ref[:, :MLA_Q_RANK], gq_ref[...]).astype(BF16)
    qall = _dot(qn, wuq_ref[...]) * ((MLA_NOPE + MLA_ROPE) ** -0.5 * LOG2E)
    for h in range(MLA_HEADS):
        lo = h * LANE
        qr = (qall[:, 512 + lo:512 + lo + LANE] * cos_t
              + qall[:, 1024 + lo:1024 + lo + LANE] * sin_t)
        qx_ref[:, 2 * lo:2 * lo + LANE] = qall[:, lo:lo + LANE].astype(BF16)
        qx_ref[:, 2 * lo + LANE:2 * lo + 2 * LANE] = qr.astype(BF16)
    ckvn = _rms(in_ref[:, MLA_Q_RANK:], gkv_ref[...])
    ckvn_ref[...] = ckvn
    cb = ckvn.astype(BF16)
    knt_ref[0] = _dot_nt(wkt_ref[...], cb).astype(BF16)
    v_ref[...] = _dot(cb, wv_ref[...]).astype(BF16)


def mla_prep(mla_in, cos_t, sin_t, gq, gkv, wuq, wkt, wv, bsz, s, tm, layer, depth, prev):
    m = mla_in.shape[0]
    nt = s // tm
    ntab = cos_t.shape[0] // tm
    row = lambda w: pl.BlockSpec((tm, w), lambda i: (i, 0))
    full = lambda a: pl.BlockSpec(a.shape, lambda i: (0,) * a.ndim)
    n_in = 8
    return pl.pallas_call(
        _mla_prep_kernel,
        out_shape=(jax.ShapeDtypeStruct((m, 1024), BF16),
                   jax.ShapeDtypeStruct((bsz, 512, s), BF16),
                   jax.ShapeDtypeStruct((m, 512), BF16),
                   jax.ShapeDtypeStruct((depth, m, MLA_KV_RANK), F32)),
        grid=(m // tm,),
        in_specs=[row(MLA_W),
                  pl.BlockSpec((tm, LANE), lambda i: (i % ntab, 0)),
                  pl.BlockSpec((tm, LANE), lambda i: (i % ntab, 0)),
                  full(gq), full(gkv), full(wuq), full(wkt), full(wv)]
        + [pl.BlockSpec(memory_space=pl.ANY)] * len(prev),
        out_specs=(row(1024), pl.BlockSpec((1, 512, tm), lambda i: (i // nt, 0, i % nt)),
                   row(512), pl.BlockSpec((None, tm, MLA_KV_RANK), lambda i: (layer, i, 0))),
        input_output_aliases={n_in + i: 3 + i for i in range(len(prev))},
        compiler_params=_params(("parallel",)),
        name="mla_prep",
    )(mla_in, cos_t, sin_t, gq, gkv, wuq, wkt, wv, *prev)


def _cumsum_kernel(x_ref, c_ref, *, w):
    s = x_ref.shape[1]
    r = lax.broadcasted_iota(jnp.int32, (w, w), 0)
    c = lax.broadcasted_iota(jnp.int32, (w, w), 1)
    upper = (r <= c).astype(BF16)
    n = x_ref.shape[0]

    def running(xg):
        h1 = xg.astype(BF16)
        r1 = xg - h1.astype(F32)
        h2 = r1.astype(BF16)
        h3 = (r1 - h2.astype(F32)).astype(BF16)
        parts = _dot(jnp.concatenate([h1, h2, h3], axis=0), upper)
        return (parts[:n] + parts[n:2 * n]) + parts[2 * n:]

    local = [running(x_ref[:, g * w:(g + 1) * w]) for g in range(s // w)]
    carry = jnp.zeros((n, 1), F32)
    for g, cum in enumerate(local):
        cum = cum + carry
        c_ref[:, g * w:(g + 1) * w] = cum
        carry = cum[:, w - 1:w]


def fox_cumsum(x, row0, bsz):
    n, h, s = x.shape
    rows = bsz * h
    assert (row0 * h) % rows == 0
    out = pl.pallas_call(
        functools.partial(_cumsum_kernel, w=min(LANE, s)),
        out_shape=jax.ShapeDtypeStruct((rows, s), F32),
        grid=(1,),
        in_specs=[pl.BlockSpec((rows, s), lambda i: (row0 * h // rows, 0))],
        out_specs=pl.BlockSpec((rows, s), lambda i: (0, 0)),
        compiler_params=_params(("arbitrary",)),
        name="fox_cumsum",
    )(x.reshape(n * h, s))
    return out.reshape(bsz, h, s)


def _pair_rows_mask(hh):
    sub = lax.broadcasted_iota(jnp.int32, (LANE, 1), 0)
    return (sub < FOX_DIM) if hh == 0 else (sub >= FOX_DIM)


def _fox_finish(acc0, acc1):
    lane = lax.broadcasted_iota(jnp.int32, (1, LANE), 1)
    o0 = acc0 / pltpu.roll(acc0, FOX_DIM, axis=1)
    o1 = acc1 / pltpu.roll(acc1, FOX_DIM, axis=1)
    return jnp.where(lane < FOX_DIM, o0, o1)


def _causal_schedule(nq):
    todo = {i: list(range(i + 1)) for i in range(nq)}
    order = []
    while any(todo.values()):
        for i in reversed(range(nq)):
            if todo[i]:
                order.append((i, todo[i].pop(0)))
    return order


def _fox_prompt_kernel(q_ref, kt_ref, vt_ref, ct_ref, o_ref, *, t, nq):
    hp = pl.program_id(1)
    lane = lax.broadcasted_iota(jnp.int32, (1, LANE), 1)
    lo = lane < FOX_DIM
    span = lambda i: slice(i * t, (i + 1) * t)
    ct = [ct_ref[0, 2 * hp + hh] for hh in range(2)]
    qs, cref = [], []
    for i in range(nq):
        q = q_ref[span(i), :].astype(F32) * (FOX_DIM ** -0.5 * LOG2E)
        qs.append((jnp.where(lo, q, 0.0).astype(BF16), jnp.where(lo, 0.0, q).astype(BF16)))
        cref.append([c[:, i * t:i * t + 1] for c in ct])
    kts = [kt_ref[0, :, span(j)].astype(BF16) for j in range(nq)]
    vts = [[jnp.where(_pair_rows_mask(hh), vt_ref[0, :, span(j)], 1.0).astype(BF16) for hh in range(2)]
           for j in range(nq)]
    causal = (lax.broadcasted_iota(jnp.int32, (t, t), 0) >= lax.broadcasted_iota(jnp.int32, (t, t), 1))

    def scores(i, j):
        out = []
        for hh in range(2):
            s = _dot(qs[i][hh], kts[j]) + (cref[i][hh] - ct[hh][:, span(j)]) * LOG2E
            out.append(jnp.where(causal, s, NEG_BIG) if i == j else s)
        return out

    state = [[(jnp.full((t, 1), NEG_BIG, F32), jnp.zeros((t, LANE), F32)) for _ in range(2)]
             for _ in range(nq)]
    order = _causal_schedule(nq)
    ss = scores(*order[0])
    for n, (i, j) in enumerate(order):
        nxt = scores(*order[n + 1]) if n + 1 < len(order) else None
        m_new = [jnp.maximum(state[i][hh][0], jnp.max(ss[hh], axis=1, keepdims=True)) for hh in range(2)]
        ps = [jnp.exp2(ss[hh] - m_new[hh]).astype(BF16) for hh in range(2)]
        pvs = [_dot_nt(ps[hh], vts[j][hh]) for hh in range(2)]
        state[i] = [(m_new[hh], jnp.exp2(state[i][hh][0] - m_new[hh]) * state[i][hh][1] + pvs[hh])
                    for hh in range(2)]
        ss = nxt
    for i in range(nq):
        o_ref[span(i), :] = _fox_finish(state[i][0][1], state[i][1][1]).astype(BF16)


def fox_prompt(hv, kt, vt, cumt, row0, bsz, s, t):
    m = hv.shape[0]
    return pl.pallas_call(
        functools.partial(_fox_prompt_kernel, t=t, nq=s // t),
        out_shape=jax.ShapeDtypeStruct((m, BRANCH_W), BF16),
        grid=(bsz, FOX_HEADS // 2),
        in_specs=[pl.BlockSpec((s, LANE), lambda b, h: (b, OFF_FQ // LANE + h)),
                  pl.BlockSpec((1, LANE, s), lambda b, h: (row0 + b, h, 0)),
                  pl.BlockSpec((1, LANE, s), lambda b, h: (row0 + b, h, 0)),
                  pl.BlockSpec((1, FOX_HEADS, 1, s), lambda b, h: (b, 0, 0, 0))],
        out_specs=pl.BlockSpec((s, LANE), lambda b, h: (b, h)),
        compiler_params=_params(("parallel", "parallel")),
        name="fox_prompt",
    )(hv, kt, vt, cumt.reshape(bsz, FOX_HEADS, 1, s))


def _mla_keys(knt, kpt):
    n = knt.shape[1]
    return jnp.concatenate([knt, kpt.astype(BF16), jnp.zeros((LANE - MLA_ROPE, n), BF16)], axis=0)


def _mla_prompt_kernel(q_ref, knt_ref, kpt_ref, v_ref, o_ref, *, t, nq):
    hs = range(2)
    span = lambda i: slice(i * t, (i + 1) * t)
    cols = lambda hh, w: slice(hh * w, (hh + 1) * w)
    keys = [[_mla_keys(knt_ref[0, cols(hh, LANE), span(j)], kpt_ref[0, :, span(j)]) for hh in hs]
            for j in range(nq)]
    mask = (lax.broadcasted_iota(jnp.int32, (t, t), 0) // CHUNK
            >= lax.broadcasted_iota(jnp.int32, (t, t), 1) // CHUNK)

    def scores(i, j):
        ss = [_dot(q_ref[span(i), cols(hh, 2 * LANE)], keys[j][hh]) for hh in hs]
        return [jnp.where(mask, s, NEG_BIG) for s in ss] if i == j else ss

    state = [[(jnp.full((t, 1), NEG_BIG, F32), jnp.zeros((t, 1), F32), jnp.zeros((t, LANE), F32))
              for _ in hs] for _ in range(nq)]
    order = _causal_schedule(nq)
    ss = scores(*order[0])
    for n, (i, j) in enumerate(order):
        nxt = scores(*order[n + 1]) if n + 1 < len(order) else None
        m_new = [jnp.maximum(state[i][hh][0], jnp.max(ss[hh], axis=1, keepdims=True)) for hh in hs]
        ps = [jnp.exp2(ss[hh] - m_new[hh]) for hh in hs]
        pvs = [_dot(ps[hh].astype(BF16), v_ref[span(j), cols(hh, LANE)]) for hh in hs]
        new = []
        for hh in hs:
            alpha = jnp.exp2(state[i][hh][0] - m_new[hh])
            new.append((m_new[hh], alpha * state[i][hh][1] + jnp.sum(ps[hh], axis=1, keepdims=True),
                        alpha * state[i][hh][2] + pvs[hh]))
        state[i] = new
        ss = nxt
    for i in range(nq):
        o_ref[span(i), :] = jnp.concatenate([state[i][hh][2] / state[i][hh][1] for hh in hs],
                                            axis=1).astype(BF16)


def mla_prompt(qx, knt, kpt, v, row0, bsz, s, t):
    assert t % CHUNK == 0
    m = qx.shape[0]
    return pl.pallas_call(
        functools.partial(_mla_prompt_kernel, t=t, nq=s // t),
        out_shape=jax.ShapeDtypeStruct((m, BRANCH_W), BF16),
        grid=(bsz, MLA_HEADS // 2),
        in_specs=[pl.BlockSpec((s, 4 * LANE), lambda b, h: (b, h)),
                  pl.BlockSpec((1, 2 * LANE, s), lambda b, h: (b, h, 0)),
                  pl.BlockSpec((1, MLA_ROPE, s), lambda b, h: (row0 + b, 0, 0)),
                  pl.BlockSpec((s, 2 * LANE), lambda b, h: (b, h))],
        out_specs=pl.BlockSpec((s, 2 * LANE), lambda b, h: (b, h)),
        compiler_params=_params(("parallel", "parallel")),
        name="mla_prompt",
    )(qx, knt, kpt, v)


def _fox_sample_kernel(q_ref, ktn_ref, vtn_ref, ktc_ref, vtc_ref, ctn_ref, ctc_ref, o_ref, *, t, p):
    lane = lax.broadcasted_iota(jnp.int32, (1, LANE), 1)
    lo = lane < FOX_DIM
    causal = (lax.broadcasted_iota(jnp.int32, (t, t), 0) >= lax.broadcasted_iota(jnp.int32, (t, t), 1))
    for hp in range(FOX_HEADS // 2):
        pair = slice(hp * LANE, (hp + 1) * LANE)
        q = q_ref[:, pair].astype(F32) * (FOX_DIM ** -0.5 * LOG2E)
        kt_c = ktc_ref[0, pair, :].astype(BF16)
        kt_n = ktn_ref[0, pair, :].astype(BF16)
        vt_c = vtc_ref[0, pair, :]
        vt_n = vtn_ref[0, pair, :]
        accs = []
        for hh in range(2):
            head = 2 * hp + hh
            qh = (jnp.where(lo, q, 0.0) if hh == 0 else jnp.where(lo, 0.0, q)).astype(BF16)
            cc = ctc_ref[0, head:head + 1, :]
            ctot = cc[:, p - 1:p]
            s_c = _dot(qh, kt_c) + (ctot - cc) * LOG2E
            s_n = _dot(qh, kt_n) - ctn_ref[0, head:head + 1, :] * LOG2E
            s_n = jnp.where(causal, s_n, NEG_BIG)
            m = jnp.maximum(jnp.max(s_c, axis=1, keepdims=True), jnp.max(s_n, axis=1, keepdims=True))
            rows = _pair_rows_mask(hh)
            accs.append(_dot_nt(jnp.exp2(s_c - m).astype(BF16), jnp.where(rows, vt_c, 1.0).astype(BF16))
                        + _dot_nt(jnp.exp2(s_n - m).astype(BF16), jnp.where(rows, vt_n, 1.0).astype(BF16)))
        o_ref[:, pair] = _fox_finish(accs[0], accs[1]).astype(BF16)


def fox_sample(hv, kt_n, vt_n, kt_c, vt_c, cumt_n, cumt_c, bsz, t, p, layer):
    cidx = lambda b: (layer * bsz + b, 0, 0)
    return pl.pallas_call(
        functools.partial(_fox_sample_kernel, t=t, p=p),
        out_shape=jax.ShapeDtypeStruct((bsz * t, BRANCH_W), BF16),
        grid=(bsz,),
        in_specs=[pl.BlockSpec((t, BRANCH_W), lambda b: (b, OFF_FQ // BRANCH_W)),
                  pl.BlockSpec((1, BRANCH_W, t), cidx),
                  pl.BlockSpec((1, BRANCH_W, t), cidx),
                  pl.BlockSpec((1, BRANCH_W, p), cidx),
                  pl.BlockSpec((1, BRANCH_W, p), cidx),
                  pl.BlockSpec((1, FOX_HEADS, t), lambda b: (b, 0, 0)),
                  pl.BlockSpec((1, FOX_HEADS, p), cidx)],
        out_specs=pl.BlockSpec((t, BRANCH_W), lambda b: (b, 0)),
        compiler_params=_params(("parallel",)),
        name="fox_sample",
    )(hv, kt_n, vt_n, kt_c, vt_c, cumt_n, cumt_c)


def _mla_sample_kernel(q_ref, kntn_ref, kptn_ref, vn_ref, ckv_ref, kptc_ref, wkt_ref, wv_ref, o_ref,
                       *, t, p):
    qc = (p + lax.broadcasted_iota(jnp.int32, (t, t), 0)) // CHUNK
    kc = (p + lax.broadcasted_iota(jnp.int32, (t, t), 1)) // CHUNK
    kpt_c, kpt_n = kptc_ref[0], kptn_ref[0]
    cb = ckv_ref[...].astype(BF16)
    for h in range(MLA_HEADS):
        cs = slice(h * LANE, (h + 1) * LANE)
        knt_c = _dot_nt(wkt_ref[cs, :], cb).astype(BF16)
        v_c = _dot(cb, wv_ref[:, cs]).astype(BF16)
        q = q_ref[:, 2 * h * LANE:2 * (h + 1) * LANE]
        s_c = _dot(q, _mla_keys(knt_c, kpt_c))
        s_n = _dot(q, _mla_keys(kntn_ref[0, cs, :], kpt_n))
        s_n = jnp.where(qc >= kc, s_n, NEG_BIG)
        m = jnp.maximum(jnp.max(s_c, axis=1, keepdims=True), jnp.max(s_n, axis=1, keepdims=True))
        p_c = jnp.exp2(s_c - m)
        p_n = jnp.exp2(s_n - m)
        l = jnp.sum(p_c, axis=1, keepdims=True) + jnp.sum(p_n, axis=1, keepdims=True)
        o = _dot(p_c.astype(BF16), v_c) + _dot(p_n.astype(BF16), vn_ref[:, cs])
        o_ref[:, cs] = (o / l).astype(BF16)


def mla_sample(qx, knt_n, kpt_n, v_n, ckv_c, kpt_c, wkt, wv, bsz, t, p, layer):
    assert (p - 1) // CHUNK <= p // CHUNK
    full = lambda a: pl.BlockSpec(a.shape, lambda b: (0,) * a.ndim)
    return pl.pallas_call(
        functools.partial(_mla_sample_kernel, t=t, p=p),
        out_shape=jax.ShapeDtypeStruct((bsz * t, BRANCH_W), BF16),
        grid=(bsz,),
        in_specs=[pl.BlockSpec((t, 8 * LANE), lambda b: (b, 0)),
                  pl.BlockSpec((1, BRANCH_W, t), lambda b: (b, 0, 0)),
                  pl.BlockSpec((1, MLA_ROPE, t), lambda b: (layer * bsz + b, 0, 0)),
                  pl.BlockSpec((t, BRANCH_W), lambda b: (b, 0)),
                  pl.BlockSpec((p, MLA_KV_RANK), lambda b: (layer * bsz + b, 0)),
                  pl.BlockSpec((1, MLA_ROPE, p), lambda b: (layer * bsz + b, 0, 0)),
                  full(wkt), full(wv)],
        out_specs=pl.BlockSpec((t, BRANCH_W), lambda b: (b, 0)),
        compiler_params=_params(("parallel",)),
        name="mla_sample",
    )(qx, knt_n, kpt_n, v_n, ckv_c, kpt_c, wkt, wv)


def _hgrn_gates(z, lb, tri):
    logf = _log_sigmoid(z) + jnp.log(1.0 + lb * jnp.exp(jnp.minimum(-z, EXP_CLIP)))
    k = (1.0 - lb) * (1.0 / (1.0 + jnp.exp(z)))
    h1 = logf.astype(BF16)
    r1 = logf - h1.astype(F32)
    h2 = r1.astype(BF16)
    h3 = (r1 - h2.astype(F32)).astype(BF16)
    hcat = jnp.concatenate([h1, h2, h3], axis=1)
    g = tri.shape[0]
    parts = jnp.concatenate([_dot(tri, hcat[r:r + g, :]) for r in range(0, z.shape[0], g)], axis=0)
    lc = ((parts[:, :LANE] + parts[:, LANE:2 * LANE]) + parts[:, 2 * LANE:]) * LOG2E
    return k, lc


def _hgrn_local(q, z, lb, tri, v_b, ln, sel, k_ref, lc_ref):
    k, lc = _hgrn_gates(z, lb, tri)
    k_ref[...] = k
    lc_ref[...] = lc
    nchunk = q.shape[0] // ln
    nsb = q.shape[0] // SUB_BLOCK
    per = ln // SUB_BLOCK
    half = SUB_BLOCK // 2
    rows = lambda a, i: a[i * SUB_BLOCK:(i + 1) * SUB_BLOCK, :]
    lcb = [jnp.zeros((1, LANE), F32) if i % per == 0 else lc_ref[i * SUB_BLOCK - 1:i * SUB_BLOCK, :]
           for i in range(nsb)]
    lcb_rows = jnp.concatenate([jnp.broadcast_to(b, (SUB_BLOCK, LANE)) for b in lcb], axis=0)
    last = [lc_ref[(c + 1) * ln - 1:(c + 1) * ln, :] for c in range(nchunk)]
    last_rows = jnp.concatenate([jnp.broadcast_to(b, (ln, LANE)) for b in last], axis=0)
    qh = (q * jnp.exp2(lc - lcb_rows)).astype(BF16)
    qe = (q * jnp.exp2(lc)).astype(BF16)
    kdec = (k * jnp.exp2(last_rows - lc)).astype(BF16)
    a_off = {}
    for i in range(nsb):
        n = (i % per) * SUB_BLOCK
        if n:
            c0 = i * SUB_BLOCK - n
            kt = (k[c0:c0 + n, :] * jnp.exp2(lcb[i] - lc[c0:c0 + n, :])).astype(BF16)
            a_off[i] = _dot_nt(rows(qh, i), kt)
    yield None
    pieces = []
    for i in range(nsb):
        q_i, lc_i = rows(q, i), rows(lc, i)
        cols = []
        for s in range(SUB_BLOCK):
            lo = 0 if s < half else half
            r = i * SUB_BLOCK + s
            d = lc_i[lo:, :] - lc_ref[r:r + 1, :]
            d = (jnp.concatenate([jnp.minimum(d[:half, :], 0.0), d[half:, :]], axis=0) if s < half
                 else jnp.minimum(d, 0.0))
            w = (q_i[lo:, :] * k_ref[r:r + 1, :]) * jnp.exp2(d)
            if lo:
                w = jnp.concatenate([jnp.zeros((lo, LANE), F32), w], axis=0)
            cols.append(w.astype(BF16))
        pieces.append(jnp.concatenate(cols, axis=1))
    a_all = _dot(jnp.concatenate(pieces, axis=0), sel)
    inc = [_dot_tn(v_b[c * ln:(c + 1) * ln, :], kdec[c * ln:(c + 1) * ln, :]) for c in range(nchunk)]
    yield None
    pair_ok = (lax.broadcasted_iota(jnp.int32, (SUB_BLOCK, LANE), 0)
               >= lax.broadcasted_iota(jnp.int32, (SUB_BLOCK, LANE), 1))
    off = []
    for i in range(nsb):
        n = (i % per) * SUB_BLOCK
        off.append(_dot(a_off[i].astype(BF16), v_b[i * SUB_BLOCK - n:i * SUB_BLOCK, :]) if n
                   else jnp.zeros((SUB_BLOCK, LANE), F32))
    diag = [_dot(jnp.where(pair_ok, rows(a_all, i), 0.0)[:, :SUB_BLOCK].astype(BF16), rows(v_b, i))
            for i in range(nsb)]
    local = jnp.concatenate(off, axis=0) + jnp.concatenate(diag, axis=0)
    dec = [jnp.exp2(b) for b in last]
    yield local, qe, inc, dec


def _hgrn_kernel(*refs, ln, nchunk, has_init):
    refs = list(refs)
    hq_ref, hf_ref, hi_ref, hg_ref, lb_ref, go_ref, sel_ref = refs[:7]
    s0_ref = refs[7] if has_init else None
    o_ref, sout_ref, st_ref, k_ref, lc_ref = refs[-5:]
    step = pl.program_id(1)
    nrows = ln * nchunk

    @pl.when(step == 0)
    def _():
        for h in range(HG_HEADS):
            st_ref[h] = s0_ref[0, h].T if has_init else jnp.zeros((HG_DV, HG_DK), F32)

    ng = min(nrows, 2 * LANE)
    assert ng % ln == 0 and nrows % ng == 0
    ri = lax.broadcasted_iota(jnp.int32, (ng, ng), 0)
    ci = lax.broadcasted_iota(jnp.int32, (ng, ng), 1)
    tri = ((ri >= ci) & (ri // ln == ci // ln)).astype(BF16)

    def finish(h, local, qe, inc, dec):
        cs = slice(h * LANE, (h + 1) * LANE)
        st = st_ref[h]
        parts = []
        for c in range(nchunk):
            parts.append(_dot_nt(qe[c * ln:(c + 1) * ln, :], st.astype(BF16)))
            st = st * dec[c] + inc[c]
        st_ref[h] = st
        o = local + jnp.concatenate(parts, axis=0)
        o_ref[:, cs] = (_rms(o, go_ref[...]) * _sigmoid(hg_ref[:, cs].astype(F32))).astype(BF16)

    def start(h):
        cs = slice(h * LANE, (h + 1) * LANE)
        gen = _hgrn_local(hq_ref[:, cs], hf_ref[:, cs], lb_ref[:, cs], tri, hi_ref[:, cs].astype(BF16),
                          ln, sel_ref[...], k_ref.at[h], lc_ref.at[h])
        next(gen)
        return gen

    gens = {0: start(0)}
    for h in range(HG_HEADS):
        if h + 1 < HG_HEADS:
            gens[h + 1] = start(h + 1)
        next(gens[h])
        if h > 0:
            finish(h - 1, *next(gens.pop(h - 1)))
    finish(HG_HEADS - 1, *next(gens.pop(HG_HEADS - 1)))

    @pl.when(step == pl.num_programs(1) - 1)
    def _():
        for h in range(HG_HEADS):
            sout_ref[0, h] = st_ref[h].T


def hgrn(hqf, hv, lb, g_out, bsz, s, ln, rows, s0=None, s0_row0=0):
    m = hqf.shape[0]
    ns = s // rows
    has_init = s0 is not None
    sel = (np.arange(SUB_BLOCK * LANE)[:, None] // LANE == np.arange(LANE)[None, :])
    sel = jnp.asarray(sel, BF16)
    blk = lambda col: pl.BlockSpec((rows, BRANCH_W), lambda b, i: (b * ns + i, col))
    ins = [hqf, hqf, hv, hv, lb, g_out, sel]
    specs = [blk(0), blk(1), blk(0), blk(1),
             pl.BlockSpec((1, BRANCH_W), lambda b, i: (0, 0)),
             pl.BlockSpec((1, HG_DV), lambda b, i: (0, 0)),
             pl.BlockSpec(sel.shape, lambda b, i: (0, 0))]
    if has_init:
        ins.append(s0)
        specs.append(pl.BlockSpec((1, HG_HEADS, HG_DK, HG_DV), lambda b, i: (s0_row0 + b, 0, 0, 0)))
    return pl.pallas_call(
        functools.partial(_hgrn_kernel, ln=ln, nchunk=rows // ln, has_init=has_init),
        out_shape=(jax.ShapeDtypeStruct((m, BRANCH_W), BF16),
                   jax.ShapeDtypeStruct((bsz, HG_HEADS, HG_DK, HG_DV), F32)),
        grid=(bsz, ns),
        in_specs=specs,
        out_specs=(pl.BlockSpec((rows, BRANCH_W), lambda b, i: (b * ns + i, 0)),
                   pl.BlockSpec((1, HG_HEADS, HG_DK, HG_DV), lambda b, i: (b, 0, 0, 0))),
        scratch_shapes=[pltpu.VMEM((HG_HEADS, HG_DV, HG_DK), F32),
                        pltpu.VMEM((HG_HEADS, rows, LANE), F32),
                        pltpu.VMEM((HG_HEADS, rows, LANE), F32)],
        compiler_params=_params(("parallel", "arbitrary")),
        name="hgrn",
    )(*ins)


def _merge_kernel(of_ref, om_ref, oh_ref, x_ref, wg_ref, wb_ref, wo_ref, g0_ref, g1_ref, o_ref):
    x = x_ref[...]
    h = _rms(x, g0_ref[...]).astype(BF16)
    branches = (of_ref, om_ref, oh_ref)
    gates = [_dot(h, wg_ref[i]) for i in range(3)]
    outs = [_dot(branches[i][...], wb_ref[i]) for i in range(3)]
    merged = (_sigmoid(gates[0]) * outs[0] + _sigmoid(gates[1]) * outs[1]) + _sigmoid(gates[2]) * outs[2]
    y = _dot(merged.astype(BF16), wo_ref[...])
    o_ref[...] = x + _rms(y, g1_ref[...])


def merge_out(o_fox, o_mla, o_hg, x, wg, wb, wo, g0, g1, tm):
    m = x.shape[0]
    row = lambda w: pl.BlockSpec((tm, w), lambda i: (i, 0))
    vec = pl.BlockSpec((1, D_MODEL), lambda i: (0, 0))
    return pl.pallas_call(
        _merge_kernel,
        out_shape=jax.ShapeDtypeStruct((m, D_MODEL), F32),
        grid=(m // tm,),
        in_specs=[row(BRANCH_W), row(BRANCH_W), row(BRANCH_W), row(D_MODEL),
                  _resident(wg.shape), _resident(wb.shape), _resident(wo.shape), vec, vec],
        out_specs=row(D_MODEL),
        compiler_params=_params(("parallel",)),
        name="merge_out",
    )(o_fox, o_mla, o_hg, x, wg, wb, wo, g0, g1)


def _matmul2_kernel(x_ref, w_ref, a_ref, b_ref):
    y = _dot(x_ref[...].astype(BF16), w_ref[...])
    n = a_ref.shape[1]
    a_ref[...] = y[:, :n]
    b_ref[...] = y[:, n:]


def mem_kv(mem, w, tm):
    m, k = mem.shape
    n = w.shape[1] // 2
    row = lambda w_: pl.BlockSpec((tm, w_), lambda i: (i, 0))
    return pl.pallas_call(
        _matmul2_kernel,
        out_shape=(jax.ShapeDtypeStruct((m, n), F32), jax.ShapeDtypeStruct((m, n), F32)),
        grid=(m // tm,),
        in_specs=[row(k), _resident(w.shape)],
        out_specs=(row(n), row(n)),
        compiler_params=_params(("parallel",)),
        name="mem_kv",
    )(mem, w)


def _cross_kernel(x_ref, mk_ref, mv_ref, wq_ref, wo_ref, g2_ref, g3_ref, o_ref):
    x = x_ref[...]
    h = _rms(x, g2_ref[...]).astype(BF16)
    q = _dot(h, wq_ref[...])
    qb = (q * (X_DIM ** -0.5 * LOG2E)).astype(BF16)
    cols = [slice(hd * X_DIM, (hd + 1) * X_DIM) for hd in range(X_HEADS)]
    ss = [_dot_nt(qb[:, cs], mk_ref[:, cs].astype(BF16)) for cs in cols]
    ps = [jnp.exp2(s - jnp.max(s, axis=1, keepdims=True)) for s in ss]
    pvs = [_dot(p.astype(BF16), mv_ref[:, cs].astype(BF16)) for p, cs in zip(ps, cols)]
    outs = [pv / jnp.sum(p, axis=1, keepdims=True) for pv, p in zip(pvs, ps)]
    ox = jnp.concatenate(outs, axis=1).astype(BF16)
    o_ref[...] = x + _rms(_dot(ox, wo_ref[...]), g3_ref[...])


def cross_block(x, mk, mv, wq, wo, g2, g3, bsz, s, tm, mem_row0):
    m = x.shape[0]
    nt = s // tm
    vec = pl.BlockSpec((1, D_MODEL), lambda b, i: (0, 0))
    return pl.pallas_call(
        _cross_kernel,
        out_shape=jax.ShapeDtypeStruct((m, D_MODEL), F32),
        grid=(bsz, nt),
        in_specs=[pl.BlockSpec((tm, D_MODEL), lambda b, i: (b * nt + i, 0)),
                  pl.BlockSpec((N_MEM, X_HEADS * X_DIM), lambda b, i: (mem_row0 + b, 0)),
                  pl.BlockSpec((N_MEM, X_HEADS * X_DIM), lambda b, i: (mem_row0 + b, 0)),
                  _resident(wq.shape), _resident(wo.shape), vec, vec],
        out_specs=pl.BlockSpec((tm, D_MODEL), lambda b, i: (b * nt + i, 0)),
        compiler_params=_params(("parallel", "parallel")),
        name="cross_attn",
    )(x, mk, mv, wq, wo, g2, g3)


def _mlp_kernel(x_ref, wu_ref, wd_ref, g4_ref, g5_ref, o_ref, h_ref, acc_ref):
    j = pl.program_id(1)

    @pl.when(j == 0)
    def _():
        h_ref[...] = _rms(x_ref[...], g4_ref[...]).astype(BF16)
        acc_ref[...] = jnp.zeros_like(acc_ref)

    cols = pl.ds(pl.multiple_of(j * D_MODEL, D_MODEL), D_MODEL)
    u = jnp.square(jnp.maximum(_dot(h_ref[...], wu_ref[:, cols]), 0.0)).astype(BF16)
    acc_ref[...] += _dot(u, wd_ref[j])

    @pl.when(j == pl.num_programs(1) - 1)
    def _():
        o_ref[...] = x_ref[...] + _rms(acc_ref[...], g5_ref[...])


def mlp_block(x, wu3, wd3, g4, g5, tm):
    m = x.shape[0]
    nj = wd3.shape[0]
    vec = pl.BlockSpec((1, D_MODEL), lambda i, j: (0, 0))
    return pl.pallas_call(
        _mlp_kernel,
        out_shape=jax.ShapeDtypeStruct((m, D_MODEL), F32),
        grid=(m // tm, nj),
        in_specs=[pl.BlockSpec((tm, D_MODEL), lambda i, j: (i, 0)),
                  _resident(wu3.shape), _resident(wd3.shape), vec, vec],
        out_specs=pl.BlockSpec((tm, D_MODEL), lambda i, j: (i, 0)),
        scratch_shapes=[pltpu.VMEM((tm, D_MODEL), BF16), pltpu.VMEM((tm, D_MODEL), F32)],
        compiler_params=_params(("parallel", "arbitrary")),
        name="mlp",
    )(x, wu3, wd3, g4, g5)


def _prep_layer_weights(w_in, w_mla_uq, w_mla_ukv, w_branch, w_out, w_xq, w_mem_k, w_mem_v, w_xo,
                        w_up, w_down):
    idx = np.cumsum((0,) + IN_SIZES)
    seg = lambda i: w_in[:, idx[i]:idx[i + 1]]
    fq, fk, fv, ff, cq, ckv, kpe, hq, hf, hi, hg, ga, gb, gc = (seg(i) for i in range(14))
    half = MLA_ROPE // 2
    kpe_sw = jnp.concatenate([kpe[:, half:], kpe[:, :half]], axis=1)
    pad = jnp.zeros((D_MODEL, IN_TN - HV_W), w_in.dtype)
    w_p = jnp.concatenate([hq, hf, cq, ckv, hi, hg, fq, pad], axis=1).astype(BF16)
    w_gate = jnp.stack([ga, gb, gc]).astype(BF16)
    w_t = jnp.concatenate([fk, fv, kpe, kpe_sw, ff], axis=1).T.astype(BF16)
    hd = MLA_NOPE + MLA_ROPE
    zq = jnp.zeros((MLA_Q_RANK, LANE - MLA_ROPE), w_mla_uq.dtype)
    nope, rope_n, rope_s = [], [], []
    for h in range(MLA_HEADS):
        base = h * hd
        nope.append(w_mla_uq[:, base:base + MLA_NOPE])
        x1 = w_mla_uq[:, base + MLA_NOPE:base + MLA_NOPE + half]
        x2 = w_mla_uq[:, base + MLA_NOPE + half:base + hd]
        rope_n += [x1, x2, zq]
        rope_s += [x2, x1, zq]
    wuq = jnp.concatenate(nope + rope_n + rope_s, axis=1).astype(BF16)
    kvd = MLA_NOPE + MLA_V
    wkt = jnp.concatenate([w_mla_ukv[:, h * kvd:h * kvd + MLA_NOPE] for h in range(MLA_HEADS)],
                          axis=1).T.astype(BF16)
    wv = jnp.concatenate([w_mla_ukv[:, h * kvd + MLA_NOPE:(h + 1) * kvd] for h in range(MLA_HEADS)],
                         axis=1).astype(BF16)
    nff = D_FF // D_MODEL
    return dict(
        w_in3=w_p, w_t=w_t, w_gate=w_gate, wuq=wuq, wkt=wkt, wv=wv,
        wb=w_branch.astype(BF16), wo=w_out.astype(BF16), wxq=w_xq.astype(BF16), wxo=w_xo.astype(BF16),
        wmem=jnp.concatenate([w_mem_k, w_mem_v], axis=1).astype(BF16),
        wu3=w_up.astype(BF16),
        wd3=w_down.astype(BF16).reshape(nff, D_MODEL, D_MODEL))


def _rope_tables(pos):
    half = MLA_ROPE // 2
    freq = ROPE_THETA ** (-jnp.arange(half, dtype=F32) / half)
    ang = pos.astype(F32)[:, None] * freq[None, :]
    cos, sin = jnp.cos(ang), jnp.sin(ang)
    z = jnp.zeros((pos.shape[0], LANE - MLA_ROPE), F32)
    cos_r = jnp.concatenate([cos, cos, z], axis=1)
    sin_r = jnp.concatenate([-sin, sin, z], axis=1)
    return cos_r, sin_r, cos_r[:, :MLA_ROPE].T, sin_r[:, :MLA_ROPE].T


def _tile(n, pref):
    t = min(n, pref)
    assert n % t == 0
    return t


def _layer(x, bsz, s, pos0, w, lb, b_fox, g_q, g_kv, g_hout, g_norm, mem_k, mem_v, mem_row0, past, cfg,
           layer, depth, shared):
    m = bsz * s
    g = lambda i: g_norm[i][None, :]
    tm_in = _tile(s, cfg["tm_in"])
    tm_p = _tile(s, cfg["tm_prep"])
    cos_r, sin_r, cos_c, sin_c = _rope_tables(pos0 + jnp.arange(s))
    hqf, mla_in, hv, kt, vt, kpet, logft = in_proj(x, g(0), w["w_in3"], w["w_t"], cos_c, sin_c,
                                                   b_fox[:, None], bsz, s, tm_in, layer, depth, shared[:4])
    qx, knt, v, ckv_n = mla_prep(mla_in, cos_r, sin_r, g_q[None, :], g_kv[None, :],
                                 w["wuq"], w["wkt"], w["wv"], bsz, s, tm_p, layer, depth, shared[4:])
    row0 = layer * bsz
    rows = lambda a: a.reshape((depth * bsz,) + a.shape[2:])
    cumt = fox_cumsum(rows(logft), row0, bsz)
    if past is None:
        t = _tile(s, cfg["t_attn"])
        o_fox = fox_prompt(hv, rows(kt), rows(vt), cumt, row0, bsz, s, t)
        o_mla = mla_prompt(qx, knt, rows(kpet), v, row0, bsz, s, t)
        o_hg, hg_state = hgrn(hqf, hv, lb[None, :], g_hout[None, :], bsz, s, CHUNK,
                              _tile(s, cfg["hg_rows"]))
    else:
        c_kt, c_vt, c_cumt, c_ckv, c_kpt, c_hg = past
        p = c_kt.shape[2]
        o_fox = fox_sample(hv, rows(kt), rows(vt), c_kt, c_vt, cumt, c_cumt, bsz, s, p, layer)
        o_mla = mla_sample(qx, knt, rows(kpet), v, c_ckv, c_kpt, w["wkt"], w["wv"], bsz, s, p, layer)
        o_hg, hg_state = hgrn(hqf, hv, lb[None, :], g_hout[None, :], bsz, s, s, s, s0=c_hg, s0_row0=row0)
    x = merge_out(o_fox, o_mla, o_hg, x, w["w_gate"], w["wb"], w["wo"], g(0), g(1),
                  _tile(m, cfg["tm_merge"]))
    x = cross_block(x, mem_k, mem_v, w["wxq"], w["wxo"], g(2), g(3), bsz, s, _tile(s, cfg["tm_cross"]),
                    mem_row0)
    x = mlp_block(x, w["wu3"], w["wd3"], g(4), g(5), _tile(m, cfg["tm_mlp"]))
    return x, (kt, vt, kpet, logft, ckv_n), hg_state


def _from_feature_major(stacked, heads):
    a = jnp.swapaxes(stacked, 2, 3)
    if heads:
        a = a.reshape(a.shape[:3] + (heads, a.shape[3] // heads))
    return a


def _assemble_states(shared, hg_states, bsz, s):
    kt, vt, kpet, logft, ckv = shared
    return (_from_feature_major(kt, FOX_HEADS), _from_feature_major(vt, FOX_HEADS),
            _from_feature_major(logft, 0), ckv.reshape(ckv.shape[0], bsz, s, MLA_KV_RANK),
            _from_feature_major(kpet, 0), jnp.stack(hg_states))


_CFG = dict(tm_in=1024, tm_prep=1024, t_attn=512, hg_rows=512, tm_merge=512, tm_cross=1024,
            tm_mlp=1024, tm_mem=512)


def kernel(x_prompt, x_sample, cache_fox_k, cache_fox_v, cache_fox_logf, cache_mla_ckv, cache_mla_kpe,
           state_hgrn, cache_mem_k, cache_mem_v, mem_prompt, w_in, b_fox, g_mla_q, w_mla_uq, g_mla_kv,
           w_mla_ukv, g_hgrn_out, lb_hgrn, w_branch, w_out, w_xq, w_mem_k, w_mem_v, w_xo, w_up, w_down,
           g_norm):
    cfg = _CFG
    depth = w_in.shape[0]
    lb_p = jax.nn.softmax(lb_hgrn.astype(F32), axis=0)
    lb_all = jnp.cumsum(lb_p, axis=0) - lb_p[0]
    ws = [_prep_layer_weights(w_in[l], w_mla_uq[l], w_mla_ukv[l], w_branch[l], w_out[l], w_xq[l],
                              w_mem_k[l], w_mem_v[l], w_xo[l], w_up[l], w_down[l]) for l in range(depth)]

    def run_layer(x, bsz, s, pos0, l, mk, mv, mem_row0, past, shared):
        return _layer(x, bsz, s, pos0, ws[l], lb_all[l], b_fox[l], g_mla_q[l], g_mla_kv[l],
                      g_hgrn_out[l], g_norm[l], mk, mv, mem_row0, past, cfg, l, depth, shared)

    bp, sp, _ = x_prompt.shape
    x = x_prompt.reshape(bp * sp, D_MODEL)
    mem = mem_prompt.reshape(bp * N_MEM, D_MODEL)
    shared, hg_states, p_mem = (), [], []
    for l in range(depth):
        mk, mv = mem_kv(mem, ws[l]["wmem"], _tile(bp * N_MEM, cfg["tm_mem"]))
        x, shared, hg = run_layer(x, bp, sp, 0, l, mk, mv, 0, None, shared)
        hg_states.append(hg)
        p_mem.append((mk.reshape(bp, N_MEM, X_HEADS, X_DIM), mv.reshape(bp, N_MEM, X_HEADS, X_DIM)))
    y_prompt = x.reshape(bp, sp, D_MODEL)
    p_out = _assemble_states(shared, hg_states, bp, sp) + tuple(jnp.stack(a) for a in zip(*p_mem))

    bs, ts, _ = x_sample.shape
    p = cache_fox_k.shape[2]
    fm = lambda c: jnp.moveaxis(c, 2, -1)
    c_kt = fm(cache_fox_k).reshape(depth * bs, BRANCH_W, p)
    c_vt = fm(cache_fox_v).reshape(depth * bs, BRANCH_W, p)
    c_kpt = fm(cache_mla_kpe).reshape(depth * bs, MLA_ROPE, p)
    c_cumt = fox_cumsum(fm(cache_fox_logf).reshape(depth * bs, FOX_HEADS, p), 0, depth * bs)
    c_ckv = cache_mla_ckv.reshape(depth * bs * p, MLA_KV_RANK)
    c_hg = state_hgrn.reshape((depth * bs,) + state_hgrn.shape[2:])
    c_mk = cache_mem_k.reshape(depth * bs * N_MEM, X_HEADS * X_DIM)
    c_mv = cache_mem_v.reshape(depth * bs * N_MEM, X_HEADS * X_DIM)
    x = x_sample.reshape(bs * ts, D_MODEL)
    shared, hg_states = (), []
    for l in range(depth):
        past = (c_kt, c_vt, c_cumt, c_ckv, c_kpt, c_hg)
        x, shared, hg = run_layer(x, bs, ts, p, l, c_mk, c_mv, l * bs, past, shared)
        hg_states.append(hg)
    y_sample = x.reshape(bs, ts, D_MODEL)
    return (y_prompt, y_sample, *p_out, *_assemble_states(shared, hg_states, bs, ts))
```

```python
import functools

import numpy as np
import jax
import jax.numpy as jnp
from jax import lax
from jax.experimental import pallas as pl
from jax.experimental.pallas import tpu as pltpu

F32 = jnp.float32
BF16 = jnp.bfloat16

D_MODEL = 1024
CHUNK = 64
N_MEM = 256
EPS = 1e-6
NEG_BIG = -1e30
EXP_CLIP = 80.0
FOX_HEADS = 8
FOX_DIM = 64
MLA_HEADS = 4
MLA_Q_RANK = 384
MLA_KV_RANK = 256
MLA_NOPE = 128
MLA_ROPE = 64
MLA_V = 128
ROPE_THETA = 10000.0
HG_HEADS = 4
HG_DK = 128
HG_DV = 128
X_HEADS = 4
X_DIM = 128
D_FF = 4 * D_MODEL
BRANCH_W = 512
IN_SIZES = (512, 512, 512, FOX_HEADS, MLA_Q_RANK, MLA_KV_RANK, MLA_ROPE, 512, 512, 512, 512,
            D_MODEL, D_MODEL, D_MODEL)

LANE = 128
SUB_BLOCK = 16
VMEM_LIMIT = 56 * 1024 * 1024
LOG2E = 1.4426950408889634

IN_TN = 1664
HQF_W, MLA_W, HV_W = 1024, MLA_Q_RANK + MLA_KV_RANK, 1536
OFF_FQ = 1024
T_FK, T_FV, T_KPE, T_KPE_SW, T_FF, NT_IN = 0, 512, 1024, 1088, 1152, 1160


def _params(sem, vmem=VMEM_LIMIT):
    return pltpu.CompilerParams(dimension_semantics=sem, vmem_limit_bytes=vmem)


def _dot(a, b):
    return jnp.dot(a, b, preferred_element_type=F32)


def _dot_nt(a, b):
    return lax.dot_general(a, b, (((1,), (1,)), ((), ())), preferred_element_type=F32)


def _dot_tn(a, b):
    return lax.dot_general(a, b, (((0,), (0,)), ((), ())), preferred_element_type=F32)


def _rms(x, g):
    y = x * lax.rsqrt(jnp.mean(x * x, axis=-1, keepdims=True) + EPS)
    return y * g


def _log_sigmoid(z):
    return jnp.minimum(z, 0.0) - jnp.log(1.0 + jnp.exp(-jnp.abs(z)))


def _sigmoid(z):
    return 1.0 / (1.0 + jnp.exp(-z))


def _resident(shape):
    nd = len(shape)
    return pl.BlockSpec(shape, lambda *_: (0,) * nd, pipeline_mode=pl.Buffered(1))


def _in_proj_kernel(x_ref, g_ref, w_ref, wt_ref, cos_ref, sin_ref, bf_ref, *rest):
    hqf_ref, mla_ref, hv_ref, kt_ref, vt_ref, kpe_ref, lf_ref, h_ref = rest[-8:]
    j = pl.program_id(1)

    @pl.when(j == 0)
    def _():
        h = _rms(x_ref[...], g_ref[...]).astype(BF16)
        h_ref[...] = h
        yt = _dot_nt(wt_ref[...], h)
        kt_ref[0] = yt[T_FK:T_FK + BRANCH_W]
        vt_ref[0] = yt[T_FV:T_FV + BRANCH_W]
        kpe_ref[0] = (yt[T_KPE:T_KPE + MLA_ROPE] * cos_ref[...]
                      + yt[T_KPE_SW:T_KPE_SW + MLA_ROPE] * sin_ref[...])
        lf_ref[0] = _log_sigmoid(yt[T_FF:T_FF + FOX_HEADS] + bf_ref[...])
        y = _dot(h, w_ref[:, :IN_TN])
        hqf_ref[...] = y[:, :HQF_W]
        mla_ref[...] = y[:, HQF_W:]

    @pl.when(j == 1)
    def _():
        hv_ref[...] = _dot(h_ref[...], w_ref[:, IN_TN:])[:, :HV_W].astype(BF16)


def in_proj(x, g, w3, wt, cos_t, sin_t, b_col, bsz, s, tm, layer, depth, prev):
    m, k = x.shape
    assert w3.shape[1] == 2 * IN_TN and wt.shape[0] == NT_IN
    nt = s // tm
    ntab = cos_t.shape[1] // tm
    feats = (BRANCH_W, BRANCH_W, MLA_ROPE, FOX_HEADS)
    tspec = lambda rows: pl.BlockSpec((None, 1, rows, tm), lambda i, j: (layer, i // nt, 0, i % nt))
    row = lambda w: pl.BlockSpec((tm, w), lambda i, j: (i, 0))
    n_in, n_row = 7, 3
    return pl.pallas_call(
        _in_proj_kernel,
        out_shape=(jax.ShapeDtypeStruct((m, HQF_W), F32), jax.ShapeDtypeStruct((m, MLA_W), F32),
                   jax.ShapeDtypeStruct((m, HV_W), BF16))
        + tuple(jax.ShapeDtypeStruct((depth, bsz, f, s), F32) for f in feats),
        grid=(m // tm, 2),
        in_specs=[pl.BlockSpec((tm, k), lambda i, j: (i, 0)),
                  pl.BlockSpec((1, k), lambda i, j: (0, 0)),
                  _resident(w3.shape),
                  _resident(wt.shape),
                  pl.BlockSpec((MLA_ROPE, tm), lambda i, j: (0, i % ntab)),
                  pl.BlockSpec((MLA_ROPE, tm), lambda i, j: (0, i % ntab)),
                  pl.BlockSpec((FOX_HEADS, 1), lambda i, j: (0, 0))]
        + [pl.BlockSpec(memory_space=pl.ANY)] * len(prev),
        out_specs=(row(HQF_W), row(MLA_W), row(HV_W)) + tuple(tspec(f) for f in feats),
        scratch_shapes=[pltpu.VMEM((tm, k), BF16)],
        input_output_aliases={n_in + i: n_row + i for i in range(len(prev))},
        compiler_params=_params(("parallel", "arbitrary")),
        name="in_proj",
    )(x, g, w3, wt, cos_t, sin_t, b_col, *prev)


def _mla_prep_kernel(in_ref, cs_ref, sn_ref, gq_ref, gkv_ref, wuq_ref, wkt_ref, wv_ref, *rest):
    qx_ref, knt_ref, v_ref, ckvn_ref = rest[-4:]
    cos_t = cs_ref[...]
    sin_t = sn_ref[...]
    qn = _rms(in_ref[:, :MLA_Q_RANK], gq_ref[...]).astype(BF16)
    qall = _dot(qn, wuq_ref[...]) * ((MLA_NOPE + MLA_ROPE) ** -0.5 * LOG2E)
    for h in range(MLA_HEADS):
        lo = h * LANE
        qr = (qall[:, 512 + lo:512 + lo + LANE] * cos_t
              + qall[:, 1024 + lo:1024 + lo + LANE] * sin_t)
        qx_ref[:, 2 * lo:2 * lo + LANE] = qall[:, lo:lo + LANE].astype(BF16)
        qx_ref[:, 2 * lo + LANE:2 * lo + 2 * LANE] = qr.astype(BF16)
    ckvn = _rms(in_ref[:, MLA_Q_RANK:], gkv_ref[...])
    ckvn_ref[...] = ckvn
    cb = ckvn.astype(BF16)
    knt_ref[0] = _dot_nt(wkt_ref[...], cb).astype(BF16)
    v_ref[...] = _dot(cb, wv_ref[...]).astype(BF16)


def mla_prep(mla_in, cos_t, sin_t, gq, gkv, wuq, wkt, wv, bsz, s, tm, layer, depth, prev):
    m = mla_in.shape[0]
    nt = s // tm
    ntab = cos_t.shape[0] // tm
    row = lambda w: pl.BlockSpec((tm, w), lambda i: (i, 0))
    full = lambda a: pl.BlockSpec(a.shape, lambda i: (0,) * a.ndim)
    n_in = 8
    return pl.pallas_call(
        _mla_prep_kernel,
        out_shape=(jax.ShapeDtypeStruct((m, 1024), BF16),
                   jax.ShapeDtypeStruct((bsz, 512, s), BF16),
                   jax.ShapeDtypeStruct((m, 512), BF16),
                   jax.ShapeDtypeStruct((depth, m, MLA_KV_RANK), F32)),
        grid=(m // tm,),
        in_specs=[row(MLA_W),
                  pl.BlockSpec((tm, LANE), lambda i: (i % ntab, 0)),
                  pl.BlockSpec((tm, LANE), lambda i: (i % ntab, 0)),
                  full(gq), full(gkv), full(wuq), full(wkt), full(wv)]
        + [pl.BlockSpec(memory_space=pl.ANY)] * len(prev),
        out_specs=(row(1024), pl.BlockSpec((1, 512, tm), lambda i: (i // nt, 0, i % nt)),
                   row(512), pl.BlockSpec((None, tm, MLA_KV_RANK), lambda i: (layer, i, 0))),
        input_output_aliases={n_in + i: 3 + i for i in range(len(prev))},
        compiler_params=_params(("parallel",)),
        name="mla_prep",
    )(mla_in, cos_t, sin_t, gq, gkv, wuq, wkt, wv, *prev)


def _cumsum_kernel(x_ref, c_ref, *, w):
    s = x_ref.shape[1]
    r = lax.broadcasted_iota(jnp.int32, (w, w), 0)
    c = lax.broadcasted_iota(jnp.int32, (w, w), 1)
    upper = (r <= c).astype(BF16)
    n = x_ref.shape[0]

    def running(xg):
        h1 = xg.astype(BF16)
        r1 = xg - h1.astype(F32)
        h2 = r1.astype(BF16)
        h3 = (r1 - h2.astype(F32)).astype(BF16)
        parts = _dot(jnp.concatenate([h1, h2, h3], axis=0), upper)
        return (parts[:n] + parts[n:2 * n]) + parts[2 * n:]

    local = [running(x_ref[:, g * w:(g + 1) * w]) for g in range(s // w)]
    carry = jnp.zeros((n, 1), F32)
    for g, cum in enumerate(local):
        cum = cum + carry
        c_ref[:, g * w:(g + 1) * w] = cum
        carry = cum[:, w - 1:w]


def fox_cumsum(x, row0, bsz):
    n, h, s = x.shape
    rows = bsz * h
    assert (row0 * h) % rows == 0
    out = pl.pallas_call(
        functools.partial(_cumsum_kernel, w=min(LANE, s)),
        out_shape=jax.ShapeDtypeStruct((rows, s), F32),
        grid=(1,),
        in_specs=[pl.BlockSpec((rows, s), lambda i: (row0 * h // rows, 0))],
        out_specs=pl.BlockSpec((rows, s), lambda i: (0, 0)),
        compiler_params=_params(("arbitrary",)),
        name="fox_cumsum",
    )(x.reshape(n * h, s))
    return out.reshape(bsz, h, s)


def _pair_rows_mask(hh):
    sub = lax.broadcasted_iota(jnp.int32, (LANE, 1), 0)
    return (sub < FOX_DIM) if hh == 0 else (sub >= FOX_DIM)


def _fox_finish(acc0, acc1):
    lane = lax.broadcasted_iota(jnp.int32, (1, LANE), 1)
    o0 = acc0 / pltpu.roll(acc0, FOX_DIM, axis=1)
    o1 = acc1 / pltpu.roll(acc1, FOX_DIM, axis=1)
    return jnp.where(lane < FOX_DIM, o0, o1)


def _causal_schedule(nq):
    todo = {i: list(range(i + 1)) for i in range(nq)}
    order = []
    while any(todo.values()):
        for i in reversed(range(nq)):
            if todo[i]:
                order.append((i, todo[i].pop(0)))
    return order


def _fox_prompt_kernel(q_ref, kt_ref, vt_ref, ct_ref, o_ref, *, t, nq):
    hp = pl.program_id(1)
    lane = lax.broadcasted_iota(jnp.int32, (1, LANE), 1)
    lo = lane < FOX_DIM
    span = lambda i: slice(i * t, (i + 1) * t)
    ct = [ct_ref[0, 2 * hp + hh] for hh in range(2)]
    qs, cref = [], []
    for i in range(nq):
        q = q_ref[span(i), :].astype(F32) * (FOX_DIM ** -0.5 * LOG2E)
        qs.append((jnp.where(lo, q, 0.0).astype(BF16), jnp.where(lo, 0.0, q).astype(BF16)))
        cref.append([c[:, i * t:i * t + 1] for c in ct])
    kts = [kt_ref[0, :, span(j)].astype(BF16) for j in range(nq)]
    vts = [[jnp.where(_pair_rows_mask(hh), vt_ref[0, :, span(j)], 1.0).astype(BF16) for hh in range(2)]
           for j in range(nq)]
    causal = (lax.broadcasted_iota(jnp.int32, (t, t), 0) >= lax.broadcasted_iota(jnp.int32, (t, t), 1))

    def scores(i, j):
        out = []
        for hh in range(2):
            s = _dot(qs[i][hh], kts[j]) + (cref[i][hh] - ct[hh][:, span(j)]) * LOG2E
            out.append(jnp.where(causal, s, NEG_BIG) if i == j else s)
        return out

    state = [[(jnp.full((t, 1), NEG_BIG, F32), jnp.zeros((t, LANE), F32)) for _ in range(2)]
             for _ in range(nq)]
    order = _causal_schedule(nq)
    ss = scores(*order[0])
    for n, (i, j) in enumerate(order):
        nxt = scores(*order[n + 1]) if n + 1 < len(order) else None
        m_new = [jnp.maximum(state[i][hh][0], jnp.max(ss[hh], axis=1, keepdims=True)) for hh in range(2)]
        ps = [jnp.exp2(ss[hh] - m_new[hh]).astype(BF16) for hh in range(2)]
        pvs = [_dot_nt(ps[hh], vts[j][hh]) for hh in range(2)]
        state[i] = [(m_new[hh], jnp.exp2(state[i][hh][0] - m_new[hh]) * state[i][hh][1] + pvs[hh])
                    for hh in range(2)]
        ss = nxt
    for i in range(nq):
        o_ref[span(i), :] = _fox_finish(state[i][0][1], state[i][1][1]).astype(BF16)


def fox_prompt(hv, kt, vt, cumt, row0, bsz, s, t):
    m = hv.shape[0]
    return pl.pallas_call(
        functools.partial(_fox_prompt_kernel, t=t, nq=s // t),
        out_shape=jax.ShapeDtypeStruct((m, BRANCH_W), BF16),
        grid=(bsz, FOX_HEADS // 2),
        in_specs=[pl.BlockSpec((s, LANE), lambda b, h: (b, OFF_FQ // LANE + h)),
                  pl.BlockSpec((1, LANE, s), lambda b, h: (row0 + b, h, 0)),
                  pl.BlockSpec((1, LANE, s), lambda b, h: (row0 + b, h, 0)),
                  pl.BlockSpec((1, FOX_HEADS, 1, s), lambda b, h: (b, 0, 0, 0))],
        out_specs=pl.BlockSpec((s, LANE), lambda b, h: (b, h)),
        compiler_params=_params(("parallel", "parallel")),
        name="fox_prompt",
    )(hv, kt, vt, cumt.reshape(bsz, FOX_HEADS, 1, s))


def _mla_keys(knt, kpt):
    n = knt.shape[1]
    return jnp.concatenate([knt, kpt.astype(BF16), jnp.zeros((LANE - MLA_ROPE, n), BF16)], axis=0)


def _mla_prompt_kernel(q_ref, knt_ref, kpt_ref, v_ref, o_ref, *, t, nq):
    hs = range(2)
    span = lambda i: slice(i * t, (i + 1) * t)
    cols = lambda hh, w: slice(hh * w, (hh + 1) * w)
    keys = [[_mla_keys(knt_ref[0, cols(hh, LANE), span(j)], kpt_ref[0, :, span(j)]) for hh in hs]
            for j in range(nq)]
    mask = (lax.broadcasted_iota(jnp.int32, (t, t), 0) // CHUNK
            >= lax.broadcasted_iota(jnp.int32, (t, t), 1) // CHUNK)

    def scores(i, j):
        ss = [_dot(q_ref[span(i), cols(hh, 2 * LANE)], keys[j][hh]) for hh in hs]
        return [jnp.where(mask, s, NEG_BIG) for s in ss] if i == j else ss

    state = [[(jnp.full((t, 1), NEG_BIG, F32), jnp.zeros((t, 1), F32), jnp.zeros((t, LANE), F32))
              for _ in hs] for _ in range(nq)]
    order = _causal_schedule(nq)
    ss = scores(*order[0])
    for n, (i, j) in enumerate(order):
        nxt = scores(*order[n + 1]) if n + 1 < len(order) else None
        m_new = [jnp.maximum(state[i][hh][0], jnp.max(ss[hh], axis=1, keepdims=True)) for hh in hs]
        ps = [jnp.exp2(ss[hh] - m_new[hh]) for hh in hs]
        pvs = [_dot(ps[hh].astype(BF16), v_ref[span(j), cols(hh, LANE)]) for hh in hs]
        new = []
        for hh in hs:
            alpha = jnp.exp2(state[i][hh][0] - m_new[hh])
            new.append((m_new[hh], alpha * state[i][hh][1] + jnp.sum(ps[hh], axis=1, keepdims=True),
                        alpha * state[i][hh][2] + pvs[hh]))
        state[i] = new
        ss = nxt
    for i in range(nq):
        o_ref[span(i), :] = jnp.concatenate([state[i][hh][2] / state[i][hh][1] for hh in hs],
                                            axis=1).astype(BF16)


def mla_prompt(qx, knt, kpt, v, row0, bsz, s, t):
    assert t % CHUNK == 0
    m = qx.shape[0]
    return pl.pallas_call(
        functools.partial(_mla_prompt_kernel, t=t, nq=s // t),
        out_shape=jax.ShapeDtypeStruct((m, BRANCH_W), BF16),
        grid=(bsz, MLA_HEADS // 2),
        in_specs=[pl.BlockSpec((s, 4 * LANE), lambda b, h: (b, h)),
                  pl.BlockSpec((1, 2 * LANE, s), lambda b, h: (b, h, 0)),
                  pl.BlockSpec((1, MLA_ROPE, s), lambda b, h: (row0 + b, 0, 0)),
                  pl.BlockSpec((s, 2 * LANE), lambda b, h: (b, h))],
        out_specs=pl.BlockSpec((s, 2 * LANE), lambda b, h: (b, h)),
        compiler_params=_params(("parallel", "parallel")),
        name="mla_prompt",
    )(qx, knt, kpt, v)


def _fox_sample_kernel(q_ref, ktn_ref, vtn_ref, ktc_ref, vtc_ref, ctn_ref, ctc_ref, o_ref, *, t, p):
    lane = lax.broadcasted_iota(jnp.int32, (1, LANE), 1)
    lo = lane < FOX_DIM
    causal = (lax.broadcasted_iota(jnp.int32, (t, t), 0) >= lax.broadcasted_iota(jnp.int32, (t, t), 1))
    for hp in range(FOX_HEADS // 2):
        pair = slice(hp * LANE, (hp + 1) * LANE)
        q = q_ref[:, pair].astype(F32) * (FOX_DIM ** -0.5 * LOG2E)
        kt_c = ktc_ref[0, pair, :].astype(BF16)
        kt_n = ktn_ref[0, pair, :].astype(BF16)
        vt_c = vtc_ref[0, pair, :]
        vt_n = vtn_ref[0, pair, :]
        accs = []
        for hh in range(2):
            head = 2 * hp + hh
            qh = (jnp.where(lo, q, 0.0) if hh == 0 else jnp.where(lo, 0.0, q)).astype(BF16)
            cc = ctc_ref[0, head:head + 1, :]
            ctot = cc[:, p - 1:p]
            s_c = _dot(qh, kt_c) + (ctot - cc) * LOG2E
            s_n = _dot(qh, kt_n) - ctn_ref[0, head:head + 1, :] * LOG2E
            s_n = jnp.where(causal, s_n, NEG_BIG)
            m = jnp.maximum(jnp.max(s_c, axis=1, keepdims=True), jnp.max(s_n, axis=1, keepdims=True))
            rows = _pair_rows_mask(hh)
            accs.append(_dot_nt(jnp.exp2(s_c - m).astype(BF16), jnp.where(rows, vt_c, 1.0).astype(BF16))
                        + _dot_nt(jnp.exp2(s_n - m).astype(BF16), jnp.where(rows, vt_n, 1.0).astype(BF16)))
        o_ref[:, pair] = _fox_finish(accs[0], accs[1]).astype(BF16)


def fox_sample(hv, kt_n, vt_n, kt_c, vt_c, cumt_n, cumt_c, bsz, t, p, layer):
    cidx = lambda b: (layer * bsz + b, 0, 0)
    return pl.pallas_call(
        functools.partial(_fox_sample_kernel, t=t, p=p),
        out_shape=jax.ShapeDtypeStruct((bsz * t, BRANCH_W), BF16),
        grid=(bsz,),
        in_specs=[pl.BlockSpec((t, BRANCH_W), lambda b: (b, OFF_FQ // BRANCH_W)),
                  pl.BlockSpec((1, BRANCH_W, t), cidx),
                  pl.BlockSpec((1, BRANCH_W, t), cidx),
                  pl.BlockSpec((1, BRANCH_W, p), cidx),
                  pl.BlockSpec((1, BRANCH_W, p), cidx),
                  pl.BlockSpec((1, FOX_HEADS, t), lambda b: (b, 0, 0)),
                  pl.BlockSpec((1, FOX_HEADS, p), cidx)],
        out_specs=pl.BlockSpec((t, BRANCH_W), lambda b: (b, 0)),
        compiler_params=_params(("parallel",)),
        name="fox_sample",
    )(hv, kt_n, vt_n, kt_c, vt_c, cumt_n, cumt_c)


def _mla_sample_kernel(q_ref, kntn_ref, kptn_ref, vn_ref, ckv_ref, kptc_ref, wkt_ref, wv_ref, o_ref,
                       *, t, p):
    qc = (p + lax.broadcasted_iota(jnp.int32, (t, t), 0)) // CHUNK
    kc = (p + lax.broadcasted_iota(jnp.int32, (t, t), 1)) // CHUNK
    kpt_c, kpt_n = kptc_ref[0], kptn_ref[0]
    cb = ckv_ref[...].astype(BF16)
    for h in range(MLA_HEADS):
        cs = slice(h * LANE, (h + 1) * LANE)
        knt_c = _dot_nt(wkt_ref[cs, :], cb).astype(BF16)
        v_c = _dot(cb, wv_ref[:, cs]).astype(BF16)
        q = q_ref[:, 2 * h * LANE:2 * (h + 1) * LANE]
        s_c = _dot(q, _mla_keys(knt_c, kpt_c))
        s_n = _dot(q, _mla_keys(kntn_ref[0, cs, :], kpt_n))
        s_n = jnp.where(qc >= kc, s_n, NEG_BIG)
        m = jnp.maximum(jnp.max(s_c, axis=1, keepdims=True), jnp.max(s_n, axis=1, keepdims=True))
        p_c = jnp.exp2(s_c - m)
        p_n = jnp.exp2(s_n - m)
        l = jnp.sum(p_c, axis=1, keepdims=True) + jnp.sum(p_n, axis=1, keepdims=True)
        o = _dot(p_c.astype(BF16), v_c) + _dot(p_n.astype(BF16), vn_ref[:, cs])
        o_ref[:, cs] = (o / l).astype(BF16)


def mla_sample(qx, knt_n, kpt_n, v_n, ckv_c, kpt_c, wkt, wv, bsz, t, p, layer):
    assert (p - 1) // CHUNK <= p // CHUNK
    full = lambda a: pl.BlockSpec(a.shape, lambda b: (0,) * a.ndim)
    return pl.pallas_call(
        functools.partial(_mla_sample_kernel, t=t, p=p),
        out_shape=jax.ShapeDtypeStruct((bsz * t, BRANCH_W), BF16),
        grid=(bsz,),
        in_specs=[pl.BlockSpec((t, 8 * LANE), lambda b: (b, 0)),
                  pl.BlockSpec((1, BRANCH_W, t), lambda b: (b, 0, 0)),
                  pl.BlockSpec((1, MLA_ROPE, t), lambda b: (layer * bsz + b, 0, 0)),
                  pl.BlockSpec((t, BRANCH_W), lambda b: (b, 0)),
                  pl.BlockSpec((p, MLA_KV_RANK), lambda b: (layer * bsz + b, 0)),
                  pl.BlockSpec((1, MLA_ROPE, p), lambda b: (layer * bsz + b, 0, 0)),
                  full(wkt), full(wv)],
        out_specs=pl.BlockSpec((t, BRANCH_W), lambda b: (b, 0)),
        compiler_params=_params(("parallel",)),
        name="mla_sample",
    )(qx, knt_n, kpt_n, v_n, ckv_c, kpt_c, wkt, wv)


def _hgrn_gates(z, lb, tri):
    logf = _log_sigmoid(z) + jnp.log(1.0 + lb * jnp.exp(jnp.minimum(-z, EXP_CLIP)))
    k = (1.0 - lb) * (1.0 / (1.0 + jnp.exp(z)))
    h1 = logf.astype(BF16)
    r1 = logf - h1.astype(F32)
    h2 = r1.astype(BF16)
    h3 = (r1 - h2.astype(F32)).astype(BF16)
    hcat = jnp.concatenate([h1, h2, h3], axis=1)
    g = tri.shape[0]
    parts = jnp.concatenate([_dot(tri, hcat[r:r + g, :]) for r in range(0, z.shape[0], g)], axis=0)
    lc = ((parts[:, :LANE] + parts[:, LANE:2 * LANE]) + parts[:, 2 * LANE:]) * LOG2E
    return k, lc


def _hgrn_local(q, z, lb, tri, v_b, ln, sel, k_ref, lc_ref):
    k, lc = _hgrn_gates(z, lb, tri)
    k_ref[...] = k
    lc_ref[...] = lc
    nchunk = q.shape[0] // ln
    nsb = q.shape[0] // SUB_BLOCK
    per = ln // SUB_BLOCK
    half = SUB_BLOCK // 2
    rows = lambda a, i: a[i * SUB_BLOCK:(i + 1) * SUB_BLOCK, :]
    lcb = [jnp.zeros((1, LANE), F32) if i % per == 0 else lc_ref[i * SUB_BLOCK - 1:i * SUB_BLOCK, :]
           for i in range(nsb)]
    lcb_rows = jnp.concatenate([jnp.broadcast_to(b, (SUB_BLOCK, LANE)) for b in lcb], axis=0)
    last = [lc_ref[(c + 1) * ln - 1:(c + 1) * ln, :] for c in range(nchunk)]
    last_rows = jnp.concatenate([jnp.broadcast_to(b, (ln, LANE)) for b in last], axis=0)
    qh = (q * jnp.exp2(lc - lcb_rows)).astype(BF16)
    qe = (q * jnp.exp2(lc)).astype(BF16)
    kdec = (k * jnp.exp2(last_rows - lc)).astype(BF16)
    a_off = {}
    for i in range(nsb):
        n = (i % per) * SUB_BLOCK
        if n:
            c0 = i * SUB_BLOCK - n
            kt = (k[c0:c0 + n, :] * jnp.exp2(lcb[i] - lc[c0:c0 + n, :])).astype(BF16)
            a_off[i] = _dot_nt(rows(qh, i), kt)
    yield None
    pieces = []
    for i in range(nsb):
        q_i, lc_i = rows(q, i), rows(lc, i)
        cols = []
        for s in range(SUB_BLOCK):
            lo = 0 if s < half else half
            r = i * SUB_BLOCK + s
            d = lc_i[lo:, :] - lc_ref[r:r + 1, :]
            d = (jnp.concatenate([jnp.minimum(d[:half, :], 0.0), d[half:, :]], axis=0) if s < half
                 else jnp.minimum(d, 0.0))
            w = (q_i[lo:, :] * k_ref[r:r + 1, :]) * jnp.exp2(d)
            if lo:
                w = jnp.concatenate([jnp.zeros((lo, LANE), F32), w], axis=0)
            cols.append(w.astype(BF16))
        pieces.append(jnp.concatenate(cols, axis=1))
    a_all = _dot(jnp.concatenate(pieces, axis=0), sel)
    inc = [_dot_tn(v_b[c * ln:(c + 1) * ln, :], kdec[c * ln:(c + 1) * ln, :]) for c in range(nchunk)]
    yield None
    pair_ok = (lax.broadcasted_iota(jnp.int32, (SUB_BLOCK, LANE), 0)
               >= lax.broadcasted_iota(jnp.int32, (SUB_BLOCK, LANE), 1))
    off = []
    for i in range(nsb):
        n = (i % per) * SUB_BLOCK
        off.append(_dot(a_off[i].astype(BF16), v_b[i * SUB_BLOCK - n:i * SUB_BLOCK, :]) if n
                   else jnp.zeros((SUB_BLOCK, LANE), F32))
    diag = [_dot(jnp.where(pair_ok, rows(a_all, i), 0.0)[:, :SUB_BLOCK].astype(BF16), rows(v_b, i))
            for i in range(nsb)]
    local = jnp.concatenate(off, axis=0) + jnp.concatenate(diag, axis=0)
    dec = [jnp.exp2(b) for b in last]
    yield local, qe, inc, dec


def _hgrn_kernel(*refs, ln, nchunk, has_init):
    refs = list(refs)
    hq_ref, hf_ref, hi_ref, hg_ref, lb_ref, go_ref, sel_ref = refs[:7]
    s0_ref = refs[7] if has_init else None
    o_ref, sout_ref, st_ref, k_ref, lc_ref = refs[-5:]
    step = pl.program_id(1)
    nrows = ln * nchunk

    @pl.when(step == 0)
    def _():
        for h in range(HG_HEADS):
            st_ref[h] = s0_ref[0, h].T if has_init else jnp.zeros((HG_DV, HG_DK), F32)

    ng = min(nrows, 2 * LANE)
    assert ng % ln == 0 and nrows % ng == 0
    ri = lax.broadcasted_iota(jnp.int32, (ng, ng), 0)
    ci = lax.broadcasted_iota(jnp.int32, (ng, ng), 1)
    tri = ((ri >= ci) & (ri // ln == ci // ln)).astype(BF16)

    def finish(h, local, qe, inc, dec):
        cs = slice(h * LANE, (h + 1) * LANE)
        st = st_ref[h]
        parts = []
        for c in range(nchunk):
            parts.append(_dot_nt(qe[c * ln:(c + 1) * ln, :], st.astype(BF16)))
            st = st * dec[c] + inc[c]
        st_ref[h] = st
        o = local + jnp.concatenate(parts, axis=0)
        o_ref[:, cs] = (_rms(o, go_ref[...]) * _sigmoid(hg_ref[:, cs].astype(F32))).astype(BF16)

    def start(h):
        cs = slice(h * LANE, (h + 1) * LANE)
        gen = _hgrn_local(hq_ref[:, cs], hf_ref[:, cs], lb_ref[:, cs], tri, hi_ref[:, cs].astype(BF16),
                          ln, sel_ref[...], k_ref.at[h], lc_ref.at[h])
        next(gen)
        return gen

    gens = {0: start(0)}
    for h in range(HG_HEADS):
        if h + 1 < HG_HEADS:
            gens[h + 1] = start(h + 1)
        next(gens[h])
        if h > 0:
            finish(h - 1, *next(gens.pop(h - 1)))
    finish(HG_HEADS - 1, *next(gens.pop(HG_HEADS - 1)))

    @pl.when(step == pl.num_programs(1) - 1)
    def _():
        for h in range(HG_HEADS):
            sout_ref[0, h] = st_ref[h].T


def hgrn(hqf, hv, lb, g_out, bsz, s, ln, rows, s0=None, s0_row0=0):
    m = hqf.shape[0]
    ns = s // rows
    has_init = s0 is not None
    sel = (np.arange(SUB_BLOCK * LANE)[:, None] // LANE == np.arange(LANE)[None, :])
    sel = jnp.asarray(sel, BF16)
    blk = lambda col: pl.BlockSpec((rows, BRANCH_W), lambda b, i: (b * ns + i, col))
    ins = [hqf, hqf, hv, hv, lb, g_out, sel]
    specs = [blk(0), blk(1), blk(0), blk(1),
             pl.BlockSpec((1, BRANCH_W), lambda b, i: (0, 0)),
             pl.BlockSpec((1, HG_DV), lambda b, i: (0, 0)),
             pl.BlockSpec(sel.shape, lambda b, i: (0, 0))]
    if has_init:
        ins.append(s0)
        specs.append(pl.BlockSpec((1, HG_HEADS, HG_DK, HG_DV), lambda b, i: (s0_row0 + b, 0, 0, 0)))
    return pl.pallas_call(
        functools.partial(_hgrn_kernel, ln=ln, nchunk=rows // ln, has_init=has_init),
        out_shape=(jax.ShapeDtypeStruct((m, BRANCH_W), BF16),
                   jax.ShapeDtypeStruct((bsz, HG_HEADS, HG_DK, HG_DV), F32)),
        grid=(bsz, ns),
        in_specs=specs,
        out_specs=(pl.BlockSpec((rows, BRANCH_W), lambda b, i: (b * ns + i, 0)),
                   pl.BlockSpec((1, HG_HEADS, HG_DK, HG_DV), lambda b, i: (b, 0, 0, 0))),
        scratch_shapes=[pltpu.VMEM((HG_HEADS, HG_DV, HG_DK), F32),
                        pltpu.VMEM((HG_HEADS, rows, LANE), F32),
                        pltpu.VMEM((HG_HEADS, rows, LANE), F32)],
        compiler_params=_params(("parallel", "arbitrary")),
        name="hgrn",
    )(*ins)


def _merge_kernel(of_ref, om_ref, oh_ref, x_ref, wg_ref, wb_ref, wo_ref, g0_ref, g1_ref, o_ref):
    x = x_ref[...]
    h = _rms(x, g0_ref[...]).astype(BF16)
    branches = (of_ref, om_ref, oh_ref)
    gates = [_dot(h, wg_ref[i]) for i in range(3)]
    outs = [_dot(branches[i][...], wb_ref[i]) for i in range(3)]
    merged = (_sigmoid(gates[0]) * outs[0] + _sigmoid(gates[1]) * outs[1]) + _sigmoid(gates[2]) * outs[2]
    y = _dot(merged.astype(BF16), wo_ref[...])
    o_ref[...] = x + _rms(y, g1_ref[...])


def merge_out(o_fox, o_mla, o_hg, x, wg, wb, wo, g0, g1, tm):
    m = x.shape[0]
    row = lambda w: pl.BlockSpec((tm, w), lambda i: (i, 0))
    vec = pl.BlockSpec((1, D_MODEL), lambda i: (0, 0))
    return pl.pallas_call(
        _merge_kernel,
        out_shape=jax.ShapeDtypeStruct((m, D_MODEL), F32),
        grid=(m // tm,),
        in_specs=[row(BRANCH_W), row(BRANCH_W), row(BRANCH_W), row(D_MODEL),
                  _resident(wg.shape), _resident(wb.shape), _resident(wo.shape), vec, vec],
        out_specs=row(D_MODEL),
        compiler_params=_params(("parallel",)),
        name="merge_out",
    )(o_fox, o_mla, o_hg, x, wg, wb, wo, g0, g1)


def _mem_kv_kernel(x_ref, w_ref, *rest):
    k_ref, v_ref = rest[-2:]
    tm = x_ref.shape[0]
    y = _dot(x_ref[...].astype(BF16), w_ref[...])
    for h in range(X_HEADS):
        k_ref[pl.ds(h, tm, stride=X_HEADS), :] = y[:, h * X_DIM:(h + 1) * X_DIM]
        v_ref[pl.ds(h, tm, stride=X_HEADS), :] = y[:, (X_HEADS + h) * X_DIM:(X_HEADS + h + 1) * X_DIM]


def mem_kv(mem, w, tm, layer, depth, prev):
    m, k = mem.shape
    out = jax.ShapeDtypeStruct((depth, m * X_HEADS, X_DIM), F32)
    ospec = pl.BlockSpec((None, tm * X_HEADS, X_DIM), lambda i: (layer, i, 0))
    return pl.pallas_call(
        _mem_kv_kernel,
        out_shape=(out, out),
        grid=(m // tm,),
        in_specs=[pl.BlockSpec((tm, k), lambda i: (i, 0)), _resident(w.shape)]
        + [pl.BlockSpec(memory_space=pl.ANY)] * len(prev),
        out_specs=(ospec, ospec),
        input_output_aliases={2 + i: i for i in range(len(prev))},
        compiler_params=_params(("parallel",)),
        name="mem_kv",
    )(mem, w, *prev)


def _cross_kernel(x_ref, mk_ref, mv_ref, wq_ref, wo_ref, g2_ref, g3_ref, o_ref):
    x = x_ref[...]
    h = _rms(x, g2_ref[...]).astype(BF16)
    q = _dot(h, wq_ref[...])
    qb = (q * (X_DIM ** -0.5 * LOG2E)).astype(BF16)
    cols = [slice(hd * X_DIM, (hd + 1) * X_DIM) for hd in range(X_HEADS)]
    head = lambda ref, hd: ref[pl.ds(hd, N_MEM, stride=X_HEADS), :].astype(BF16)
    ss = [_dot_nt(qb[:, cs], head(mk_ref, hd)) for hd, cs in enumerate(cols)]
    ps = [jnp.exp2(s - jnp.max(s, axis=1, keepdims=True)) for s in ss]
    pvs = [_dot(p.astype(BF16), head(mv_ref, hd)) for hd, p in enumerate(ps)]
    outs = [pv / jnp.sum(p, axis=1, keepdims=True) for pv, p in zip(pvs, ps)]
    ox = jnp.concatenate(outs, axis=1).astype(BF16)
    o_ref[...] = x + _rms(_dot(ox, wo_ref[...]), g3_ref[...])


def cross_block(x, mk, mv, wq, wo, g2, g3, bsz, s, tm, mem_row0):
    m = x.shape[0]
    nt = s // tm
    vec = pl.BlockSpec((1, D_MODEL), lambda b, i: (0, 0))
    return pl.pallas_call(
        _cross_kernel,
        out_shape=jax.ShapeDtypeStruct((m, D_MODEL), F32),
        grid=(bsz, nt),
        in_specs=[pl.BlockSpec((tm, D_MODEL), lambda b, i: (b * nt + i, 0)),
                  pl.BlockSpec((N_MEM * X_HEADS, X_DIM), lambda b, i: (mem_row0 + b, 0)),
                  pl.BlockSpec((N_MEM * X_HEADS, X_DIM), lambda b, i: (mem_row0 + b, 0)),
                  _resident(wq.shape), _resident(wo.shape), vec, vec],
        out_specs=pl.BlockSpec((tm, D_MODEL), lambda b, i: (b * nt + i, 0)),
        compiler_params=_params(("parallel", "parallel")),
        name="cross_attn",
    )(x, mk, mv, wq, wo, g2, g3)


def _mlp_kernel(x_ref, wu_ref, wd_ref, g4_ref, g5_ref, o_ref, h_ref, acc_ref):
    j = pl.program_id(1)

    @pl.when(j == 0)
    def _():
        h_ref[...] = _rms(x_ref[...], g4_ref[...]).astype(BF16)
        acc_ref[...] = jnp.zeros_like(acc_ref)

    cols = pl.ds(pl.multiple_of(j * D_MODEL, D_MODEL), D_MODEL)
    u = jnp.square(jnp.maximum(_dot(h_ref[...], wu_ref[:, cols]), 0.0)).astype(BF16)
    acc_ref[...] += _dot(u, wd_ref[j])

    @pl.when(j == pl.num_programs(1) - 1)
    def _():
        o_ref[...] = x_ref[...] + _rms(acc_ref[...], g5_ref[...])


def mlp_block(x, wu3, wd3, g4, g5, tm):
    m = x.shape[0]
    nj = wd3.shape[0]
    vec = pl.BlockSpec((1, D_MODEL), lambda i, j: (0, 0))
    return pl.pallas_call(
        _mlp_kernel,
        out_shape=jax.ShapeDtypeStruct((m, D_MODEL), F32),
        grid=(m // tm, nj),
        in_specs=[pl.BlockSpec((tm, D_MODEL), lambda i, j: (i, 0)),
                  _resident(wu3.shape), _resident(wd3.shape), vec, vec],
        out_specs=pl.BlockSpec((tm, D_MODEL), lambda i, j: (i, 0)),
        scratch_shapes=[pltpu.VMEM((tm, D_MODEL), BF16), pltpu.VMEM((tm, D_MODEL), F32)],
        compiler_params=_params(("parallel", "arbitrary")),
        name="mlp",
    )(x, wu3, wd3, g4, g5)


def _prep_layer_weights(w_in, w_mla_uq, w_mla_ukv, w_branch, w_out, w_xq, w_mem_k, w_mem_v, w_xo,
                        w_up, w_down):
    idx = np.cumsum((0,) + IN_SIZES)
    seg = lambda i: w_in[:, idx[i]:idx[i + 1]]
    fq, fk, fv, ff, cq, ckv, kpe, hq, hf, hi, hg, ga, gb, gc = (seg(i) for i in range(14))
    half = MLA_ROPE // 2
    kpe_sw = jnp.concatenate([kpe[:, half:], kpe[:, :half]], axis=1)
    pad = jnp.zeros((D_MODEL, IN_TN - HV_W), w_in.dtype)
    w_p = jnp.concatenate([hq, hf, cq, ckv, hi, hg, fq, pad], axis=1).astype(BF16)
    w_gate = jnp.stack([ga, gb, gc]).astype(BF16)
    w_t = jnp.concatenate([fk, fv, kpe, kpe_sw, ff], axis=1).T.astype(BF16)
    hd = MLA_NOPE + MLA_ROPE
    zq = jnp.zeros((MLA_Q_RANK, LANE - MLA_ROPE), w_mla_uq.dtype)
    nope, rope_n, rope_s = [], [], []
    for h in range(MLA_HEADS):
        base = h * hd
        nope.append(w_mla_uq[:, base:base + MLA_NOPE])
        x1 = w_mla_uq[:, base + MLA_NOPE:base + MLA_NOPE + half]
        x2 = w_mla_uq[:, base + MLA_NOPE + half:base + hd]
        rope_n += [x1, x2, zq]
        rope_s += [x2, x1, zq]
    wuq = jnp.concatenate(nope + rope_n + rope_s, axis=1).astype(BF16)
    kvd = MLA_NOPE + MLA_V
    wkt = jnp.concatenate([w_mla_ukv[:, h * kvd:h * kvd + MLA_NOPE] for h in range(MLA_HEADS)],
                          axis=1).T.astype(BF16)
    wv = jnp.concatenate([w_mla_ukv[:, h * kvd + MLA_NOPE:(h + 1) * kvd] for h in range(MLA_HEADS)],
                         axis=1).astype(BF16)
    nff = D_FF // D_MODEL
    return dict(
        w_in3=w_p, w_t=w_t, w_gate=w_gate, wuq=wuq, wkt=wkt, wv=wv,
        wb=w_branch.astype(BF16), wo=w_out.astype(BF16), wxq=w_xq.astype(BF16), wxo=w_xo.astype(BF16),
        wmem=jnp.concatenate([w_mem_k, w_mem_v], axis=1).astype(BF16),
        wu3=w_up.astype(BF16),
        wd3=w_down.astype(BF16).reshape(nff, D_MODEL, D_MODEL))


def _rope_tables(pos):
    half = MLA_ROPE // 2
    freq = ROPE_THETA ** (-jnp.arange(half, dtype=F32) / half)
    ang = pos.astype(F32)[:, None] * freq[None, :]
    cos, sin = jnp.cos(ang), jnp.sin(ang)
    z = jnp.zeros((pos.shape[0], LANE - MLA_ROPE), F32)
    cos_r = jnp.concatenate([cos, cos, z], axis=1)
    sin_r = jnp.concatenate([-sin, sin, z], axis=1)
    return cos_r, sin_r, cos_r[:, :MLA_ROPE].T, sin_r[:, :MLA_ROPE].T


def _tile(n, pref):
    t = min(n, pref)
    assert n % t == 0
    return t


def _layer(x, bsz, s, pos0, w, lb, b_fox, g_q, g_kv, g_hout, g_norm, mem_k, mem_v, mem_row0, past, cfg,
           layer, depth, shared):
    m = bsz * s
    g = lambda i: g_norm[i][None, :]
    tm_in = _tile(s, cfg["tm_in"])
    tm_p = _tile(s, cfg["tm_prep"])
    cos_r, sin_r, cos_c, sin_c = _rope_tables(pos0 + jnp.arange(s))
    hqf, mla_in, hv, kt, vt, kpet, logft = in_proj(x, g(0), w["w_in3"], w["w_t"], cos_c, sin_c,
                                                   b_fox[:, None], bsz, s, tm_in, layer, depth, shared[:4])
    qx, knt, v, ckv_n = mla_prep(mla_in, cos_r, sin_r, g_q[None, :], g_kv[None, :],
                                 w["wuq"], w["wkt"], w["wv"], bsz, s, tm_p, layer, depth, shared[4:])
    row0 = layer * bsz
    rows = lambda a: a.reshape((depth * bsz,) + a.shape[2:])
    cumt = fox_cumsum(rows(logft), row0, bsz)
    if past is None:
        t = _tile(s, cfg["t_attn"])
        o_fox = fox_prompt(hv, rows(kt), rows(vt), cumt, row0, bsz, s, t)
        o_mla = mla_prompt(qx, knt, rows(kpet), v, row0, bsz, s, t)
        o_hg, hg_state = hgrn(hqf, hv, lb[None, :], g_hout[None, :], bsz, s, CHUNK,
                              _tile(s, cfg["hg_rows"]))
    else:
        c_kt, c_vt, c_cumt, c_ckv, c_kpt, c_hg = past
        p = c_kt.shape[2]
        o_fox = fox_sample(hv, rows(kt), rows(vt), c_kt, c_vt, cumt, c_cumt, bsz, s, p, layer)
        o_mla = mla_sample(qx, knt, rows(kpet), v, c_ckv, c_kpt, w["wkt"], w["wv"], bsz, s, p, layer)
        o_hg, hg_state = hgrn(hqf, hv, lb[None, :], g_hout[None, :], bsz, s, s, s, s0=c_hg, s0_row0=row0)
    x = merge_out(o_fox, o_mla, o_hg, x, w["w_gate"], w["wb"], w["wo"], g(0), g(1),
                  _tile(m, cfg["tm_merge"]))
    x = cross_block(x, mem_k, mem_v, w["wxq"], w["wxo"], g(2), g(3), bsz, s, _tile(s, cfg["tm_cross"]),
                    mem_row0)
    x = mlp_block(x, w["wu3"], w["wd3"], g(4), g(5), _tile(m, cfg["tm_mlp"]))
    return x, (kt, vt, kpet, logft, ckv_n), hg_state


def _from_feature_major(stacked, heads):
    a = jnp.swapaxes(stacked, 2, 3)
    if heads:
        a = a.reshape(a.shape[:3] + (heads, a.shape[3] // heads))
    return a


def _assemble_states(shared, hg_states, bsz, s):
    kt, vt, kpet, logft, ckv = shared
    return (_from_feature_major(kt, FOX_HEADS), _from_feature_major(vt, FOX_HEADS),
            _from_feature_major(logft, 0), ckv.reshape(ckv.shape[0], bsz, s, MLA_KV_RANK),
            _from_feature_major(kpet, 0), jnp.stack(hg_states))


_CFG = dict(tm_in=1024, tm_prep=1024, t_attn=512, hg_rows=512, tm_merge=512, tm_cross=1024,
            tm_mlp=1024, tm_mem=512)


def kernel(x_prompt, x_sample, cache_fox_k, cache_fox_v, cache_fox_logf, cache_mla_ckv, cache_mla_kpe,
           state_hgrn, cache_mem_k, cache_mem_v, mem_prompt, w_in, b_fox, g_mla_q, w_mla_uq, g_mla_kv,
           w_mla_ukv, g_hgrn_out, lb_hgrn, w_branch, w_out, w_xq, w_mem_k, w_mem_v, w_xo, w_up, w_down,
           g_norm):
    cfg = _CFG
    depth = w_in.shape[0]
    lb_p = jax.nn.softmax(lb_hgrn.astype(F32), axis=0)
    lb_all = jnp.cumsum(lb_p, axis=0) - lb_p[0]
    ws = [_prep_layer_weights(w_in[l], w_mla_uq[l], w_mla_ukv[l], w_branch[l], w_out[l], w_xq[l],
                              w_mem_k[l], w_mem_v[l], w_xo[l], w_up[l], w_down[l]) for l in range(depth)]

    def run_layer(x, bsz, s, pos0, l, mk, mv, mem_row0, past, shared):
        return _layer(x, bsz, s, pos0, ws[l], lb_all[l], b_fox[l], g_mla_q[l], g_mla_kv[l],
                      g_hgrn_out[l], g_norm[l], mk, mv, mem_row0, past, cfg, l, depth, shared)

    bp, sp, _ = x_prompt.shape
    x = x_prompt.reshape(bp * sp, D_MODEL)
    mem = mem_prompt.reshape(bp * N_MEM, D_MODEL)
    shared, hg_states, p_mem = (), [], ()
    mem_rows = lambda a: a.reshape(-1, X_DIM)
    for l in range(depth):
        p_mem = mem_kv(mem, ws[l]["wmem"], _tile(bp * N_MEM, cfg["tm_mem"]), l, depth, p_mem)
        x, shared, hg = run_layer(x, bp, sp, 0, l, mem_rows(p_mem[0]), mem_rows(p_mem[1]), l * bp, None,
                                  shared)
        hg_states.append(hg)
    y_prompt = x.reshape(bp, sp, D_MODEL)
    p_out = _assemble_states(shared, hg_states, bp, sp) + tuple(
        a.reshape(depth, bp, N_MEM, X_HEADS, X_DIM) for a in p_mem)

    bs, ts, _ = x_sample.shape
    p = cache_fox_k.shape[2]
    fm = lambda c: jnp.moveaxis(c, 2, -1)
    c_kt = fm(cache_fox_k).reshape(depth * bs, BRANCH_W, p)
    c_vt = fm(cache_fox_v).reshape(depth * bs, BRANCH_W, p)
    c_kpt = fm(cache_mla_kpe).reshape(depth * bs, MLA_ROPE, p)
    c_cumt = fox_cumsum(fm(cache_fox_logf).reshape(depth * bs, FOX_HEADS, p), 0, depth * bs)
    c_ckv = cache_mla_ckv.reshape(depth * bs * p, MLA_KV_RANK)
    c_hg = state_hgrn.reshape((depth * bs,) + state_hgrn.shape[2:])
    c_mk, c_mv = mem_rows(cache_mem_k), mem_rows(cache_mem_v)
    x = x_sample.reshape(bs * ts, D_MODEL)
    shared, hg_states = (), []
    for l in range(depth):
        past = (c_kt, c_vt, c_cumt, c_ckv, c_kpt, c_hg)
        x, shared, hg = run_layer(x, bs, ts, p, l, c_mk, c_mv, l * bs, past, shared)
        hg_states.append(hg)
    y_sample = x.reshape(bs, ts, D_MODEL)
    return (y_prompt, y_sample, *p_out, *_assemble_states(shared, hg_states, bs, ts))
```

```python
import functools

import numpy as np
import jax
import jax.numpy as jnp
from jax import lax
from jax.experimental import pallas as pl
from jax.experimental.pallas import tpu as pltpu

F32 = jnp.float32
BF16 = jnp.bfloat16

D_MODEL = 1024
CHUNK = 64
N_MEM = 256
EPS = 1e-6
NEG_BIG = -1e30
EXP_CLIP = 80.0
FOX_HEADS = 8
FOX_DIM = 64
MLA_HEADS = 4
MLA_Q_RANK = 384
MLA_KV_RANK = 256
MLA_NOPE = 128
MLA_ROPE = 64
MLA_V = 128
ROPE_THETA = 10000.0
HG_HEADS = 4
HG_DK = 128
HG_DV = 128
X_HEADS = 4
X_DIM = 128
D_FF = 4 * D_MODEL
BRANCH_W = 512
IN_SIZES = (512, 512, 512, FOX_HEADS, MLA_Q_RANK, MLA_KV_RANK, MLA_ROPE, 512, 512, 512, 512,
            D_MODEL, D_MODEL, D_MODEL)

LANE = 128
SUB_BLOCK = 16
VMEM_LIMIT = 56 * 1024 * 1024
LOG2E = 1.4426950408889634

IN_TN = 1664
HQF_W, MLA_W, HV_W = 1024, MLA_Q_RANK + MLA_KV_RANK, 1536
OFF_FQ = 1024
T_FK, T_FV, T_KPE, T_KPE_SW, T_FF, NT_IN = 0, 512, 1024, 1088, 1152, 1160


def _params(sem, vmem=VMEM_LIMIT):
    return pltpu.CompilerParams(dimension_semantics=sem, vmem_limit_bytes=vmem)


def _dot(a, b):
    return jnp.dot(a, b, preferred_element_type=F32)


def _dot_nt(a, b):
    return lax.dot_general(a, b, (((1,), (1,)), ((), ())), preferred_element_type=F32)


def _dot_tn(a, b):
    return lax.dot_general(a, b, (((0,), (0,)), ((), ())), preferred_element_type=F32)


def _rms(x, g):
    y = x * lax.rsqrt(jnp.mean(x * x, axis=-1, keepdims=True) + EPS)
    return y * g


def _log_sigmoid(z):
    return jnp.minimum(z, 0.0) - jnp.log(1.0 + jnp.exp(-jnp.abs(z)))


def _sigmoid(z):
    return 1.0 / (1.0 + jnp.exp(-z))


def _resident(shape):
    nd = len(shape)
    return pl.BlockSpec(shape, lambda *_: (0,) * nd, pipeline_mode=pl.Buffered(1))


def _resident_layer(stacked, layer):
    nd = stacked.ndim - 1
    return pl.BlockSpec((None,) + stacked.shape[1:], lambda *_: (layer,) + (0,) * nd,
                        pipeline_mode=pl.Buffered(1))


def _in_proj_kernel(x_ref, g_ref, w_ref, wt_ref, cos_ref, sin_ref, bf_ref, *rest):
    hqf_ref, mla_ref, hv_ref, kt_ref, vt_ref, kpe_ref, lf_ref, h_ref = rest[-8:]
    j = pl.program_id(1)

    @pl.when(j == 0)
    def _():
        h = _rms(x_ref[...], g_ref[...]).astype(BF16)
        h_ref[...] = h
        yt = _dot_nt(wt_ref[...], h)
        kt_ref[0] = yt[T_FK:T_FK + BRANCH_W]
        vt_ref[0] = yt[T_FV:T_FV + BRANCH_W]
        kpe_ref[0] = (yt[T_KPE:T_KPE + MLA_ROPE] * cos_ref[...]
                      + yt[T_KPE_SW:T_KPE_SW + MLA_ROPE] * sin_ref[...])
        lf_ref[0] = _log_sigmoid(yt[T_FF:T_FF + FOX_HEADS] + bf_ref[...])
        y = _dot(h, w_ref[:, :IN_TN])
        hqf_ref[...] = y[:, :HQF_W]
        mla_ref[...] = y[:, HQF_W:]

    @pl.when(j == 1)
    def _():
        hv_ref[...] = _dot(h_ref[...], w_ref[:, IN_TN:])[:, :HV_W].astype(BF16)


def in_proj(x, g, w3, wt, cos_t, sin_t, b_col, bsz, s, tm, layer, depth, prev):
    m, k = x.shape
    assert w3.shape[1] == 2 * IN_TN and wt.shape[0] == NT_IN
    nt = s // tm
    ntab = cos_t.shape[1] // tm
    feats = (BRANCH_W, BRANCH_W, MLA_ROPE, FOX_HEADS)
    tspec = lambda rows: pl.BlockSpec((None, 1, rows, tm), lambda i, j: (layer, i // nt, 0, i % nt))
    row = lambda w: pl.BlockSpec((tm, w), lambda i, j: (i, 0))
    n_in, n_row = 7, 3
    return pl.pallas_call(
        _in_proj_kernel,
        out_shape=(jax.ShapeDtypeStruct((m, HQF_W), F32), jax.ShapeDtypeStruct((m, MLA_W), F32),
                   jax.ShapeDtypeStruct((m, HV_W), BF16))
        + tuple(jax.ShapeDtypeStruct((depth, bsz, f, s), F32) for f in feats),
        grid=(m // tm, 2),
        in_specs=[pl.BlockSpec((tm, k), lambda i, j: (i, 0)),
                  pl.BlockSpec((1, k), lambda i, j: (0, 0)),
                  _resident(w3.shape),
                  _resident(wt.shape),
                  pl.BlockSpec((MLA_ROPE, tm), lambda i, j: (0, i % ntab)),
                  pl.BlockSpec((MLA_ROPE, tm), lambda i, j: (0, i % ntab)),
                  pl.BlockSpec((FOX_HEADS, 1), lambda i, j: (0, 0))]
        + [pl.BlockSpec(memory_space=pl.ANY)] * len(prev),
        out_specs=(row(HQF_W), row(MLA_W), row(HV_W)) + tuple(tspec(f) for f in feats),
        scratch_shapes=[pltpu.VMEM((tm, k), BF16)],
        input_output_aliases={n_in + i: n_row + i for i in range(len(prev))},
        compiler_params=_params(("parallel", "arbitrary")),
        name="in_proj",
    )(x, g, w3, wt, cos_t, sin_t, b_col, *prev)


def _mla_prep_kernel(in_ref, cs_ref, sn_ref, gq_ref, gkv_ref, wuq_ref, wkt_ref, wv_ref, *rest):
    qx_ref, knt_ref, v_ref, ckvn_ref = rest[-4:]
    cos_t = cs_ref[...]
    sin_t = sn_ref[...]
    qn = _rms(in_ref[:, :MLA_Q_RANK], gq_ref[...]).astype(BF16)
    qall = _dot(qn, wuq_ref[...]) * ((MLA_NOPE + MLA_ROPE) ** -0.5 * LOG2E)
    for h in range(MLA_HEADS):
        lo = h * LANE
        qr = (qall[:, 512 + lo:512 + lo + LANE] * cos_t
              + qall[:, 1024 + lo:1024 + lo + LANE] * sin_t)
        qx_ref[:, 2 * lo:2 * lo + LANE] = qall[:, lo:lo + LANE].astype(BF16)
        qx_ref[:, 2 * lo + LANE:2 * lo + 2 * LANE] = qr.astype(BF16)
    ckvn = _rms(in_ref[:, MLA_Q_RANK:], gkv_ref[...])
    ckvn_ref[...] = ckvn
    cb = ckvn.astype(BF16)
    knt_ref[0] = _dot_nt(wkt_ref[...], cb).astype(BF16)
    v_ref[...] = _dot(cb, wv_ref[...]).astype(BF16)


def mla_prep(mla_in, cos_t, sin_t, gq, gkv, wuq, wkt, wv, bsz, s, tm, layer, depth, prev):
    m = mla_in.shape[0]
    nt = s // tm
    ntab = cos_t.shape[0] // tm
    row = lambda w: pl.BlockSpec((tm, w), lambda i: (i, 0))
    full = lambda a: pl.BlockSpec(a.shape, lambda i: (0,) * a.ndim)
    n_in = 8
    return pl.pallas_call(
        _mla_prep_kernel,
        out_shape=(jax.ShapeDtypeStruct((m, 1024), BF16),
                   jax.ShapeDtypeStruct((bsz, 512, s), BF16),
                   jax.ShapeDtypeStruct((m, 512), BF16),
                   jax.ShapeDtypeStruct((depth, m, MLA_KV_RANK), F32)),
        grid=(m // tm,),
        in_specs=[row(MLA_W),
                  pl.BlockSpec((tm, LANE), lambda i: (i % ntab, 0)),
                  pl.BlockSpec((tm, LANE), lambda i: (i % ntab, 0)),
                  full(gq), full(gkv), full(wuq), full(wkt), full(wv)]
        + [pl.BlockSpec(memory_space=pl.ANY)] * len(prev),
        out_specs=(row(1024), pl.BlockSpec((1, 512, tm), lambda i: (i // nt, 0, i % nt)),
                   row(512), pl.BlockSpec((None, tm, MLA_KV_RANK), lambda i: (layer, i, 0))),
        input_output_aliases={n_in + i: 3 + i for i in range(len(prev))},
        compiler_params=_params(("parallel",)),
        name="mla_prep",
    )(mla_in, cos_t, sin_t, gq, gkv, wuq, wkt, wv, *prev)


def _cumsum_kernel(x_ref, c_ref, *, w):
    s = x_ref.shape[1]
    r = lax.broadcasted_iota(jnp.int32, (w, w), 0)
    c = lax.broadcasted_iota(jnp.int32, (w, w), 1)
    upper = (r <= c).astype(BF16)
    n = x_ref.shape[0]

    def running(xg):
        h1 = xg.astype(BF16)
        r1 = xg - h1.astype(F32)
        h2 = r1.astype(BF16)
        h3 = (r1 - h2.astype(F32)).astype(BF16)
        parts = _dot(jnp.concatenate([h1, h2, h3], axis=0), upper)
        return (parts[:n] + parts[n:2 * n]) + parts[2 * n:]

    local = [running(x_ref[:, g * w:(g + 1) * w]) for g in range(s // w)]
    carry = jnp.zeros((n, 1), F32)
    for g, cum in enumerate(local):
        cum = cum + carry
        c_ref[:, g * w:(g + 1) * w] = cum
        carry = cum[:, w - 1:w]


def fox_cumsum(x, row0, bsz):
    n, h, s = x.shape
    rows = bsz * h
    assert (row0 * h) % rows == 0
    out = pl.pallas_call(
        functools.partial(_cumsum_kernel, w=min(LANE, s)),
        out_shape=jax.ShapeDtypeStruct((rows, s), F32),
        grid=(1,),
        in_specs=[pl.BlockSpec((rows, s), lambda i: (row0 * h // rows, 0))],
        out_specs=pl.BlockSpec((rows, s), lambda i: (0, 0)),
        compiler_params=_params(("arbitrary",)),
        name="fox_cumsum",
    )(x.reshape(n * h, s))
    return out.reshape(bsz, h, s)


def _pair_rows_mask(hh):
    sub = lax.broadcasted_iota(jnp.int32, (LANE, 1), 0)
    return (sub < FOX_DIM) if hh == 0 else (sub >= FOX_DIM)


def _fox_finish(acc0, acc1):
    lane = lax.broadcasted_iota(jnp.int32, (1, LANE), 1)
    o0 = acc0 / pltpu.roll(acc0, FOX_DIM, axis=1)
    o1 = acc1 / pltpu.roll(acc1, FOX_DIM, axis=1)
    return jnp.where(lane < FOX_DIM, o0, o1)


def _causal_schedule(nq):
    todo = {i: list(range(i + 1)) for i in range(nq)}
    order = []
    while any(todo.values()):
        for i in reversed(range(nq)):
            if todo[i]:
                order.append((i, todo[i].pop(0)))
    return order


def _fox_prompt_kernel(q_ref, kt_ref, vt_ref, ct_ref, o_ref, *, t, nq):
    hp = pl.program_id(1)
    lane = lax.broadcasted_iota(jnp.int32, (1, LANE), 1)
    lo = lane < FOX_DIM
    span = lambda i: slice(i * t, (i + 1) * t)
    ct = [ct_ref[0, 2 * hp + hh] for hh in range(2)]
    qs, cref = [], []
    for i in range(nq):
        q = q_ref[span(i), :].astype(F32) * (FOX_DIM ** -0.5 * LOG2E)
        qs.append((jnp.where(lo, q, 0.0).astype(BF16), jnp.where(lo, 0.0, q).astype(BF16)))
        cref.append([c[:, i * t:i * t + 1] for c in ct])
    kts = [kt_ref[0, :, span(j)].astype(BF16) for j in range(nq)]
    vts = [[jnp.where(_pair_rows_mask(hh), vt_ref[0, :, span(j)], 1.0).astype(BF16) for hh in range(2)]
           for j in range(nq)]
    causal = (lax.broadcasted_iota(jnp.int32, (t, t), 0) >= lax.broadcasted_iota(jnp.int32, (t, t), 1))

    def scores(i, j):
        out = []
        for hh in range(2):
            s = _dot(qs[i][hh], kts[j]) + (cref[i][hh] - ct[hh][:, span(j)]) * LOG2E
            out.append(jnp.where(causal, s, NEG_BIG) if i == j else s)
        return out

    state = [[(jnp.full((t, 1), NEG_BIG, F32), jnp.zeros((t, LANE), F32)) for _ in range(2)]
             for _ in range(nq)]
    order = _causal_schedule(nq)
    ss = scores(*order[0])
    for n, (i, j) in enumerate(order):
        nxt = scores(*order[n + 1]) if n + 1 < len(order) else None
        m_new = [jnp.maximum(state[i][hh][0], jnp.max(ss[hh], axis=1, keepdims=True)) for hh in range(2)]
        ps = [jnp.exp2(ss[hh] - m_new[hh]).astype(BF16) for hh in range(2)]
        pvs = [_dot_nt(ps[hh], vts[j][hh]) for hh in range(2)]
        state[i] = [(m_new[hh], jnp.exp2(state[i][hh][0] - m_new[hh]) * state[i][hh][1] + pvs[hh])
                    for hh in range(2)]
        ss = nxt
    for i in range(nq):
        o_ref[span(i), :] = _fox_finish(state[i][0][1], state[i][1][1]).astype(BF16)


def fox_prompt(hv, kt, vt, cumt, row0, bsz, s, t):
    m = hv.shape[0]
    return pl.pallas_call(
        functools.partial(_fox_prompt_kernel, t=t, nq=s // t),
        out_shape=jax.ShapeDtypeStruct((m, BRANCH_W), BF16),
        grid=(bsz, FOX_HEADS // 2),
        in_specs=[pl.BlockSpec((s, LANE), lambda b, h: (b, OFF_FQ // LANE + h)),
                  pl.BlockSpec((1, LANE, s), lambda b, h: (row0 + b, h, 0)),
                  pl.BlockSpec((1, LANE, s), lambda b, h: (row0 + b, h, 0)),
                  pl.BlockSpec((1, FOX_HEADS, 1, s), lambda b, h: (b, 0, 0, 0))],
        out_specs=pl.BlockSpec((s, LANE), lambda b, h: (b, h)),
        compiler_params=_params(("parallel", "parallel")),
        name="fox_prompt",
    )(hv, kt, vt, cumt.reshape(bsz, FOX_HEADS, 1, s))


def _mla_keys(knt, kpt):
    n = knt.shape[1]
    return jnp.concatenate([knt, kpt.astype(BF16), jnp.zeros((LANE - MLA_ROPE, n), BF16)], axis=0)


def _mla_prompt_kernel(q_ref, knt_ref, kpt_ref, v_ref, o_ref, *, t, nq):
    hs = range(2)
    span = lambda i: slice(i * t, (i + 1) * t)
    cols = lambda hh, w: slice(hh * w, (hh + 1) * w)
    keys = [[_mla_keys(knt_ref[0, cols(hh, LANE), span(j)], kpt_ref[0, :, span(j)]) for hh in hs]
            for j in range(nq)]
    mask = (lax.broadcasted_iota(jnp.int32, (t, t), 0) // CHUNK
            >= lax.broadcasted_iota(jnp.int32, (t, t), 1) // CHUNK)

    def scores(i, j):
        ss = [_dot(q_ref[span(i), cols(hh, 2 * LANE)], keys[j][hh]) for hh in hs]
        return [jnp.where(mask, s, NEG_BIG) for s in ss] if i == j else ss

    state = [[(jnp.full((t, 1), NEG_BIG, F32), jnp.zeros((t, 1), F32), jnp.zeros((t, LANE), F32))
              for _ in hs] for _ in range(nq)]
    order = _causal_schedule(nq)
    ss = scores(*order[0])
    for n, (i, j) in enumerate(order):
        nxt = scores(*order[n + 1]) if n + 1 < len(order) else None
        m_new = [jnp.maximum(state[i][hh][0], jnp.max(ss[hh], axis=1, keepdims=True)) for hh in hs]
        ps = [jnp.exp2(ss[hh] - m_new[hh]) for hh in hs]
        pvs = [_dot(ps[hh].astype(BF16), v_ref[span(j), cols(hh, LANE)]) for hh in hs]
        new = []
        for hh in hs:
            alpha = jnp.exp2(state[i][hh][0] - m_new[hh])
            new.append((m_new[hh], alpha * state[i][hh][1] + jnp.sum(ps[hh], axis=1, keepdims=True),
                        alpha * state[i][hh][2] + pvs[hh]))
        state[i] = new
        ss = nxt
    for i in range(nq):
        o_ref[span(i), :] = jnp.concatenate([state[i][hh][2] / state[i][hh][1] for hh in hs],
                                            axis=1).astype(BF16)


def mla_prompt(qx, knt, kpt, v, row0, bsz, s, t):
    assert t % CHUNK == 0
    m = qx.shape[0]
    return pl.pallas_call(
        functools.partial(_mla_prompt_kernel, t=t, nq=s // t),
        out_shape=jax.ShapeDtypeStruct((m, BRANCH_W), BF16),
        grid=(bsz, MLA_HEADS // 2),
        in_specs=[pl.BlockSpec((s, 4 * LANE), lambda b, h: (b, h)),
                  pl.BlockSpec((1, 2 * LANE, s), lambda b, h: (b, h, 0)),
                  pl.BlockSpec((1, MLA_ROPE, s), lambda b, h: (row0 + b, 0, 0)),
                  pl.BlockSpec((s, 2 * LANE), lambda b, h: (b, h))],
        out_specs=pl.BlockSpec((s, 2 * LANE), lambda b, h: (b, h)),
        compiler_params=_params(("parallel", "parallel")),
        name="mla_prompt",
    )(qx, knt, kpt, v)


def _fox_sample_kernel(q_ref, ktn_ref, vtn_ref, ktc_ref, vtc_ref, ctn_ref, ctc_ref, o_ref, *, t, p):
    lane = lax.broadcasted_iota(jnp.int32, (1, LANE), 1)
    lo = lane < FOX_DIM
    causal = (lax.broadcasted_iota(jnp.int32, (t, t), 0) >= lax.broadcasted_iota(jnp.int32, (t, t), 1))
    for hp in range(FOX_HEADS // 2):
        pair = slice(hp * LANE, (hp + 1) * LANE)
        q = q_ref[:, pair].astype(F32) * (FOX_DIM ** -0.5 * LOG2E)
        kt_c = ktc_ref[0, pair, :].astype(BF16)
        kt_n = ktn_ref[0, pair, :].astype(BF16)
        vt_c = vtc_ref[0, pair, :]
        vt_n = vtn_ref[0, pair, :]
        accs = []
        for hh in range(2):
            head = 2 * hp + hh
            qh = (jnp.where(lo, q, 0.0) if hh == 0 else jnp.where(lo, 0.0, q)).astype(BF16)
            cc = ctc_ref[0, head:head + 1, :]
            ctot = cc[:, p - 1:p]
            s_c = _dot(qh, kt_c) + (ctot - cc) * LOG2E
            s_n = _dot(qh, kt_n) - ctn_ref[0, head:head + 1, :] * LOG2E
            s_n = jnp.where(causal, s_n, NEG_BIG)
            m = jnp.maximum(jnp.max(s_c, axis=1, keepdims=True), jnp.max(s_n, axis=1, keepdims=True))
            rows = _pair_rows_mask(hh)
            accs.append(_dot_nt(jnp.exp2(s_c - m).astype(BF16), jnp.where(rows, vt_c, 1.0).astype(BF16))
                        + _dot_nt(jnp.exp2(s_n - m).astype(BF16), jnp.where(rows, vt_n, 1.0).astype(BF16)))
        o_ref[:, pair] = _fox_finish(accs[0], accs[1]).astype(BF16)


def fox_sample(hv, kt_n, vt_n, kt_c, vt_c, cumt_n, cumt_c, bsz, t, p, layer):
    cidx = lambda b: (layer * bsz + b, 0, 0)
    return pl.pallas_call(
        functools.partial(_fox_sample_kernel, t=t, p=p),
        out_shape=jax.ShapeDtypeStruct((bsz * t, BRANCH_W), BF16),
        grid=(bsz,),
        in_specs=[pl.BlockSpec((t, BRANCH_W), lambda b: (b, OFF_FQ // BRANCH_W)),
                  pl.BlockSpec((1, BRANCH_W, t), cidx),
                  pl.BlockSpec((1, BRANCH_W, t), cidx),
                  pl.BlockSpec((1, BRANCH_W, p), cidx),
                  pl.BlockSpec((1, BRANCH_W, p), cidx),
                  pl.BlockSpec((1, FOX_HEADS, t), lambda b: (b, 0, 0)),
                  pl.BlockSpec((1, FOX_HEADS, p), cidx)],
        out_specs=pl.BlockSpec((t, BRANCH_W), lambda b: (b, 0)),
        compiler_params=_params(("parallel",)),
        name="fox_sample",
    )(hv, kt_n, vt_n, kt_c, vt_c, cumt_n, cumt_c)


def _mla_sample_kernel(q_ref, kntn_ref, kptn_ref, vn_ref, ckv_ref, kptc_ref, wkt_ref, wv_ref, o_ref,
                       *, t, p):
    qc = (p + lax.broadcasted_iota(jnp.int32, (t, t), 0)) // CHUNK
    kc = (p + lax.broadcasted_iota(jnp.int32, (t, t), 1)) // CHUNK
    kpt_c, kpt_n = kptc_ref[0], kptn_ref[0]
    cb = ckv_ref[...].astype(BF16)
    for h in range(MLA_HEADS):
        cs = slice(h * LANE, (h + 1) * LANE)
        knt_c = _dot_nt(wkt_ref[cs, :], cb).astype(BF16)
        v_c = _dot(cb, wv_ref[:, cs]).astype(BF16)
        q = q_ref[:, 2 * h * LANE:2 * (h + 1) * LANE]
        s_c = _dot(q, _mla_keys(knt_c, kpt_c))
        s_n = _dot(q, _mla_keys(kntn_ref[0, cs, :], kpt_n))
        s_n = jnp.where(qc >= kc, s_n, NEG_BIG)
        m = jnp.maximum(jnp.max(s_c, axis=1, keepdims=True), jnp.max(s_n, axis=1, keepdims=True))
        p_c = jnp.exp2(s_c - m)
        p_n = jnp.exp2(s_n - m)
        l = jnp.sum(p_c, axis=1, keepdims=True) + jnp.sum(p_n, axis=1, keepdims=True)
        o = _dot(p_c.astype(BF16), v_c) + _dot(p_n.astype(BF16), vn_ref[:, cs])
        o_ref[:, cs] = (o / l).astype(BF16)


def mla_sample(qx, knt_n, kpt_n, v_n, ckv_c, kpt_c, wkt, wv, bsz, t, p, layer):
    assert (p - 1) // CHUNK <= p // CHUNK
    full = lambda a: pl.BlockSpec(a.shape, lambda b: (0,) * a.ndim)
    return pl.pallas_call(
        functools.partial(_mla_sample_kernel, t=t, p=p),
        out_shape=jax.ShapeDtypeStruct((bsz * t, BRANCH_W), BF16),
        grid=(bsz,),
        in_specs=[pl.BlockSpec((t, 8 * LANE), lambda b: (b, 0)),
                  pl.BlockSpec((1, BRANCH_W, t), lambda b: (b, 0, 0)),
                  pl.BlockSpec((1, MLA_ROPE, t), lambda b: (layer * bsz + b, 0, 0)),
                  pl.BlockSpec((t, BRANCH_W), lambda b: (b, 0)),
                  pl.BlockSpec((p, MLA_KV_RANK), lambda b: (layer * bsz + b, 0)),
                  pl.BlockSpec((1, MLA_ROPE, p), lambda b: (layer * bsz + b, 0, 0)),
                  full(wkt), full(wv)],
        out_specs=pl.BlockSpec((t, BRANCH_W), lambda b: (b, 0)),
        compiler_params=_params(("parallel",)),
        name="mla_sample",
    )(qx, knt_n, kpt_n, v_n, ckv_c, kpt_c, wkt, wv)


def _hgrn_gates(z, lb, tri):
    logf = _log_sigmoid(z) + jnp.log(1.0 + lb * jnp.exp(jnp.minimum(-z, EXP_CLIP)))
    k = (1.0 - lb) * (1.0 / (1.0 + jnp.exp(z)))
    h1 = logf.astype(BF16)
    r1 = logf - h1.astype(F32)
    h2 = r1.astype(BF16)
    h3 = (r1 - h2.astype(F32)).astype(BF16)
    hcat = jnp.concatenate([h1, h2, h3], axis=1)
    g = tri.shape[0]
    parts = jnp.concatenate([_dot(tri, hcat[r:r + g, :]) for r in range(0, z.shape[0], g)], axis=0)
    lc = ((parts[:, :LANE] + parts[:, LANE:2 * LANE]) + parts[:, 2 * LANE:]) * LOG2E
    return k, lc


def _hgrn_local(q, z, lb, tri, v_b, ln, sel, k_ref, lc_ref):
    k, lc = _hgrn_gates(z, lb, tri)
    k_ref[...] = k
    lc_ref[...] = lc
    nchunk = q.shape[0] // ln
    nsb = q.shape[0] // SUB_BLOCK
    per = ln // SUB_BLOCK
    half = SUB_BLOCK // 2
    rows = lambda a, i: a[i * SUB_BLOCK:(i + 1) * SUB_BLOCK, :]
    lcb = [jnp.zeros((1, LANE), F32) if i % per == 0 else lc_ref[i * SUB_BLOCK - 1:i * SUB_BLOCK, :]
           for i in range(nsb)]
    lcb_rows = jnp.concatenate([jnp.broadcast_to(b, (SUB_BLOCK, LANE)) for b in lcb], axis=0)
    last = [lc_ref[(c + 1) * ln - 1:(c + 1) * ln, :] for c in range(nchunk)]
    last_rows = jnp.concatenate([jnp.broadcast_to(b, (ln, LANE)) for b in last], axis=0)
    qh = (q * jnp.exp2(lc - lcb_rows)).astype(BF16)
    qe = (q * jnp.exp2(lc)).astype(BF16)
    kdec = (k * jnp.exp2(last_rows - lc)).astype(BF16)
    a_off = {}
    for i in range(nsb):
        n = (i % per) * SUB_BLOCK
        if n:
            c0 = i * SUB_BLOCK - n
            kt = (k[c0:c0 + n, :] * jnp.exp2(lcb[i] - lc[c0:c0 + n, :])).astype(BF16)
            a_off[i] = _dot_nt(rows(qh, i), kt)
    yield None
    pieces = []
    for i in range(nsb):
        q_i, lc_i = rows(q, i), rows(lc, i)
        cols = []
        for s in range(SUB_BLOCK):
            lo = 0 if s < half else half
            r = i * SUB_BLOCK + s
            d = lc_i[lo:, :] - lc_ref[r:r + 1, :]
            d = (jnp.concatenate([jnp.minimum(d[:half, :], 0.0), d[half:, :]], axis=0) if s < half
                 else jnp.minimum(d, 0.0))
            w = (q_i[lo:, :] * k_ref[r:r + 1, :]) * jnp.exp2(d)
            if lo:
                w = jnp.concatenate([jnp.zeros((lo, LANE), F32), w], axis=0)
            cols.append(w.astype(BF16))
        pieces.append(jnp.concatenate(cols, axis=1))
    a_all = _dot(jnp.concatenate(pieces, axis=0), sel)
    inc = [_dot_tn(v_b[c * ln:(c + 1) * ln, :], kdec[c * ln:(c + 1) * ln, :]) for c in range(nchunk)]
    yield None
    pair_ok = (lax.broadcasted_iota(jnp.int32, (SUB_BLOCK, LANE), 0)
               >= lax.broadcasted_iota(jnp.int32, (SUB_BLOCK, LANE), 1))
    off = []
    for i in range(nsb):
        n = (i % per) * SUB_BLOCK
        off.append(_dot(a_off[i].astype(BF16), v_b[i * SUB_BLOCK - n:i * SUB_BLOCK, :]) if n
                   else jnp.zeros((SUB_BLOCK, LANE), F32))
    diag = [_dot(jnp.where(pair_ok, rows(a_all, i), 0.0)[:, :SUB_BLOCK].astype(BF16), rows(v_b, i))
            for i in range(nsb)]
    local = jnp.concatenate(off, axis=0) + jnp.concatenate(diag, axis=0)
    dec = [jnp.exp2(b) for b in last]
    yield local, qe, inc, dec


def _hgrn_kernel(*refs, ln, nchunk, has_init):
    refs = list(refs)
    hq_ref, hf_ref, hi_ref, hg_ref, lb_ref, go_ref, sel_ref = refs[:7]
    s0_ref = refs[7] if has_init else None
    o_ref, sout_ref, st_ref, k_ref, lc_ref = refs[-5:]
    step = pl.program_id(1)
    nrows = ln * nchunk

    @pl.when(step == 0)
    def _():
        for h in range(HG_HEADS):
            st_ref[h] = s0_ref[0, h].T if has_init else jnp.zeros((HG_DV, HG_DK), F32)

    ng = min(nrows, 2 * LANE)
    assert ng % ln == 0 and nrows % ng == 0
    ri = lax.broadcasted_iota(jnp.int32, (ng, ng), 0)
    ci = lax.broadcasted_iota(jnp.int32, (ng, ng), 1)
    tri = ((ri >= ci) & (ri // ln == ci // ln)).astype(BF16)

    def finish(h, local, qe, inc, dec):
        cs = slice(h * LANE, (h + 1) * LANE)
        st = st_ref[h]
        parts = []
        for c in range(nchunk):
            parts.append(_dot_nt(qe[c * ln:(c + 1) * ln, :], st.astype(BF16)))
            st = st * dec[c] + inc[c]
        st_ref[h] = st
        o = local + jnp.concatenate(parts, axis=0)
        o_ref[:, cs] = (_rms(o, go_ref[...]) * _sigmoid(hg_ref[:, cs].astype(F32))).astype(BF16)

    def start(h):
        cs = slice(h * LANE, (h + 1) * LANE)
        gen = _hgrn_local(hq_ref[:, cs], hf_ref[:, cs], lb_ref[:, cs], tri, hi_ref[:, cs].astype(BF16),
                          ln, sel_ref[...], k_ref.at[h], lc_ref.at[h])
        next(gen)
        return gen

    gens = {0: start(0)}
    for h in range(HG_HEADS):
        if h + 1 < HG_HEADS:
            gens[h + 1] = start(h + 1)
        next(gens[h])
        if h > 0:
            finish(h - 1, *next(gens.pop(h - 1)))
    finish(HG_HEADS - 1, *next(gens.pop(HG_HEADS - 1)))

    @pl.when(step == pl.num_programs(1) - 1)
    def _():
        for h in range(HG_HEADS):
            sout_ref[0, h] = st_ref[h].T


def hgrn(hqf, hv, lb, g_out, bsz, s, ln, rows, s0=None, s0_row0=0):
    m = hqf.shape[0]
    ns = s // rows
    has_init = s0 is not None
    sel = (np.arange(SUB_BLOCK * LANE)[:, None] // LANE == np.arange(LANE)[None, :])
    sel = jnp.asarray(sel, BF16)
    blk = lambda col: pl.BlockSpec((rows, BRANCH_W), lambda b, i: (b * ns + i, col))
    ins = [hqf, hqf, hv, hv, lb, g_out, sel]
    specs = [blk(0), blk(1), blk(0), blk(1),
             pl.BlockSpec((1, BRANCH_W), lambda b, i: (0, 0)),
             pl.BlockSpec((1, HG_DV), lambda b, i: (0, 0)),
             pl.BlockSpec(sel.shape, lambda b, i: (0, 0))]
    if has_init:
        ins.append(s0)
        specs.append(pl.BlockSpec((1, HG_HEADS, HG_DK, HG_DV), lambda b, i: (s0_row0 + b, 0, 0, 0)))
    return pl.pallas_call(
        functools.partial(_hgrn_kernel, ln=ln, nchunk=rows // ln, has_init=has_init),
        out_shape=(jax.ShapeDtypeStruct((m, BRANCH_W), BF16),
                   jax.ShapeDtypeStruct((bsz, HG_HEADS, HG_DK, HG_DV), F32)),
        grid=(bsz, ns),
        in_specs=specs,
        out_specs=(pl.BlockSpec((rows, BRANCH_W), lambda b, i: (b * ns + i, 0)),
                   pl.BlockSpec((1, HG_HEADS, HG_DK, HG_DV), lambda b, i: (b, 0, 0, 0))),
        scratch_shapes=[pltpu.VMEM((HG_HEADS, HG_DV, HG_DK), F32),
                        pltpu.VMEM((HG_HEADS, rows, LANE), F32),
                        pltpu.VMEM((HG_HEADS, rows, LANE), F32)],
        compiler_params=_params(("parallel", "arbitrary")),
        name="hgrn",
    )(*ins)


def _merge_kernel(of_ref, om_ref, oh_ref, x_ref, wg_ref, wb_ref, wo_ref, g0_ref, g1_ref, o_ref):
    x = x_ref[...]
    h = _rms(x, g0_ref[...]).astype(BF16)
    branches = (of_ref, om_ref, oh_ref)
    gates = [_dot(h, wg_ref[i]) for i in range(3)]
    outs = [_dot(branches[i][...], wb_ref[i]) for i in range(3)]
    merged = (_sigmoid(gates[0]) * outs[0] + _sigmoid(gates[1]) * outs[1]) + _sigmoid(gates[2]) * outs[2]
    y = _dot(merged.astype(BF16), wo_ref[...])
    o_ref[...] = x + _rms(y, g1_ref[...])


def merge_out(o_fox, o_mla, o_hg, x, wg, wb, wo, g0, g1, tm, layer):
    m = x.shape[0]
    row = lambda w: pl.BlockSpec((tm, w), lambda i: (i, 0))
    vec = pl.BlockSpec((1, D_MODEL), lambda i: (0, 0))
    return pl.pallas_call(
        _merge_kernel,
        out_shape=jax.ShapeDtypeStruct((m, D_MODEL), F32),
        grid=(m // tm,),
        in_specs=[row(BRANCH_W), row(BRANCH_W), row(BRANCH_W), row(D_MODEL),
                  _resident(wg.shape), _resident_layer(wb, layer), _resident_layer(wo, layer), vec, vec],
        out_specs=row(D_MODEL),
        compiler_params=_params(("parallel",)),
        name="merge_out",
    )(o_fox, o_mla, o_hg, x, wg, wb, wo, g0, g1)


def _mem_kv_kernel(x_ref, w_ref, *rest):
    k_ref, v_ref = rest[-2:]
    tm = x_ref.shape[0]
    y = _dot(x_ref[...].astype(BF16), w_ref[...])
    for h in range(X_HEADS):
        k_ref[pl.ds(h, tm, stride=X_HEADS), :] = y[:, h * X_DIM:(h + 1) * X_DIM]
        v_ref[pl.ds(h, tm, stride=X_HEADS), :] = y[:, (X_HEADS + h) * X_DIM:(X_HEADS + h + 1) * X_DIM]


def mem_kv(mem, w, tm, layer, depth, prev):
    m, k = mem.shape
    out = jax.ShapeDtypeStruct((depth, m * X_HEADS, X_DIM), F32)
    ospec = pl.BlockSpec((None, tm * X_HEADS, X_DIM), lambda i: (layer, i, 0))
    return pl.pallas_call(
        _mem_kv_kernel,
        out_shape=(out, out),
        grid=(m // tm,),
        in_specs=[pl.BlockSpec((tm, k), lambda i: (i, 0)), _resident_layer(w, layer)]
        + [pl.BlockSpec(memory_space=pl.ANY)] * len(prev),
        out_specs=(ospec, ospec),
        input_output_aliases={2 + i: i for i in range(len(prev))},
        compiler_params=_params(("parallel",)),
        name="mem_kv",
    )(mem, w, *prev)


def _cross_kernel(x_ref, mk_ref, mv_ref, wq_ref, wo_ref, g2_ref, g3_ref, o_ref):
    x = x_ref[...]
    h = _rms(x, g2_ref[...]).astype(BF16)
    q = _dot(h, wq_ref[...])
    qb = (q * (X_DIM ** -0.5 * LOG2E)).astype(BF16)
    cols = [slice(hd * X_DIM, (hd + 1) * X_DIM) for hd in range(X_HEADS)]
    head = lambda ref, hd: ref[pl.ds(hd, N_MEM, stride=X_HEADS), :].astype(BF16)
    ss = [_dot_nt(qb[:, cs], head(mk_ref, hd)) for hd, cs in enumerate(cols)]
    ps = [jnp.exp2(s - jnp.max(s, axis=1, keepdims=True)) for s in ss]
    pvs = [_dot(p.astype(BF16), head(mv_ref, hd)) for hd, p in enumerate(ps)]
    outs = [pv / jnp.sum(p, axis=1, keepdims=True) for pv, p in zip(pvs, ps)]
    ox = jnp.concatenate(outs, axis=1).astype(BF16)
    o_ref[...] = x + _rms(_dot(ox, wo_ref[...]), g3_ref[...])


def cross_block(x, mk, mv, wq, wo, g2, g3, bsz, s, tm, mem_row0, layer):
    m = x.shape[0]
    nt = s // tm
    vec = pl.BlockSpec((1, D_MODEL), lambda b, i: (0, 0))
    return pl.pallas_call(
        _cross_kernel,
        out_shape=jax.ShapeDtypeStruct((m, D_MODEL), F32),
        grid=(bsz, nt),
        in_specs=[pl.BlockSpec((tm, D_MODEL), lambda b, i: (b * nt + i, 0)),
                  pl.BlockSpec((N_MEM * X_HEADS, X_DIM), lambda b, i: (mem_row0 + b, 0)),
                  pl.BlockSpec((N_MEM * X_HEADS, X_DIM), lambda b, i: (mem_row0 + b, 0)),
                  _resident_layer(wq, layer), _resident_layer(wo, layer), vec, vec],
        out_specs=pl.BlockSpec((tm, D_MODEL), lambda b, i: (b * nt + i, 0)),
        compiler_params=_params(("parallel", "parallel")),
        name="cross_attn",
    )(x, mk, mv, wq, wo, g2, g3)


def _mlp_kernel(x_ref, wu_ref, wd_ref, g4_ref, g5_ref, o_ref, h_ref, acc_ref):
    j = pl.program_id(1)

    @pl.when(j == 0)
    def _():
        h_ref[...] = _rms(x_ref[...], g4_ref[...]).astype(BF16)
        acc_ref[...] = jnp.zeros_like(acc_ref)

    cols = pl.ds(pl.multiple_of(j * D_MODEL, D_MODEL), D_MODEL)
    u = jnp.square(jnp.maximum(_dot(h_ref[...], wu_ref[:, cols]), 0.0)).astype(BF16)
    acc_ref[...] += _dot(u, wd_ref[j])

    @pl.when(j == pl.num_programs(1) - 1)
    def _():
        o_ref[...] = x_ref[...] + _rms(acc_ref[...], g5_ref[...])


def mlp_block(x, wu3, wd3, g4, g5, tm, layer):
    m = x.shape[0]
    nj = wd3.shape[1]
    vec = pl.BlockSpec((1, D_MODEL), lambda i, j: (0, 0))
    return pl.pallas_call(
        _mlp_kernel,
        out_shape=jax.ShapeDtypeStruct((m, D_MODEL), F32),
        grid=(m // tm, nj),
        in_specs=[pl.BlockSpec((tm, D_MODEL), lambda i, j: (i, 0)),
                  _resident_layer(wu3, layer), _resident_layer(wd3, layer), vec, vec],
        out_specs=pl.BlockSpec((tm, D_MODEL), lambda i, j: (i, 0)),
        scratch_shapes=[pltpu.VMEM((tm, D_MODEL), BF16), pltpu.VMEM((tm, D_MODEL), F32)],
        compiler_params=_params(("parallel", "arbitrary")),
        name="mlp",
    )(x, wu3, wd3, g4, g5)


def _prep_stacked_weights(w_branch, w_out, w_xq, w_mem_k, w_mem_v, w_xo, w_up, w_down):
    depth, nff = w_up.shape[0], D_FF // D_MODEL
    return dict(
        wb=w_branch.astype(BF16), wo=w_out.astype(BF16), wxq=w_xq.astype(BF16), wxo=w_xo.astype(BF16),
        wmem=jnp.concatenate([w_mem_k, w_mem_v], axis=2).astype(BF16),
        wu3=w_up.astype(BF16),
        wd3=w_down.astype(BF16).reshape(depth, nff, D_MODEL, D_MODEL))


def _prep_layer_weights(w_in, w_mla_uq, w_mla_ukv):
    idx = np.cumsum((0,) + IN_SIZES)
    seg = lambda i: w_in[:, idx[i]:idx[i + 1]]
    fq, fk, fv, ff, cq, ckv, kpe, hq, hf, hi, hg, ga, gb, gc = (seg(i) for i in range(14))
    half = MLA_ROPE // 2
    kpe_sw = jnp.concatenate([kpe[:, half:], kpe[:, :half]], axis=1)
    pad = jnp.zeros((D_MODEL, IN_TN - HV_W), w_in.dtype)
    w_p = jnp.concatenate([hq, hf, cq, ckv, hi, hg, fq, pad], axis=1).astype(BF16)
    w_gate = jnp.stack([ga, gb, gc]).astype(BF16)
    w_t = jnp.concatenate([fk, fv, kpe, kpe_sw, ff], axis=1).T.astype(BF16)
    hd = MLA_NOPE + MLA_ROPE
    zq = jnp.zeros((MLA_Q_RANK, LANE - MLA_ROPE), w_mla_uq.dtype)
    nope, rope_n, rope_s = [], [], []
    for h in range(MLA_HEADS):
        base = h * hd
        nope.append(w_mla_uq[:, base:base + MLA_NOPE])
        x1 = w_mla_uq[:, base + MLA_NOPE:base + MLA_NOPE + half]
        x2 = w_mla_uq[:, base + MLA_NOPE + half:base + hd]
        rope_n += [x1, x2, zq]
        rope_s += [x2, x1, zq]
    wuq = jnp.concatenate(nope + rope_n + rope_s, axis=1).astype(BF16)
    kvd = MLA_NOPE + MLA_V
    wkt = jnp.concatenate([w_mla_ukv[:, h * kvd:h * kvd + MLA_NOPE] for h in range(MLA_HEADS)],
                          axis=1).T.astype(BF16)
    wv = jnp.concatenate([w_mla_ukv[:, h * kvd + MLA_NOPE:(h + 1) * kvd] for h in range(MLA_HEADS)],
                         axis=1).astype(BF16)
    return dict(w_in3=w_p, w_t=w_t, w_gate=w_gate, wuq=wuq, wkt=wkt, wv=wv)


def _rope_tables(pos):
    half = MLA_ROPE // 2
    freq = ROPE_THETA ** (-jnp.arange(half, dtype=F32) / half)
    ang = pos.astype(F32)[:, None] * freq[None, :]
    cos, sin = jnp.cos(ang), jnp.sin(ang)
    z = jnp.zeros((pos.shape[0], LANE - MLA_ROPE), F32)
    cos_r = jnp.concatenate([cos, cos, z], axis=1)
    sin_r = jnp.concatenate([-sin, sin, z], axis=1)
    return cos_r, sin_r, cos_r[:, :MLA_ROPE].T, sin_r[:, :MLA_ROPE].T


def _tile(n, pref):
    t = min(n, pref)
    assert n % t == 0
    return t


def _layer(x, bsz, s, pos0, w, lb, b_fox, g_q, g_kv, g_hout, g_norm, mem_k, mem_v, mem_row0, past, cfg,
           layer, depth, shared):
    m = bsz * s
    g = lambda i: g_norm[i][None, :]
    tm_in = _tile(s, cfg["tm_in"])
    tm_p = _tile(s, cfg["tm_prep"])
    cos_r, sin_r, cos_c, sin_c = _rope_tables(pos0 + jnp.arange(s))
    hqf, mla_in, hv, kt, vt, kpet, logft = in_proj(x, g(0), w["w_in3"], w["w_t"], cos_c, sin_c,
                                                   b_fox[:, None], bsz, s, tm_in, layer, depth, shared[:4])
    qx, knt, v, ckv_n = mla_prep(mla_in, cos_r, sin_r, g_q[None, :], g_kv[None, :],
                                 w["wuq"], w["wkt"], w["wv"], bsz, s, tm_p, layer, depth, shared[4:])
    row0 = layer * bsz
    rows = lambda a: a.reshape((depth * bsz,) + a.shape[2:])
    cumt = fox_cumsum(rows(logft), row0, bsz)
    if past is None:
        t = _tile(s, cfg["t_attn"])
        o_fox = fox_prompt(hv, rows(kt), rows(vt), cumt, row0, bsz, s, t)
        o_mla = mla_prompt(qx, knt, rows(kpet), v, row0, bsz, s, t)
        o_hg, hg_state = hgrn(hqf, hv, lb[None, :], g_hout[None, :], bsz, s, CHUNK,
                              _tile(s, cfg["hg_rows"]))
    else:
        c_kt, c_vt, c_cumt, c_ckv, c_kpt, c_hg = past
        p = c_kt.shape[2]
        o_fox = fox_sample(hv, rows(kt), rows(vt), c_kt, c_vt, cumt, c_cumt, bsz, s, p, layer)
        o_mla = mla_sample(qx, knt, rows(kpet), v, c_ckv, c_kpt, w["wkt"], w["wv"], bsz, s, p, layer)
        o_hg, hg_state = hgrn(hqf, hv, lb[None, :], g_hout[None, :], bsz, s, s, s, s0=c_hg, s0_row0=row0)
    x = merge_out(o_fox, o_mla, o_hg, x, w["w_gate"], w["wb"], w["wo"], g(0), g(1),
                  _tile(m, cfg["tm_merge"]), layer)
    x = cross_block(x, mem_k, mem_v, w["wxq"], w["wxo"], g(2), g(3), bsz, s, _tile(s, cfg["tm_cross"]),
                    mem_row0, layer)
    x = mlp_block(x, w["wu3"], w["wd3"], g(4), g(5), _tile(m, cfg["tm_mlp"]), layer)
    return x, (kt, vt, kpet, logft, ckv_n), hg_state


def _from_feature_major(stacked, heads):
    a = jnp.swapaxes(stacked, 2, 3)
    if heads:
        a = a.reshape(a.shape[:3] + (heads, a.shape[3] // heads))
    return a


def _assemble_states(shared, hg_states, bsz, s):
    kt, vt, kpet, logft, ckv = shared
    return (_from_feature_major(kt, FOX_HEADS), _from_feature_major(vt, FOX_HEADS),
            _from_feature_major(logft, 0), ckv.reshape(ckv.shape[0], bsz, s, MLA_KV_RANK),
            _from_feature_major(kpet, 0), jnp.stack(hg_states))


_CFG = dict(tm_in=1024, tm_prep=1024, t_attn=512, hg_rows=512, tm_merge=512, tm_cross=1024,
            tm_mlp=1024, tm_mem=512)


def kernel(x_prompt, x_sample, cache_fox_k, cache_fox_v, cache_fox_logf, cache_mla_ckv, cache_mla_kpe,
           state_hgrn, cache_mem_k, cache_mem_v, mem_prompt, w_in, b_fox, g_mla_q, w_mla_uq, g_mla_kv,
           w_mla_ukv, g_hgrn_out, lb_hgrn, w_branch, w_out, w_xq, w_mem_k, w_mem_v, w_xo, w_up, w_down,
           g_norm):
    cfg = _CFG
    depth = w_in.shape[0]
    lb_p = jax.nn.softmax(lb_hgrn.astype(F32), axis=0)
    lb_all = jnp.cumsum(lb_p, axis=0) - lb_p[0]
    stacked = _prep_stacked_weights(w_branch, w_out, w_xq, w_mem_k, w_mem_v, w_xo, w_up, w_down)
    ws = [dict(stacked, **_prep_layer_weights(w_in[l], w_mla_uq[l], w_mla_ukv[l])) for l in range(depth)]

    def run_layer(x, bsz, s, pos0, l, mk, mv, mem_row0, past, shared):
        return _layer(x, bsz, s, pos0, ws[l], lb_all[l], b_fox[l], g_mla_q[l], g_mla_kv[l],
                      g_hgrn_out[l], g_norm[l], mk, mv, mem_row0, past, cfg, l, depth, shared)

    bp, sp, _ = x_prompt.shape
    x = x_prompt.reshape(bp * sp, D_MODEL)
    mem = mem_prompt.reshape(bp * N_MEM, D_MODEL)
    shared, hg_states, p_mem = (), [], ()
    mem_rows = lambda a: a.reshape(-1, X_DIM)
    for l in range(depth):
        p_mem = mem_kv(mem, ws[l]["wmem"], _tile(bp * N_MEM, cfg["tm_mem"]), l, depth, p_mem)
        x, shared, hg = run_layer(x, bp, sp, 0, l, mem_rows(p_mem[0]), mem_rows(p_mem[1]), l * bp, None,
                                  shared)
        hg_states.append(hg)
    y_prompt = x.reshape(bp, sp, D_MODEL)
    p_out = _assemble_states(shared, hg_states, bp, sp) + tuple(
        a.reshape(depth, bp, N_MEM, X_HEADS, X_DIM) for a in p_mem)

    bs, ts, _ = x_sample.shape
    p = cache_fox_k.shape[2]
    fm = lambda c: jnp.moveaxis(c, 2, -1)
    c_kt = fm(cache_fox_k).reshape(depth * bs, BRANCH_W, p)
    c_vt = fm(cache_fox_v).reshape(depth * bs, BRANCH_W, p)
    c_kpt = fm(cache_mla_kpe).reshape(depth * bs, MLA_ROPE, p)
    c_cumt = fox_cumsum(fm(cache_fox_logf).reshape(depth * bs, FOX_HEADS, p), 0, depth * bs)
    c_ckv = cache_mla_ckv.reshape(depth * bs * p, MLA_KV_RANK)
    c_hg = state_hgrn.reshape((depth * bs,) + state_hgrn.shape[2:])
    c_mk, c_mv = mem_rows(cache_mem_k), mem_rows(cache_mem_v)
    x = x_sample.reshape(bs * ts, D_MODEL)
    shared, hg_states = (), []
    for l in range(depth):
        past = (c_kt, c_vt, c_cumt, c_ckv, c_kpt, c_hg)
        x, shared, hg = run_layer(x, bs, ts, p, l, c_mk, c_mv, l * bs, past, shared)
        hg_states.append(hg)
    y_sample = x.reshape(bs, ts, D_MODEL)
    return (y_prompt, y_sample, *p_out, *_assemble_states(shared, hg_states, bs, ts))
```

```python
import functools

import numpy as np
import jax
import jax.numpy as jnp
from jax import lax
from jax.experimental import pallas as pl
from jax.experimental.pallas import tpu as pltpu

F32 = jnp.float32
BF16 = jnp.bfloat16

D_MODEL = 1024
CHUNK = 64
N_MEM = 256
EPS = 1e-6
NEG_BIG = -1e30
EXP_CLIP = 80.0
FOX_HEADS = 8
FOX_DIM = 64
MLA_HEADS = 4
MLA_Q_RANK = 384
MLA_KV_RANK = 256
MLA_NOPE = 128
MLA_ROPE = 64
MLA_V = 128
ROPE_THETA = 10000.0
HG_HEADS = 4
HG_DK = 128
HG_DV = 128
X_HEADS = 4
X_DIM = 128
D_FF = 4 * D_MODEL
BRANCH_W = 512
IN_SIZES = (512, 512, 512, FOX_HEADS, MLA_Q_RANK, MLA_KV_RANK, MLA_ROPE, 512, 512, 512, 512,
            D_MODEL, D_MODEL, D_MODEL)

LANE = 128
SUB_BLOCK = 16
VMEM_LIMIT = 56 * 1024 * 1024
LOG2E = 1.4426950408889634

IN_TN = 1664
HQF_W, MLA_W, HV_W = 1024, MLA_Q_RANK + MLA_KV_RANK, 1536
OFF_FQ = 1024
T_FK, T_FV, T_KPE, T_KPE_SW, T_FF, NT_IN = 0, 512, 1024, 1088, 1152, 1160


def _params(sem, vmem=VMEM_LIMIT):
    return pltpu.CompilerParams(dimension_semantics=sem, vmem_limit_bytes=vmem)


def _dot(a, b):
    return jnp.dot(a, b, preferred_element_type=F32)


def _dot_nt(a, b):
    return lax.dot_general(a, b, (((1,), (1,)), ((), ())), preferred_element_type=F32)


def _dot_tn(a, b):
    return lax.dot_general(a, b, (((0,), (0,)), ((), ())), preferred_element_type=F32)


def _rms(x, g):
    y = x * lax.rsqrt(jnp.mean(x * x, axis=-1, keepdims=True) + EPS)
    return y * g


def _log_sigmoid(z):
    return jnp.minimum(z, 0.0) - jnp.log(1.0 + jnp.exp(-jnp.abs(z)))


def _sigmoid(z):
    return 1.0 / (1.0 + jnp.exp(-z))


def _resident(shape):
    nd = len(shape)
    return pl.BlockSpec(shape, lambda *_: (0,) * nd, pipeline_mode=pl.Buffered(1))


def _resident_layer(stacked, layer):
    nd = stacked.ndim - 1
    return pl.BlockSpec((None,) + stacked.shape[1:], lambda *_: (layer,) + (0,) * nd,
                        pipeline_mode=pl.Buffered(1))


def _in_proj_kernel(x_ref, g_ref, w_ref, wt_ref, cos_ref, sin_ref, bf_ref, *rest):
    hqf_ref, mla_ref, hv_ref, kt_ref, vt_ref, kpe_ref, lf_ref, h_ref = rest[-8:]
    j = pl.program_id(1)

    @pl.when(j == 0)
    def _():
        h = _rms(x_ref[...], g_ref[...]).astype(BF16)
        h_ref[...] = h
        yt = _dot_nt(wt_ref[...], h)
        kt_ref[0] = yt[T_FK:T_FK + BRANCH_W]
        vt_ref[0] = yt[T_FV:T_FV + BRANCH_W]
        kpe_ref[0] = (yt[T_KPE:T_KPE + MLA_ROPE] * cos_ref[...]
                      + yt[T_KPE_SW:T_KPE_SW + MLA_ROPE] * sin_ref[...])
        lf_ref[0] = _log_sigmoid(yt[T_FF:T_FF + FOX_HEADS] + bf_ref[...])
        y = _dot(h, w_ref[:, :IN_TN])
        hqf_ref[...] = y[:, :HQF_W]
        mla_ref[...] = y[:, HQF_W:]

    @pl.when(j == 1)
    def _():
        hv_ref[...] = _dot(h_ref[...], w_ref[:, IN_TN:])[:, :HV_W].astype(BF16)


def in_proj(x, g, w3, wt, cos_t, sin_t, b_col, bsz, s, tm, layer, depth, prev):
    m, k = x.shape
    assert w3.shape[1] == 2 * IN_TN and wt.shape[0] == NT_IN
    nt = s // tm
    ntab = cos_t.shape[1] // tm
    feats = (BRANCH_W, BRANCH_W, MLA_ROPE, FOX_HEADS)
    tspec = lambda rows: pl.BlockSpec((None, 1, rows, tm), lambda i, j: (layer, i // nt, 0, i % nt))
    row = lambda w: pl.BlockSpec((tm, w), lambda i, j: (i, 0))
    n_in, n_row = 7, 3
    return pl.pallas_call(
        _in_proj_kernel,
        out_shape=(jax.ShapeDtypeStruct((m, HQF_W), F32), jax.ShapeDtypeStruct((m, MLA_W), F32),
                   jax.ShapeDtypeStruct((m, HV_W), BF16))
        + tuple(jax.ShapeDtypeStruct((depth, bsz, f, s), F32) for f in feats),
        grid=(m // tm, 2),
        in_specs=[pl.BlockSpec((tm, k), lambda i, j: (i, 0)),
                  pl.BlockSpec((1, k), lambda i, j: (0, 0)),
                  _resident(w3.shape),
                  _resident(wt.shape),
                  pl.BlockSpec((MLA_ROPE, tm), lambda i, j: (0, i % ntab)),
                  pl.BlockSpec((MLA_ROPE, tm), lambda i, j: (0, i % ntab)),
                  pl.BlockSpec((FOX_HEADS, 1), lambda i, j: (0, 0))]
        + [pl.BlockSpec(memory_space=pl.ANY)] * len(prev),
        out_specs=(row(HQF_W), row(MLA_W), row(HV_W)) + tuple(tspec(f) for f in feats),
        scratch_shapes=[pltpu.VMEM((tm, k), BF16)],
        input_output_aliases={n_in + i: n_row + i for i in range(len(prev))},
        compiler_params=_params(("parallel", "arbitrary")),
        name="in_proj",
    )(x, g, w3, wt, cos_t, sin_t, b_col, *prev)


def _mla_prep_kernel(in_ref, cs_ref, sn_ref, gq_ref, gkv_ref, wuq_ref, wkt_ref, wv_ref, *rest):
    qx_ref, knt_ref, v_ref, ckvn_ref = rest[-4:]
    cos_t = cs_ref[...]
    sin_t = sn_ref[...]
    qn = _rms(in_ref[:, :MLA_Q_RANK], gq_ref[...]).astype(BF16)
    qall = _dot(qn, wuq_ref[...]) * ((MLA_NOPE + MLA_ROPE) ** -0.5 * LOG2E)
    for h in range(MLA_HEADS):
        lo = h * LANE
        qr = (qall[:, 512 + lo:512 + lo + LANE] * cos_t
              + qall[:, 1024 + lo:1024 + lo + LANE] * sin_t)
        qx_ref[:, 2 * lo:2 * lo + LANE] = qall[:, lo:lo + LANE].astype(BF16)
        qx_ref[:, 2 * lo + LANE:2 * lo + 2 * LANE] = qr.astype(BF16)
    ckvn = _rms(in_ref[:, MLA_Q_RANK:], gkv_ref[...])
    ckvn_ref[...] = ckvn
    cb = ckvn.astype(BF16)
    knt_ref[0] = _dot_nt(wkt_ref[...], cb).astype(BF16)
    v_ref[...] = _dot(cb, wv_ref[...]).astype(BF16)


def mla_prep(mla_in, cos_t, sin_t, gq, gkv, wuq, wkt, wv, bsz, s, tm, layer, depth, prev):
    m = mla_in.shape[0]
    nt = s // tm
    ntab = cos_t.shape[0] // tm
    row = lambda w: pl.BlockSpec((tm, w), lambda i: (i, 0))
    full = lambda a: pl.BlockSpec(a.shape, lambda i: (0,) * a.ndim)
    n_in = 8
    return pl.pallas_call(
        _mla_prep_kernel,
        out_shape=(jax.ShapeDtypeStruct((m, 1024), BF16),
                   jax.ShapeDtypeStruct((bsz, 512, s), BF16),
                   jax.ShapeDtypeStruct((m, 512), BF16),
                   jax.ShapeDtypeStruct((depth, m, MLA_KV_RANK), F32)),
        grid=(m // tm,),
        in_specs=[row(MLA_W),
                  pl.BlockSpec((tm, LANE), lambda i: (i % ntab, 0)),
                  pl.BlockSpec((tm, LANE), lambda i: (i % ntab, 0)),
                  full(gq), full(gkv), full(wuq), full(wkt), full(wv)]
        + [pl.BlockSpec(memory_space=pl.ANY)] * len(prev),
        out_specs=(row(1024), pl.BlockSpec((1, 512, tm), lambda i: (i // nt, 0, i % nt)),
                   row(512), pl.BlockSpec((None, tm, MLA_KV_RANK), lambda i: (layer, i, 0))),
        input_output_aliases={n_in + i: 3 + i for i in range(len(prev))},
        compiler_params=_params(("parallel",)),
        name="mla_prep",
    )(mla_in, cos_t, sin_t, gq, gkv, wuq, wkt, wv, *prev)


def _cumsum_kernel(x_ref, c_ref, *, w):
    s = x_ref.shape[1]
    r = lax.broadcasted_iota(jnp.int32, (w, w), 0)
    c = lax.broadcasted_iota(jnp.int32, (w, w), 1)
    upper = (r <= c).astype(BF16)
    n = x_ref.shape[0]

    def running(xg):
        h1 = xg.astype(BF16)
        r1 = xg - h1.astype(F32)
        h2 = r1.astype(BF16)
        h3 = (r1 - h2.astype(F32)).astype(BF16)
        parts = _dot(jnp.concatenate([h1, h2, h3], axis=0), upper)
        return (parts[:n] + parts[n:2 * n]) + parts[2 * n:]

    local = [running(x_ref[:, g * w:(g + 1) * w]) for g in range(s // w)]
    carry = jnp.zeros((n, 1), F32)
    for g, cum in enumerate(local):
        cum = cum + carry
        c_ref[:, g * w:(g + 1) * w] = cum
        carry = cum[:, w - 1:w]


def fox_cumsum(x, row0, bsz):
    n, h, s = x.shape
    rows = bsz * h
    assert (row0 * h) % rows == 0
    out = pl.pallas_call(
        functools.partial(_cumsum_kernel, w=min(LANE, s)),
        out_shape=jax.ShapeDtypeStruct((rows, s), F32),
        grid=(1,),
        in_specs=[pl.BlockSpec((rows, s), lambda i: (row0 * h // rows, 0))],
        out_specs=pl.BlockSpec((rows, s), lambda i: (0, 0)),
        compiler_params=_params(("arbitrary",)),
        name="fox_cumsum",
    )(x.reshape(n * h, s))
    return out.reshape(bsz, h, s)


def _pair_rows_mask(hh):
    sub = lax.broadcasted_iota(jnp.int32, (LANE, 1), 0)
    return (sub < FOX_DIM) if hh == 0 else (sub >= FOX_DIM)


def _fox_finish(acc0, acc1):
    lane = lax.broadcasted_iota(jnp.int32, (1, LANE), 1)
    o0 = acc0 / pltpu.roll(acc0, FOX_DIM, axis=1)
    o1 = acc1 / pltpu.roll(acc1, FOX_DIM, axis=1)
    return jnp.where(lane < FOX_DIM, o0, o1)


def _causal_schedule(nq):
    todo = {i: list(range(i + 1)) for i in range(nq)}
    order = []
    while any(todo.values()):
        for i in reversed(range(nq)):
            if todo[i]:
                order.append((i, todo[i].pop(0)))
    return order


def _fox_prompt_kernel(q_ref, kt_ref, vt_ref, ct_ref, o_ref, *, t, nq):
    hp = pl.program_id(1)
    lane = lax.broadcasted_iota(jnp.int32, (1, LANE), 1)
    lo = lane < FOX_DIM
    span = lambda i: slice(i * t, (i + 1) * t)
    ct = [ct_ref[0, 2 * hp + hh] for hh in range(2)]
    qs, cref = [], []
    for i in range(nq):
        q = q_ref[span(i), :].astype(F32) * (FOX_DIM ** -0.5 * LOG2E)
        qs.append((jnp.where(lo, q, 0.0).astype(BF16), jnp.where(lo, 0.0, q).astype(BF16)))
        cref.append([c[:, i * t:i * t + 1] for c in ct])
    kts = [kt_ref[0, :, span(j)].astype(BF16) for j in range(nq)]
    vts = [[jnp.where(_pair_rows_mask(hh), vt_ref[0, :, span(j)], 1.0).astype(BF16) for hh in range(2)]
           for j in range(nq)]
    causal = (lax.broadcasted_iota(jnp.int32, (t, t), 0) >= lax.broadcasted_iota(jnp.int32, (t, t), 1))

    def scores(i, j):
        out = []
        for hh in range(2):
            s = _dot(qs[i][hh], kts[j]) + (cref[i][hh] - ct[hh][:, span(j)]) * LOG2E
            out.append(jnp.where(causal, s, NEG_BIG) if i == j else s)
        return out

    state = [[(jnp.full((t, 1), NEG_BIG, F32), jnp.zeros((t, LANE), F32)) for _ in range(2)]
             for _ in range(nq)]
    order = _causal_schedule(nq)
    ss = scores(*order[0])
    for n, (i, j) in enumerate(order):
        nxt = scores(*order[n + 1]) if n + 1 < len(order) else None
        m_new = [jnp.maximum(state[i][hh][0], jnp.max(ss[hh], axis=1, keepdims=True)) for hh in range(2)]
        ps = [jnp.exp2(ss[hh] - m_new[hh]).astype(BF16) for hh in range(2)]
        pvs = [_dot_nt(ps[hh], vts[j][hh]) for hh in range(2)]
        state[i] = [(m_new[hh], jnp.exp2(state[i][hh][0] - m_new[hh]) * state[i][hh][1] + pvs[hh])
                    for hh in range(2)]
        ss = nxt
    for i in range(nq):
        o_ref[span(i), :] = _fox_finish(state[i][0][1], state[i][1][1]).astype(BF16)


def fox_prompt(hv, kt, vt, cumt, row0, bsz, s, t):
    m = hv.shape[0]
    return pl.pallas_call(
        functools.partial(_fox_prompt_kernel, t=t, nq=s // t),
        out_shape=jax.ShapeDtypeStruct((m, BRANCH_W), BF16),
        grid=(bsz, FOX_HEADS // 2),
        in_specs=[pl.BlockSpec((s, LANE), lambda b, h: (b, OFF_FQ // LANE + h)),
                  pl.BlockSpec((1, LANE, s), lambda b, h: (row0 + b, h, 0)),
                  pl.BlockSpec((1, LANE, s), lambda b, h: (row0 + b, h, 0)),
                  pl.BlockSpec((1, FOX_HEADS, 1, s), lambda b, h: (b, 0, 0, 0))],
        out_specs=pl.BlockSpec((s, LANE), lambda b, h: (b, h)),
        compiler_params=_params(("parallel", "parallel")),
        name="fox_prompt",
    )(hv, kt, vt, cumt.reshape(bsz, FOX_HEADS, 1, s))


def _mla_keys(knt, kpt):
    n = knt.shape[1]
    return jnp.concatenate([knt, kpt.astype(BF16), jnp.zeros((LANE - MLA_ROPE, n), BF16)], axis=0)


def _mla_prompt_kernel(q_ref, knt_ref, kpt_ref, v_ref, o_ref, *, t, nq):
    hs = range(2)
    span = lambda i: slice(i * t, (i + 1) * t)
    cols = lambda hh, w: slice(hh * w, (hh + 1) * w)
    keys = [[_mla_keys(knt_ref[0, cols(hh, LANE), span(j)], kpt_ref[0, :, span(j)]) for hh in hs]
            for j in range(nq)]
    mask = (lax.broadcasted_iota(jnp.int32, (t, t), 0) // CHUNK
            >= lax.broadcasted_iota(jnp.int32, (t, t), 1) // CHUNK)

    def scores(i, j):
        ss = [_dot(q_ref[span(i), cols(hh, 2 * LANE)], keys[j][hh]) for hh in hs]
        return [jnp.where(mask, s, NEG_BIG) for s in ss] if i == j else ss

    state = [[(jnp.full((t, 1), NEG_BIG, F32), jnp.zeros((t, 1), F32), jnp.zeros((t, LANE), F32))
              for _ in hs] for _ in range(nq)]
    order = _causal_schedule(nq)
    ss = scores(*order[0])
    for n, (i, j) in enumerate(order):
        nxt = scores(*order[n + 1]) if n + 1 < len(order) else None
        m_new = [jnp.maximum(state[i][hh][0], jnp.max(ss[hh], axis=1, keepdims=True)) for hh in hs]
        ps = [jnp.exp2(ss[hh] - m_new[hh]) for hh in hs]
        pvs = [_dot(ps[hh].astype(BF16), v_ref[span(j), cols(hh, LANE)]) for hh in hs]
        new = []
        for hh in hs:
            alpha = jnp.exp2(state[i][hh][0] - m_new[hh])
            new.append((m_new[hh], alpha * state[i][hh][1] + jnp.sum(ps[hh], axis=1, keepdims=True),
                        alpha * state[i][hh][2] + pvs[hh]))
        state[i] = new
        ss = nxt
    for i in range(nq):
        o_ref[span(i), :] = jnp.concatenate([state[i][hh][2] / state[i][hh][1] for hh in hs],
                                            axis=1).astype(BF16)


def mla_prompt(qx, knt, kpt, v, row0, bsz, s, t):
    assert t % CHUNK == 0
    m = qx.shape[0]
    return pl.pallas_call(
        functools.partial(_mla_prompt_kernel, t=t, nq=s // t),
        out_shape=jax.ShapeDtypeStruct((m, BRANCH_W), BF16),
        grid=(bsz, MLA_HEADS // 2),
        in_specs=[pl.BlockSpec((s, 4 * LANE), lambda b, h: (b, h)),
                  pl.BlockSpec((1, 2 * LANE, s), lambda b, h: (b, h, 0)),
                  pl.BlockSpec((1, MLA_ROPE, s), lambda b, h: (row0 + b, 0, 0)),
                  pl.BlockSpec((s, 2 * LANE), lambda b, h: (b, h))],
        out_specs=pl.BlockSpec((s, 2 * LANE), lambda b, h: (b, h)),
        compiler_params=_params(("parallel", "parallel")),
        name="mla_prompt",
    )(qx, knt, kpt, v)


def _fox_sample_kernel(q_ref, ktn_ref, vtn_ref, ktc_ref, vtc_ref, ctn_ref, ctc_ref, o_ref, *, t, p):
    lane = lax.broadcasted_iota(jnp.int32, (1, LANE), 1)
    lo = lane < FOX_DIM
    causal = (lax.broadcasted_iota(jnp.int32, (t, t), 0) >= lax.broadcasted_iota(jnp.int32, (t, t), 1))
    for hp in range(FOX_HEADS // 2):
        pair = slice(hp * LANE, (hp + 1) * LANE)
        q = q_ref[:, pair].astype(F32) * (FOX_DIM ** -0.5 * LOG2E)
        kt_c = ktc_ref[0, pair, :].astype(BF16)
        kt_n = ktn_ref[0, pair, :].astype(BF16)
        vt_c = vtc_ref[0, pair, :]
        vt_n = vtn_ref[0, pair, :]
        accs = []
        for hh in range(2):
            head = 2 * hp + hh
            qh = (jnp.where(lo, q, 0.0) if hh == 0 else jnp.where(lo, 0.0, q)).astype(BF16)
            cc = ctc_ref[0, head:head + 1, :]
            ctot = cc[:, p - 1:p]
            s_c = _dot(qh, kt_c) + (ctot - cc) * LOG2E
            s_n = _dot(qh, kt_n) - ctn_ref[0, head:head + 1, :] * LOG2E
            s_n = jnp.where(causal, s_n, NEG_BIG)
            m = jnp.maximum(jnp.max(s_c, axis=1, keepdims=True), jnp.max(s_n, axis=1, keepdims=True))
            rows = _pair_rows_mask(hh)
            accs.append(_dot_nt(jnp.exp2(s_c - m).astype(BF16), jnp.where(rows, vt_c, 1.0).astype(BF16))
                        + _dot_nt(jnp.exp2(s_n - m).astype(BF16), jnp.where(rows, vt_n, 1.0).astype(BF16)))
        o_ref[:, pair] = _fox_finish(accs[0], accs[1]).astype(BF16)


def fox_sample(hv, kt_n, vt_n, kt_c, vt_c, cumt_n, cumt_c, bsz, t, p, layer):
    cidx = lambda b: (layer * bsz + b, 0, 0)
    return pl.pallas_call(
        functools.partial(_fox_sample_kernel, t=t, p=p),
        out_shape=jax.ShapeDtypeStruct((bsz * t, BRANCH_W), BF16),
        grid=(bsz,),
        in_specs=[pl.BlockSpec((t, BRANCH_W), lambda b: (b, OFF_FQ // BRANCH_W)),
                  pl.BlockSpec((1, BRANCH_W, t), cidx),
                  pl.BlockSpec((1, BRANCH_W, t), cidx),
                  pl.BlockSpec((1, BRANCH_W, p), cidx),
                  pl.BlockSpec((1, BRANCH_W, p), cidx),
                  pl.BlockSpec((1, FOX_HEADS, t), lambda b: (b, 0, 0)),
                  pl.BlockSpec((1, FOX_HEADS, p), cidx)],
        out_specs=pl.BlockSpec((t, BRANCH_W), lambda b: (b, 0)),
        compiler_params=_params(("parallel",)),
        name="fox_sample",
    )(hv, kt_n, vt_n, kt_c, vt_c, cumt_n, cumt_c)


def _mla_sample_kernel(q_ref, kntn_ref, kptn_ref, vn_ref, ckv_ref, kptc_ref, wkt_ref, wv_ref, o_ref,
                       *, t, p):
    qc = (p + lax.broadcasted_iota(jnp.int32, (t, t), 0)) // CHUNK
    kc = (p + lax.broadcasted_iota(jnp.int32, (t, t), 1)) // CHUNK
    kpt_c, kpt_n = kptc_ref[0], kptn_ref[0]
    cb = ckv_ref[...].astype(BF16)
    for h in range(MLA_HEADS):
        cs = slice(h * LANE, (h + 1) * LANE)
        knt_c = _dot_nt(wkt_ref[cs, :], cb).astype(BF16)
        v_c = _dot(cb, wv_ref[:, cs]).astype(BF16)
        q = q_ref[:, 2 * h * LANE:2 * (h + 1) * LANE]
        s_c = _dot(q, _mla_keys(knt_c, kpt_c))
        s_n = _dot(q, _mla_keys(kntn_ref[0, cs, :], kpt_n))
        s_n = jnp.where(qc >= kc, s_n, NEG_BIG)
        m = jnp.maximum(jnp.max(s_c, axis=1, keepdims=True), jnp.max(s_n, axis=1, keepdims=True))
        p_c = jnp.exp2(s_c - m)
        p_n = jnp.exp2(s_n - m)
        l = jnp.sum(p_c, axis=1, keepdims=True) + jnp.sum(p_n, axis=1, keepdims=True)
        o = _dot(p_c.astype(BF16), v_c) + _dot(p_n.astype(BF16), vn_ref[:, cs])
        o_ref[:, cs] = (o / l).astype(BF16)


def mla_sample(qx, knt_n, kpt_n, v_n, ckv_c, kpt_c, wkt, wv, bsz, t, p, layer):
    assert (p - 1) // CHUNK <= p // CHUNK
    full = lambda a: pl.BlockSpec(a.shape, lambda b: (0,) * a.ndim)
    return pl.pallas_call(
        functools.partial(_mla_sample_kernel, t=t, p=p),
        out_shape=jax.ShapeDtypeStruct((bsz * t, BRANCH_W), BF16),
        grid=(bsz,),
        in_specs=[pl.BlockSpec((t, 8 * LANE), lambda b: (b, 0)),
                  pl.BlockSpec((1, BRANCH_W, t), lambda b: (b, 0, 0)),
                  pl.BlockSpec((1, MLA_ROPE, t), lambda b: (layer * bsz + b, 0, 0)),
                  pl.BlockSpec((t, BRANCH_W), lambda b: (b, 0)),
                  pl.BlockSpec((p, MLA_KV_RANK), lambda b: (layer * bsz + b, 0)),
                  pl.BlockSpec((1, MLA_ROPE, p), lambda b: (layer * bsz + b, 0, 0)),
                  full(wkt), full(wv)],
        out_specs=pl.BlockSpec((t, BRANCH_W), lambda b: (b, 0)),
        compiler_params=_params(("parallel",)),
        name="mla_sample",
    )(qx, knt_n, kpt_n, v_n, ckv_c, kpt_c, wkt, wv)


def _hgrn_gates(z, lb, tri):
    logf = _log_sigmoid(z) + jnp.log(1.0 + lb * jnp.exp(jnp.minimum(-z, EXP_CLIP)))
    k = (1.0 - lb) * (1.0 / (1.0 + jnp.exp(z)))
    h1 = logf.astype(BF16)
    r1 = logf - h1.astype(F32)
    h2 = r1.astype(BF16)
    h3 = (r1 - h2.astype(F32)).astype(BF16)
    hcat = jnp.concatenate([h1, h2, h3], axis=1)
    g = tri.shape[0]
    parts = jnp.concatenate([_dot(tri, hcat[r:r + g, :]) for r in range(0, z.shape[0], g)], axis=0)
    lc = ((parts[:, :LANE] + parts[:, LANE:2 * LANE]) + parts[:, 2 * LANE:]) * LOG2E
    return k, lc


def _hgrn_local(q, z, lb, tri, v_b, ln, sel, k_ref, lc_ref):
    k, lc = _hgrn_gates(z, lb, tri)
    k_ref[...] = k
    lc_ref[...] = lc
    nchunk = q.shape[0] // ln
    nsb = q.shape[0] // SUB_BLOCK
    per = ln // SUB_BLOCK
    half = SUB_BLOCK // 2
    rows = lambda a, i: a[i * SUB_BLOCK:(i + 1) * SUB_BLOCK, :]
    lcb = [jnp.zeros((1, LANE), F32) if i % per == 0 else lc_ref[i * SUB_BLOCK - 1:i * SUB_BLOCK, :]
           for i in range(nsb)]
    lcb_rows = jnp.concatenate([jnp.broadcast_to(b, (SUB_BLOCK, LANE)) for b in lcb], axis=0)
    last = [lc_ref[(c + 1) * ln - 1:(c + 1) * ln, :] for c in range(nchunk)]
    last_rows = jnp.concatenate([jnp.broadcast_to(b, (ln, LANE)) for b in last], axis=0)
    qh = (q * jnp.exp2(lc - lcb_rows)).astype(BF16)
    qe = (q * jnp.exp2(lc)).astype(BF16)
    kdec = (k * jnp.exp2(last_rows - lc)).astype(BF16)
    a_off = {}
    for i in range(nsb):
        n = (i % per) * SUB_BLOCK
        if n:
            c0 = i * SUB_BLOCK - n
            kt = (k[c0:c0 + n, :] * jnp.exp2(lcb[i] - lc[c0:c0 + n, :])).astype(BF16)
            a_off[i] = _dot_nt(rows(qh, i), kt)
    yield None
    pieces = []
    for i in range(nsb):
        q_i, lc_i = rows(q, i), rows(lc, i)
        cols = []
        for s in range(SUB_BLOCK):
            lo = 0 if s < half else half
            r = i * SUB_BLOCK + s
            d = lc_i[lo:, :] - lc_ref[r:r + 1, :]
            d = (jnp.concatenate([jnp.minimum(d[:half, :], 0.0), d[half:, :]], axis=0) if s < half
                 else jnp.minimum(d, 0.0))
            w = (q_i[lo:, :] * k_ref[r:r + 1, :]) * jnp.exp2(d)
            if lo:
                w = jnp.concatenate([jnp.zeros((lo, LANE), F32), w], axis=0)
            cols.append(w.astype(BF16))
        pieces.append(jnp.concatenate(cols, axis=1))
    a_all = _dot(jnp.concatenate(pieces, axis=0), sel)
    inc = [_dot_tn(v_b[c * ln:(c + 1) * ln, :], kdec[c * ln:(c + 1) * ln, :]) for c in range(nchunk)]
    yield None
    pair_ok = (lax.broadcasted_iota(jnp.int32, (SUB_BLOCK, LANE), 0)
               >= lax.broadcasted_iota(jnp.int32, (SUB_BLOCK, LANE), 1))
    off = []
    for i in range(nsb):
        n = (i % per) * SUB_BLOCK
        off.append(_dot(a_off[i].astype(BF16), v_b[i * SUB_BLOCK - n:i * SUB_BLOCK, :]) if n
                   else jnp.zeros((SUB_BLOCK, LANE), F32))
    diag = [_dot(jnp.where(pair_ok, rows(a_all, i), 0.0)[:, :SUB_BLOCK].astype(BF16), rows(v_b, i))
            for i in range(nsb)]
    local = jnp.concatenate(off, axis=0) + jnp.concatenate(diag, axis=0)
    dec = [jnp.exp2(b) for b in last]
    yield local, qe, inc, dec


def _hgrn_kernel(*refs, ln, nchunk, has_init):
    refs = list(refs)
    hq_ref, hf_ref, hi_ref, hg_ref, lb_ref, go_ref, sel_ref = refs[:7]
    s0_ref = refs[7] if has_init else None
    o_ref, sout_ref, st_ref, k_ref, lc_ref = refs[-5:]
    step = pl.program_id(1)
    nrows = ln * nchunk

    @pl.when(step == 0)
    def _():
        for h in range(HG_HEADS):
            st_ref[h] = s0_ref[0, h].T if has_init else jnp.zeros((HG_DV, HG_DK), F32)

    ng = min(nrows, 2 * LANE)
    assert ng % ln == 0 and nrows % ng == 0
    ri = lax.broadcasted_iota(jnp.int32, (ng, ng), 0)
    ci = lax.broadcasted_iota(jnp.int32, (ng, ng), 1)
    tri = ((ri >= ci) & (ri // ln == ci // ln)).astype(BF16)

    def finish(h, local, qe, inc, dec):
        cs = slice(h * LANE, (h + 1) * LANE)
        st = st_ref[h]
        parts = []
        for c in range(nchunk):
            parts.append(_dot_nt(qe[c * ln:(c + 1) * ln, :], st.astype(BF16)))
            st = st * dec[c] + inc[c]
        st_ref[h] = st
        o = local + jnp.concatenate(parts, axis=0)
        o_ref[:, cs] = (_rms(o, go_ref[...]) * _sigmoid(hg_ref[:, cs].astype(F32))).astype(BF16)

    def start(h):
        cs = slice(h * LANE, (h + 1) * LANE)
        gen = _hgrn_local(hq_ref[:, cs], hf_ref[:, cs], lb_ref[:, cs], tri, hi_ref[:, cs].astype(BF16),
                          ln, sel_ref[...], k_ref.at[h], lc_ref.at[h])
        next(gen)
        return gen

    gens = {0: start(0)}
    for h in range(HG_HEADS):
        if h + 1 < HG_HEADS:
            gens[h + 1] = start(h + 1)
        next(gens[h])
        if h > 0:
            finish(h - 1, *next(gens.pop(h - 1)))
    finish(HG_HEADS - 1, *next(gens.pop(HG_HEADS - 1)))

    @pl.when(step == pl.num_programs(1) - 1)
    def _():
        for h in range(HG_HEADS):
            sout_ref[0, h] = st_ref[h].T


def hgrn(hqf, hv, lb, g_out, bsz, s, ln, rows, s0=None, s0_row0=0):
    m = hqf.shape[0]
    ns = s // rows
    has_init = s0 is not None
    sel = (np.arange(SUB_BLOCK * LANE)[:, None] // LANE == np.arange(LANE)[None, :])
    sel = jnp.asarray(sel, BF16)
    blk = lambda col: pl.BlockSpec((rows, BRANCH_W), lambda b, i: (b * ns + i, col))
    ins = [hqf, hqf, hv, hv, lb, g_out, sel]
    specs = [blk(0), blk(1), blk(0), blk(1),
             pl.BlockSpec((1, BRANCH_W), lambda b, i: (0, 0)),
             pl.BlockSpec((1, HG_DV), lambda b, i: (0, 0)),
             pl.BlockSpec(sel.shape, lambda b, i: (0, 0))]
    if has_init:
        ins.append(s0)
        specs.append(pl.BlockSpec((1, HG_HEADS, HG_DK, HG_DV), lambda b, i: (s0_row0 + b, 0, 0, 0)))
    return pl.pallas_call(
        functools.partial(_hgrn_kernel, ln=ln, nchunk=rows // ln, has_init=has_init),
        out_shape=(jax.ShapeDtypeStruct((m, BRANCH_W), BF16),
                   jax.ShapeDtypeStruct((bsz, HG_HEADS, HG_DK, HG_DV), F32)),
        grid=(bsz, ns),
        in_specs=specs,
        out_specs=(pl.BlockSpec((rows, BRANCH_W), lambda b, i: (b * ns + i, 0)),
                   pl.BlockSpec((1, HG_HEADS, HG_DK, HG_DV), lambda b, i: (b, 0, 0, 0))),
        scratch_shapes=[pltpu.VMEM((HG_HEADS, HG_DV, HG_DK), F32),
                        pltpu.VMEM((HG_HEADS, rows, LANE), F32),
                        pltpu.VMEM((HG_HEADS, rows, LANE), F32)],
        compiler_params=_params(("parallel", "arbitrary")),
        name="hgrn",
    )(*ins)


def _merge_kernel(of_ref, om_ref, oh_ref, x_ref, wg_ref, wb_ref, wo_ref, g0_ref, g1_ref, o_ref):
    x = x_ref[...]
    h = _rms(x, g0_ref[...]).astype(BF16)
    branches = (of_ref, om_ref, oh_ref)
    gates = [_dot(h, wg_ref[i]) for i in range(3)]
    outs = [_dot(branches[i][...], wb_ref[i]) for i in range(3)]
    merged = (_sigmoid(gates[0]) * outs[0] + _sigmoid(gates[1]) * outs[1]) + _sigmoid(gates[2]) * outs[2]
    y = _dot(merged.astype(BF16), wo_ref[...])
    o_ref[...] = x + _rms(y, g1_ref[...])


def merge_out(o_fox, o_mla, o_hg, x, wg, wb, wo, g0, g1, tm, layer):
    m = x.shape[0]
    row = lambda w: pl.BlockSpec((tm, w), lambda i: (i, 0))
    vec = pl.BlockSpec((1, D_MODEL), lambda i: (0, 0))
    return pl.pallas_call(
        _merge_kernel,
        out_shape=jax.ShapeDtypeStruct((m, D_MODEL), F32),
        grid=(m // tm,),
        in_specs=[row(BRANCH_W), row(BRANCH_W), row(BRANCH_W), row(D_MODEL),
                  _resident(wg.shape), _resident_layer(wb, layer), _resident_layer(wo, layer), vec, vec],
        out_specs=row(D_MODEL),
        compiler_params=_params(("parallel",)),
        name="merge_out",
    )(o_fox, o_mla, o_hg, x, wg, wb, wo, g0, g1)


def _mem_kv_kernel(x_ref, w_ref, *rest):
    k_ref, v_ref = rest[-2:]
    tm = x_ref.shape[0]
    y = _dot(x_ref[...].astype(BF16), w_ref[...])
    for h in range(X_HEADS):
        k_ref[pl.ds(h, tm, stride=X_HEADS), :] = y[:, h * X_DIM:(h + 1) * X_DIM]
        v_ref[pl.ds(h, tm, stride=X_HEADS), :] = y[:, (X_HEADS + h) * X_DIM:(X_HEADS + h + 1) * X_DIM]


def mem_kv(mem, w, tm, layer, depth, prev):
    m, k = mem.shape
    out = jax.ShapeDtypeStruct((depth, m * X_HEADS, X_DIM), F32)
    ospec = pl.BlockSpec((None, tm * X_HEADS, X_DIM), lambda i: (layer, i, 0))
    return pl.pallas_call(
        _mem_kv_kernel,
        out_shape=(out, out),
        grid=(m // tm,),
        in_specs=[pl.BlockSpec((tm, k), lambda i: (i, 0)), _resident_layer(w, layer)]
        + [pl.BlockSpec(memory_space=pl.ANY)] * len(prev),
        out_specs=(ospec, ospec),
        input_output_aliases={2 + i: i for i in range(len(prev))},
        compiler_params=_params(("parallel",)),
        name="mem_kv",
    )(mem, w, *prev)


def _cross_kernel(x_ref, mk_ref, mv_ref, wq_ref, wo_ref, g2_ref, g3_ref, o_ref):
    x = x_ref[...]
    h = _rms(x, g2_ref[...]).astype(BF16)
    q = _dot(h, wq_ref[...])
    qb = (q * (X_DIM ** -0.5 * LOG2E)).astype(BF16)
    cols = [slice(hd * X_DIM, (hd + 1) * X_DIM) for hd in range(X_HEADS)]
    head = lambda ref, hd: ref[pl.ds(hd, N_MEM, stride=X_HEADS), :].astype(BF16)
    ss = [_dot_nt(qb[:, cs], head(mk_ref, hd)) for hd, cs in enumerate(cols)]
    ps = [jnp.exp2(s - jnp.max(s, axis=1, keepdims=True)) for s in ss]
    pvs = [_dot(p.astype(BF16), head(mv_ref, hd)) for hd, p in enumerate(ps)]
    outs = [pv / jnp.sum(p, axis=1, keepdims=True) for pv, p in zip(pvs, ps)]
    ox = jnp.concatenate(outs, axis=1).astype(BF16)
    o_ref[...] = x + _rms(_dot(ox, wo_ref[...]), g3_ref[...])


def cross_block(x, mk, mv, wq, wo, g2, g3, bsz, s, tm, mem_row0, layer):
    m = x.shape[0]
    nt = s // tm
    vec = pl.BlockSpec((1, D_MODEL), lambda b, i: (0, 0))
    return pl.pallas_call(
        _cross_kernel,
        out_shape=jax.ShapeDtypeStruct((m, D_MODEL), F32),
        grid=(bsz, nt),
        in_specs=[pl.BlockSpec((tm, D_MODEL), lambda b, i: (b * nt + i, 0)),
                  pl.BlockSpec((N_MEM * X_HEADS, X_DIM), lambda b, i: (mem_row0 + b, 0)),
                  pl.BlockSpec((N_MEM * X_HEADS, X_DIM), lambda b, i: (mem_row0 + b, 0)),
                  _resident_layer(wq, layer), _resident_layer(wo, layer), vec, vec],
        out_specs=pl.BlockSpec((tm, D_MODEL), lambda b, i: (b * nt + i, 0)),
        compiler_params=_params(("parallel", "parallel")),
        name="cross_attn",
    )(x, mk, mv, wq, wo, g2, g3)


def _mlp_kernel(x_ref, wu_ref, wd_ref, g4_ref, g5_ref, o_ref, h_ref, acc_ref):
    j = pl.program_id(1)

    @pl.when(j == 0)
    def _():
        h_ref[...] = _rms(x_ref[...], g4_ref[...]).astype(BF16)
        acc_ref[...] = jnp.zeros_like(acc_ref)

    cols = pl.ds(pl.multiple_of(j * D_MODEL, D_MODEL), D_MODEL)
    u = jnp.square(jnp.maximum(_dot(h_ref[...], wu_ref[:, cols]), 0.0)).astype(BF16)
    acc_ref[...] += _dot(u, wd_ref[j])

    @pl.when(j == pl.num_programs(1) - 1)
    def _():
        o_ref[...] = x_ref[...] + _rms(acc_ref[...], g5_ref[...])


def mlp_block(x, wu3, wd3, g4, g5, tm, layer):
    m = x.shape[0]
    nj = wd3.shape[1]
    vec = pl.BlockSpec((1, D_MODEL), lambda i, j: (0, 0))
    return pl.pallas_call(
        _mlp_kernel,
        out_shape=jax.ShapeDtypeStruct((m, D_MODEL), F32),
        grid=(m // tm, nj),
        in_specs=[pl.BlockSpec((tm, D_MODEL), lambda i, j: (i, 0)),
                  _resident_layer(wu3, layer), _resident_layer(wd3, layer), vec, vec],
        out_specs=pl.BlockSpec((tm, D_MODEL), lambda i, j: (i, 0)),
        scratch_shapes=[pltpu.VMEM((tm, D_MODEL), BF16), pltpu.VMEM((tm, D_MODEL), F32)],
        compiler_params=_params(("parallel", "arbitrary")),
        name="mlp",
    )(x, wu3, wd3, g4, g5)


def _prep_stacked_weights(w_branch, w_out, w_xq, w_mem_k, w_mem_v, w_xo, w_up, w_down):
    depth, nff = w_up.shape[0], D_FF // D_MODEL
    return dict(
        wb=w_branch.astype(BF16), wo=w_out.astype(BF16), wxq=w_xq.astype(BF16), wxo=w_xo.astype(BF16),
        wmem=jnp.concatenate([w_mem_k, w_mem_v], axis=2).astype(BF16),
        wu3=w_up.astype(BF16),
        wd3=w_down.astype(BF16).reshape(depth, nff, D_MODEL, D_MODEL))


def _prep_layer_weights(w_in, w_mla_uq, w_mla_ukv):
    idx = np.cumsum((0,) + IN_SIZES)
    seg = lambda i: w_in[:, idx[i]:idx[i + 1]]
    fq, fk, fv, ff, cq, ckv, kpe, hq, hf, hi, hg, ga, gb, gc = (seg(i) for i in range(14))
    half = MLA_ROPE // 2
    kpe_sw = jnp.concatenate([kpe[:, half:], kpe[:, :half]], axis=1)
    pad = jnp.zeros((D_MODEL, IN_TN - HV_W), w_in.dtype)
    w_p = jnp.concatenate([hq, hf, cq, ckv, hi, hg, fq, pad], axis=1).astype(BF16)
    w_gate = jnp.stack([ga, gb, gc]).astype(BF16)
    w_t = jnp.concatenate([fk, fv, kpe, kpe_sw, ff], axis=1).T.astype(BF16)
    hd = MLA_NOPE + MLA_ROPE
    zq = jnp.zeros((MLA_Q_RANK, LANE - MLA_ROPE), w_mla_uq.dtype)
    nope, rope_n, rope_s = [], [], []
    for h in range(MLA_HEADS):
        base = h * hd
        nope.append(w_mla_uq[:, base:base + MLA_NOPE])
        x1 = w_mla_uq[:, base + MLA_NOPE:base + MLA_NOPE + half]
        x2 = w_mla_uq[:, base + MLA_NOPE + half:base + hd]
        rope_n += [x1, x2, zq]
        rope_s += [x2, x1, zq]
    wuq = jnp.concatenate(nope + rope_n + rope_s, axis=1).astype(BF16)
    kvd = MLA_NOPE + MLA_V
    wkt = jnp.concatenate([w_mla_ukv[:, h * kvd:h * kvd + MLA_NOPE] for h in range(MLA_HEADS)],
                          axis=1).T.astype(BF16)
    wv = jnp.concatenate([w_mla_ukv[:, h * kvd + MLA_NOPE:(h + 1) * kvd] for h in range(MLA_HEADS)],
                         axis=1).astype(BF16)
    return dict(w_in3=w_p, w_t=w_t, w_gate=w_gate, wuq=wuq, wkt=wkt, wv=wv)


def _rope_tables(pos):
    half = MLA_ROPE // 2
    freq = ROPE_THETA ** (-jnp.arange(half, dtype=F32) / half)
    ang = pos.astype(F32)[:, None] * freq[None, :]
    cos, sin = jnp.cos(ang), jnp.sin(ang)
    z = jnp.zeros((pos.shape[0], LANE - MLA_ROPE), F32)
    cos_r = jnp.concatenate([cos, cos, z], axis=1)
    sin_r = jnp.concatenate([-sin, sin, z], axis=1)
    return cos_r, sin_r, cos_r[:, :MLA_ROPE].T, sin_r[:, :MLA_ROPE].T


def _tile(n, pref):
    t = min(n, pref)
    assert n % t == 0
    return t


def _layer(x, bsz, s, pos0, w, lb, b_fox, g_q, g_kv, g_hout, g_norm, mem_k, mem_v, mem_row0, past, cfg,
           layer, depth, shared):
    m = bsz * s
    g = lambda i: g_norm[i][None, :]
    tm_in = _tile(s, cfg["tm_in"])
    tm_p = _tile(s, cfg["tm_prep"])
    cos_r, sin_r, cos_c, sin_c = _rope_tables(pos0 + jnp.arange(s))
    hqf, mla_in, hv, kt, vt, kpet, logft = in_proj(x, g(0), w["w_in3"], w["w_t"], cos_c, sin_c,
                                                   b_fox[:, None], bsz, s, tm_in, layer, depth, shared[:4])
    qx, knt, v, ckv_n = mla_prep(mla_in, cos_r, sin_r, g_q[None, :], g_kv[None, :],
                                 w["wuq"], w["wkt"], w["wv"], bsz, s, tm_p, layer, depth, shared[4:])
    row0 = layer * bsz
    rows = lambda a: a.reshape((depth * bsz,) + a.shape[2:])
    cumt = fox_cumsum(rows(logft), row0, bsz)
    if past is None:
        t = _tile(s, cfg["t_attn"])
        o_fox = fox_prompt(hv, rows(kt), rows(vt), cumt, row0, bsz, s, t)
        o_mla = mla_prompt(qx, knt, rows(kpet), v, row0, bsz, s, t)
        o_hg, hg_state = hgrn(hqf, hv, lb[None, :], g_hout[None, :], bsz, s, CHUNK,
                              _tile(s, cfg["hg_rows"]))
    else:
        c_kt, c_vt, c_cumt, c_ckv, c_kpt, c_hg = past
        p = c_kt.shape[2]
        o_fox = fox_sample(hv, rows(kt), rows(vt), c_kt, c_vt, cumt, c_cumt, bsz, s, p, layer)
        o_mla = mla_sample(qx, knt, rows(kpet), v, c_ckv, c_kpt, w["wkt"], w["wv"], bsz, s, p, layer)
        o_hg, hg_state = hgrn(hqf, hv, lb[None, :], g_hout[None, :], bsz, s, s, s, s0=c_hg, s0_row0=row0)
    x = merge_out(o_fox, o_mla, o_hg, x, w["w_gate"], w["wb"], w["wo"], g(0), g(1),
                  _tile(m, cfg["tm_merge"]), layer)
    x = cross_block(x, mem_k, mem_v, w["wxq"], w["wxo"], g(2), g(3), bsz, s, _tile(s, cfg["tm_cross"]),
                    mem_row0, layer)
    x = mlp_block(x, w["wu3"], w["wd3"], g(4), g(5), _tile(m, cfg["tm_mlp"]), layer)
    return x, (kt, vt, kpet, logft, ckv_n), hg_state


def _from_feature_major(stacked, heads):
    a = jnp.swapaxes(stacked, 2, 3)
    if heads:
        a = a.reshape(a.shape[:3] + (heads, a.shape[3] // heads))
    return a


def _assemble_states(shared, hg_states, bsz, s):
    kt, vt, kpet, logft, ckv = shared
    return (_from_feature_major(kt, FOX_HEADS), _from_feature_major(vt, FOX_HEADS),
            _from_feature_major(logft, 0), ckv.reshape(ckv.shape[0], bsz, s, MLA_KV_RANK),
            _from_feature_major(kpet, 0), jnp.stack(hg_states))


_CFG = dict(tm_in=1024, tm_prep=1024, t_attn=512, hg_rows=1024, tm_merge=512, tm_cross=1024,
            tm_mlp=1024, tm_mem=512)


def kernel(x_prompt, x_sample, cache_fox_k, cache_fox_v, cache_fox_logf, cache_mla_ckv, cache_mla_kpe,
           state_hgrn, cache_mem_k, cache_mem_v, mem_prompt, w_in, b_fox, g_mla_q, w_mla_uq, g_mla_kv,
           w_mla_ukv, g_hgrn_out, lb_hgrn, w_branch, w_out, w_xq, w_mem_k, w_mem_v, w_xo, w_up, w_down,
           g_norm):
    cfg = _CFG
    depth = w_in.shape[0]
    lb_p = jax.nn.softmax(lb_hgrn.astype(F32), axis=0)
    lb_all = jnp.cumsum(lb_p, axis=0) - lb_p[0]
    stacked = _prep_stacked_weights(w_branch, w_out, w_xq, w_mem_k, w_mem_v, w_xo, w_up, w_down)
    ws = [dict(stacked, **_prep_layer_weights(w_in[l], w_mla_uq[l], w_mla_ukv[l])) for l in range(depth)]

    def run_layer(x, bsz, s, pos0, l, mk, mv, mem_row0, past, shared):
        return _layer(x, bsz, s, pos0, ws[l], lb_all[l], b_fox[l], g_mla_q[l], g_mla_kv[l],
                      g_hgrn_out[l], g_norm[l], mk, mv, mem_row0, past, cfg, l, depth, shared)

    bp, sp, _ = x_prompt.shape
    x = x_prompt.reshape(bp * sp, D_MODEL)
    mem = mem_prompt.reshape(bp * N_MEM, D_MODEL)
    shared, hg_states, p_mem = (), [], ()
    mem_rows = lambda a: a.reshape(-1, X_DIM)
    for l in range(depth):
        p_mem = mem_kv(mem, ws[l]["wmem"], _tile(bp * N_MEM, cfg["tm_mem"]), l, depth, p_mem)
        x, shared, hg = run_layer(x, bp, sp, 0, l, mem_rows(p_mem[0]), mem_rows(p_mem[1]), l * bp, None,
                                  shared)
        hg_states.append(hg)
    y_prompt = x.reshape(bp, sp, D_MODEL)
    p_out = _assemble_states(shared, hg_states, bp, sp) + tuple(
        a.reshape(depth, bp, N_MEM, X_HEADS, X_DIM) for a in p_mem)

    bs, ts, _ = x_sample.shape
    p = cache_fox_k.shape[2]
    fm = lambda c: jnp.moveaxis(c, 2, -1)
    c_kt = fm(cache_fox_k).reshape(depth * bs, BRANCH_W, p)
    c_vt = fm(cache_fox_v).reshape(depth * bs, BRANCH_W, p)
    c_kpt = fm(cache_mla_kpe).reshape(depth * bs, MLA_ROPE, p)
    c_cumt = fox_cumsum(fm(cache_fox_logf).reshape(depth * bs, FOX_HEADS, p), 0, depth * bs)
    c_ckv = cache_mla_ckv.reshape(depth * bs * p, MLA_KV_RANK)
    c_hg = state_hgrn.reshape((depth * bs,) + state_hgrn.shape[2:])
    c_mk, c_mv = mem_rows(cache_mem_k), mem_rows(cache_mem_v)
    x = x_sample.reshape(bs * ts, D_MODEL)
    shared, hg_states = (), []
    for l in range(depth):
        past = (c_kt, c_vt, c_cumt, c_ckv, c_kpt, c_hg)
        x, shared, hg = run_layer(x, bs, ts, p, l, c_mk, c_mv, l * bs, past, shared)
        hg_states.append(hg)
    y_sample = x.reshape(bs, ts, D_MODEL)
    return (y_prompt, y_sample, *p_out, *_assemble_states(shared, hg_states, bs, ts))
```
